```python
import math
import jax
import jax.numpy as jnp
from jax import lax
import numpy as np

D_MODEL = 2048
BATCH = 2
SEQ = 16384
DEPTH = 4

NSA_HEADS = 8
NSA_GROUPS = 2
NSA_HPG = NSA_HEADS // NSA_GROUPS
HEAD_DIM = 128
CMP_LEN = 32
CMP_STRIDE = 16
CMP_HIDDEN = 128
SEL_BLOCK = 64
SEL_TOPK = 16
WINDOW = 512
Q_BLOCK = 128
DN_HEADS = 8
DN_DK = 128
DN_DV = 128
DN_CONV = 4
DN_CHUNK = 64
REL_BUCKETS = 32
REL_MAX_DIST = 128
D_FF = 5632
EPS = 1e-6
NEG_INF = -1e30
BIG = 1e30

NSA_Q_DIM = NSA_HEADS * HEAD_DIM
NSA_KV_DIM = NSA_GROUPS * HEAD_DIM
DN_QK_DIM = DN_HEADS * DN_DK
DN_V_DIM = DN_HEADS * DN_DV
IN_SPLITS = (NSA_Q_DIM, NSA_KV_DIM, NSA_KV_DIM, NSA_KV_DIM, NSA_KV_DIM, NSA_KV_DIM, NSA_KV_DIM,
             3 * NSA_HEADS, DN_QK_DIM, DN_QK_DIM, DN_V_DIM, DN_V_DIM, DN_HEADS, DN_HEADS,
             D_MODEL, D_MODEL)
IN_DIM = sum(IN_SPLITS)
CONV_DIM = 2 * DN_QK_DIM + DN_V_DIM

kernel_name = 'nsa_deltanet_hybrid'


def rms_norm(x, w):
    xf = x.astype(jnp.float32)
    y = xf * lax.rsqrt(jnp.mean(xf * xf, axis=-1, keepdims=True) + EPS)
    return (y * w.astype(jnp.float32)).astype(x.dtype)


def l2_norm(x):
    xf = x.astype(jnp.float32)
    return xf * lax.rsqrt(jnp.sum(xf * xf, axis=-1, keepdims=True) + EPS)


def swiglu(x, w_gate, w_up, w_down):
    return (jax.nn.silu(x @ w_gate) * (x @ w_up)) @ w_down


def masked_softmax(logits, mask):
    p = jax.nn.softmax(jnp.where(mask, logits, NEG_INF), axis=-1)
    return jnp.where(mask, p, 0.0)


def rel_bucket(dist):
    n = jnp.maximum(dist, 0)
    max_exact = REL_BUCKETS // 2
    nf = jnp.maximum(n, max_exact).astype(jnp.float32)
    large = max_exact + (jnp.log(nf / max_exact) / math.log(REL_MAX_DIST / max_exact)
                         * (REL_BUCKETS - max_exact)).astype(jnp.int32)
    return jnp.where(n < max_exact, n, jnp.minimum(large, REL_BUCKETS - 1))


def rel_bias_shared(dist, table):
    b = table[rel_bucket(dist)].astype(jnp.float32)
    return jnp.moveaxis(b, -1, 0).reshape(NSA_GROUPS, NSA_HPG, *dist.shape)


def rel_bias_grouped(dist, table):
    tg = jnp.transpose(table.reshape(REL_BUCKETS, NSA_GROUPS, NSA_HPG), (1, 0, 2)).astype(jnp.float32)
    b = jax.vmap(lambda t, d: t[rel_bucket(d)], in_axes=(0, 1), out_axes=1)(tg, dist)
    return jnp.moveaxis(b, -1, 2)


def compress_blocks(kv, pos, w1, b1, w2):
    b, g, s, dh = kv.shape
    r = CMP_LEN // CMP_STRIDE
    n_cmp = s // CMP_STRIDE - r + 1
    pieces = kv.reshape(b, g, s // CMP_STRIDE, CMP_STRIDE, dh)
    blocks = jnp.concatenate([pieces[:, :, j:j + n_cmp] for j in range(r)], axis=3)
    flat = (blocks + pos).reshape(b, g, n_cmp, CMP_LEN * dh)
    return jax.nn.silu(flat @ w1 + b1) @ w2


def nsa_mixer(q, k_c, v_c, k_s, v_s, k_w, v_w, gates, rel_table):
    bsz, seq = q.shape[0], q.shape[1]
    n_cmp = k_c.shape[2]
    n_sb = seq // SEL_BLOCK
    n_sel = min(SEL_TOPK, n_sb)
    n_qb = seq // Q_BLOCK
    f32 = jnp.float32

    def to_blocks(t):
        t = t.reshape(bsz, n_qb, Q_BLOCK, NSA_GROUPS, NSA_HPG, t.shape[-1])
        return jnp.transpose(t, (1, 0, 3, 4, 2, 5))

    q_blocks = to_blocks(q)
    g_blocks = to_blocks(gates)
    starts = jnp.arange(n_qb) * Q_BLOCK

    c_start = jnp.arange(n_cmp) * CMP_STRIDE
    c_end = c_start + CMP_LEN - 1
    s_start = jnp.arange(n_sb) * SEL_BLOCK
    overlap = ((c_start[:, None] < s_start[None, :] + SEL_BLOCK)
               & (c_start[:, None] + CMP_LEN > s_start[None, :])).astype(f32)
    kb_s = k_s.reshape(bsz, NSA_GROUPS, n_sb, SEL_BLOCK, HEAD_DIM)
    vb_s = v_s.reshape(bsz, NSA_GROUPS, n_sb, SEL_BLOCK, HEAD_DIM)
    pad = ((0, 0), (0, 0), (WINDOW, 0), (0, 0))
    k_wp = jnp.pad(k_w, pad)
    v_wp = jnp.pad(v_w, pad)
    gather = jax.vmap(jax.vmap(lambda blk, ix: blk[ix]))
    blk_id = jnp.arange(n_sb)
    win_off = jnp.arange(WINDOW + Q_BLOCK)
    tok_off = jnp.arange(SEL_BLOCK)

    def step(args):
        qb, gb, q0 = args
        t = q0 + jnp.arange(Q_BLOCK)
        s_c = jnp.einsum('bghqd,bgnd->bghqn', qb, k_c).astype(f32) \
            + rel_bias_shared(t[:, None] - c_end[None, :], rel_table)
        p_c = masked_softmax(s_c, c_end[None, :] <= t[:, None])
        o_c = jnp.einsum('bghqn,bgnd->bghqd', p_c.astype(v_c.dtype), v_c)
        imp = jnp.einsum('bghqn,nj->bgqj', p_c, overlap)
        cur = (t // SEL_BLOCK)[:, None]
        eligible = s_start[None, :] <= t[:, None]
        forced = (blk_id[None, :] == 0) | (blk_id[None, :] == cur) | (blk_id[None, :] == cur - 1)
        score = jnp.where(eligible, jnp.where(forced, BIG, imp), NEG_INF)
        top_score, idx = lax.top_k(score, n_sel)
        valid = top_score > -1.0
        kg = gather(kb_s, idx).reshape(bsz, NSA_GROUPS, Q_BLOCK, n_sel * SEL_BLOCK, HEAD_DIM)
        vg = gather(vb_s, idx).reshape(bsz, NSA_GROUPS, Q_BLOCK, n_sel * SEL_BLOCK, HEAD_DIM)
        pos = (idx[..., None] * SEL_BLOCK + tok_off).reshape(bsz, NSA_GROUPS, Q_BLOCK, n_sel * SEL_BLOCK)
        mask_s = (pos <= t[:, None]) & jnp.repeat(valid, SEL_BLOCK, axis=-1)
        s_s = jnp.einsum('bghqd,bgqmd->bghqm', qb, kg).astype(f32) + rel_bias_grouped(t[:, None] - pos, rel_table)
        p_s = masked_softmax(s_s, mask_s[:, :, None])
        o_s = jnp.einsum('bghqm,bgqmd->bghqd', p_s.astype(vg.dtype), vg)
        kw = lax.dynamic_slice_in_dim(k_wp, q0, WINDOW + Q_BLOCK, axis=2)
        vw = lax.dynamic_slice_in_dim(v_wp, q0, WINDOW + Q_BLOCK, axis=2)
        pos_w = q0 - WINDOW + win_off
        dist_w = t[:, None] - pos_w[None, :]
        mask_w = (dist_w >= 0) & (dist_w < WINDOW) & (pos_w[None, :] >= 0)
        s_w = jnp.einsum('bghqd,bgkd->bghqk', qb, kw).astype(f32) + rel_bias_shared(dist_w, rel_table)
        p_w = masked_softmax(s_w, mask_w)
        o_w = jnp.einsum('bghqk,bgkd->bghqd', p_w.astype(vw.dtype), vw)
        return gb[..., 0:1] * o_c + gb[..., 1:2] * o_s + gb[..., 2:3] * o_w

    out = lax.map(step, (q_blocks, g_blocks, starts))
    return jnp.transpose(out, (1, 0, 4, 2, 3, 5)).reshape(bsz, seq, NSA_HEADS * HEAD_DIM)


def causal_conv(x, w):
    return lax.conv_general_dilated(x, w[:, None, :].astype(x.dtype), window_strides=(1,),
                                    padding=[(DN_CONV - 1, 0)], dimension_numbers=('NWC', 'WIO', 'NWC'),
                                    feature_group_count=x.shape[-1])


def chunk_gated_delta_rule(q, k, v, g, beta):
    bsz, seq, nh, dk = q.shape
    dv = v.shape[-1]
    c = DN_CHUNK
    n = seq // c

    def to_chunks(t):
        return jnp.moveaxis(t, 2, 1).reshape(bsz, nh, n, c, *t.shape[3:])

    q, k, v, g, beta = (to_chunks(t) for t in (q, k, v, g, beta))
    gc = jnp.cumsum(g, axis=-1)
    causal = jnp.tril(jnp.ones((c, c), bool))
    strict = jnp.tril(jnp.ones((c, c), bool), -1)
    decay = jnp.exp(jnp.where(causal, gc[..., :, None] - gc[..., None, :], -jnp.inf))
    k_beta = k * beta[..., None]
    kk = jnp.einsum('bhnid,bhnjd->bhnij', k_beta, k) * decay
    lower = jnp.where(strict, kk, 0.0) + jnp.eye(c, dtype=jnp.float32)
    u = lax.linalg.triangular_solve(lower, v * beta[..., None], left_side=True, lower=True, unit_diagonal=True)
    w = lax.linalg.triangular_solve(lower, k_beta * jnp.exp(gc)[..., None], left_side=True, lower=True,
                                    unit_diagonal=True)
    qk = jnp.where(causal, jnp.einsum('bhnid,bhnjd->bhnij', q, k) * decay, 0.0)
    q_dec = q * jnp.exp(gc)[..., None]
    k_dec = k * jnp.exp(gc[..., -1:] - gc)[..., None]
    g_last = jnp.exp(gc[..., -1])
    xs = tuple(jnp.moveaxis(t, 2, 0) for t in (q_dec, k_dec, u, w, qk, g_last))

    def step(state, inp):
        qd, kd, u_i, w_i, qk_i, gl = inp
        v_new = u_i - jnp.einsum('bhck,bhkv->bhcv', w_i, state)
        o = jnp.einsum('bhck,bhkv->bhcv', qd, state) + jnp.einsum('bhij,bhjv->bhiv', qk_i, v_new)
        state = state * gl[..., None, None] + jnp.einsum('bhck,bhcv->bhkv', kd, v_new)
        return state, o

    state0 = jnp.zeros((bsz, nh, dk, dv), jnp.float32)
    _, o = lax.scan(step, state0, xs)
    return jnp.transpose(jnp.moveaxis(o, 0, 2).reshape(bsz, nh, seq, dv), (0, 2, 1, 3))


def gated_deltanet(dq, dk, dv, dz, db, da, conv_w, a_log, dt_bias, norm_w):
    bsz, seq, _ = dq.shape
    f32 = jnp.float32
    qkv = jax.nn.silu(causal_conv(jnp.concatenate([dq, dk, dv], axis=-1), conv_w))
    q, k, v = jnp.split(qkv, [DN_QK_DIM, 2 * DN_QK_DIM], axis=-1)
    q = l2_norm(q.reshape(bsz, seq, DN_HEADS, DN_DK)) * (DN_DK ** -0.5)
    k = l2_norm(k.reshape(bsz, seq, DN_HEADS, DN_DK))
    v = v.reshape(bsz, seq, DN_HEADS, DN_DV).astype(f32)
    beta = jax.nn.sigmoid(db.astype(f32))
    g = -jnp.exp(a_log.astype(f32)) * jax.nn.softplus(da.astype(f32) + dt_bias.astype(f32))
    o = chunk_gated_delta_rule(q, k, v, g, beta)
    o = rms_norm(o, norm_w) * jax.nn.silu(dz.reshape(bsz, seq, DN_HEADS, DN_DV).astype(f32))
    return o.reshape(bsz, seq, DN_V_DIM).astype(dq.dtype)


def token_mixer(u, w_in, q_norm, k_norm, cmp_pos, cmp_w1, cmp_b1, cmp_w2, dn_conv, dn_a_log, dn_dt_bias,
                dn_norm, w_branch_nsa, w_branch_dn, w_out, rel_table):
    bsz, seq, _ = u.shape
    offsets = [int(o) for o in np.cumsum(IN_SPLITS)[:-1]]
    (nq, kc, vc, ksel, vsel, kwin, vwin, ngate, dq, dk, dv, dz, db, da, mga, mgb) = \
        jnp.split(u @ w_in, offsets, axis=-1)

    def grp(t):
        return jnp.transpose(t.reshape(bsz, seq, NSA_GROUPS, HEAD_DIM), (0, 2, 1, 3))

    q = rms_norm(nq.reshape(bsz, seq, NSA_HEADS, HEAD_DIM), q_norm) * (HEAD_DIM ** -0.5)
    k_c = rms_norm(compress_blocks(grp(kc), cmp_pos[0], cmp_w1[0], cmp_b1[0], cmp_w2[0]), k_norm[0])
    v_c = compress_blocks(grp(vc), cmp_pos[1], cmp_w1[1], cmp_b1[1], cmp_w2[1])
    k_s = rms_norm(grp(ksel), k_norm[1])
    k_w = rms_norm(grp(kwin), k_norm[2])
    gates = jax.nn.sigmoid(ngate.reshape(bsz, seq, NSA_HEADS, 3))
    y_nsa = nsa_mixer(q, k_c, v_c, k_s, grp(vsel), k_w, grp(vwin), gates, rel_table)
    y_dn = gated_deltanet(dq, dk, dv, dz, db, da, dn_conv, dn_a_log, dn_dt_bias, dn_norm)
    merged = jax.nn.sigmoid(mga) * (y_nsa @ w_branch_nsa) + jax.nn.sigmoid(mgb) * (y_dn @ w_branch_dn)
    return merged @ w_out


def setup_inputs(seed: int = 0) -> dict:
    key = jax.random.key(seed)
    ks = jax.random.split(key, 32)
    f32 = jnp.float32
    L, D, F = DEPTH, D_MODEL, D_FF

    def nrm(k, shape, fan_in):
        return jax.random.normal(k, shape, f32) * (fan_in ** -0.5)

    def gain(k, shape):
        return 1.0 + 0.02 * jax.random.normal(k, shape, f32)

    dt = jnp.exp(jax.random.uniform(ks[20], (L, DN_HEADS), f32, math.log(1e-3), math.log(1e-1)))
    return {
        'x': jax.random.normal(ks[0], (BATCH, SEQ, D), f32),
        'rel_table': 0.2 * jax.random.normal(ks[1], (REL_BUCKETS, NSA_HEADS), f32),
        'norm_ffn1': gain(ks[2], (L, D)),
        'ffn1_gate': nrm(ks[3], (L, D, F), D),
        'ffn1_up': nrm(ks[4], (L, D, F), D),
        'ffn1_down': nrm(ks[5], (L, F, D), F),
        'norm_mix': gain(ks[6], (L, D)),
        'w_in': nrm(ks[7], (L, D, IN_DIM), D),
        'q_norm': gain(ks[8], (L, HEAD_DIM)),
        'k_norm': gain(ks[9], (L, 3, HEAD_DIM)),
        'cmp_pos': 0.02 * jax.random.normal(ks[10], (L, 2, CMP_LEN, HEAD_DIM), f32),
        'cmp_w1': nrm(ks[11], (L, 2, CMP_LEN * HEAD_DIM, CMP_HIDDEN), CMP_LEN * HEAD_DIM),
        'cmp_b1': 0.02 * jax.random.normal(ks[12], (L, 2, CMP_HIDDEN), f32),
        'cmp_w2': nrm(ks[13], (L, 2, CMP_HIDDEN, HEAD_DIM), CMP_HIDDEN),
        'dn_conv': nrm(ks[14], (L, DN_CONV, CONV_DIM), DN_CONV),
        'dn_a_log': jnp.log(jax.random.uniform(ks[15], (L, DN_HEADS), f32, 1.0, 16.0)),
        'dn_dt_bias': dt + jnp.log(-jnp.expm1(-dt)),
        'dn_norm': gain(ks[16], (L, DN_DV)),
        'w_branch_nsa': nrm(ks[17], (L, NSA_Q_DIM, D), NSA_Q_DIM),
        'w_branch_dn': nrm(ks[18], (L, DN_V_DIM, D), DN_V_DIM),
        'w_out': nrm(ks[19], (L, D, D), D),
        'norm_ffn2': gain(ks[21], (L, D)),
        'ffn2_gate': nrm(ks[22], (L, D, F), D),
        'ffn2_up': nrm(ks[23], (L, D, F), D),
        'ffn2_down': nrm(ks[24], (L, F, D), F),
    }


def reference(x, rel_table, norm_ffn1, ffn1_gate, ffn1_up, ffn1_down, norm_mix, w_in, q_norm, k_norm,
              cmp_pos, cmp_w1, cmp_b1, cmp_w2, dn_conv, dn_a_log, dn_dt_bias, dn_norm, w_branch_nsa,
              w_branch_dn, w_out, norm_ffn2, ffn2_gate, ffn2_up, ffn2_down):
    h = x
    for l in range(DEPTH):
        h = h + 0.5 * swiglu(rms_norm(h, norm_ffn1[l]), ffn1_gate[l], ffn1_up[l], ffn1_down[l])
        u = rms_norm(h, norm_mix[l])
        h = h + token_mixer(u, w_in[l], q_norm[l], k_norm[l], cmp_pos[l], cmp_w1[l], cmp_b1[l], cmp_w2[l],
                            dn_conv[l], dn_a_log[l], dn_dt_bias[l], dn_norm[l], w_branch_nsa[l],
                            w_branch_dn[l], w_out[l], rel_table)
        h = h + 0.5 * swiglu(rms_norm(h, norm_ffn2[l]), ffn2_gate[l], ffn2_up[l], ffn2_down[l])
    return h
```

```python
import functools
import math

import numpy as np
import jax
import jax.numpy as jnp
from jax import lax
from jax.experimental import pallas as pl
from jax.experimental.pallas import tpu as pltpu

F32 = jnp.float32
BF16 = jnp.bfloat16
HI = lax.Precision.HIGHEST

D_MODEL = 2048
D_FF = 5632
N_HEADS = 8
N_GROUPS = 2
HPG = N_HEADS // N_GROUPS
HEAD_DIM = 128
CMP_LEN = 32
CMP_STRIDE = 16
CMP_HIDDEN = 128
SEL_BLOCK = 64
SEL_TOPK = 16
WINDOW = 512
Q_BLOCK = 128
DN_HEADS = 8
DN_DIM = 128
DN_CONV = 4
DN_CHUNK = 64
REL_BUCKETS = 32
REL_MAX_DIST = 128
EPS = 1e-6
NEG_INF = -1e30
BIG = 1e30

LANES = 128
SUBLANES = 8
VMEM_LIMIT = 56 * 1024 * 1024

OFF_DQ, OFF_DK, OFF_DV = 0, 1024, 2048
OFF_NQ = 3072
OFF_DZ = 4096
OFF_MGA = 5120
OFF_MGB = 7168
OFF_KV6 = 9216
OFF_NGATE = 10752
OFF_DBA = 10880
PROJ_USED = 11008
PROJ_DIM = 11264

CMP_PAD = 16
N_CMP_PAD = 1152
NEAR_ROWS = 24


def _rel_bucket_thresholds():
    n = np.arange(0, 4 * REL_MAX_DIST, dtype=np.int64)
    max_exact = REL_BUCKETS // 2
    nf = np.maximum(n, max_exact).astype(np.float32)
    large = max_exact + (np.log(nf / np.float32(max_exact)) / np.float32(math.log(REL_MAX_DIST / max_exact))
                         * np.float32(REL_BUCKETS - max_exact)).astype(np.int32)
    bucket = np.where(n < max_exact, n, np.minimum(large, REL_BUCKETS - 1))
    assert np.all(np.diff(bucket) >= 0)
    return [int(np.argmax(bucket >= b)) for b in range(REL_BUCKETS)]


REL_THRESH = _rel_bucket_thresholds()
FAR_DIST = REL_THRESH[REL_BUCKETS - 1]
assert FAR_DIST <= Q_BLOCK


def _cparams(sem):
    return pltpu.CompilerParams(dimension_semantics=sem, vmem_limit_bytes=VMEM_LIMIT)


def _rms_rows(x, w_row):
    ms = jnp.mean(x * x, axis=-1, keepdims=True)
    return x * lax.rsqrt(ms + EPS) * w_row


def _ffn_kernel(x_ref, nw_ref, wg_ref, wu_ref, wd_ref, o_ref, xn_ref, acc_ref):
    j = pl.program_id(1)

    @pl.when(j == 0)
    def _():
        xn_ref[...] = _rms_rows(x_ref[...], nw_ref[...]).astype(BF16)
        acc_ref[...] = jnp.zeros_like(acc_ref)

    xn = xn_ref[...]
    g = jnp.dot(xn, wg_ref[...], preferred_element_type=F32)
    u = jnp.dot(xn, wu_ref[...], preferred_element_type=F32)
    a = (g * jax.nn.sigmoid(g) * u).astype(BF16)
    acc_ref[...] += jnp.dot(a, wd_ref[...], preferred_element_type=F32)

    @pl.when(j == pl.num_programs(1) - 1)
    def _():
        o_ref[...] = x_ref[...] + 0.5 * acc_ref[...]


def _ffn(h, nw, wg, wu, wd, tm=512, tf=512):
    n, d = h.shape
    f = wg.shape[1]
    return pl.pallas_call(
        _ffn_kernel,
        grid=(n // tm, f // tf),
        in_specs=[
            pl.BlockSpec((tm, d), lambda i, j: (i, 0)),
            pl.BlockSpec((1, d), lambda i, j: (0, 0)),
            pl.BlockSpec((d, tf), lambda i, j: (0, j)),
            pl.BlockSpec((d, tf), lambda i, j: (0, j)),
            pl.BlockSpec((tf, d), lambda i, j: (j, 0)),
        ],
        out_specs=pl.BlockSpec((tm, d), lambda i, j: (i, 0)),
        out_shape=jax.ShapeDtypeStruct((n, d), F32),
        scratch_shapes=[pltpu.VMEM((tm, d), BF16), pltpu.VMEM((tm, d), F32)],
        compiler_params=_cparams(("parallel", "arbitrary")),
        name="ffn",
    )(h, nw, wg, wu, wd)


def _norm_matmul_kernel(x_ref, nw_ref, w_ref, o_ref, xn_ref):
    @pl.when(pl.program_id(1) == 0)
    def _():
        xn_ref[...] = _rms_rows(x_ref[...], nw_ref[...]).astype(BF16)

    o_ref[...] = jnp.dot(xn_ref[...], w_ref[...], preferred_element_type=F32)


def _norm_matmul(h, nw, w, tm=512, tn=1024):
    n, d = h.shape
    nout = w.shape[1]
    return pl.pallas_call(
        _norm_matmul_kernel,
        grid=(n // tm, nout // tn),
        in_specs=[
            pl.BlockSpec((tm, d), lambda i, j: (i, 0)),
            pl.BlockSpec((1, d), lambda i, j: (0, 0)),
            pl.BlockSpec((d, tn), lambda i, j: (0, j)),
        ],
        out_specs=pl.BlockSpec((tm, tn), lambda i, j: (i, j)),
        out_shape=jax.ShapeDtypeStruct((n, nout), F32),
        scratch_shapes=[pltpu.VMEM((tm, d), BF16)],
        compiler_params=_cparams(("parallel", "arbitrary")),
        name="in_proj",
    )(h, nw, w)


def _merge_kernel(ya_ref, yb_ref, wa_ref, wb_ref, ga_ref, gb_ref, o_ref):
    a = jnp.dot(ya_ref[...], wa_ref[...], preferred_element_type=F32)
    b = jnp.dot(yb_ref[...], wb_ref[...], preferred_element_type=F32)
    o_ref[...] = (jax.nn.sigmoid(ga_ref[...]) * a + jax.nn.sigmoid(gb_ref[...]) * b).astype(o_ref.dtype)


def _merge(ya, yb, wa, wb, proj, tm=512, tn=1024):
    n, k = ya.shape
    d = wa.shape[1]
    ga0, gb0 = OFF_MGA // tn, OFF_MGB // tn
    return pl.pallas_call(
        _merge_kernel,
        grid=(n // tm, d // tn),
        in_specs=[
            pl.BlockSpec((tm, k), lambda i, j: (i, 0)),
            pl.BlockSpec((tm, k), lambda i, j: (i, 0)),
            pl.BlockSpec((k, tn), lambda i, j: (0, j)),
            pl.BlockSpec((k, tn), lambda i, j: (0, j)),
            pl.BlockSpec((tm, tn), lambda i, j: (i, ga0 + j)),
            pl.BlockSpec((tm, tn), lambda i, j: (i, gb0 + j)),
        ],
        out_specs=pl.BlockSpec((tm, tn), lambda i, j: (i, j)),
        out_shape=jax.ShapeDtypeStruct((n, d), BF16),
        compiler_params=_cparams(("parallel", "arbitrary")),
        name="merge",
    )(ya, yb, wa, wb, proj, proj)


def _out_proj_kernel(m_ref, w_ref, h_ref, o_ref):
    o_ref[...] = h_ref[...] + jnp.dot(m_ref[...], w_ref[...], preferred_element_type=F32)


def _out_proj(merged, w, h, tm=512, tn=1024):
    n, k = merged.shape
    d = w.shape[1]
    return pl.pallas_call(
        _out_proj_kernel,
        grid=(n // tm, d // tn),
        in_specs=[
            pl.BlockSpec((tm, k), lambda i, j: (i, 0)),
            pl.BlockSpec((k, tn), lambda i, j: (0, j)),
            pl.BlockSpec((tm, tn), lambda i, j: (i, j)),
        ],
        out_specs=pl.BlockSpec((tm, tn), lambda i, j: (i, j)),
        out_shape=jax.ShapeDtypeStruct((n, d), F32),
        compiler_params=_cparams(("parallel", "arbitrary")),
        name="out_proj",
    )(merged, w, h)


def _nsa_prep_kernel(nq_ref, kv_ref, ng_ref, qn_ref, kn_ref, q_out, kv_out, g_out):
    scale = HEAD_DIM ** -0.5
    for h in range(N_HEADS):
        sl = slice(h * HEAD_DIM, (h + 1) * HEAD_DIM)
        q_out[:, sl] = (_rms_rows(nq_ref[:, sl], qn_ref[...]) * scale).astype(BF16)
    for c in range(6 * N_GROUPS):
        sl = slice(c * HEAD_DIM, (c + 1) * HEAD_DIM)
        kind = c // N_GROUPS
        x = kv_ref[:, sl]
        if kind == 2:
            x = _rms_rows(x, kn_ref[1:2, :])
        elif kind == 4:
            x = _rms_rows(x, kn_ref[2:3, :])
        kv_out[:, sl] = x.astype(BF16)
    g_out[...] = jax.nn.sigmoid(ng_ref[...])


def _nsa_prep(proj, q_norm, k_norm, tm=512):
    n = proj.shape[0]
    kvw = 6 * N_GROUPS * HEAD_DIM
    return pl.pallas_call(
        _nsa_prep_kernel,
        grid=(n // tm,),
        in_specs=[
            pl.BlockSpec((tm, 1024), lambda i: (i, OFF_NQ // 1024)),
            pl.BlockSpec((tm, kvw), lambda i: (i, OFF_KV6 // kvw)),
            pl.BlockSpec((tm, LANES), lambda i: (i, OFF_NGATE // LANES)),
            pl.BlockSpec((1, HEAD_DIM), lambda i: (0, 0)),
            pl.BlockSpec((SUBLANES, HEAD_DIM), lambda i: (0, 0)),
        ],
        out_specs=[
            pl.BlockSpec((tm, 1024), lambda i: (i, 0)),
            pl.BlockSpec((tm, kvw), lambda i: (i, 0)),
            pl.BlockSpec((tm, LANES), lambda i: (i, 0)),
        ],
        out_shape=[
            jax.ShapeDtypeStruct((n, 1024), BF16),
            jax.ShapeDtypeStruct((n, kvw), BF16),
            jax.ShapeDtypeStruct((n, LANES), F32),
        ],
        compiler_params=_cparams(("parallel",)),
        name="nsa_prep",
    )(proj, proj, proj, q_norm, k_norm)


def _compress_kernel(p_ref, w1_ref, pos_ref, b1_ref, w2_ref, kn_ref, o_ref):
    half = CMP_STRIDE * HEAD_DIM
    p = p_ref[...]
    w1 = w1_ref[...]
    a = jnp.dot(p, w1[:half].astype(BF16), preferred_element_type=F32)
    b = jnp.dot(p, w1[half:].astype(BF16), preferred_element_type=F32)
    pos = jnp.broadcast_to(pos_ref[...], (SUBLANES, 2 * half))
    c0 = jnp.dot(pos, w1, preferred_element_type=F32, precision=HI)[0:1, :] + b1_ref[...]
    n = p.shape[0]
    hid = a + pltpu.roll(b, n - 1, 0) + c0
    hid = hid * jax.nn.sigmoid(hid)
    out = jnp.dot(hid.astype(BF16), w2_ref[...].astype(BF16), preferred_element_type=F32)
    normed = _rms_rows(out, kn_ref[0:1, :])
    out = jnp.where(pl.program_id(0) == 0, normed, out)
    row = lax.broadcasted_iota(jnp.int32, out.shape, 0)
    o_ref[...] = jnp.where(row < n - 1, out, 0.0)


def _compress(pieces, w1, pos, b1, w2, k_norm):
    _, bsz, ng, npc, width = pieces.shape
    return pl.pallas_call(
        _compress_kernel,
        grid=(2, bsz, ng),
        in_specs=[
            pl.BlockSpec((None, None, None, npc, width), lambda c, b, g: (c, b, g, 0, 0)),
            pl.BlockSpec((None, 2 * width, CMP_HIDDEN), lambda c, b, g: (c, 0, 0)),
            pl.BlockSpec((None, 1, 2 * width), lambda c, b, g: (c, 0, 0)),
            pl.BlockSpec((None, 1, CMP_HIDDEN), lambda c, b, g: (c, 0, 0)),
            pl.BlockSpec((None, CMP_HIDDEN, HEAD_DIM), lambda c, b, g: (c, 0, 0)),
            pl.BlockSpec((SUBLANES, HEAD_DIM), lambda c, b, g: (0, 0)),
        ],
        out_specs=pl.BlockSpec((None, None, None, npc, HEAD_DIM), lambda c, b, g: (c, b, g, 0, 0)),
        out_shape=jax.ShapeDtypeStruct((2, bsz, ng, npc, HEAD_DIM), F32),
        compiler_params=_cparams(("arbitrary", "arbitrary", "arbitrary")),
        name="compress",
    )(pieces, w1, pos, b1, w2, k_norm)


def _bias_kernel(tab_ref, bd_ref, bp_ref, bc_ref):
    g = pl.program_id(0)

    def lookup(dist, head):
        v = jnp.full(dist.shape, tab_ref[0, head], F32)
        for b in range(1, REL_BUCKETS):
            v = jnp.where(dist >= REL_THRESH[b], tab_ref[b, head], v)
        return v - tab_ref[REL_BUCKETS - 1, head]

    ki = lax.broadcasted_iota(jnp.int32, (Q_BLOCK, Q_BLOCK), 0)
    qj = lax.broadcasted_iota(jnp.int32, (Q_BLOCK, Q_BLOCK), 1)
    r = lax.broadcasted_iota(jnp.int32, (NEAR_ROWS, Q_BLOCK), 0)
    qc = lax.broadcasted_iota(jnp.int32, (NEAR_ROWS, Q_BLOCK), 1)
    dist_c = qc - CMP_STRIDE * (r - CMP_PAD) - (CMP_LEN - 1)
    for h in range(HPG):
        head = g * HPG + h
        sl = slice(h * Q_BLOCK, (h + 1) * Q_BLOCK)
        bd_ref[:, sl] = lookup(qj - ki, head)
        bp_ref[:, sl] = lookup(qj - ki + Q_BLOCK, head)
        bc_ref[:, sl] = lookup(dist_c, head)


def _bias_tiles(rel_table):
    wide = HPG * Q_BLOCK
    return pl.pallas_call(
        _bias_kernel,
        grid=(N_GROUPS,),
        in_specs=[pl.BlockSpec(memory_space=pltpu.SMEM)],
        out_specs=[
            pl.BlockSpec((None, Q_BLOCK, wide), lambda g: (g, 0, 0)),
            pl.BlockSpec((None, Q_BLOCK, wide), lambda g: (g, 0, 0)),
            pl.BlockSpec((None, NEAR_ROWS, wide), lambda g: (g, 0, 0)),
        ],
        out_shape=[
            jax.ShapeDtypeStruct((N_GROUPS, Q_BLOCK, wide), F32),
            jax.ShapeDtypeStruct((N_GROUPS, Q_BLOCK, wide), F32),
            jax.ShapeDtypeStruct((N_GROUPS, NEAR_ROWS, wide), F32),
        ],
        compiler_params=_cparams(("arbitrary",)),
        name="rel_bias",
    )(rel_table)


def _tile4(x):
    return jnp.concatenate([x] * HPG, axis=1)


def _nsa_attn_kernel(qT_ref, gT_ref, ks_ref, vsT_ref, kw_ref, vwT_ref, kc_ref, vcT_ref, ovT_ref,
                     bd_ref, bp_ref, bc_ref, o_ref, sc_ref, sel_ref):
    qb = pl.program_id(2)
    qT = qT_ref[...]
    qpos = qb * Q_BLOCK + lax.broadcasted_iota(jnp.int32, (1, Q_BLOCK), 1)

    sc_ref[...] = jnp.dot(kc_ref[...], qT, preferred_element_type=F32)
    r0 = pl.multiple_of(qb * SUBLANES, SUBLANES)
    sc_ref[pl.ds(r0, NEAR_ROWS), :] += bc_ref[...]
    wide = HPG * Q_BLOCK
    rown = lax.broadcasted_iota(jnp.int32, (N_CMP_PAD, wide), 0)
    qpos4 = qb * Q_BLOCK + (lax.broadcasted_iota(jnp.int32, (1, wide), 1) & (Q_BLOCK - 1))
    vis4 = (rown >= CMP_PAD) & (CMP_STRIDE * rown <= qpos4 + (CMP_STRIDE * CMP_PAD - CMP_LEN + 1))
    s = jnp.where(vis4, sc_ref[...], NEG_INF)
    m = jnp.max(s, axis=0, keepdims=True)
    p = jnp.where(vis4, jnp.exp(s - m), 0.0)
    l = jnp.sum(p, axis=0, keepdims=True)
    inv = jnp.where(l > 0.0, 1.0 / l, 0.0)
    pn = p * inv
    o_c = jnp.dot(vcT_ref[...], pn.astype(BF16), preferred_element_type=F32)
    psum = pn[:, 0:Q_BLOCK]
    for h in range(1, HPG):
        psum = psum + pn[:, h * Q_BLOCK:(h + 1) * Q_BLOCK]
    p_hi = psum.astype(BF16)
    p_lo = (psum - p_hi.astype(F32)).astype(BF16)
    ovT = ovT_ref[...]
    imp = (jnp.dot(ovT, p_hi, preferred_element_type=F32)
           + jnp.dot(ovT, p_lo, preferred_element_type=F32))

    n_sb = imp.shape[0]
    jblk = lax.broadcasted_iota(jnp.int32, (n_sb, Q_BLOCK), 0)
    cur = qpos // SEL_BLOCK
    eligible = jblk * SEL_BLOCK <= qpos
    forced = (jblk == 0) | (jblk == cur) | (jblk == cur - 1)
    score = jnp.where(eligible, jnp.where(forced, BIG, imp), NEG_INF)
    selmask = jnp.full((n_sb, Q_BLOCK), NEG_INF, F32)
    for _ in range(min(SEL_TOPK, n_sb)):
        top = jnp.max(score, axis=0, keepdims=True)
        idx = jnp.min(jnp.where(score == top, jblk, n_sb), axis=0, keepdims=True)
        pick = jblk == idx
        selmask = jnp.where(pick & (top > -1.0), 0.0, selmask)
        score = jnp.where(pick, -3e38, score)
    sel_ref[...] = selmask

    ki = lax.broadcasted_iota(jnp.int32, (Q_BLOCK, wide), 0)
    qj = lax.broadcasted_iota(jnp.int32, (Q_BLOCK, wide), 1) & (Q_BLOCK - 1)
    causal4 = ki <= qj
    half = SEL_BLOCK

    def absent(cond):
        return jnp.where(cond, 0.0, NEG_INF)

    def sel_mask_tile(kt):
        a = jnp.broadcast_to(sel_ref[pl.ds(2 * kt, 1), :], (half, Q_BLOCK))
        b = jnp.broadcast_to(sel_ref[pl.ds(2 * kt + 1, 1), :], (half, Q_BLOCK))
        return _tile4(jnp.concatenate([a, b], axis=0))

    def scores(k_ref, kt):
        k = k_ref[pl.ds(pl.multiple_of(kt * Q_BLOCK, Q_BLOCK), Q_BLOCK), :]
        return jnp.dot(k, qT, preferred_element_type=F32)

    def first(s, vT_ref, kt):
        m = jnp.max(s, axis=0, keepdims=True)
        p = jnp.exp(s - m)
        l = jnp.sum(p, axis=0, keepdims=True)
        vT = vT_ref[:, pl.ds(pl.multiple_of(kt * Q_BLOCK, Q_BLOCK), Q_BLOCK)]
        return m, l, jnp.dot(vT, p.astype(BF16), preferred_element_type=F32)

    def update(state, s, vT_ref, kt):
        m, l, acc = state
        m_new = jnp.maximum(m, jnp.max(s, axis=0, keepdims=True))
        alpha = jnp.exp(m - m_new)
        p = jnp.exp(s - m_new)
        l = alpha * l + jnp.sum(p, axis=0, keepdims=True)
        vT = vT_ref[:, pl.ds(pl.multiple_of(kt * Q_BLOCK, Q_BLOCK), Q_BLOCK)]
        acc = alpha * acc + jnp.dot(vT, p.astype(BF16), preferred_element_type=F32)
        return m_new, l, acc

    bd = bd_ref[...]
    bp = bp_ref[...]

    s = scores(ks_ref, qb) + bd + sel_mask_tile(qb)
    st = first(jnp.where(causal4, s, NEG_INF), vsT_ref, qb)
    kt = jnp.maximum(qb - 1, 0)
    s = scores(ks_ref, kt) + bp + sel_mask_tile(kt)
    st = update(st, s + absent(qb >= 1), vsT_ref, kt)

    def far_body(kt, st):
        return update(st, scores(ks_ref, kt) + sel_mask_tile(kt), vsT_ref, kt)

    st = lax.fori_loop(0, jnp.maximum(qb - 1, 0), far_body, st)
    o_s = st[2] * (1.0 / st[1])

    s = scores(kw_ref, qb) + bd
    st = first(jnp.where(causal4, s, NEG_INF), vwT_ref, qb)
    kt = jnp.maximum(qb - 1, 0)
    s = scores(kw_ref, kt) + bp
    st = update(st, s + absent(qb >= 1), vwT_ref, kt)
    n_wt = WINDOW // Q_BLOCK
    for back in range(2, n_wt):
        kt = jnp.maximum(qb - back, 0)
        st = update(st, scores(kw_ref, kt) + absent(qb >= back), vwT_ref, kt)
    kt = jnp.maximum(qb - n_wt, 0)
    s = jnp.where(ki > qj, scores(kw_ref, kt), NEG_INF)
    st = update(st, s + absent(qb >= n_wt), vwT_ref, kt)
    o_w = st[2] * (1.0 / st[1])

    o_ref[...] = (gT_ref[0:1, :] * o_c + gT_ref[1:2, :] * o_s + gT_ref[2:3, :] * o_w).astype(o_ref.dtype)


def _nsa_attn(qT, gT, ks, vsT, kw, vwT, kc, vcT, ovT, bd, bp, bc):
    bsz, ng, nqb, _, wide = qT.shape
    seq = ks.shape[2]
    n_sb = ovT.shape[0]
    big = lambda b, g, q: (b, g, 0, 0)
    blk = lambda b, g, q: (b, g, q, 0, 0)
    grp = lambda b, g, q: (g, 0, 0)
    single = dict(pipeline_mode=pl.Buffered(1))
    return pl.pallas_call(
        _nsa_attn_kernel,
        grid=(bsz, ng, nqb),
        in_specs=[
            pl.BlockSpec((None, None, None, HEAD_DIM, wide), blk),
            pl.BlockSpec((None, None, None, SUBLANES, wide), blk),
            pl.BlockSpec((None, None, seq, HEAD_DIM), big, **single),
            pl.BlockSpec((None, None, HEAD_DIM, seq), big, **single),
            pl.BlockSpec((None, None, seq, HEAD_DIM), big, **single),
            pl.BlockSpec((None, None, HEAD_DIM, seq), big, **single),
            pl.BlockSpec((None, None, N_CMP_PAD, HEAD_DIM), big),
            pl.BlockSpec((None, None, HEAD_DIM, N_CMP_PAD), big),
            pl.BlockSpec((n_sb, N_CMP_PAD), lambda b, g, q: (0, 0)),
            pl.BlockSpec((None, Q_BLOCK, wide), grp),
            pl.BlockSpec((None, Q_BLOCK, wide), grp),
            pl.BlockSpec((None, NEAR_ROWS, wide), grp),
        ],
        out_specs=pl.BlockSpec((None, None, None, HEAD_DIM, wide), blk),
        out_shape=jax.ShapeDtypeStruct((bsz, ng, nqb, HEAD_DIM, wide), BF16),
        scratch_shapes=[pltpu.VMEM((N_CMP_PAD, wide), F32), pltpu.VMEM((n_sb, Q_BLOCK), F32)],
        compiler_params=_cparams(("arbitrary", "arbitrary", "arbitrary")),
        name="nsa_attn",
    )(qT, gT, ks, vsT, kw, vwT, kc, vcT, ovT, bd, bp, bc)


def _dn_prep_kernel(x_ref, halo_ref, w_ref, dba_ref, alog_ref, dtb_ref, o_ref, beta_ref, g_ref, buf_ref,
                    *, blocks_per_seq):
    i = pl.program_id(0)
    j = pl.program_id(1)
    t = x_ref.shape[0]
    first_of_seq = (i % blocks_per_seq) == 0
    buf_ref[0:SUBLANES, :] = jnp.where(first_of_seq, 0.0, halo_ref[...])
    buf_ref[SUBLANES:, :] = x_ref[...]
    scale = jnp.where(j == 0, DN_DIM ** -0.5, 1.0)
    for h in range(DN_HEADS):
        sl = slice(h * DN_DIM, (h + 1) * DN_DIM)
        y = w_ref[DN_CONV - 1:DN_CONV, sl] * x_ref[:, sl]
        for back in range(1, DN_CONV):
            y = y + w_ref[DN_CONV - 1 - back:DN_CONV - back, sl] * buf_ref[pl.ds(SUBLANES - back, t), sl]
        y = y * jax.nn.sigmoid(y)
        nrm = y * lax.rsqrt(jnp.sum(y * y, axis=-1, keepdims=True) + EPS) * scale
        o_ref[:, sl] = jnp.where(j < 2, nrm, y)

    @pl.when(j == 0)
    def _():
        dba = dba_ref[...]
        beta_ref[...] = jax.nn.sigmoid(dba)
        z = dba + dtb_ref[...]
        softplus = jnp.maximum(z, 0.0) + jnp.log1p(jnp.exp(-jnp.abs(z)))
        g_ref[...] = -jnp.exp(alog_ref[...]) * softplus


def _dn_prep(proj, conv_w, alog_row, dtb_row, seq, t=512):
    n = proj.shape[0]
    width = DN_HEADS * DN_DIM
    hb = t // SUBLANES
    return pl.pallas_call(
        functools.partial(_dn_prep_kernel, blocks_per_seq=seq // t),
        grid=(n // t, 3),
        in_specs=[
            pl.BlockSpec((t, width), lambda i, j: (i, j)),
            pl.BlockSpec((SUBLANES, width), lambda i, j: (jnp.maximum(i * hb - 1, 0), j)),
            pl.BlockSpec((DN_CONV, width), lambda i, j: (0, j)),
            pl.BlockSpec((t, LANES), lambda i, j: (i, OFF_DBA // LANES)),
            pl.BlockSpec((1, LANES), lambda i, j: (0, 0)),
            pl.BlockSpec((1, LANES), lambda i, j: (0, 0)),
        ],
        out_specs=[
            pl.BlockSpec((t, width), lambda i, j: (i, j)),
            pl.BlockSpec((t, LANES), lambda i, j: (i, 0)),
            pl.BlockSpec((t, LANES), lambda i, j: (i, 0)),
        ],
        out_shape=[
            jax.ShapeDtypeStruct((n, 3 * width), F32),
            jax.ShapeDtypeStruct((n, LANES), F32),
            jax.ShapeDtypeStruct((n, LANES), F32),
        ],
        scratch_shapes=[pltpu.VMEM((t + SUBLANES, width), F32)],
        compiler_params=_cparams(("arbitrary", "arbitrary")),
        name="dn_prep",
    )(proj, proj, conv_w, proj, alog_row, dtb_row)


def _dot_nt(a, b, precision=None):
    return lax.dot_general(a, b, (((1,), (1,)), ((), ())), preferred_element_type=F32, precision=precision)


def _dot_tn(a, b, precision=None):
    return lax.dot_general(a, b, (((0,), (0,)), ((), ())), preferred_element_type=F32, precision=precision)


def _deltanet_kernel(q_ref, k_ref, v_ref, beta_ref, g_ref, z_ref, nw_ref, o_ref, state_ref):
    h = pl.program_id(1)
    c = DN_CHUNK
    n_chunks = q_ref.shape[0] // c

    @pl.when(pl.program_id(2) == 0)
    def _():
        state_ref[...] = jnp.zeros_like(state_ref)

    li = lax.broadcasted_iota(jnp.int32, (LANES, LANES), 0)
    beta_all = jnp.dot(beta_ref[...], (li == h).astype(F32), preferred_element_type=F32, precision=HI)
    g_all = jnp.dot(g_ref[...], (li == h + DN_HEADS).astype(F32), preferred_element_type=F32, precision=HI)

    ri = lax.broadcasted_iota(jnp.int32, (c, c), 0)
    ci = lax.broadcasted_iota(jnp.int32, (c, c), 1)
    tri = (ri >= ci).astype(F32)
    causal = ri >= ci
    strict = ri > ci
    eye = (ri == ci).astype(F32)

    pre = []
    for n in range(n_chunks):
        rows = slice(n * c, (n + 1) * c)
        q = q_ref[rows, :]
        k = k_ref[rows, :]
        v = v_ref[rows, :]
        beta = beta_all[rows, :]
        gc = jnp.dot(tri, g_all[rows, :], preferred_element_type=F32, precision=HI)
        g_last = gc[c - 1:c, :]
        eg = jnp.exp(gc)
        kb = k * beta
        gcc = gc[:, 0:c]
        gcr = jnp.sum(gcc * eye, axis=0, keepdims=True)
        decay = jnp.exp(jnp.where(causal, gcc - gcr, NEG_INF))
        kk = _dot_nt(kb, k) * decay
        low = jnp.where(strict, kk, 0.0)
        qk = jnp.where(causal, _dot_nt(q, k) * decay, 0.0)
        inv = eye - low
        pw = low
        for _ in range(int(math.log2(c)) - 1):
            pw = jnp.dot(pw, pw, preferred_element_type=F32, precision=HI)
            inv = inv + jnp.dot(inv, pw, preferred_element_type=F32, precision=HI)
        rhs = jnp.concatenate([v * beta, kb * eg], axis=1)
        uw = jnp.dot(inv, rhs, preferred_element_type=F32, precision=HI)
        pre.append((uw[:, :DN_DIM], uw[:, DN_DIM:], qk, q * eg, k * jnp.exp(g_last - gc), jnp.exp(g_last)))

    state = state_ref[...]
    nw = nw_ref[...]
    for n in range(n_chunks):
        u, w, qk, qd, kd, gl = pre[n]
        v_new = u - jnp.dot(w, state, preferred_element_type=F32)
        o = jnp.dot(qd, state, preferred_element_type=F32) + jnp.dot(qk, v_new, preferred_element_type=F32)
        state = state * gl + _dot_tn(kd, v_new)
        rows = slice(n * c, (n + 1) * c)
        z = z_ref[rows, :]
        o_ref[rows, :] = (_rms_rows(o, nw) * (z * jax.nn.sigmoid(z))).astype(o_ref.dtype)
    state_ref[...] = state


def _deltanet(qkv, beta, g, proj, norm_w, bsz, seq, t=512):
    n = qkv.shape[0]
    spb = seq // t
    zoff = OFF_DZ // DN_DIM
    tok = lambda b, h, i: (b * spb + i, 0)
    return pl.pallas_call(
        _deltanet_kernel,
        grid=(bsz, DN_HEADS, spb),
        in_specs=[
            pl.BlockSpec((t, DN_DIM), lambda b, h, i: (b * spb + i, h)),
            pl.BlockSpec((t, DN_DIM), lambda b, h, i: (b * spb + i, DN_HEADS + h)),
            pl.BlockSpec((t, DN_DIM), lambda b, h, i: (b * spb + i, 2 * DN_HEADS + h)),
            pl.BlockSpec((t, LANES), tok),
            pl.BlockSpec((t, LANES), tok),
            pl.BlockSpec((t, DN_DIM), lambda b, h, i: (b * spb + i, zoff + h)),
            pl.BlockSpec((1, DN_DIM), lambda b, h, i: (0, 0)),
        ],
        out_specs=pl.BlockSpec((t, DN_DIM), lambda b, h, i: (b * spb + i, h)),
        out_shape=jax.ShapeDtypeStruct((n, DN_HEADS * DN_DIM), BF16),
        scratch_shapes=[pltpu.VMEM((DN_DIM, DN_DIM), F32)],
        compiler_params=_cparams(("arbitrary", "arbitrary", "arbitrary")),
        name="deltanet",
    )(qkv, qkv, qkv, beta, g, proj, norm_w)


def _regroup_w_in(w_in):
    d = w_in.shape[0]
    sizes = (1024, 256, 256, 256, 256, 256, 256, 24, 1024, 1024, 1024, 1024, 8, 8, 2048, 2048)
    offs = np.concatenate([[0], np.cumsum(sizes)])
    (nq, kc, vc, ksel, vsel, kwin, vwin, ngate, dq, dk, dv, dz, db, da, mga, mgb) = (
        w_in[:, offs[i]:offs[i + 1]] for i in range(len(sizes)))
    zeros = lambda w: jnp.zeros((d, w), w_in.dtype)
    cols = [dq, dk, dv, nq, dz, mga, mgb, kc, vc, ksel, vsel, kwin, vwin,
            ngate, zeros(LANES - 24), db, da, zeros(LANES - 16), zeros(PROJ_DIM - PROJ_USED)]
    return jnp.concatenate(cols, axis=1).astype(BF16)


def _overlap_t(n_sb, n_pieces):
    n = np.arange(N_CMP_PAD) - CMP_PAD
    j = np.arange(n_sb)
    valid = (n >= 0) & (n < n_pieces - 1)
    c0 = n * CMP_STRIDE
    s0 = j * SEL_BLOCK
    ov = (c0[None, :] < s0[:, None] + SEL_BLOCK) & (c0[None, :] + CMP_LEN > s0[:, None]) & valid[None, :]
    return jnp.asarray(ov.astype(np.float32), dtype=BF16)


def _token_mixer(h, bsz, seq, norm_mix, w_in, q_norm, k_norm, cmp_pos, cmp_w1, cmp_b1, cmp_w2, dn_conv,
                 dn_a_log, dn_dt_bias, dn_norm, w_a, w_b, w_out, bias_tiles):
    n = bsz * seq
    nqb = seq // Q_BLOCK
    npc = seq // CMP_STRIDE
    n_sb = seq // SEL_BLOCK
    assert npc + CMP_PAD <= N_CMP_PAD and nqb * SUBLANES + NEAR_ROWS <= N_CMP_PAD
    proj = _norm_matmul(h, norm_mix.reshape(1, -1), _regroup_w_in(w_in))

    k_norm8 = jnp.concatenate([k_norm, jnp.zeros((SUBLANES - 3, HEAD_DIM), F32)], axis=0)
    q_bf, kv_bf, gates = _nsa_prep(proj, q_norm.reshape(1, -1), k_norm8)
    kv = kv_bf.reshape(bsz, seq, 6, N_GROUPS, HEAD_DIM)
    by_group = lambda x: jnp.transpose(x, (0, 2, 1, 3))
    pieces = jnp.stack([by_group(kv[:, :, 0]), by_group(kv[:, :, 1])], axis=0)
    pieces = pieces.reshape(2, bsz, N_GROUPS, npc, CMP_STRIDE * HEAD_DIM)
    cmp = _compress(pieces, cmp_w1, cmp_pos.reshape(2, 1, CMP_LEN * HEAD_DIM), cmp_b1.reshape(2, 1, -1),
                    cmp_w2, k_norm8)
    cmp = jnp.pad(cmp, ((0, 0), (0, 0), (0, 0), (CMP_PAD, N_CMP_PAD - CMP_PAD - npc), (0, 0))).astype(BF16)
    kc = cmp[0]
    vcT = jnp.swapaxes(cmp[1], -1, -2)
    ks = by_group(kv[:, :, 2])
    vsT = jnp.transpose(kv[:, :, 3], (0, 2, 3, 1))
    kw = by_group(kv[:, :, 4])
    vwT = jnp.transpose(kv[:, :, 5], (0, 2, 3, 1))
    qT = jnp.transpose(q_bf.reshape(bsz, nqb, Q_BLOCK, N_GROUPS, HPG, HEAD_DIM), (0, 3, 1, 5, 4, 2))
    qT = qT.reshape(bsz, N_GROUPS, nqb, HEAD_DIM, HPG * Q_BLOCK)
    gT = jnp.transpose(gates[:, :3 * N_HEADS].reshape(bsz, nqb, Q_BLOCK, N_GROUPS, HPG, 3), (0, 3, 1, 5, 4, 2))
    gT = gT.reshape(bsz, N_GROUPS, nqb, 3, HPG * Q_BLOCK)
    gT = jnp.pad(gT, ((0, 0), (0, 0), (0, 0), (0, SUBLANES - 3), (0, 0)))
    bd, bp, bc = bias_tiles
    oT = _nsa_attn(qT, gT, ks, vsT, kw, vwT, kc, vcT, _overlap_t(n_sb, npc), bd, bp, bc)
    y_nsa = jnp.transpose(oT.reshape(bsz, N_GROUPS, nqb, HEAD_DIM, HPG, Q_BLOCK), (0, 2, 5, 1, 4, 3))
    y_nsa = y_nsa.reshape(n, N_HEADS * HEAD_DIM)

    lane_pad = lambda v, off: jnp.zeros((1, LANES), F32).at[0, off:off + DN_HEADS].set(v)
    qkv, beta, g = _dn_prep(proj, dn_conv, lane_pad(dn_a_log, DN_HEADS), lane_pad(dn_dt_bias, DN_HEADS), seq)
    y_dn = _deltanet(qkv, beta, g, proj, dn_norm.reshape(1, -1), bsz, seq)

    merged = _merge(y_nsa, y_dn, w_a.astype(BF16), w_b.astype(BF16), proj)
    return _out_proj(merged, w_out.astype(BF16), h)


def kernel(x, rel_table, norm_ffn1, ffn1_gate, ffn1_up, ffn1_down, norm_mix, w_in, q_norm, k_norm, cmp_pos, cmp_w1, cmp_b1, cmp_w2, dn_conv, dn_a_log, dn_dt_bias, dn_norm, w_branch_nsa, w_branch_dn, w_out, norm_ffn2, ffn2_gate, ffn2_up, ffn2_down):
    bsz, seq, d = x.shape
    depth = w_in.shape[0]
    h = x.reshape(bsz * seq, d)
    bias_tiles = _bias_tiles(rel_table)
    for l in range(depth):
        h = _ffn(h, norm_ffn1[l].reshape(1, -1), ffn1_gate[l].astype(BF16), ffn1_up[l].astype(BF16),
                 ffn1_down[l].astype(BF16))
        h = _token_mixer(h, bsz, seq, norm_mix[l], w_in[l], q_norm[l], k_norm[l], cmp_pos[l], cmp_w1[l],
                         cmp_b1[l], cmp_w2[l], dn_conv[l], dn_a_log[l], dn_dt_bias[l], dn_norm[l],
                         w_branch_nsa[l], w_branch_dn[l], w_out[l], bias_tiles)
        h = _ffn(h, norm_ffn2[l].reshape(1, -1), ffn2_gate[l].astype(BF16), ffn2_up[l].astype(BF16),
                 ffn2_down[l].astype(BF16))
    return h.reshape(bsz, seq, d)
```

```python
import functools
import math

import numpy as np
import jax
import jax.numpy as jnp
from jax import lax
from jax.experimental import pallas as pl
from jax.experimental.pallas import tpu as pltpu

F32 = jnp.float32
BF16 = jnp.bfloat16
HI = lax.Precision.HIGHEST

D_MODEL = 2048
D_FF = 5632
N_HEADS = 8
N_GROUPS = 2
HPG = N_HEADS // N_GROUPS
HEAD_DIM = 128
CMP_LEN = 32
CMP_STRIDE = 16
CMP_HIDDEN = 128
SEL_BLOCK = 64
SEL_TOPK = 16
WINDOW = 512
Q_BLOCK = 128
DN_HEADS = 8
DN_DIM = 128
DN_CONV = 4
DN_CHUNK = 64
DN_GROUP = 256
REL_BUCKETS = 32
REL_MAX_DIST = 128
EPS = 1e-6
NEG_INF = -1e30
BIG = 1e30

LANES = 128
SUBLANES = 8
VMEM_LIMIT = 56 * 1024 * 1024

OFF_DQ, OFF_DK, OFF_DV = 0, 1024, 2048
OFF_NQ = 3072
OFF_DZ = 4096
OFF_MGA = 5120
OFF_MGB = 7168
OFF_KV6 = 9216
OFF_NGATE = 10752
OFF_DBA = 10880
PROJ_USED = 11008
PROJ_DIM = 11264

CMP_PAD = 16
N_CMP_PAD = 1152
NEAR_ROWS = 24
FAR_TILES = 8


def _rel_bucket_thresholds():
    n = np.arange(0, 4 * REL_MAX_DIST, dtype=np.int64)
    max_exact = REL_BUCKETS // 2
    nf = np.maximum(n, max_exact).astype(np.float32)
    large = max_exact + (np.log(nf / np.float32(max_exact)) / np.float32(math.log(REL_MAX_DIST / max_exact))
                         * np.float32(REL_BUCKETS - max_exact)).astype(np.int32)
    bucket = np.where(n < max_exact, n, np.minimum(large, REL_BUCKETS - 1))
    assert np.all(np.diff(bucket) >= 0)
    return [int(np.argmax(bucket >= b)) for b in range(REL_BUCKETS)]


REL_THRESH = _rel_bucket_thresholds()
FAR_DIST = REL_THRESH[REL_BUCKETS - 1]
assert FAR_DIST <= Q_BLOCK


def _cparams(sem):
    return pltpu.CompilerParams(dimension_semantics=sem, vmem_limit_bytes=VMEM_LIMIT)


def _rms_rows(x, w_row):
    ms = jnp.mean(x * x, axis=-1, keepdims=True)
    return x * lax.rsqrt(ms + EPS) * w_row


def _ffn_kernel(x_ref, nw_ref, wg_ref, wu_ref, wd_ref, o_ref, xn_ref, acc_ref):
    j = pl.program_id(1)

    @pl.when(j == 0)
    def _():
        xn_ref[...] = _rms_rows(x_ref[...], nw_ref[...]).astype(BF16)
        acc_ref[...] = jnp.zeros_like(acc_ref)

    xn = xn_ref[...]
    g = jnp.dot(xn, wg_ref[...], preferred_element_type=F32)
    u = jnp.dot(xn, wu_ref[...], preferred_element_type=F32)
    a = (g * jax.nn.sigmoid(g) * u).astype(BF16)
    acc_ref[...] += jnp.dot(a, wd_ref[...], preferred_element_type=F32)

    @pl.when(j == pl.num_programs(1) - 1)
    def _():
        o_ref[...] = x_ref[...] + 0.5 * acc_ref[...]


def _ffn(h, nw, wg, wu, wd, tm=512, tf=512):
    n, d = h.shape
    f = wg.shape[1]
    return pl.pallas_call(
        _ffn_kernel,
        grid=(n // tm, f // tf),
        in_specs=[
            pl.BlockSpec((tm, d), lambda i, j: (i, 0)),
            pl.BlockSpec((1, d), lambda i, j: (0, 0)),
            pl.BlockSpec((d, tf), lambda i, j: (0, j)),
            pl.BlockSpec((d, tf), lambda i, j: (0, j)),
            pl.BlockSpec((tf, d), lambda i, j: (j, 0)),
        ],
        out_specs=pl.BlockSpec((tm, d), lambda i, j: (i, 0)),
        out_shape=jax.ShapeDtypeStruct((n, d), F32),
        scratch_shapes=[pltpu.VMEM((tm, d), BF16), pltpu.VMEM((tm, d), F32)],
        compiler_params=_cparams(("parallel", "arbitrary")),
        name="ffn",
    )(h, nw, wg, wu, wd)


def _norm_matmul_kernel(x_ref, nw_ref, w_ref, o_ref, xn_ref):
    @pl.when(pl.program_id(1) == 0)
    def _():
        xn_ref[...] = _rms_rows(x_ref[...], nw_ref[...]).astype(BF16)

    o_ref[...] = jnp.dot(xn_ref[...], w_ref[...], preferred_element_type=F32)


def _norm_matmul(h, nw, w, tm=512, tn=1024):
    n, d = h.shape
    nout = w.shape[1]
    return pl.pallas_call(
        _norm_matmul_kernel,
        grid=(n // tm, nout // tn),
        in_specs=[
            pl.BlockSpec((tm, d), lambda i, j: (i, 0)),
            pl.BlockSpec((1, d), lambda i, j: (0, 0)),
            pl.BlockSpec((d, tn), lambda i, j: (0, j)),
        ],
        out_specs=pl.BlockSpec((tm, tn), lambda i, j: (i, j)),
        out_shape=jax.ShapeDtypeStruct((n, nout), F32),
        scratch_shapes=[pltpu.VMEM((tm, d), BF16)],
        compiler_params=_cparams(("parallel", "arbitrary")),
        name="in_proj",
    )(h, nw, w)


def _merge_kernel(ya_ref, yb_ref, wa_ref, wb_ref, ga_ref, gb_ref, o_ref):
    a = jnp.dot(ya_ref[...], wa_ref[...], preferred_element_type=F32)
    b = jnp.dot(yb_ref[...], wb_ref[...], preferred_element_type=F32)
    o_ref[...] = (jax.nn.sigmoid(ga_ref[...]) * a + jax.nn.sigmoid(gb_ref[...]) * b).astype(o_ref.dtype)


def _merge(ya, yb, wa, wb, proj, tm=512, tn=1024):
    n, k = ya.shape
    d = wa.shape[1]
    ga0, gb0 = OFF_MGA // tn, OFF_MGB // tn
    return pl.pallas_call(
        _merge_kernel,
        grid=(n // tm, d // tn),
        in_specs=[
            pl.BlockSpec((tm, k), lambda i, j: (i, 0)),
            pl.BlockSpec((tm, k), lambda i, j: (i, 0)),
            pl.BlockSpec((k, tn), lambda i, j: (0, j)),
            pl.BlockSpec((k, tn), lambda i, j: (0, j)),
            pl.BlockSpec((tm, tn), lambda i, j: (i, ga0 + j)),
            pl.BlockSpec((tm, tn), lambda i, j: (i, gb0 + j)),
        ],
        out_specs=pl.BlockSpec((tm, tn), lambda i, j: (i, j)),
        out_shape=jax.ShapeDtypeStruct((n, d), BF16),
        compiler_params=_cparams(("parallel", "arbitrary")),
        name="merge",
    )(ya, yb, wa, wb, proj, proj)


def _out_proj_kernel(m_ref, w_ref, h_ref, o_ref):
    o_ref[...] = h_ref[...] + jnp.dot(m_ref[...], w_ref[...], preferred_element_type=F32)


def _out_proj(merged, w, h, tm=512, tn=1024):
    n, k = merged.shape
    d = w.shape[1]
    return pl.pallas_call(
        _out_proj_kernel,
        grid=(n // tm, d // tn),
        in_specs=[
            pl.BlockSpec((tm, k), lambda i, j: (i, 0)),
            pl.BlockSpec((k, tn), lambda i, j: (0, j)),
            pl.BlockSpec((tm, tn), lambda i, j: (i, j)),
        ],
        out_specs=pl.BlockSpec((tm, tn), lambda i, j: (i, j)),
        out_shape=jax.ShapeDtypeStruct((n, d), F32),
        compiler_params=_cparams(("parallel", "arbitrary")),
        name="out_proj",
    )(merged, w, h)


def _nsa_prep_kernel(nq_ref, kv_ref, ng_ref, qn_ref, kn_ref, q_out, kv_out, g_out):
    scale = HEAD_DIM ** -0.5
    for h in range(N_HEADS):
        sl = slice(h * HEAD_DIM, (h + 1) * HEAD_DIM)
        q_out[:, sl] = (_rms_rows(nq_ref[:, sl], qn_ref[...]) * scale).astype(BF16)
    for c in range(6 * N_GROUPS):
        sl = slice(c * HEAD_DIM, (c + 1) * HEAD_DIM)
        kind = c // N_GROUPS
        x = kv_ref[:, sl]
        if kind == 2:
            x = _rms_rows(x, kn_ref[1:2, :])
        elif kind == 4:
            x = _rms_rows(x, kn_ref[2:3, :])
        kv_out[:, sl] = x.astype(BF16)
    g_out[...] = jax.nn.sigmoid(ng_ref[...])


def _nsa_prep(proj, q_norm, k_norm, tm=512):
    n = proj.shape[0]
    kvw = 6 * N_GROUPS * HEAD_DIM
    return pl.pallas_call(
        _nsa_prep_kernel,
        grid=(n // tm,),
        in_specs=[
            pl.BlockSpec((tm, 1024), lambda i: (i, OFF_NQ // 1024)),
            pl.BlockSpec((tm, kvw), lambda i: (i, OFF_KV6 // kvw)),
            pl.BlockSpec((tm, LANES), lambda i: (i, OFF_NGATE // LANES)),
            pl.BlockSpec((1, HEAD_DIM), lambda i: (0, 0)),
            pl.BlockSpec((SUBLANES, HEAD_DIM), lambda i: (0, 0)),
        ],
        out_specs=[
            pl.BlockSpec((tm, 1024), lambda i: (i, 0)),
            pl.BlockSpec((tm, kvw), lambda i: (i, 0)),
            pl.BlockSpec((tm, LANES), lambda i: (i, 0)),
        ],
        out_shape=[
            jax.ShapeDtypeStruct((n, 1024), BF16),
            jax.ShapeDtypeStruct((n, kvw), BF16),
            jax.ShapeDtypeStruct((n, LANES), F32),
        ],
        compiler_params=_cparams(("parallel",)),
        name="nsa_prep",
    )(proj, proj, proj, q_norm, k_norm)


def _compress_kernel(p_ref, w1_ref, pos_ref, b1_ref, w2_ref, kn_ref, o_ref):
    half = CMP_STRIDE * HEAD_DIM
    p = p_ref[...]
    w1 = w1_ref[...]
    a = jnp.dot(p, w1[:half].astype(BF16), preferred_element_type=F32)
    b = jnp.dot(p, w1[half:].astype(BF16), preferred_element_type=F32)
    pos = jnp.broadcast_to(pos_ref[...], (SUBLANES, 2 * half))
    c0 = jnp.dot(pos, w1, preferred_element_type=F32, precision=HI)[0:1, :] + b1_ref[...]
    n = p.shape[0]
    hid = a + pltpu.roll(b, n - 1, 0) + c0
    hid = hid * jax.nn.sigmoid(hid)
    out = jnp.dot(hid.astype(BF16), w2_ref[...].astype(BF16), preferred_element_type=F32)
    normed = _rms_rows(out, kn_ref[0:1, :])
    out = jnp.where(pl.program_id(0) == 0, normed, out)
    row = lax.broadcasted_iota(jnp.int32, out.shape, 0)
    o_ref[...] = jnp.where(row < n - 1, out, 0.0)


def _compress(pieces, w1, pos, b1, w2, k_norm):
    _, bsz, ng, npc, width = pieces.shape
    return pl.pallas_call(
        _compress_kernel,
        grid=(2, bsz, ng),
        in_specs=[
            pl.BlockSpec((None, None, None, npc, width), lambda c, b, g: (c, b, g, 0, 0)),
            pl.BlockSpec((None, 2 * width, CMP_HIDDEN), lambda c, b, g: (c, 0, 0)),
            pl.BlockSpec((None, 1, 2 * width), lambda c, b, g: (c, 0, 0)),
            pl.BlockSpec((None, 1, CMP_HIDDEN), lambda c, b, g: (c, 0, 0)),
            pl.BlockSpec((None, CMP_HIDDEN, HEAD_DIM), lambda c, b, g: (c, 0, 0)),
            pl.BlockSpec((SUBLANES, HEAD_DIM), lambda c, b, g: (0, 0)),
        ],
        out_specs=pl.BlockSpec((None, None, None, npc, HEAD_DIM), lambda c, b, g: (c, b, g, 0, 0)),
        out_shape=jax.ShapeDtypeStruct((2, bsz, ng, npc, HEAD_DIM), F32),
        compiler_params=_cparams(("arbitrary", "arbitrary", "arbitrary")),
        name="compress",
    )(pieces, w1, pos, b1, w2, k_norm)


def _bias_kernel(tab_ref, bd_ref, bp_ref, bc_ref):
    g = pl.program_id(0)

    def lookup(dist, head):
        v = jnp.full(dist.shape, tab_ref[0, head], F32)
        for b in range(1, REL_BUCKETS):
            v = jnp.where(dist >= REL_THRESH[b], tab_ref[b, head], v)
        return v - tab_ref[REL_BUCKETS - 1, head]

    ki = lax.broadcasted_iota(jnp.int32, (Q_BLOCK, Q_BLOCK), 0)
    qj = lax.broadcasted_iota(jnp.int32, (Q_BLOCK, Q_BLOCK), 1)
    r = lax.broadcasted_iota(jnp.int32, (NEAR_ROWS, Q_BLOCK), 0)
    qc = lax.broadcasted_iota(jnp.int32, (NEAR_ROWS, Q_BLOCK), 1)
    dist_c = qc - CMP_STRIDE * (r - CMP_PAD) - (CMP_LEN - 1)
    for h in range(HPG):
        head = g * HPG + h
        sl = slice(h * Q_BLOCK, (h + 1) * Q_BLOCK)
        bd_ref[:, sl] = lookup(qj - ki, head)
        bp_ref[:, sl] = lookup(qj - ki + Q_BLOCK, head)
        bc_ref[:, sl] = lookup(dist_c, head)


def _bias_tiles(rel_table):
    wide = HPG * Q_BLOCK
    return pl.pallas_call(
        _bias_kernel,
        grid=(N_GROUPS,),
        in_specs=[pl.BlockSpec(memory_space=pltpu.SMEM)],
        out_specs=[
            pl.BlockSpec((None, Q_BLOCK, wide), lambda g: (g, 0, 0)),
            pl.BlockSpec((None, Q_BLOCK, wide), lambda g: (g, 0, 0)),
            pl.BlockSpec((None, NEAR_ROWS, wide), lambda g: (g, 0, 0)),
        ],
        out_shape=[
            jax.ShapeDtypeStruct((N_GROUPS, Q_BLOCK, wide), F32),
            jax.ShapeDtypeStruct((N_GROUPS, Q_BLOCK, wide), F32),
            jax.ShapeDtypeStruct((N_GROUPS, NEAR_ROWS, wide), F32),
        ],
        compiler_params=_cparams(("arbitrary",)),
        name="rel_bias",
    )(rel_table)


def _tile4(x):
    return jnp.concatenate([x] * HPG, axis=1)


def _nsa_attn_kernel(qT_ref, gT_ref, ks_ref, vsT_ref, kw_ref, vwT_ref, kc_ref, vcT_ref, ovT_ref,
                     bd_ref, bp_ref, bc_ref, o_ref, sc_ref, sel_ref, far_ref):
    qb = pl.program_id(2)
    qT = qT_ref[...]
    qpos = qb * Q_BLOCK + lax.broadcasted_iota(jnp.int32, (1, Q_BLOCK), 1)

    sc_ref[...] = jnp.dot(kc_ref[...], qT, preferred_element_type=F32)
    r0 = pl.multiple_of(qb * SUBLANES, SUBLANES)
    sc_ref[pl.ds(r0, NEAR_ROWS), :] += bc_ref[...]
    wide = HPG * Q_BLOCK
    rown = lax.broadcasted_iota(jnp.int32, (N_CMP_PAD, wide), 0)
    qpos4 = qb * Q_BLOCK + (lax.broadcasted_iota(jnp.int32, (1, wide), 1) & (Q_BLOCK - 1))
    vis4 = (rown >= CMP_PAD) & (CMP_STRIDE * rown <= qpos4 + (CMP_STRIDE * CMP_PAD - CMP_LEN + 1))
    s = jnp.where(vis4, sc_ref[...], NEG_INF)
    m = jnp.max(s, axis=0, keepdims=True)
    p = jnp.where(vis4, jnp.exp(s - m), 0.0)
    l = jnp.sum(p, axis=0, keepdims=True)
    inv = jnp.where(l > 0.0, 1.0 / l, 0.0)
    pn = p * inv
    o_c = jnp.dot(vcT_ref[...], pn.astype(BF16), preferred_element_type=F32)
    psum = pn[:, 0:Q_BLOCK]
    for h in range(1, HPG):
        psum = psum + pn[:, h * Q_BLOCK:(h + 1) * Q_BLOCK]
    p_hi = psum.astype(BF16)
    p_lo = (psum - p_hi.astype(F32)).astype(BF16)
    ovT = ovT_ref[...]
    imp = (jnp.dot(ovT, p_hi, preferred_element_type=F32)
           + jnp.dot(ovT, p_lo, preferred_element_type=F32))

    n_sb = imp.shape[0]
    jblk = lax.broadcasted_iota(jnp.int32, (n_sb, Q_BLOCK), 0)
    cur = qpos // SEL_BLOCK
    eligible = jblk * SEL_BLOCK <= qpos
    forced = (jblk == 0) | (jblk == cur) | (jblk == cur - 1)
    score = jnp.where(eligible, jnp.where(forced, BIG, imp), NEG_INF)
    selmask = jnp.full((n_sb, Q_BLOCK), NEG_INF, F32)
    for _ in range(min(SEL_TOPK, n_sb)):
        top = jnp.max(score, axis=0, keepdims=True)
        idx = jnp.min(jnp.where(score == top, jblk, n_sb), axis=0, keepdims=True)
        pick = jblk == idx
        selmask = jnp.where(pick & (top > -1.0), 0.0, selmask)
        score = jnp.where(pick, -3e38, score)
    sel_ref[...] = selmask
    far_ref[...] = jnp.where(jblk < 2 * (qb - 1), selmask, NEG_INF)

    ki =lax.broadcasted_iota(jnp.int32, (Q_BLOCK, wide), 0)
    qj = lax.broadcasted_iota(jnp.int32, (Q_BLOCK, wide), 1) & (Q_BLOCK - 1)
    causal4 = ki <= qj
    half = SEL_BLOCK

    def absent(cond):
        return jnp.where(cond, 0.0, NEG_INF)

    def sel_mask_tile(kt):
        a = jnp.broadcast_to(sel_ref[pl.ds(2 * kt, 1), :], (half, Q_BLOCK))
        b = jnp.broadcast_to(sel_ref[pl.ds(2 * kt + 1, 1), :], (half, Q_BLOCK))
        return _tile4(jnp.concatenate([a, b], axis=0))

    def scores(k_ref, kt, width=Q_BLOCK):
        k = k_ref[pl.ds(pl.multiple_of(kt * width, width), width), :]
        return jnp.dot(k, qT, preferred_element_type=F32)

    def first(s, vT_ref, kt):
        m = jnp.max(s, axis=0, keepdims=True)
        p = jnp.exp(s - m)
        l = jnp.sum(p, axis=0, keepdims=True)
        vT = vT_ref[:, pl.ds(pl.multiple_of(kt * Q_BLOCK, Q_BLOCK), Q_BLOCK)]
        return m, l, jnp.dot(vT, p.astype(BF16), preferred_element_type=F32)

    def update(state, s, vT_ref, kt, width=Q_BLOCK):
        m, l, acc = state
        m_new = jnp.maximum(m, jnp.max(s, axis=0, keepdims=True))
        alpha = jnp.exp(m - m_new)
        p = jnp.exp(s - m_new)
        l = alpha * l + jnp.sum(p, axis=0, keepdims=True)
        vT = vT_ref[:, pl.ds(pl.multiple_of(kt * width, width), width)]
        acc = alpha * acc + jnp.dot(vT, p.astype(BF16), preferred_element_type=F32)
        return m_new, l, acc

    bd = bd_ref[...]
    bp = bp_ref[...]

    s = scores(ks_ref, qb) + bd + sel_mask_tile(qb)
    st = first(jnp.where(causal4, s, NEG_INF), vsT_ref, qb)
    kt = jnp.maximum(qb - 1, 0)
    s = scores(ks_ref, kt) + bp + sel_mask_tile(kt)
    st = update(st, s + absent(qb >= 1), vsT_ref, kt)

    far_w = FAR_TILES * Q_BLOCK
    blocks_per_iter = far_w // SEL_BLOCK

    def far_body(c, st):
        rows = [jnp.broadcast_to(far_ref[pl.ds(blocks_per_iter * c + u, 1), :], (half, Q_BLOCK))
                for u in range(blocks_per_iter)]
        mask = _tile4(jnp.concatenate(rows, axis=0))
        return update(st, scores(ks_ref, c, far_w) + mask, vsT_ref, c, far_w)

    n_far = (jnp.maximum(qb - 1, 0) + FAR_TILES - 1) // FAR_TILES
    st = lax.fori_loop(0, n_far, far_body, st)
    o_s = st[2] * (1.0 / st[1])

    s = scores(kw_ref, qb) + bd
    st = first(jnp.where(causal4, s, NEG_INF), vwT_ref, qb)
    kt = jnp.maximum(qb - 1, 0)
    s = scores(kw_ref, kt) + bp
    st = update(st, s + absent(qb >= 1), vwT_ref, kt)
    n_wt = WINDOW // Q_BLOCK
    for back in range(2, n_wt):
        kt = jnp.maximum(qb - back, 0)
        st = update(st, scores(kw_ref, kt) + absent(qb >= back), vwT_ref, kt)
    kt = jnp.maximum(qb - n_wt, 0)
    s = jnp.where(ki > qj, scores(kw_ref, kt), NEG_INF)
    st = update(st, s + absent(qb >= n_wt), vwT_ref, kt)
    o_w = st[2] * (1.0 / st[1])

    o_ref[...] = (gT_ref[0:1, :] * o_c + gT_ref[1:2, :] * o_s + gT_ref[2:3, :] * o_w).astype(o_ref.dtype)


def _nsa_attn(qT, gT, ks, vsT, kw, vwT, kc, vcT, ovT, bd, bp, bc):
    bsz, ng, nqb, _, wide = qT.shape
    seq = ks.shape[2]
    n_sb = ovT.shape[0]
    big = lambda b, g, q: (b, g, 0, 0)
    blk = lambda b, g, q: (b, g, q, 0, 0)
    grp = lambda b, g, q: (g, 0, 0)
    single = dict(pipeline_mode=pl.Buffered(1))
    return pl.pallas_call(
        _nsa_attn_kernel,
        grid=(bsz, ng, nqb),
        in_specs=[
            pl.BlockSpec((None, None, None, HEAD_DIM, wide), blk),
            pl.BlockSpec((None, None, None, SUBLANES, wide), blk),
            pl.BlockSpec((None, None, seq, HEAD_DIM), big, **single),
            pl.BlockSpec((None, None, HEAD_DIM, seq), big, **single),
            pl.BlockSpec((None, None, seq, HEAD_DIM), big, **single),
            pl.BlockSpec((None, None, HEAD_DIM, seq), big, **single),
            pl.BlockSpec((None, None, N_CMP_PAD, HEAD_DIM), big),
            pl.BlockSpec((None, None, HEAD_DIM, N_CMP_PAD), big),
            pl.BlockSpec((n_sb, N_CMP_PAD), lambda b, g, q: (0, 0)),
            pl.BlockSpec((None, Q_BLOCK, wide), grp),
            pl.BlockSpec((None, Q_BLOCK, wide), grp),
            pl.BlockSpec((None, NEAR_ROWS, wide), grp),
        ],
        out_specs=pl.BlockSpec((None, None, None, HEAD_DIM, wide), blk),
        out_shape=jax.ShapeDtypeStruct((bsz, ng, nqb, HEAD_DIM, wide), BF16),
        scratch_shapes=[pltpu.VMEM((N_CMP_PAD, wide), F32), pltpu.VMEM((n_sb, Q_BLOCK), F32),
                        pltpu.VMEM((n_sb, Q_BLOCK), F32)],
        compiler_params=_cparams(("arbitrary", "arbitrary", "arbitrary")),
        name="nsa_attn",
    )(qT, gT, ks, vsT, kw, vwT, kc, vcT, ovT, bd, bp, bc)


def _dn_prep_kernel(x_ref, halo_ref, w_ref, dba_ref, alog_ref, dtb_ref, o_ref, beta_ref, g_ref, buf_ref,
                    *, blocks_per_seq):
    i = pl.program_id(0)
    j = pl.program_id(1)
    t = x_ref.shape[0]
    first_of_seq = (i % blocks_per_seq) == 0
    buf_ref[0:SUBLANES, :] = jnp.where(first_of_seq, 0.0, halo_ref[...])
    buf_ref[SUBLANES:, :] = x_ref[...]
    scale = jnp.where(j == 0, DN_DIM ** -0.5, 1.0)
    for h in range(DN_HEADS):
        sl = slice(h * DN_DIM, (h + 1) * DN_DIM)
        y = w_ref[DN_CONV - 1:DN_CONV, sl] * x_ref[:, sl]
        for back in range(1, DN_CONV):
            y = y + w_ref[DN_CONV - 1 - back:DN_CONV - back, sl] * buf_ref[pl.ds(SUBLANES - back, t), sl]
        y = y * jax.nn.sigmoid(y)
        nrm = y * lax.rsqrt(jnp.sum(y * y, axis=-1, keepdims=True) + EPS) * scale
        o_ref[:, sl] = jnp.where(j < 2, nrm, y)

    @pl.when(j == 0)
    def _():
        dba = dba_ref[...]
        beta = jax.nn.sigmoid(dba)
        z = dba + dtb_ref[...]
        softplus = jnp.maximum(z, 0.0) + jnp.log1p(jnp.exp(-jnp.abs(z)))
        g = -jnp.exp(alog_ref[...]) * softplus
        rt = lax.broadcasted_iota(jnp.int32, (t, t), 0)
        ct = lax.broadcasted_iota(jnp.int32, (t, t), 1)
        blocktri = jnp.where((rt >= ct) & (rt // DN_CHUNK == ct // DN_CHUNK), 1.0, 0.0)
        g = jnp.dot(blocktri, g, preferred_element_type=F32, precision=HI)
        wide = DN_HEADS * DN_DIM
        src = lax.broadcasted_iota(jnp.int32, (LANES, wide), 0)
        head = lax.broadcasted_iota(jnp.int32, (LANES, wide), 1) // DN_DIM
        beta_ref[...] = jnp.dot(beta, jnp.where(src == head, 1.0, 0.0), preferred_element_type=F32, precision=HI)
        g_ref[...] = jnp.dot(g, jnp.where(src == head + DN_HEADS, 1.0, 0.0), preferred_element_type=F32,
                             precision=HI)


def _dn_prep(proj, conv_w, alog_row, dtb_row, seq, t=512):
    n = proj.shape[0]
    width = DN_HEADS * DN_DIM
    hb = t // SUBLANES
    return pl.pallas_call(
        functools.partial(_dn_prep_kernel, blocks_per_seq=seq // t),
        grid=(n // t, 3),
        in_specs=[
            pl.BlockSpec((t, width), lambda i, j: (i, j)),
            pl.BlockSpec((SUBLANES, width), lambda i, j: (jnp.maximum(i * hb - 1, 0), j)),
            pl.BlockSpec((DN_CONV, width), lambda i, j: (0, j)),
            pl.BlockSpec((t, LANES), lambda i, j: (i, OFF_DBA // LANES)),
            pl.BlockSpec((1, LANES), lambda i, j: (0, 0)),
            pl.BlockSpec((1, LANES), lambda i, j: (0, 0)),
        ],
        out_specs=[
            pl.BlockSpec((t, width), lambda i, j: (i, j)),
            pl.BlockSpec((t, width), lambda i, j: (i, 0)),
            pl.BlockSpec((t, width), lambda i, j: (i, 0)),
        ],
        out_shape=[
            jax.ShapeDtypeStruct((n, 3 * width), F32),
            jax.ShapeDtypeStruct((n, width), F32),
            jax.ShapeDtypeStruct((n, width), F32),
        ],
        scratch_shapes=[pltpu.VMEM((t + SUBLANES, width), F32)],
        compiler_params=_cparams(("arbitrary", "arbitrary")),
        name="dn_prep",
    )(proj, proj, conv_w, proj, alog_row, dtb_row)


def _dot_nt(a, b, precision=None):
    return lax.dot_general(a, b, (((1,), (1,)), ((), ())), preferred_element_type=F32, precision=precision)


def _dot_tn(a, b, precision=None):
    return lax.dot_general(a, b, (((0,), (0,)), ((), ())), preferred_element_type=F32, precision=precision)


DN_MM_MODE = "bf16"


def _mm_dn(a, b):
    if DN_MM_MODE == "hi":
        return jnp.dot(a, b, preferred_element_type=F32, precision=HI)
    a_hi = a.astype(BF16)
    b_hi = b.astype(BF16)
    out = jnp.dot(a_hi, b_hi, preferred_element_type=F32)
    if DN_MM_MODE == "x3":
        a_lo = (a - a_hi.astype(F32)).astype(BF16)
        b_lo = (b - b_hi.astype(F32)).astype(BF16)
        out = out + jnp.dot(a_lo, b_hi, preferred_element_type=F32) + jnp.dot(a_hi, b_lo, preferred_element_type=F32)
    return out


def _deltanet_kernel(q_ref, k_ref, v_ref, beta_ref, g_ref, z_ref, nw_ref, o_ref, state_ref):
    c = DN_CHUNK
    t = q_ref.shape[0]
    n_chunks = t // c

    @pl.when(pl.program_id(2) == 0)
    def _():
        state_ref[...] = jnp.zeros_like(state_ref)

    gc_all = g_ref[...]
    beta_all = beta_ref[...]

    gsz = DN_GROUP
    cpg = gsz // c
    ri = lax.broadcasted_iota(jnp.int32, (gsz, gsz), 0)
    ci = lax.broadcasted_iota(jnp.int32, (gsz, gsz), 1)
    same = (ri // c) == (ci // c)
    causal = (ri >= ci) & same
    strict = (ri > ci) & same
    eye = jnp.where(ri == ci, 1.0, 0.0)
    chunk_of_col = lax.broadcasted_iota(jnp.int32, (DN_DIM, gsz), 1) // c

    groups = range(t // gsz)
    rows = [slice(gi * gsz, (gi + 1) * gsz) for gi in groups]
    q = [q_ref[r, :] for r in rows]
    k = [k_ref[r, :] for r in rows]
    gc = [gc_all[r, :] for r in rows]
    eg = [jnp.exp(x) for x in gc]
    kb = [k[gi] * beta_all[rows[gi], :] for gi in groups]
    rhs = [jnp.concatenate([kb[gi] * eg[gi], v_ref[rows[gi], :] * beta_all[rows[gi], :]], axis=1) for gi in groups]
    decay = []
    for gi in groups:
        gc2 = jnp.concatenate([gc[gi]] * (gsz // DN_DIM), axis=1)
        gcr = jnp.sum(gc2 * eye, axis=0, keepdims=True)
        decay.append(jnp.exp(jnp.where(causal, gc2 - gcr, NEG_INF)))
    a2 = [_dot_nt(jnp.concatenate([kb[gi], q[gi]], axis=0).astype(BF16), k[gi].astype(BF16)) for gi in groups]
    low = [jnp.where(strict, a2[gi][:gsz] * decay[gi], 0.0) for gi in groups]
    qk = [a2[gi][gsz:] * decay[gi] for gi in groups]
    inv = [eye - x for x in low]
    pw = [_mm_dn(x, x) for x in low]
    for _ in range(int(math.log2(c)) - 2):
        r = [_mm_dn(jnp.concatenate([pw[gi], inv[gi]], axis=0), pw[gi]) for gi in groups]
        inv = [inv[gi] + r[gi][gsz:] for gi in groups]
        pw = [r[gi][:gsz] for gi in groups]
    inv = [inv[gi] + _mm_dn(inv[gi], pw[gi]) for gi in groups]
    wu = [_mm_dn(inv[gi], rhs[gi]) for gi in groups]
    lasts, res = [], []
    for gi in groups:
        last = [gc[gi][(n + 1) * c - 1:(n + 1) * c, :] for n in range(cpg)]
        lasts.append(last)
        g_last_rows = jnp.concatenate([jnp.broadcast_to(x, (c, DN_DIM)) for x in last], axis=0)
        kd_t = jnp.transpose(k[gi] * jnp.exp(g_last_rows - gc[gi]))
        xs = [jnp.where(chunk_of_col == n, kd_t, 0.0) for n in range(cpg)]
        xs += [qk[gi][n * c:(n + 1) * c, :] for n in range(cpg)]
        res.append(jnp.dot(jnp.concatenate(xs, axis=0).astype(BF16), wu[gi].astype(BF16),
                           preferred_element_type=F32))
    pre = []
    for gi in groups:
        qd = q[gi] * eg[gi]
        for n in range(cpg):
            kw = res[gi][n * DN_DIM:(n + 1) * DN_DIM, :DN_DIM]
            ku = res[gi][n * DN_DIM:(n + 1) * DN_DIM, DN_DIM:]
            base = cpg * DN_DIM + n * c
            coef = qd[n * c:(n + 1) * c, :] - res[gi][base:base + c, :DN_DIM]
            qku = res[gi][base:base + c, DN_DIM:]
            lhs = jnp.concatenate([-kw, coef], axis=0).astype(BF16)
            pre.append((lhs, ku, qku, jnp.exp(lasts[gi][n])))

    state = state_ref[...]
    nw = nw_ref[...]
    for n in range(n_chunks):
        lhs, ku, qku, gl = pre[n]
        r = jnp.dot(lhs, state.astype(BF16), preferred_element_type=F32)
        o = r[DN_DIM:] + qku
        state = state * gl + r[:DN_DIM] + ku
        rows = slice(n * c, (n + 1) * c)
        z = z_ref[rows, :]
        o_ref[rows, :] = (_rms_rows(o, nw) * (z * jax.nn.sigmoid(z))).astype(o_ref.dtype)
    state_ref[...] = state


def _deltanet(qkv, beta, g, proj, norm_w, bsz, seq, t=512):
    n = qkv.shape[0]
    spb = seq // t
    zoff = OFF_DZ // DN_DIM
    tok = lambda b, h, i: (b * spb + i, h)
    return pl.pallas_call(
        _deltanet_kernel,
        grid=(bsz, DN_HEADS, spb),
        in_specs=[
            pl.BlockSpec((t, DN_DIM), lambda b, h, i: (b * spb + i, h)),
            pl.BlockSpec((t, DN_DIM), lambda b, h, i: (b * spb + i, DN_HEADS + h)),
            pl.BlockSpec((t, DN_DIM), lambda b, h, i: (b * spb + i, 2 * DN_HEADS + h)),
            pl.BlockSpec((t, LANES), tok),
            pl.BlockSpec((t, LANES), tok),
            pl.BlockSpec((t, DN_DIM), lambda b, h, i: (b * spb + i, zoff + h)),
            pl.BlockSpec((1, DN_DIM), lambda b, h, i: (0, 0)),
        ],
        out_specs=pl.BlockSpec((t, DN_DIM), lambda b, h, i: (b * spb + i, h)),
        out_shape=jax.ShapeDtypeStruct((n, DN_HEADS * DN_DIM), BF16),
        scratch_shapes=[pltpu.VMEM((DN_DIM, DN_DIM), F32)],
        compiler_params=_cparams(("arbitrary", "arbitrary", "arbitrary")),
        name="deltanet",
    )(qkv, qkv, qkv, beta, g, proj, norm_w)


def _regroup_w_in(w_in):
    d = w_in.shape[0]
    sizes = (1024, 256, 256, 256, 256, 256, 256, 24, 1024, 1024, 1024, 1024, 8, 8, 2048, 2048)
    offs = np.concatenate([[0], np.cumsum(sizes)])
    (nq, kc, vc, ksel, vsel, kwin, vwin, ngate, dq, dk, dv, dz, db, da, mga, mgb) = (
        w_in[:, offs[i]:offs[i + 1]] for i in range(len(sizes)))
    zeros = lambda w: jnp.zeros((d, w), w_in.dtype)
    cols = [dq, dk, dv, nq, dz, mga, mgb, kc, vc, ksel, vsel, kwin, vwin,
            ngate, zeros(LANES - 24), db, da, zeros(LANES - 16), zeros(PROJ_DIM - PROJ_USED)]
    return jnp.concatenate(cols, axis=1).astype(BF16)


def _overlap_t(n_sb, n_pieces):
    n = np.arange(N_CMP_PAD) - CMP_PAD
    j = np.arange(n_sb)
    valid = (n >= 0) & (n < n_pieces - 1)
    c0 = n * CMP_STRIDE
    s0 = j * SEL_BLOCK
    ov = (c0[None, :] < s0[:, None] + SEL_BLOCK) & (c0[None, :] + CMP_LEN > s0[:, None]) & valid[None, :]
    return jnp.asarray(ov.astype(np.float32), dtype=BF16)


def _token_mixer(h, bsz, seq, norm_mix, w_in, q_norm, k_norm, cmp_pos, cmp_w1, cmp_b1, cmp_w2, dn_conv,
                 dn_a_log, dn_dt_bias, dn_norm, w_a, w_b, w_out, bias_tiles):
    n = bsz * seq
    nqb = seq // Q_BLOCK
    npc = seq // CMP_STRIDE
    n_sb = seq // SEL_BLOCK
    assert npc + CMP_PAD <= N_CMP_PAD and nqb * SUBLANES + NEAR_ROWS <= N_CMP_PAD
    assert nqb % FAR_TILES == 0
    proj = _norm_matmul(h, norm_mix.reshape(1, -1), _regroup_w_in(w_in))

    k_norm8 = jnp.concatenate([k_norm, jnp.zeros((SUBLANES - 3, HEAD_DIM), F32)], axis=0)
    q_bf, kv_bf, gates = _nsa_prep(proj, q_norm.reshape(1, -1), k_norm8)
    kv = kv_bf.reshape(bsz, seq, 6, N_GROUPS, HEAD_DIM)
    by_group = lambda x: jnp.transpose(x, (0, 2, 1, 3))
    pieces = jnp.stack([by_group(kv[:, :, 0]), by_group(kv[:, :, 1])], axis=0)
    pieces = pieces.reshape(2, bsz, N_GROUPS, npc, CMP_STRIDE * HEAD_DIM)
    cmp = _compress(pieces, cmp_w1, cmp_pos.reshape(2, 1, CMP_LEN * HEAD_DIM), cmp_b1.reshape(2, 1, -1),
                    cmp_w2, k_norm8)
    cmp = jnp.pad(cmp, ((0, 0), (0, 0), (0, 0), (CMP_PAD, N_CMP_PAD - CMP_PAD - npc), (0, 0))).astype(BF16)
    kc = cmp[0]
    vcT = jnp.swapaxes(cmp[1], -1, -2)
    ks = by_group(kv[:, :, 2])
    vsT = jnp.transpose(kv[:, :, 3], (0, 2, 3, 1))
    kw = by_group(kv[:, :, 4])
    vwT = jnp.transpose(kv[:, :, 5], (0, 2, 3, 1))
    qT = jnp.transpose(q_bf.reshape(bsz, nqb, Q_BLOCK, N_GROUPS, HPG, HEAD_DIM), (0, 3, 1, 5, 4, 2))
    qT = qT.reshape(bsz, N_GROUPS, nqb, HEAD_DIM, HPG * Q_BLOCK)
    gT = jnp.transpose(gates[:, :3 * N_HEADS].reshape(bsz, nqb, Q_BLOCK, N_GROUPS, HPG, 3), (0, 3, 1, 5, 4, 2))
    gT = gT.reshape(bsz, N_GROUPS, nqb, 3, HPG * Q_BLOCK)
    gT = jnp.pad(gT, ((0, 0), (0, 0), (0, 0), (0, SUBLANES - 3), (0, 0)))
    bd, bp, bc = bias_tiles
    oT = _nsa_attn(qT, gT, ks, vsT, kw, vwT, kc, vcT, _overlap_t(n_sb, npc), bd, bp, bc)
    y_nsa = jnp.transpose(oT.reshape(bsz, N_GROUPS, nqb, HEAD_DIM, HPG, Q_BLOCK), (0, 2, 5, 1, 4, 3))
    y_nsa = y_nsa.reshape(n, N_HEADS * HEAD_DIM)

    lane_pad = lambda v, off: jnp.zeros((1, LANES), F32).at[0, off:off + DN_HEADS].set(v)
    qkv, beta, g = _dn_prep(proj, dn_conv, lane_pad(dn_a_log, DN_HEADS), lane_pad(dn_dt_bias, DN_HEADS), seq)
    y_dn = _deltanet(qkv, beta, g, proj, dn_norm.reshape(1, -1), bsz, seq)

    merged = _merge(y_nsa, y_dn, w_a.astype(BF16), w_b.astype(BF16), proj)
    return _out_proj(merged, w_out.astype(BF16), h)


def kernel(x, rel_table, norm_ffn1, ffn1_gate, ffn1_up, ffn1_down, norm_mix, w_in, q_norm, k_norm, cmp_pos, cmp_w1, cmp_b1, cmp_w2, dn_conv, dn_a_log, dn_dt_bias, dn_norm, w_branch_nsa, w_branch_dn, w_out, norm_ffn2, ffn2_gate, ffn2_up, ffn2_down):
    bsz, seq, d = x.shape
    depth = w_in.shape[0]
    h = x.reshape(bsz * seq, d)
    bias_tiles = _bias_tiles(rel_table)
    for l in range(depth):
        h = _ffn(h, norm_ffn1[l].reshape(1, -1), ffn1_gate[l].astype(BF16), ffn1_up[l].astype(BF16),
                 ffn1_down[l].astype(BF16))
        h = _token_mixer(h, bsz, seq, norm_mix[l], w_in[l], q_norm[l], k_norm[l], cmp_pos[l], cmp_w1[l],
                         cmp_b1[l], cmp_w2[l], dn_conv[l], dn_a_log[l], dn_dt_bias[l], dn_norm[l],
                         w_branch_nsa[l], w_branch_dn[l], w_out[l], bias_tiles)
        h = _ffn(h, norm_ffn2[l].reshape(1, -1), ffn2_gate[l].astype(BF16), ffn2_up[l].astype(BF16),
                 ffn2_down[l].astype(BF16))
    return h.reshape(bsz, seq, d)
```

```python
import functools
import math

import numpy as np
import jax
import jax.numpy as jnp
from jax import lax
from jax.experimental import pallas as pl
from jax.experimental.pallas import tpu as pltpu

F32 = jnp.float32
BF16 = jnp.bfloat16
HI = lax.Precision.HIGHEST

D_MODEL = 2048
D_FF = 5632
N_HEADS = 8
N_GROUPS = 2
HPG = N_HEADS // N_GROUPS
HEAD_DIM = 128
CMP_LEN = 32
CMP_STRIDE = 16
CMP_HIDDEN = 128
SEL_BLOCK = 64
SEL_TOPK = 16
WINDOW = 512
Q_BLOCK = 128
DN_HEADS = 8
DN_DIM = 128
DN_CONV = 4
DN_CHUNK = 64
DN_GROUP = 256
REL_BUCKETS = 32
REL_MAX_DIST = 128
EPS = 1e-6
LOG2E = math.log2(math.e)
NEG_INF = -1e30
BIG = 1e30

LANES = 128
SUBLANES = 8
VMEM_LIMIT = 56 * 1024 * 1024

OFF_DQ, OFF_DK, OFF_DV = 0, 1024, 2048
OFF_NQ = 3072
OFF_DZ = 4096
OFF_MGA = 5120
OFF_MGB = 7168
OFF_KV6 = 9216
OFF_NGATE = 10752
OFF_DBA = 10880
PROJ_USED = 11008
PROJ_DIM = 11264

CMP_PAD = 16
N_CMP_PAD = 1152
NEAR_ROWS = 24
NEAR_KEYS = 2 * Q_BLOCK
WIN_KEYS = WINDOW + Q_BLOCK
V_ROWS = HEAD_DIM + 16
FAR_SUB = 4
FAR_TILES = 2 * FAR_SUB


def _rel_bucket_thresholds():
    n = np.arange(0, 4 * REL_MAX_DIST, dtype=np.int64)
    max_exact = REL_BUCKETS // 2
    nf = np.maximum(n, max_exact).astype(np.float32)
    large = max_exact + (np.log(nf / np.float32(max_exact)) / np.float32(math.log(REL_MAX_DIST / max_exact))
                         * np.float32(REL_BUCKETS - max_exact)).astype(np.int32)
    bucket = np.where(n < max_exact, n, np.minimum(large, REL_BUCKETS - 1))
    assert np.all(np.diff(bucket) >= 0)
    return [int(np.argmax(bucket >= b)) for b in range(REL_BUCKETS)]


REL_THRESH = _rel_bucket_thresholds()
FAR_DIST = REL_THRESH[REL_BUCKETS - 1]
assert FAR_DIST <= Q_BLOCK


def _cparams(sem):
    return pltpu.CompilerParams(dimension_semantics=sem, vmem_limit_bytes=VMEM_LIMIT)


def _rms_rows(x, w_row):
    ms = jnp.mean(x * x, axis=-1, keepdims=True)
    return x * lax.rsqrt(ms + EPS) * w_row


def _ffn_kernel(x_ref, nw_ref, wg_ref, wu_ref, wd_ref, o_ref, xn_ref, acc_ref):
    j = pl.program_id(1)

    @pl.when(j == 0)
    def _():
        xn_ref[...] = _rms_rows(x_ref[...], nw_ref[...]).astype(BF16)
        acc_ref[...] = jnp.zeros_like(acc_ref)

    xn = xn_ref[...]
    g = jnp.dot(xn, wg_ref[...], preferred_element_type=F32)
    u = jnp.dot(xn, wu_ref[...], preferred_element_type=F32)
    a = (g * jax.nn.sigmoid(g) * u).astype(BF16)
    acc_ref[...] += jnp.dot(a, wd_ref[...], preferred_element_type=F32)

    @pl.when(j == pl.num_programs(1) - 1)
    def _():
        o_ref[...] = x_ref[...] + 0.5 * acc_ref[...]


def _ffn(h, nw, wg, wu, wd, tm=512, tf=512):
    n, d = h.shape
    f = wg.shape[1]
    return pl.pallas_call(
        _ffn_kernel,
        grid=(n // tm, f // tf),
        in_specs=[
            pl.BlockSpec((tm, d), lambda i, j: (i, 0)),
            pl.BlockSpec((1, d), lambda i, j: (0, 0)),
            pl.BlockSpec((d, tf), lambda i, j: (0, j)),
            pl.BlockSpec((d, tf), lambda i, j: (0, j)),
            pl.BlockSpec((tf, d), lambda i, j: (j, 0)),
        ],
        out_specs=pl.BlockSpec((tm, d), lambda i, j: (i, 0)),
        out_shape=jax.ShapeDtypeStruct((n, d), F32),
        scratch_shapes=[pltpu.VMEM((tm, d), BF16), pltpu.VMEM((tm, d), F32)],
        compiler_params=_cparams(("parallel", "arbitrary")),
        name="ffn",
    )(h, nw, wg, wu, wd)


def _norm_matmul_kernel(x_ref, nw_ref, w_ref, o_ref, xn_ref):
    @pl.when(pl.program_id(1) == 0)
    def _():
        xn_ref[...] = _rms_rows(x_ref[...], nw_ref[...]).astype(BF16)

    o_ref[...] = jnp.dot(xn_ref[...], w_ref[...], preferred_element_type=F32)


def _norm_matmul(h, nw, w, tm=512, tn=1024):
    n, d = h.shape
    nout = w.shape[1]
    return pl.pallas_call(
        _norm_matmul_kernel,
        grid=(n // tm, nout // tn),
        in_specs=[
            pl.BlockSpec((tm, d), lambda i, j: (i, 0)),
            pl.BlockSpec((1, d), lambda i, j: (0, 0)),
            pl.BlockSpec((d, tn), lambda i, j: (0, j)),
        ],
        out_specs=pl.BlockSpec((tm, tn), lambda i, j: (i, j)),
        out_shape=jax.ShapeDtypeStruct((n, nout), F32),
        scratch_shapes=[pltpu.VMEM((tm, d), BF16)],
        compiler_params=_cparams(("parallel", "arbitrary")),
        name="in_proj",
    )(h, nw, w)


def _merge_kernel(ya_ref, yb_ref, wa_ref, wb_ref, ga_ref, gb_ref, o_ref):
    a = jnp.dot(ya_ref[...], wa_ref[...], preferred_element_type=F32)
    b = jnp.dot(yb_ref[...], wb_ref[...], preferred_element_type=F32)
    o_ref[...] = (jax.nn.sigmoid(ga_ref[...]) * a + jax.nn.sigmoid(gb_ref[...]) * b).astype(o_ref.dtype)


def _merge(ya, yb, wa, wb, proj, tm=512, tn=1024):
    n, k = ya.shape
    d = wa.shape[1]
    ga0, gb0 = OFF_MGA // tn, OFF_MGB // tn
    return pl.pallas_call(
        _merge_kernel,
        grid=(n // tm, d // tn),
        in_specs=[
            pl.BlockSpec((tm, k), lambda i, j: (i, 0)),
            pl.BlockSpec((tm, k), lambda i, j: (i, 0)),
            pl.BlockSpec((k, tn), lambda i, j: (0, j)),
            pl.BlockSpec((k, tn), lambda i, j: (0, j)),
            pl.BlockSpec((tm, tn), lambda i, j: (i, ga0 + j)),
            pl.BlockSpec((tm, tn), lambda i, j: (i, gb0 + j)),
        ],
        out_specs=pl.BlockSpec((tm, tn), lambda i, j: (i, j)),
        out_shape=jax.ShapeDtypeStruct((n, d), BF16),
        compiler_params=_cparams(("parallel", "arbitrary")),
        name="merge",
    )(ya, yb, wa, wb, proj, proj)


def _out_proj_kernel(m_ref, w_ref, h_ref, o_ref):
    o_ref[...] = h_ref[...] + jnp.dot(m_ref[...], w_ref[...], preferred_element_type=F32)


def _out_proj(merged, w, h, tm=512, tn=1024):
    n, k = merged.shape
    d = w.shape[1]
    return pl.pallas_call(
        _out_proj_kernel,
        grid=(n // tm, d // tn),
        in_specs=[
            pl.BlockSpec((tm, k), lambda i, j: (i, 0)),
            pl.BlockSpec((k, tn), lambda i, j: (0, j)),
            pl.BlockSpec((tm, tn), lambda i, j: (i, j)),
        ],
        out_specs=pl.BlockSpec((tm, tn), lambda i, j: (i, j)),
        out_shape=jax.ShapeDtypeStruct((n, d), F32),
        compiler_params=_cparams(("parallel", "arbitrary")),
        name="out_proj",
    )(merged, w, h)


def _nsa_prep_kernel(nq_ref, kv_ref, ng_ref, qn_ref, kn_ref, q_out, kv_out, g_out):
    scale = HEAD_DIM ** -0.5 * LOG2E
    for h in range(N_HEADS):
        sl = slice(h * HEAD_DIM, (h + 1) * HEAD_DIM)
        q_out[:, sl] = (_rms_rows(nq_ref[:, sl], qn_ref[...]) * scale).astype(BF16)
    for c in range(6 * N_GROUPS):
        sl = slice(c * HEAD_DIM, (c + 1) * HEAD_DIM)
        kind = c // N_GROUPS
        x = kv_ref[:, sl]
        if kind == 2:
            x = _rms_rows(x, kn_ref[1:2, :])
        elif kind == 4:
            x = _rms_rows(x, kn_ref[2:3, :])
        kv_out[:, sl] = x.astype(BF16)
    g_out[...] = jax.nn.sigmoid(ng_ref[...])


def _nsa_prep(proj, q_norm, k_norm, tm=512):
    n = proj.shape[0]
    kvw = 6 * N_GROUPS * HEAD_DIM
    return pl.pallas_call(
        _nsa_prep_kernel,
        grid=(n // tm,),
        in_specs=[
            pl.BlockSpec((tm, 1024), lambda i: (i, OFF_NQ // 1024)),
            pl.BlockSpec((tm, kvw), lambda i: (i, OFF_KV6 // kvw)),
            pl.BlockSpec((tm, LANES), lambda i: (i, OFF_NGATE // LANES)),
            pl.BlockSpec((1, HEAD_DIM), lambda i: (0, 0)),
            pl.BlockSpec((SUBLANES, HEAD_DIM), lambda i: (0, 0)),
        ],
        out_specs=[
            pl.BlockSpec((tm, 1024), lambda i: (i, 0)),
            pl.BlockSpec((tm, kvw), lambda i: (i, 0)),
            pl.BlockSpec((tm, LANES), lambda i: (i, 0)),
        ],
        out_shape=[
            jax.ShapeDtypeStruct((n, 1024), BF16),
            jax.ShapeDtypeStruct((n, kvw), BF16),
            jax.ShapeDtypeStruct((n, LANES), F32),
        ],
        compiler_params=_cparams(("parallel",)),
        name="nsa_prep",
    )(proj, proj, proj, q_norm, k_norm)


def _compress_kernel(p_ref, w1_ref, pos_ref, b1_ref, w2_ref, kn_ref, o_ref):
    half = CMP_STRIDE * HEAD_DIM
    p = p_ref[...]
    w1 = w1_ref[...]
    a = jnp.dot(p, w1[:half].astype(BF16), preferred_element_type=F32)
    b = jnp.dot(p, w1[half:].astype(BF16), preferred_element_type=F32)
    pos = jnp.broadcast_to(pos_ref[...], (SUBLANES, 2 * half))
    c0 = jnp.dot(pos, w1, preferred_element_type=F32, precision=HI)[0:1, :] + b1_ref[...]
    n = p.shape[0]
    hid = a + pltpu.roll(b, n - 1, 0) + c0
    hid = hid * jax.nn.sigmoid(hid)
    out = jnp.dot(hid.astype(BF16), w2_ref[...].astype(BF16), preferred_element_type=F32)
    normed = _rms_rows(out, kn_ref[0:1, :])
    out = jnp.where(pl.program_id(0) == 0, normed, out)
    row = lax.broadcasted_iota(jnp.int32, out.shape, 0)
    o_ref[...] = jnp.where(row < n - 1, out, 0.0)


def _compress(pieces, w1, pos, b1, w2, k_norm):
    _, bsz, ng, npc, width = pieces.shape
    return pl.pallas_call(
        _compress_kernel,
        grid=(2, bsz, ng),
        in_specs=[
            pl.BlockSpec((None, None, None, npc, width), lambda c, b, g: (c, b, g, 0, 0)),
            pl.BlockSpec((None, 2 * width, CMP_HIDDEN), lambda c, b, g: (c, 0, 0)),
            pl.BlockSpec((None, 1, 2 * width), lambda c, b, g: (c, 0, 0)),
            pl.BlockSpec((None, 1, CMP_HIDDEN), lambda c, b, g: (c, 0, 0)),
            pl.BlockSpec((None, CMP_HIDDEN, HEAD_DIM), lambda c, b, g: (c, 0, 0)),
            pl.BlockSpec((SUBLANES, HEAD_DIM), lambda c, b, g: (0, 0)),
        ],
        out_specs=pl.BlockSpec((None, None, None, npc, HEAD_DIM), lambda c, b, g: (c, b, g, 0, 0)),
        out_shape=jax.ShapeDtypeStruct((2, bsz, ng, npc, HEAD_DIM), F32),
        compiler_params=_cparams(("arbitrary", "arbitrary", "arbitrary")),
        name="compress",
    )(pieces, w1, pos, b1, w2, k_norm)


def _bias_kernel(tab_ref, bn_ref, bw_ref, bc_ref):
    g = pl.program_id(0)
    n_wt = WINDOW // Q_BLOCK

    def lookup(dist, head):
        v = jnp.full(dist.shape, tab_ref[0, head], F32)
        for b in range(1, REL_BUCKETS):
            v = jnp.where(dist >= REL_THRESH[b], tab_ref[b, head], v)
        return (v - tab_ref[REL_BUCKETS - 1, head]) * LOG2E

    ki = lax.broadcasted_iota(jnp.int32, (Q_BLOCK, Q_BLOCK), 0)
    qj = lax.broadcasted_iota(jnp.int32, (Q_BLOCK, Q_BLOCK), 1)
    r = lax.broadcasted_iota(jnp.int32, (NEAR_ROWS, Q_BLOCK), 0)
    qc = lax.broadcasted_iota(jnp.int32, (NEAR_ROWS, Q_BLOCK), 1)
    dist_c = qc - CMP_STRIDE * (r - CMP_PAD) - (CMP_LEN - 1)
    for h in range(HPG):
        head = g * HPG + h
        sl = slice(h * Q_BLOCK, (h + 1) * Q_BLOCK)
        diag = jnp.where(ki <= qj, lookup(qj - ki, head), NEG_INF)
        prev = lookup(qj - ki + Q_BLOCK, head)
        bn_ref[0:Q_BLOCK, sl] = prev
        bn_ref[Q_BLOCK:, sl] = diag
        bw_ref[0:Q_BLOCK, sl] = jnp.where(ki > qj, 0.0, NEG_INF)
        bw_ref[Q_BLOCK:(n_wt - 1) * Q_BLOCK, sl] = jnp.zeros(((n_wt - 2) * Q_BLOCK, Q_BLOCK), F32)
        bw_ref[(n_wt - 1) * Q_BLOCK:n_wt * Q_BLOCK, sl] = prev
        bw_ref[n_wt * Q_BLOCK:, sl] = diag
        bc_ref[:, sl] = lookup(dist_c, head)


def _bias_tiles(rel_table):
    wide = HPG * Q_BLOCK
    return pl.pallas_call(
        _bias_kernel,
        grid=(N_GROUPS,),
        in_specs=[pl.BlockSpec(memory_space=pltpu.SMEM)],
        out_specs=[
            pl.BlockSpec((None, NEAR_KEYS, wide), lambda g: (g, 0, 0)),
            pl.BlockSpec((None, WIN_KEYS, wide), lambda g: (g, 0, 0)),
            pl.BlockSpec((None, NEAR_ROWS, wide), lambda g: (g, 0, 0)),
        ],
        out_shape=[
            jax.ShapeDtypeStruct((N_GROUPS, NEAR_KEYS, wide), F32),
            jax.ShapeDtypeStruct((N_GROUPS, WIN_KEYS, wide), F32),
            jax.ShapeDtypeStruct((N_GROUPS, NEAR_ROWS, wide), F32),
        ],
        compiler_params=_cparams(("arbitrary",)),
        name="rel_bias",
    )(rel_table)


def _tile4(x):
    return jnp.concatenate([x] * HPG, axis=1)


def _nsa_attn_kernel(qT_ref, gT_ref, ks_ref, vsT_ref, kw_ref, vwT_ref, kc_ref, vcT_ref, ovT_ref,
                     bn_ref, bw_ref, bc_ref, o_ref, sc_ref, sel_ref, far_ref, sa_ref, sb_ref):
    qb = pl.program_id(2)
    qT = qT_ref[...]
    qpos = qb * Q_BLOCK + lax.broadcasted_iota(jnp.int32, (1, Q_BLOCK), 1)

    sc_ref[...] = jnp.dot(kc_ref[...], qT, preferred_element_type=F32)
    r0 = pl.multiple_of(qb * SUBLANES, SUBLANES)
    sc_ref[pl.ds(r0, NEAR_ROWS), :] += bc_ref[...]
    wide = HPG * Q_BLOCK
    rown = lax.broadcasted_iota(jnp.int32, (N_CMP_PAD, wide), 0)
    qpos4 = qb * Q_BLOCK + (lax.broadcasted_iota(jnp.int32, (1, wide), 1) & (Q_BLOCK - 1))
    vis4 = (rown >= CMP_PAD) & (CMP_STRIDE * rown <= qpos4 + (CMP_STRIDE * CMP_PAD - CMP_LEN + 1))
    s = jnp.where(vis4, sc_ref[...], NEG_INF)
    m = jnp.max(s, axis=0, keepdims=True)
    p = jnp.exp2(s - m)
    l = jnp.sum(p, axis=0, keepdims=True)
    inv = jnp.where(qpos4 >= CMP_LEN - 1, 1.0 / l, 0.0)
    pn = p * inv
    o_c = jnp.dot(vcT_ref[...], pn.astype(BF16), preferred_element_type=F32)
    psum = pn[:, 0:Q_BLOCK]
    for h in range(1, HPG):
        psum = psum + pn[:, h * Q_BLOCK:(h + 1) * Q_BLOCK]
    p_hi = psum.astype(BF16)
    p_lo = (psum - p_hi.astype(F32)).astype(BF16)
    ovT = ovT_ref[...]
    imp = (jnp.dot(ovT, p_hi, preferred_element_type=F32)
           + jnp.dot(ovT, p_lo, preferred_element_type=F32))

    n_sb = imp.shape[0]
    jblk = lax.broadcasted_iota(jnp.int32, (n_sb, Q_BLOCK), 0)
    cur = qpos // SEL_BLOCK
    eligible = jblk * SEL_BLOCK <= qpos
    forced = (jblk == 0) | (jblk == cur) | (jblk == cur - 1)
    score = jnp.where(eligible, jnp.where(forced, BIG, imp), NEG_INF)
    selmask = jnp.full((n_sb, Q_BLOCK), NEG_INF, F32)
    for _ in range(min(SEL_TOPK, n_sb)):
        top = jnp.max(score, axis=0, keepdims=True)
        idx = jnp.min(jnp.where(score == top, jblk, n_sb), axis=0, keepdims=True)
        pick = jblk == idx
        selmask = jnp.where(pick & (top > -1.0), 0.0, selmask)
        score = jnp.where(pick, -3e38, score)
    sel_ref[...] = selmask
    far_ref[...] = jnp.where(jblk < 2 * (qb - 1), selmask, NEG_INF)

    half = SEL_BLOCK

    def absent(cond):
        return jnp.where(cond, 0.0, NEG_INF)

    def sel_mask_tile(kt):
        a = jnp.broadcast_to(sel_ref[pl.ds(2 * kt, 1), :], (half, Q_BLOCK))
        b = jnp.broadcast_to(sel_ref[pl.ds(2 * kt + 1, 1), :], (half, Q_BLOCK))
        return _tile4(jnp.concatenate([a, b], axis=0))

    def key_rows(k_ref, kt, width=Q_BLOCK):
        return k_ref[pl.ds(pl.multiple_of(kt * width, width), width), :]

    def value_cols(vT_ref, kt, width=Q_BLOCK):
        return vT_ref[:, pl.ds(pl.multiple_of(kt * width, width), width)]

    def scores(k_ref, kt, width=Q_BLOCK):
        return jnp.dot(key_rows(k_ref, kt, width), qT, preferred_element_type=F32)

    def first(s, vT):
        m = jnp.max(s, axis=0, keepdims=True)
        p = jnp.exp2(s - m)
        return m, jnp.dot(vT, p.astype(BF16), preferred_element_type=F32)

    def update(state, s, vT_ref, kt, width=Q_BLOCK):
        m, acc = state
        m_new = jnp.maximum(m, jnp.max(s, axis=0, keepdims=True))
        alpha = jnp.exp2(m - m_new)
        p = jnp.exp2(s - m_new)
        acc = alpha * acc + jnp.dot(value_cols(vT_ref, kt, width), p.astype(BF16), preferred_element_type=F32)
        return m_new, acc

    def normalized(state):
        acc = state[1]
        return acc[:HEAD_DIM] * (1.0 / acc[HEAD_DIM:HEAD_DIM + 1])

    kp = jnp.maximum(qb - 1, 0)
    k2 = jnp.concatenate([key_rows(ks_ref, kp), key_rows(ks_ref, qb)], axis=0)
    near_mask = jnp.concatenate([sel_mask_tile(kp) + absent(qb >= 1), sel_mask_tile(qb)], axis=0)
    s = jnp.dot(k2, qT, preferred_element_type=F32) + bn_ref[...] + near_mask
    st = first(s, jnp.concatenate([value_cols(vsT_ref, kp), value_cols(vsT_ref, qb)], axis=1))


    sub_w = FAR_SUB * Q_BLOCK
    blocks_per_sub = sub_w // SEL_BLOCK

    def far_mask(cs):
        rows = [jnp.broadcast_to(far_ref[pl.ds(blocks_per_sub * cs + v, 1), :], (half, Q_BLOCK))
                for v in range(blocks_per_sub)]
        return _tile4(jnp.concatenate(rows, axis=0))

    last_chunk = ks_ref.shape[0] // sub_w - 1

    def far_body(i, st):
        c0 = 2 * i
        sb_ref[...] = scores(ks_ref, c0 + 1, sub_w)
        st = update(st, sa_ref[...] + far_mask(c0), vsT_ref, c0, sub_w)
        sa_ref[...] = scores(ks_ref, jnp.minimum(c0 + 2, last_chunk), sub_w)
        st = update(st, sb_ref[...] + far_mask(c0 + 1), vsT_ref, c0 + 1, sub_w)
        return st

    n_far = (jnp.maximum(qb - 1, 0) + FAR_TILES - 1) // FAR_TILES
    sa_ref[...] = scores(ks_ref, 0, sub_w)
    st = lax.fori_loop(0, n_far, far_body, st)
    o_s = normalized(st)

    n_wt = WINDOW // Q_BLOCK
    tiles = [jnp.maximum(qb - back, 0) for back in range(n_wt, -1, -1)]
    kwin = jnp.concatenate([key_rows(kw_ref, kt) for kt in tiles], axis=0)
    sw = jnp.dot(kwin, qT, preferred_element_type=F32) + bw_ref[...]
    sw = jnp.concatenate(
        [sw[u * Q_BLOCK:(u + 1) * Q_BLOCK] + absent(qb >= n_wt - u) for u in range(n_wt)]
        + [sw[n_wt * Q_BLOCK:]], axis=0)
    o_w = normalized(first(sw, jnp.concatenate([value_cols(vwT_ref, kt) for kt in tiles], axis=1)))

    o_ref[...] = (gT_ref[0:1, :] * o_c + gT_ref[1:2, :] * o_s + gT_ref[2:3, :] * o_w).astype(o_ref.dtype)


def _nsa_attn(qT, gT, ks, vsT, kw, vwT, kc, vcT, ovT, bn, bw, bc):
    bsz, ng, nqb, _, wide = qT.shape
    seq = ks.shape[2]
    n_sb = ovT.shape[0]
    big = lambda b, g, q: (b, g, 0, 0)
    blk = lambda b, g, q: (b, g, q, 0, 0)
    grp = lambda b, g, q: (g, 0, 0)
    single = dict(pipeline_mode=pl.Buffered(1))
    return pl.pallas_call(
        _nsa_attn_kernel,
        grid=(bsz, ng, nqb),
        in_specs=[
            pl.BlockSpec((None, None, None, HEAD_DIM, wide), blk),
            pl.BlockSpec((None, None, None, SUBLANES, wide), blk),
            pl.BlockSpec((None, None, seq, HEAD_DIM), big, **single),
            pl.BlockSpec((None, None, V_ROWS, seq), big, **single),
            pl.BlockSpec((None, None, seq, HEAD_DIM), big, **single),
            pl.BlockSpec((None, None, V_ROWS, seq), big, **single),
            pl.BlockSpec((None, None, N_CMP_PAD, HEAD_DIM), big),
            pl.BlockSpec((None, None, HEAD_DIM, N_CMP_PAD), big),
            pl.BlockSpec((n_sb, N_CMP_PAD), lambda b, g, q: (0, 0)),
            pl.BlockSpec((None, NEAR_KEYS, wide), grp),
            pl.BlockSpec((None, WIN_KEYS, wide), grp),
            pl.BlockSpec((None, NEAR_ROWS, wide), grp),
        ],
        out_specs=pl.BlockSpec((None, None, None, HEAD_DIM, wide), blk),
        out_shape=jax.ShapeDtypeStruct((bsz, ng, nqb, HEAD_DIM, wide), BF16),
        scratch_shapes=[pltpu.VMEM((N_CMP_PAD, wide), F32), pltpu.VMEM((n_sb, Q_BLOCK), F32),
                        pltpu.VMEM((n_sb, Q_BLOCK), F32),
                        pltpu.VMEM((FAR_SUB * Q_BLOCK, wide), F32), pltpu.VMEM((FAR_SUB * Q_BLOCK, wide), F32)],
        compiler_params=_cparams(("arbitrary", "arbitrary", "arbitrary")),
        name="nsa_attn",
    )(qT, gT, ks, vsT, kw, vwT, kc, vcT, ovT, bn, bw, bc)


def _dn_prep_kernel(x_ref, halo_ref, w_ref, dba_ref, alog_ref, dtb_ref, o_ref, beta_ref, g_ref, buf_ref,
                    *, blocks_per_seq):
    i = pl.program_id(0)
    j = pl.program_id(1)
    t = x_ref.shape[0]
    first_of_seq = (i % blocks_per_seq) == 0
    buf_ref[0:SUBLANES, :] = jnp.where(first_of_seq, 0.0, halo_ref[...])
    buf_ref[SUBLANES:, :] = x_ref[...]
    scale = jnp.where(j == 0, DN_DIM ** -0.5, 1.0)
    for h in range(DN_HEADS):
        sl = slice(h * DN_DIM, (h + 1) * DN_DIM)
        y = w_ref[DN_CONV - 1:DN_CONV, sl] * x_ref[:, sl]
        for back in range(1, DN_CONV):
            y = y + w_ref[DN_CONV - 1 - back:DN_CONV - back, sl] * buf_ref[pl.ds(SUBLANES - back, t), sl]
        y = y * jax.nn.sigmoid(y)
        nrm = y * lax.rsqrt(jnp.sum(y * y, axis=-1, keepdims=True) + EPS) * scale
        o_ref[:, sl] = jnp.where(j < 2, nrm, y)

    @pl.when(j == 0)
    def _():
        dba = dba_ref[...]
        beta = jax.nn.sigmoid(dba)
        z = dba + dtb_ref[...]
        softplus = jnp.maximum(z, 0.0) + jnp.log1p(jnp.exp(-jnp.abs(z)))
        g = -jnp.exp(alog_ref[...]) * softplus
        rt = lax.broadcasted_iota(jnp.int32, (t, t), 0)
        ct = lax.broadcasted_iota(jnp.int32, (t, t), 1)
        blocktri = jnp.where((rt >= ct) & (rt // DN_CHUNK == ct // DN_CHUNK), 1.0, 0.0)
        g = jnp.dot(blocktri, g, preferred_element_type=F32, precision=HI)
        wide = DN_HEADS * DN_DIM
        src = lax.broadcasted_iota(jnp.int32, (LANES, wide), 0)
        head = lax.broadcasted_iota(jnp.int32, (LANES, wide), 1) // DN_DIM
        beta_ref[...] = jnp.dot(beta, jnp.where(src == head, 1.0, 0.0), preferred_element_type=F32, precision=HI)
        g_ref[...] = jnp.dot(g, jnp.where(src == head + DN_HEADS, 1.0, 0.0), preferred_element_type=F32,
                             precision=HI)


def _dn_prep(proj, conv_w, alog_row, dtb_row, seq, t=512):
    n = proj.shape[0]
    width = DN_HEADS * DN_DIM
    hb = t // SUBLANES
    return pl.pallas_call(
        functools.partial(_dn_prep_kernel, blocks_per_seq=seq // t),
        grid=(n // t, 3),
        in_specs=[
            pl.BlockSpec((t, width), lambda i, j: (i, j)),
            pl.BlockSpec((SUBLANES, width), lambda i, j: (jnp.maximum(i * hb - 1, 0), j)),
            pl.BlockSpec((DN_CONV, width), lambda i, j: (0, j)),
            pl.BlockSpec((t, LANES), lambda i, j: (i, OFF_DBA // LANES)),
            pl.BlockSpec((1, LANES), lambda i, j: (0, 0)),
            pl.BlockSpec((1, LANES), lambda i, j: (0, 0)),
        ],
        out_specs=[
            pl.BlockSpec((t, width), lambda i, j: (i, j)),
            pl.BlockSpec((t, width), lambda i, j: (i, 0)),
            pl.BlockSpec((t, width), lambda i, j: (i, 0)),
        ],
        out_shape=[
            jax.ShapeDtypeStruct((n, 3 * width), F32),
            jax.ShapeDtypeStruct((n, width), F32),
            jax.ShapeDtypeStruct((n, width), F32),
        ],
        scratch_shapes=[pltpu.VMEM((t + SUBLANES, width), F32)],
        compiler_params=_cparams(("arbitrary", "arbitrary")),
        name="dn_prep",
    )(proj, proj, conv_w, proj, alog_row, dtb_row)


def _dot_nt(a, b, precision=None):
    return lax.dot_general(a, b, (((1,), (1,)), ((), ())), preferred_element_type=F32, precision=precision)


def _dot_tn(a, b, precision=None):
    return lax.dot_general(a, b, (((0,), (0,)), ((), ())), preferred_element_type=F32, precision=precision)


DN_MM_MODE = "bf16"


def _mm_dn(a, b):
    if DN_MM_MODE == "hi":
        return jnp.dot(a, b, preferred_element_type=F32, precision=HI)
    a_hi = a.astype(BF16)
    b_hi = b.astype(BF16)
    out = jnp.dot(a_hi, b_hi, preferred_element_type=F32)
    if DN_MM_MODE == "x3":
        a_lo = (a - a_hi.astype(F32)).astype(BF16)
        b_lo = (b - b_hi.astype(F32)).astype(BF16)
        out = out + jnp.dot(a_lo, b_hi, preferred_element_type=F32) + jnp.dot(a_hi, b_lo, preferred_element_type=F32)
    return out


def _deltanet_kernel(q_ref, k_ref, v_ref, beta_ref, g_ref, z_ref, nw_ref, o_ref, state_ref):
    c = DN_CHUNK
    t = q_ref.shape[0]
    n_chunks = t // c

    @pl.when(pl.program_id(2) == 0)
    def _():
        state_ref[...] = jnp.zeros_like(state_ref)

    gc_all = g_ref[...]
    beta_all = beta_ref[...]

    gsz = DN_GROUP
    cpg = gsz // c
    ri = lax.broadcasted_iota(jnp.int32, (gsz, gsz), 0)
    ci = lax.broadcasted_iota(jnp.int32, (gsz, gsz), 1)
    same = (ri // c) == (ci // c)
    causal = (ri >= ci) & same
    strict = (ri > ci) & same
    eye = jnp.where(ri == ci, 1.0, 0.0)
    chunk_of_col = lax.broadcasted_iota(jnp.int32, (DN_DIM, gsz), 1) // c

    groups = range(t // gsz)
    rows = [slice(gi * gsz, (gi + 1) * gsz) for gi in groups]
    q = [q_ref[r, :] for r in rows]
    k = [k_ref[r, :] for r in rows]
    gc = [gc_all[r, :] for r in rows]
    eg = [jnp.exp(x) for x in gc]
    kb = [k[gi] * beta_all[rows[gi], :] for gi in groups]
    rhs = [jnp.concatenate([kb[gi] * eg[gi], v_ref[rows[gi], :] * beta_all[rows[gi], :]], axis=1) for gi in groups]
    decay = []
    for gi in groups:
        gc2 = jnp.concatenate([gc[gi]] * (gsz // DN_DIM), axis=1)
        gcr = jnp.sum(gc2 * eye, axis=0, keepdims=True)
        decay.append(jnp.exp(jnp.where(causal, gc2 - gcr, NEG_INF)))
    a2 = [_dot_nt(jnp.concatenate([kb[gi], q[gi]], axis=0).astype(BF16), k[gi].astype(BF16)) for gi in groups]
    low = [jnp.where(strict, a2[gi][:gsz] * decay[gi], 0.0) for gi in groups]
    qk = [a2[gi][gsz:] * decay[gi] for gi in groups]
    inv = [eye - x for x in low]
    pw = [_mm_dn(x, x) for x in low]
    for _ in range(int(math.log2(c)) - 2):
        r = [_mm_dn(jnp.concatenate([pw[gi], inv[gi]], axis=0), pw[gi]) for gi in groups]
        inv = [inv[gi] + r[gi][gsz:] for gi in groups]
        pw = [r[gi][:gsz] for gi in groups]
    inv = [inv[gi] + _mm_dn(inv[gi], pw[gi]) for gi in groups]
    wu = [_mm_dn(inv[gi], rhs[gi]) for gi in groups]
    lasts, res = [], []
    for gi in groups:
        last = [gc[gi][(n + 1) * c - 1:(n + 1) * c, :] for n in range(cpg)]
        lasts.append(last)
        g_last_rows = jnp.concatenate([jnp.broadcast_to(x, (c, DN_DIM)) for x in last], axis=0)
        kd_t = jnp.transpose(k[gi] * jnp.exp(g_last_rows - gc[gi]))
        xs = [jnp.where(chunk_of_col == n, kd_t, 0.0) for n in range(cpg)]
        xs += [qk[gi][n * c:(n + 1) * c, :] for n in range(cpg)]
        res.append(jnp.dot(jnp.concatenate(xs, axis=0).astype(BF16), wu[gi].astype(BF16),
                           preferred_element_type=F32))
    pre = []
    for gi in groups:
        qd = q[gi] * eg[gi]
        for n in range(cpg):
            kw = res[gi][n * DN_DIM:(n + 1) * DN_DIM, :DN_DIM]
            ku = res[gi][n * DN_DIM:(n + 1) * DN_DIM, DN_DIM:]
            base = cpg * DN_DIM + n * c
            coef = qd[n * c:(n + 1) * c, :] - res[gi][base:base + c, :DN_DIM]
            qku = res[gi][base:base + c, DN_DIM:]
            lhs = jnp.concatenate([-kw, coef], axis=0).astype(BF16)
            pre.append((lhs, ku, qku, jnp.exp(lasts[gi][n])))

    state = state_ref[...]
    nw = nw_ref[...]
    for n in range(n_chunks):
        lhs, ku, qku, gl = pre[n]
        r = jnp.dot(lhs, state.astype(BF16), preferred_element_type=F32)
        o = r[DN_DIM:] + qku
        state = state * gl + r[:DN_DIM] + ku
        rows = slice(n * c, (n + 1) * c)
        z = z_ref[rows, :]
        o_ref[rows, :] = (_rms_rows(o, nw) * (z * jax.nn.sigmoid(z))).astype(o_ref.dtype)
    state_ref[...] = state


def _deltanet(qkv, beta, g, proj, norm_w, bsz, seq, t=512):
    n = qkv.shape[0]
    spb = seq // t
    zoff = OFF_DZ // DN_DIM
    tok = lambda b, h, i: (b * spb + i, h)
    return pl.pallas_call(
        _deltanet_kernel,
        grid=(bsz, DN_HEADS, spb),
        in_specs=[
            pl.BlockSpec((t, DN_DIM), lambda b, h, i: (b * spb + i, h)),
            pl.BlockSpec((t, DN_DIM), lambda b, h, i: (b * spb + i, DN_HEADS + h)),
            pl.BlockSpec((t, DN_DIM), lambda b, h, i: (b * spb + i, 2 * DN_HEADS + h)),
            pl.BlockSpec((t, LANES), tok),
            pl.BlockSpec((t, LANES), tok),
            pl.BlockSpec((t, DN_DIM), lambda b, h, i: (b * spb + i, zoff + h)),
            pl.BlockSpec((1, DN_DIM), lambda b, h, i: (0, 0)),
        ],
        out_specs=pl.BlockSpec((t, DN_DIM), lambda b, h, i: (b * spb + i, h)),
        out_shape=jax.ShapeDtypeStruct((n, DN_HEADS * DN_DIM), BF16),
        scratch_shapes=[pltpu.VMEM((DN_DIM, DN_DIM), F32)],
        compiler_params=_cparams(("arbitrary", "arbitrary", "arbitrary")),
        name="deltanet",
    )(qkv, qkv, qkv, beta, g, proj, norm_w)


def _regroup_w_in(w_in):
    d = w_in.shape[0]
    sizes = (1024, 256, 256, 256, 256, 256, 256, 24, 1024, 1024, 1024, 1024, 8, 8, 2048, 2048)
    offs = np.concatenate([[0], np.cumsum(sizes)])
    (nq, kc, vc, ksel, vsel, kwin, vwin, ngate, dq, dk, dv, dz, db, da, mga, mgb) = (
        w_in[:, offs[i]:offs[i + 1]] for i in range(len(sizes)))
    zeros = lambda w: jnp.zeros((d, w), w_in.dtype)
    cols = [dq, dk, dv, nq, dz, mga, mgb, kc, vc, ksel, vsel, kwin, vwin,
            ngate, zeros(LANES - 24), db, da, zeros(LANES - 16), zeros(PROJ_DIM - PROJ_USED)]
    return jnp.concatenate(cols, axis=1).astype(BF16)


def _overlap_t(n_sb, n_pieces):
    n = np.arange(N_CMP_PAD) - CMP_PAD
    j = np.arange(n_sb)
    valid = (n >= 0) & (n < n_pieces - 1)
    c0 = n * CMP_STRIDE
    s0 = j * SEL_BLOCK
    ov = (c0[None, :] < s0[:, None] + SEL_BLOCK) & (c0[None, :] + CMP_LEN > s0[:, None]) & valid[None, :]
    return jnp.asarray(ov.astype(np.float32), dtype=BF16)


def _token_mixer(h, bsz, seq, norm_mix, w_in, q_norm, k_norm, cmp_pos, cmp_w1, cmp_b1, cmp_w2, dn_conv,
                 dn_a_log, dn_dt_bias, dn_norm, w_a, w_b, w_out, bias_tiles):
    n = bsz * seq
    nqb = seq // Q_BLOCK
    npc = seq // CMP_STRIDE
    n_sb = seq // SEL_BLOCK
    assert npc + CMP_PAD <= N_CMP_PAD and nqb * SUBLANES + NEAR_ROWS <= N_CMP_PAD
    assert nqb % FAR_TILES == 0
    proj = _norm_matmul(h, norm_mix.reshape(1, -1), _regroup_w_in(w_in))

    k_norm8 = jnp.concatenate([k_norm, jnp.zeros((SUBLANES - 3, HEAD_DIM), F32)], axis=0)
    q_bf, kv_bf, gates = _nsa_prep(proj, q_norm.reshape(1, -1), k_norm8)
    kv = kv_bf.reshape(bsz, seq, 6, N_GROUPS, HEAD_DIM)
    by_group = lambda x: jnp.transpose(x, (0, 2, 1, 3))
    pieces = jnp.stack([by_group(kv[:, :, 0]), by_group(kv[:, :, 1])], axis=0)
    pieces = pieces.reshape(2, bsz, N_GROUPS, npc, CMP_STRIDE * HEAD_DIM)
    cmp = _compress(pieces, cmp_w1, cmp_pos.reshape(2, 1, CMP_LEN * HEAD_DIM), cmp_b1.reshape(2, 1, -1),
                    cmp_w2, k_norm8)
    cmp = jnp.pad(cmp, ((0, 0), (0, 0), (0, 0), (CMP_PAD, N_CMP_PAD - CMP_PAD - npc), (0, 0))).astype(BF16)
    kc = cmp[0]
    vcT = jnp.swapaxes(cmp[1], -1, -2)
    ks = by_group(kv[:, :, 2])
    ones_rows = jnp.zeros((bsz, N_GROUPS, V_ROWS - HEAD_DIM, seq), BF16).at[:, :, 0, :].set(1.0)
    with_ones = lambda vT: jnp.concatenate([vT, ones_rows], axis=2)
    vsT = with_ones(jnp.transpose(kv[:, :, 3], (0, 2, 3, 1)))
    kw = by_group(kv[:, :, 4])
    vwT = with_ones(jnp.transpose(kv[:, :, 5], (0, 2, 3, 1)))
    qT = jnp.transpose(q_bf.reshape(bsz, nqb, Q_BLOCK, N_GROUPS, HPG, HEAD_DIM), (0, 3, 1, 5, 4, 2))
    qT = qT.reshape(bsz, N_GROUPS, nqb, HEAD_DIM, HPG * Q_BLOCK)
    gT = jnp.transpose(gates[:, :3 * N_HEADS].reshape(bsz, nqb, Q_BLOCK, N_GROUPS, HPG, 3), (0, 3, 1, 5, 4, 2))
    gT = gT.reshape(bsz, N_GROUPS, nqb, 3, HPG * Q_BLOCK)
    gT = jnp.pad(gT, ((0, 0), (0, 0), (0, 0), (0, SUBLANES - 3), (0, 0)))
    oT = _nsa_attn(qT, gT, ks, vsT, kw, vwT, kc, vcT, _overlap_t(n_sb, npc), *bias_tiles)
    y_nsa = jnp.transpose(oT.reshape(bsz, N_GROUPS, nqb, HEAD_DIM, HPG, Q_BLOCK), (0, 2, 5, 1, 4, 3))
    y_nsa = y_nsa.reshape(n, N_HEADS * HEAD_DIM)

    lane_pad = lambda v, off: jnp.zeros((1, LANES), F32).at[0, off:off + DN_HEADS].set(v)
    qkv, beta, g = _dn_prep(proj, dn_conv, lane_pad(dn_a_log, DN_HEADS), lane_pad(dn_dt_bias, DN_HEADS), seq)
    y_dn = _deltanet(qkv, beta, g, proj, dn_norm.reshape(1, -1), bsz, seq)

    merged = _merge(y_nsa, y_dn, w_a.astype(BF16), w_b.astype(BF16), proj)
    return _out_proj(merged, w_out.astype(BF16), h)


def kernel(x, rel_table, norm_ffn1, ffn1_gate, ffn1_up, ffn1_down, norm_mix, w_in, q_norm, k_norm, cmp_pos, cmp_w1, cmp_b1, cmp_w2, dn_conv, dn_a_log, dn_dt_bias, dn_norm, w_branch_nsa, w_branch_dn, w_out, norm_ffn2, ffn2_gate, ffn2_up, ffn2_down):
    bsz, seq, d = x.shape
    depth = w_in.shape[0]
    h = x.reshape(bsz * seq, d)
    bias_tiles = _bias_tiles(rel_table)
    for l in range(depth):
        h = _ffn(h, norm_ffn1[l].reshape(1, -1), ffn1_gate[l].astype(BF16), ffn1_up[l].astype(BF16),
                 ffn1_down[l].astype(BF16))
        h = _token_mixer(h, bsz, seq, norm_mix[l], w_in[l], q_norm[l], k_norm[l], cmp_pos[l], cmp_w1[l],
                         cmp_b1[l], cmp_w2[l], dn_conv[l], dn_a_log[l], dn_dt_bias[l], dn_norm[l],
                         w_branch_nsa[l], w_branch_dn[l], w_out[l], bias_tiles)
        h = _ffn(h, norm_ffn2[l].reshape(1, -1), ffn2_gate[l].astype(BF16), ffn2_up[l].astype(BF16),
                 ffn2_down[l].astype(BF16))
    return h.reshape(bsz, seq, d)
```

```python
import functools
import math

import numpy as np
import jax
import jax.numpy as jnp
from jax import lax
from jax.experimental import pallas as pl
from jax.experimental.pallas import tpu as pltpu

F32 = jnp.float32
BF16 = jnp.bfloat16
HI = lax.Precision.HIGHEST

D_MODEL = 2048
D_FF = 5632
N_HEADS = 8
N_GROUPS = 2
HPG = N_HEADS // N_GROUPS
HEAD_DIM = 128
CMP_LEN = 32
CMP_STRIDE = 16
CMP_HIDDEN = 128
SEL_BLOCK = 64
SEL_TOPK = 16
WINDOW = 512
Q_BLOCK = 128
DN_HEADS = 8
DN_DIM = 128
DN_CONV = 4
DN_CHUNK = 64
DN_GROUP = 256
REL_BUCKETS = 32
REL_MAX_DIST = 128
EPS = 1e-6
LOG2E = math.log2(math.e)
NEG_INF = -1e30
BIG = 1e30

LANES = 128
SUBLANES = 8
VMEM_LIMIT = 56 * 1024 * 1024

OFF_DQ, OFF_DK, OFF_DV = 0, 1024, 2048
OFF_NQ = 3072
OFF_DZ = 4096
OFF_MGA = 5120
OFF_MGB = 7168
OFF_KV6 = 9216
OFF_NGATE = 10752
OFF_DBA = 10880
PROJ_USED = 11008
PROJ_DIM = 11264

CMP_PAD = 16
N_CMP_PAD = 1152
NEAR_ROWS = 24
NEAR_KEYS = 2 * Q_BLOCK
WIN_KEYS = WINDOW + Q_BLOCK
GATE_ROWS = 16
V_ROWS = HEAD_DIM + 16
FAR_SUB = 4
FAR_TILES = 2 * FAR_SUB


def _rel_bucket_thresholds():
    n = np.arange(0, 4 * REL_MAX_DIST, dtype=np.int64)
    max_exact = REL_BUCKETS // 2
    nf = np.maximum(n, max_exact).astype(np.float32)
    large = max_exact + (np.log(nf / np.float32(max_exact)) / np.float32(math.log(REL_MAX_DIST / max_exact))
                         * np.float32(REL_BUCKETS - max_exact)).astype(np.int32)
    bucket = np.where(n < max_exact, n, np.minimum(large, REL_BUCKETS - 1))
    assert np.all(np.diff(bucket) >= 0)
    return [int(np.argmax(bucket >= b)) for b in range(REL_BUCKETS)]


REL_THRESH = _rel_bucket_thresholds()
FAR_DIST = REL_THRESH[REL_BUCKETS - 1]
assert FAR_DIST <= Q_BLOCK


def _cparams(sem):
    return pltpu.CompilerParams(dimension_semantics=sem, vmem_limit_bytes=VMEM_LIMIT)


def _rms_rows(x, w_row):
    ms = jnp.mean(x * x, axis=-1, keepdims=True)
    return x * lax.rsqrt(ms + EPS) * w_row


def _ffn_kernel(x_ref, nw_ref, wg_ref, wu_ref, wd_ref, o_ref, xn_ref, acc_ref):
    j = pl.program_id(1)

    @pl.when(j == 0)
    def _():
        xn_ref[...] = _rms_rows(x_ref[...], nw_ref[...]).astype(BF16)
        acc_ref[...] = jnp.zeros_like(acc_ref)

    xn = xn_ref[...]
    g = jnp.dot(xn, wg_ref[...], preferred_element_type=F32)
    u = jnp.dot(xn, wu_ref[...], preferred_element_type=F32)
    a = (g * jax.nn.sigmoid(g) * u).astype(BF16)
    acc_ref[...] += jnp.dot(a, wd_ref[...], preferred_element_type=F32)

    @pl.when(j == pl.num_programs(1) - 1)
    def _():
        o_ref[...] = x_ref[...] + 0.5 * acc_ref[...]


def _ffn(h, nw, wg, wu, wd, tm=512, tf=512):
    n, d = h.shape
    f = wg.shape[1]
    return pl.pallas_call(
        _ffn_kernel,
        grid=(n // tm, f // tf),
        in_specs=[
            pl.BlockSpec((tm, d), lambda i, j: (i, 0)),
            pl.BlockSpec((1, d), lambda i, j: (0, 0)),
            pl.BlockSpec((d, tf), lambda i, j: (0, j)),
            pl.BlockSpec((d, tf), lambda i, j: (0, j)),
            pl.BlockSpec((tf, d), lambda i, j: (j, 0)),
        ],
        out_specs=pl.BlockSpec((tm, d), lambda i, j: (i, 0)),
        out_shape=jax.ShapeDtypeStruct((n, d), F32),
        scratch_shapes=[pltpu.VMEM((tm, d), BF16), pltpu.VMEM((tm, d), F32)],
        compiler_params=_cparams(("parallel", "arbitrary")),
        name="ffn",
    )(h, nw, wg, wu, wd)


def _norm_matmul_kernel(x_ref, nw_ref, w_ref, o_ref, xn_ref):
    @pl.when(pl.program_id(1) == 0)
    def _():
        xn_ref[...] = _rms_rows(x_ref[...], nw_ref[...]).astype(BF16)

    o_ref[...] = jnp.dot(xn_ref[...], w_ref[...], preferred_element_type=F32)


def _norm_matmul(h, nw, w, tm=1024, tn=1024):
    n, d = h.shape
    nout = w.shape[1]
    return pl.pallas_call(
        _norm_matmul_kernel,
        grid=(n // tm, nout // tn),
        in_specs=[
            pl.BlockSpec((tm, d), lambda i, j: (i, 0)),
            pl.BlockSpec((1, d), lambda i, j: (0, 0)),
            pl.BlockSpec((d, tn), lambda i, j: (0, j)),
        ],
        out_specs=pl.BlockSpec((tm, tn), lambda i, j: (i, j)),
        out_shape=jax.ShapeDtypeStruct((n, nout), F32),
        scratch_shapes=[pltpu.VMEM((tm, d), BF16)],
        compiler_params=_cparams(("parallel", "arbitrary")),
        name="in_proj",
    )(h, nw, w)


def _merge_kernel(ya_ref, yb_ref, wa_ref, wb_ref, ga_ref, gb_ref, o_ref):
    a = jnp.dot(ya_ref[...], wa_ref[...], preferred_element_type=F32)
    b = jnp.dot(yb_ref[...], wb_ref[...], preferred_element_type=F32)
    o_ref[...] = (jax.nn.sigmoid(ga_ref[...]) * a + jax.nn.sigmoid(gb_ref[...]) * b).astype(o_ref.dtype)


def _merge(ya, yb, wa, wb, proj, tm=512, tn=1024):
    n, k = ya.shape
    d = wa.shape[1]
    ga0, gb0 = OFF_MGA // tn, OFF_MGB // tn
    return pl.pallas_call(
        _merge_kernel,
        grid=(n // tm, d // tn),
        in_specs=[
            pl.BlockSpec((tm, k), lambda i, j: (i, 0)),
            pl.BlockSpec((tm, k), lambda i, j: (i, 0)),
            pl.BlockSpec((k, tn), lambda i, j: (0, j)),
            pl.BlockSpec((k, tn), lambda i, j: (0, j)),
            pl.BlockSpec((tm, tn), lambda i, j: (i, ga0 + j)),
            pl.BlockSpec((tm, tn), lambda i, j: (i, gb0 + j)),
        ],
        out_specs=pl.BlockSpec((tm, tn), lambda i, j: (i, j)),
        out_shape=jax.ShapeDtypeStruct((n, d), BF16),
        compiler_params=_cparams(("parallel", "arbitrary")),
        name="merge",
    )(ya, yb, wa, wb, proj, proj)


def _out_proj_kernel(m_ref, w_ref, h_ref, o_ref):
    o_ref[...] = h_ref[...] + jnp.dot(m_ref[...], w_ref[...], preferred_element_type=F32)


def _out_proj(merged, w, h, tm=512, tn=1024):
    n, k = merged.shape
    d = w.shape[1]
    return pl.pallas_call(
        _out_proj_kernel,
        grid=(n // tm, d // tn),
        in_specs=[
            pl.BlockSpec((tm, k), lambda i, j: (i, 0)),
            pl.BlockSpec((k, tn), lambda i, j: (0, j)),
            pl.BlockSpec((tm, tn), lambda i, j: (i, j)),
        ],
        out_specs=pl.BlockSpec((tm, tn), lambda i, j: (i, j)),
        out_shape=jax.ShapeDtypeStruct((n, d), F32),
        compiler_params=_cparams(("parallel", "arbitrary")),
        name="out_proj",
    )(merged, w, h)


def _nsa_prep_kernel(nq_ref, kv_ref, ng_ref, qn_ref, kn_ref,
                     qT_out, kc_out, vc_out, ks_out, vsT_out, kw_out, vwT_out, gT_out):
    tm = nq_ref.shape[0]
    scale = HEAD_DIM ** -0.5 * LOG2E
    for hh in range(N_HEADS):
        g, h = divmod(hh, HPG)
        qn = _rms_rows(nq_ref[:, hh * HEAD_DIM:(hh + 1) * HEAD_DIM], qn_ref[...]) * scale
        for qi in range(tm // Q_BLOCK):
            blk = qn[qi * Q_BLOCK:(qi + 1) * Q_BLOCK, :]
            qT_out[g, qi, :, h * Q_BLOCK:(h + 1) * Q_BLOCK] = jnp.transpose(blk).astype(BF16)
    ones_rows = jnp.where(lax.broadcasted_iota(jnp.int32, (V_ROWS - HEAD_DIM, tm), 0) == 0, 1.0, 0.0)
    for g in range(N_GROUPS):
        col = lambda kind: kv_ref[:, (kind * N_GROUPS + g) * HEAD_DIM:(kind * N_GROUPS + g + 1) * HEAD_DIM]
        kc_out[g] = col(0).astype(BF16)
        vc_out[g] = col(1).astype(BF16)
        ks_out[g] = _rms_rows(col(2), kn_ref[1:2, :]).astype(BF16)
        kw_out[g] = _rms_rows(col(4), kn_ref[2:3, :]).astype(BF16)
        for v_out, kind in ((vsT_out, 3), (vwT_out, 5)):
            v_out[g, 0:HEAD_DIM, :] = jnp.transpose(col(kind)).astype(BF16)
            v_out[g, HEAD_DIM:, :] = ones_rows.astype(BF16)
    sig = jax.nn.sigmoid(ng_ref[...])
    for qi in range(tm // Q_BLOCK):
        t = jnp.transpose(sig[qi * Q_BLOCK:(qi + 1) * Q_BLOCK, :])
        for g in range(N_GROUPS):
            gT_out[g, qi] = t[g * GATE_ROWS:(g + 1) * GATE_ROWS, :]


def _nsa_prep(proj, q_norm, k_norm, bsz, seq, tm=512):
    kvw = 6 * N_GROUPS * HEAD_DIM
    spb = seq // tm
    qpb = tm // Q_BLOCK
    nqb = seq // Q_BLOCK
    wide = HPG * Q_BLOCK
    keys = pl.BlockSpec((None, N_GROUPS, tm, HEAD_DIM), lambda i: (i // spb, 0, i % spb, 0))
    vals = pl.BlockSpec((None, N_GROUPS, V_ROWS, tm), lambda i: (i // spb, 0, 0, i % spb))
    keys_shape = jax.ShapeDtypeStruct((bsz, N_GROUPS, seq, HEAD_DIM), BF16)
    vals_shape = jax.ShapeDtypeStruct((bsz, N_GROUPS, V_ROWS, seq), BF16)
    return pl.pallas_call(
        _nsa_prep_kernel,
        grid=(bsz * spb,),
        in_specs=[
            pl.BlockSpec((tm, 1024), lambda i: (i, OFF_NQ // 1024)),
            pl.BlockSpec((tm, kvw), lambda i: (i, OFF_KV6 // kvw)),
            pl.BlockSpec((tm, LANES), lambda i: (i, OFF_NGATE // LANES)),
            pl.BlockSpec((1, HEAD_DIM), lambda i: (0, 0)),
            pl.BlockSpec((SUBLANES, HEAD_DIM), lambda i: (0, 0)),
        ],
        out_specs=[
            pl.BlockSpec((None, N_GROUPS, qpb, HEAD_DIM, wide), lambda i: (i // spb, 0, i % spb, 0, 0)),
            keys, keys, keys, vals, keys, vals,
            pl.BlockSpec((None, N_GROUPS, qpb, GATE_ROWS, Q_BLOCK), lambda i: (i // spb, 0, i % spb, 0, 0)),
        ],
        out_shape=[
            jax.ShapeDtypeStruct((bsz, N_GROUPS, nqb, HEAD_DIM, wide), BF16),
            keys_shape, keys_shape, keys_shape, vals_shape, keys_shape, vals_shape,
            jax.ShapeDtypeStruct((bsz, N_GROUPS, nqb, GATE_ROWS, Q_BLOCK), F32),
        ],
        compiler_params=_cparams(("parallel",)),
        name="nsa_prep",
    )(proj, proj, proj, q_norm, k_norm)


def _compress_kernel(p_ref, w1_ref, pos_ref, b1_ref, w2_ref, kn_ref, o_ref):
    half = CMP_STRIDE * HEAD_DIM
    p = p_ref[...]
    w1 = w1_ref[...]
    a = jnp.dot(p, w1[:half].astype(BF16), preferred_element_type=F32)
    b = jnp.dot(p, w1[half:].astype(BF16), preferred_element_type=F32)
    pos = jnp.broadcast_to(pos_ref[...], (SUBLANES, 2 * half))
    c0 = jnp.dot(pos, w1, preferred_element_type=F32, precision=HI)[0:1, :] + b1_ref[...]
    n = p.shape[0]
    hid = a + pltpu.roll(b, n - 1, 0) + c0
    hid = hid * jax.nn.sigmoid(hid)
    out = jnp.dot(hid.astype(BF16), w2_ref[...].astype(BF16), preferred_element_type=F32)
    normed = _rms_rows(out, kn_ref[0:1, :])
    out = jnp.where(pl.program_id(0) == 0, normed, out)
    row = lax.broadcasted_iota(jnp.int32, out.shape, 0)
    o_ref[...] = jnp.where(row < n - 1, out, 0.0)


def _compress(pieces, w1, pos, b1, w2, k_norm):
    _, bsz, ng, npc, width = pieces.shape
    return pl.pallas_call(
        _compress_kernel,
        grid=(2, bsz, ng),
        in_specs=[
            pl.BlockSpec((None, None, None, npc, width), lambda c, b, g: (c, b, g, 0, 0)),
            pl.BlockSpec((None, 2 * width, CMP_HIDDEN), lambda c, b, g: (c, 0, 0)),
            pl.BlockSpec((None, 1, 2 * width), lambda c, b, g: (c, 0, 0)),
            pl.BlockSpec((None, 1, CMP_HIDDEN), lambda c, b, g: (c, 0, 0)),
            pl.BlockSpec((None, CMP_HIDDEN, HEAD_DIM), lambda c, b, g: (c, 0, 0)),
            pl.BlockSpec((SUBLANES, HEAD_DIM), lambda c, b, g: (0, 0)),
        ],
        out_specs=pl.BlockSpec((None, None, None, npc, HEAD_DIM), lambda c, b, g: (c, b, g, 0, 0)),
        out_shape=jax.ShapeDtypeStruct((2, bsz, ng, npc, HEAD_DIM), F32),
        compiler_params=_cparams(("arbitrary", "arbitrary", "arbitrary")),
        name="compress",
    )(pieces, w1, pos, b1, w2, k_norm)


def _bias_kernel(tab_ref, bn_ref, bw_ref, bc_ref):
    g = pl.program_id(0)
    n_wt = WINDOW // Q_BLOCK

    def lookup(dist, head):
        v = jnp.full(dist.shape, tab_ref[0, head], F32)
        for b in range(1, REL_BUCKETS):
            v = jnp.where(dist >= REL_THRESH[b], tab_ref[b, head], v)
        return (v - tab_ref[REL_BUCKETS - 1, head]) * LOG2E

    ki = lax.broadcasted_iota(jnp.int32, (Q_BLOCK, Q_BLOCK), 0)
    qj = lax.broadcasted_iota(jnp.int32, (Q_BLOCK, Q_BLOCK), 1)
    r = lax.broadcasted_iota(jnp.int32, (NEAR_ROWS, Q_BLOCK), 0)
    qc = lax.broadcasted_iota(jnp.int32, (NEAR_ROWS, Q_BLOCK), 1)
    dist_c = qc - CMP_STRIDE * (r - CMP_PAD) - (CMP_LEN - 1)
    for h in range(HPG):
        head = g * HPG + h
        sl = slice(h * Q_BLOCK, (h + 1) * Q_BLOCK)
        diag = jnp.where(ki <= qj, lookup(qj - ki, head), NEG_INF)
        prev = lookup(qj - ki + Q_BLOCK, head)
        bn_ref[0:Q_BLOCK, sl] = prev
        bn_ref[Q_BLOCK:, sl] = diag
        bw_ref[0:Q_BLOCK, sl] = jnp.where(ki > qj, 0.0, NEG_INF)
        bw_ref[Q_BLOCK:(n_wt - 1) * Q_BLOCK, sl] = jnp.zeros(((n_wt - 2) * Q_BLOCK, Q_BLOCK), F32)
        bw_ref[(n_wt - 1) * Q_BLOCK:n_wt * Q_BLOCK, sl] = prev
        bw_ref[n_wt * Q_BLOCK:, sl] = diag
        bc_ref[:, sl] = jnp.where(dist_c >= 0, lookup(dist_c, head), NEG_INF)


def _bias_tiles(rel_table):
    wide = HPG * Q_BLOCK
    return pl.pallas_call(
        _bias_kernel,
        grid=(N_GROUPS,),
        in_specs=[pl.BlockSpec(memory_space=pltpu.SMEM)],
        out_specs=[
            pl.BlockSpec((None, NEAR_KEYS, wide), lambda g: (g, 0, 0)),
            pl.BlockSpec((None, WIN_KEYS, wide), lambda g: (g, 0, 0)),
            pl.BlockSpec((None, NEAR_ROWS, wide), lambda g: (g, 0, 0)),
        ],
        out_shape=[
            jax.ShapeDtypeStruct((N_GROUPS, NEAR_KEYS, wide), F32),
            jax.ShapeDtypeStruct((N_GROUPS, WIN_KEYS, wide), F32),
            jax.ShapeDtypeStruct((N_GROUPS, NEAR_ROWS, wide), F32),
        ],
        compiler_params=_cparams(("arbitrary",)),
        name="rel_bias",
    )(rel_table)


def _tile4(x):
    return jnp.concatenate([x] * HPG, axis=1)


def _nsa_attn_kernel(qT_ref, gT_ref, ks_ref, vsT_ref, kw_ref, vwT_ref, kc_ref, vcT_ref, ovT_ref,
                     bn_ref, bw_ref, bc_ref, o_ref, sc_ref, sel_ref, far_ref, sa_ref, sb_ref):
    qb = pl.program_id(2)
    qT = qT_ref[...]
    qpos = qb * Q_BLOCK + lax.broadcasted_iota(jnp.int32, (1, Q_BLOCK), 1)

    sc_ref[...] = jnp.dot(kc_ref[...], qT, preferred_element_type=F32)
    sc_ref[0:CMP_PAD, :] = jnp.full((CMP_PAD, HPG * Q_BLOCK), NEG_INF, F32)
    r0 = pl.multiple_of(qb * SUBLANES, SUBLANES)
    sc_ref[pl.ds(r0, NEAR_ROWS), :] += bc_ref[...]
    wide = HPG * Q_BLOCK
    rown = lax.broadcasted_iota(jnp.int32, (N_CMP_PAD, wide), 0)
    qpos4 = qb * Q_BLOCK + (lax.broadcasted_iota(jnp.int32, (1, wide), 1) & (Q_BLOCK - 1))
    s = jnp.where(rown < qb * SUBLANES + NEAR_ROWS, sc_ref[...], NEG_INF)
    m = jnp.max(s, axis=0, keepdims=True)
    p = jnp.exp2(s - m)
    l = jnp.sum(p, axis=0, keepdims=True)
    inv = jnp.where(qpos4 >= CMP_LEN - 1, 1.0 / l, 0.0)
    pn = p * inv
    o_c = jnp.dot(vcT_ref[...], pn.astype(BF16), preferred_element_type=F32)
    psum = pn[:, 0:Q_BLOCK]
    for h in range(1, HPG):
        psum = psum + pn[:, h * Q_BLOCK:(h + 1) * Q_BLOCK]
    p_hi = psum.astype(BF16)
    p_lo = (psum - p_hi.astype(F32)).astype(BF16)
    ovT = ovT_ref[...]
    imp = (jnp.dot(ovT, p_hi, preferred_element_type=F32)
           + jnp.dot(ovT, p_lo, preferred_element_type=F32))

    n_sb = imp.shape[0]
    jblk = lax.broadcasted_iota(jnp.int32, (n_sb, Q_BLOCK), 0)
    cur = qpos // SEL_BLOCK
    eligible = jblk * SEL_BLOCK <= qpos
    forced = (jblk == 0) | (jblk == cur) | (jblk == cur - 1)
    picked = -3e38
    score = jnp.where(eligible, jnp.where(forced, picked, imp), NEG_INF)
    for _ in range(max(min(SEL_TOPK, n_sb) - 3, 0)):
        top = jnp.max(score, axis=0, keepdims=True)
        idx = jnp.min(jnp.where(score == top, jblk, n_sb), axis=0, keepdims=True)
        idx = jnp.where(top > -1.0, idx, -1)
        score = jnp.where(jblk == idx, picked, score)
    selmask = jnp.where(score == picked, 0.0, NEG_INF)
    sel_ref[...] = selmask
    far_ref[...] = jnp.where(jblk < 2 * (qb - 1), selmask, NEG_INF)

    half = SEL_BLOCK

    def absent(cond):
        return jnp.where(cond, 0.0, NEG_INF)

    def sel_mask_tile(kt):
        a = jnp.broadcast_to(sel_ref[pl.ds(2 * kt, 1), :], (half, Q_BLOCK))
        b = jnp.broadcast_to(sel_ref[pl.ds(2 * kt + 1, 1), :], (half, Q_BLOCK))
        return _tile4(jnp.concatenate([a, b], axis=0))

    def key_rows(k_ref, kt, width=Q_BLOCK):
        return k_ref[pl.ds(pl.multiple_of(kt * width, width), width), :]

    def value_cols(vT_ref, kt, width=Q_BLOCK):
        return vT_ref[:, pl.ds(pl.multiple_of(kt * width, width), width)]

    def scores(k_ref, kt, width=Q_BLOCK):
        return jnp.dot(key_rows(k_ref, kt, width), qT, preferred_element_type=F32)

    def first(s, vT):
        m = jnp.max(s, axis=0, keepdims=True)
        p = jnp.exp2(s - m)
        return m, jnp.dot(vT, p.astype(BF16), preferred_element_type=F32)

    def update(state, s, vT_ref, kt, width=Q_BLOCK):
        m, acc = state
        m_new = jnp.maximum(m, jnp.max(s, axis=0, keepdims=True))
        alpha = jnp.exp2(m - m_new)
        p = jnp.exp2(s - m_new)
        acc = alpha * acc + jnp.dot(value_cols(vT_ref, kt, width), p.astype(BF16), preferred_element_type=F32)
        return m_new, acc

    def normalized(state):
        acc = state[1]
        return acc[:HEAD_DIM] * (1.0 / acc[HEAD_DIM:HEAD_DIM + 1])

    kp = jnp.maximum(qb - 1, 0)
    k2 = jnp.concatenate([key_rows(ks_ref, kp), key_rows(ks_ref, qb)], axis=0)
    near_mask = jnp.concatenate([sel_mask_tile(kp) + absent(qb >= 1), sel_mask_tile(qb)], axis=0)
    s = jnp.dot(k2, qT, preferred_element_type=F32) + bn_ref[...] + near_mask
    st = first(s, jnp.concatenate([value_cols(vsT_ref, kp), value_cols(vsT_ref, qb)], axis=1))


    sub_w = FAR_SUB * Q_BLOCK
    blocks_per_sub = sub_w // SEL_BLOCK

    def far_mask(cs):
        rows = [jnp.broadcast_to(far_ref[pl.ds(blocks_per_sub * cs + v, 1), :], (half, Q_BLOCK))
                for v in range(blocks_per_sub)]
        return _tile4(jnp.concatenate(rows, axis=0))

    last_chunk = ks_ref.shape[0] // sub_w - 1

    def far_body(i, st):
        c0 = 2 * i
        sb_ref[...] = scores(ks_ref, c0 + 1, sub_w)
        st = update(st, sa_ref[...] + far_mask(c0), vsT_ref, c0, sub_w)
        sa_ref[...] = scores(ks_ref, jnp.minimum(c0 + 2, last_chunk), sub_w)
        st = update(st, sb_ref[...] + far_mask(c0 + 1), vsT_ref, c0 + 1, sub_w)
        return st

    n_far = (jnp.maximum(qb - 1, 0) + FAR_TILES - 1) // FAR_TILES
    sa_ref[...] = scores(ks_ref, 0, sub_w)
    st = lax.fori_loop(0, n_far, far_body, st)
    o_s = normalized(st)

    n_wt = WINDOW // Q_BLOCK
    tiles = [jnp.maximum(qb - back, 0) for back in range(n_wt, -1, -1)]
    kwin = jnp.concatenate([key_rows(kw_ref, kt) for kt in tiles], axis=0)
    sw = jnp.dot(kwin, qT, preferred_element_type=F32) + bw_ref[...]
    sw = jnp.concatenate(
        [sw[u * Q_BLOCK:(u + 1) * Q_BLOCK] + absent(qb >= n_wt - u) for u in range(n_wt)]
        + [sw[n_wt * Q_BLOCK:]], axis=0)
    o_w = normalized(first(sw, jnp.concatenate([value_cols(vwT_ref, kt) for kt in tiles], axis=1)))

    for h in range(HPG):
        sl = slice(h * Q_BLOCK, (h + 1) * Q_BLOCK)
        gate = lambda c: gT_ref[c * HPG + h:c * HPG + h + 1, :]
        mix = gate(0) * o_c[:, sl] + gate(1) * o_s[:, sl] + gate(2) * o_w[:, sl]
        o_ref[:, h * HEAD_DIM:(h + 1) * HEAD_DIM] = jnp.transpose(mix).astype(o_ref.dtype)


def _nsa_attn(qT, gT, ks, vsT, kw, vwT, kc, vcT, ovT, bn, bw, bc):
    bsz, ng, nqb, _, wide = qT.shape
    seq = ks.shape[2]
    n_sb = ovT.shape[0]
    big = lambda b, g, q: (b, g, 0, 0)
    blk = lambda b, g, q: (b, g, q, 0, 0)
    grp = lambda b, g, q: (g, 0, 0)
    single = dict(pipeline_mode=pl.Buffered(1))
    return pl.pallas_call(
        _nsa_attn_kernel,
        grid=(bsz, ng, nqb),
        in_specs=[
            pl.BlockSpec((None, None, None, HEAD_DIM, wide), blk),
            pl.BlockSpec((None, None, None, GATE_ROWS, Q_BLOCK), blk),
            pl.BlockSpec((None, None, seq, HEAD_DIM), big, **single),
            pl.BlockSpec((None, None, V_ROWS, seq), big, **single),
            pl.BlockSpec((None, None, seq, HEAD_DIM), big, **single),
            pl.BlockSpec((None, None, V_ROWS, seq), big, **single),
            pl.BlockSpec((None, None, N_CMP_PAD, HEAD_DIM), big),
            pl.BlockSpec((None, None, HEAD_DIM, N_CMP_PAD), big),
            pl.BlockSpec((n_sb, N_CMP_PAD), lambda b, g, q: (0, 0)),
            pl.BlockSpec((None, NEAR_KEYS, wide), grp),
            pl.BlockSpec((None, WIN_KEYS, wide), grp),
            pl.BlockSpec((None, NEAR_ROWS, wide), grp),
        ],
        out_specs=pl.BlockSpec((Q_BLOCK, HPG * HEAD_DIM), lambda b, g, q: (b * nqb + q, g)),
        out_shape=jax.ShapeDtypeStruct((bsz * seq, ng * HPG * HEAD_DIM), BF16),
        scratch_shapes=[pltpu.VMEM((N_CMP_PAD, wide), F32), pltpu.VMEM((n_sb, Q_BLOCK), F32),
                        pltpu.VMEM((n_sb, Q_BLOCK), F32),
                        pltpu.VMEM((FAR_SUB * Q_BLOCK, wide), F32), pltpu.VMEM((FAR_SUB * Q_BLOCK, wide), F32)],
        compiler_params=_cparams(("arbitrary", "arbitrary", "arbitrary")),
        name="nsa_attn",
    )(qT, gT, ks, vsT, kw, vwT, kc, vcT, ovT, bn, bw, bc)


def _dn_prep_kernel(x_ref, halo_ref, w_ref, dba_ref, alog_ref, dtb_ref, o_ref, beta_ref, g_ref, buf_ref,
                    *, blocks_per_seq):
    i = pl.program_id(0)
    j = pl.program_id(1)
    t = x_ref.shape[0]
    first_of_seq = (i % blocks_per_seq) == 0
    buf_ref[0:SUBLANES, :] = jnp.where(first_of_seq, 0.0, halo_ref[...])
    buf_ref[SUBLANES:, :] = x_ref[...]
    scale = jnp.where(j == 0, DN_DIM ** -0.5, 1.0)
    for h in range(DN_HEADS):
        sl = slice(h * DN_DIM, (h + 1) * DN_DIM)
        y = w_ref[DN_CONV - 1:DN_CONV, sl] * x_ref[:, sl]
        for back in range(1, DN_CONV):
            y = y + w_ref[DN_CONV - 1 - back:DN_CONV - back, sl] * buf_ref[pl.ds(SUBLANES - back, t), sl]
        y = y * jax.nn.sigmoid(y)
        nrm = y * lax.rsqrt(jnp.sum(y * y, axis=-1, keepdims=True) + EPS) * scale
        o_ref[:, sl] = jnp.where(j < 2, nrm, y)

    @pl.when(j == 0)
    def _():
        dba = dba_ref[...]
        beta = jax.nn.sigmoid(dba)
        z = dba + dtb_ref[...]
        softplus = jnp.maximum(z, 0.0) + jnp.log1p(jnp.exp(-jnp.abs(z)))
        g = -jnp.exp(alog_ref[...]) * softplus
        rt = lax.broadcasted_iota(jnp.int32, (t, t), 0)
        ct = lax.broadcasted_iota(jnp.int32, (t, t), 1)
        blocktri = jnp.where((rt >= ct) & (rt // DN_CHUNK == ct // DN_CHUNK), 1.0, 0.0)
        g = jnp.dot(blocktri, g, preferred_element_type=F32, precision=HI)
        wide = DN_HEADS * DN_DIM
        src = lax.broadcasted_iota(jnp.int32, (LANES, wide), 0)
        head = lax.broadcasted_iota(jnp.int32, (LANES, wide), 1) // DN_DIM
        beta_ref[...] = jnp.dot(beta, jnp.where(src == head, 1.0, 0.0), preferred_element_type=F32, precision=HI)
        g_ref[...] = jnp.dot(g, jnp.where(src == head + DN_HEADS, 1.0, 0.0), preferred_element_type=F32,
                             precision=HI)


def _dn_prep(proj, conv_w, alog_row, dtb_row, seq, t=512):
    n = proj.shape[0]
    width = DN_HEADS * DN_DIM
    hb = t // SUBLANES
    return pl.pallas_call(
        functools.partial(_dn_prep_kernel, blocks_per_seq=seq // t),
        grid=(n // t, 3),
        in_specs=[
            pl.BlockSpec((t, width), lambda i, j: (i, j)),
            pl.BlockSpec((SUBLANES, width), lambda i, j: (jnp.maximum(i * hb - 1, 0), j)),
            pl.BlockSpec((DN_CONV, width), lambda i, j: (0, j)),
            pl.BlockSpec((t, LANES), lambda i, j: (i, OFF_DBA // LANES)),
            pl.BlockSpec((1, LANES), lambda i, j: (0, 0)),
            pl.BlockSpec((1, LANES), lambda i, j: (0, 0)),
        ],
        out_specs=[
            pl.BlockSpec((t, width), lambda i, j: (i, j)),
            pl.BlockSpec((t, width), lambda i, j: (i, 0)),
            pl.BlockSpec((t, width), lambda i, j: (i, 0)),
        ],
        out_shape=[
            jax.ShapeDtypeStruct((n, 3 * width), F32),
            jax.ShapeDtypeStruct((n, width), F32),
            jax.ShapeDtypeStruct((n, width), F32),
        ],
        scratch_shapes=[pltpu.VMEM((t + SUBLANES, width), F32)],
        compiler_params=_cparams(("arbitrary", "arbitrary")),
        name="dn_prep",
    )(proj, proj, conv_w, proj, alog_row, dtb_row)


def _dot_nt(a, b, precision=None):
    return lax.dot_general(a, b, (((1,), (1,)), ((), ())), preferred_element_type=F32, precision=precision)


def _dot_tn(a, b, precision=None):
    return lax.dot_general(a, b, (((0,), (0,)), ((), ())), preferred_element_type=F32, precision=precision)


DN_MM_MODE = "bf16"


def _mm_dn(a, b):
    if DN_MM_MODE == "hi":
        return jnp.dot(a, b, preferred_element_type=F32, precision=HI)
    a_hi = a.astype(BF16)
    b_hi = b.astype(BF16)
    out = jnp.dot(a_hi, b_hi, preferred_element_type=F32)
    if DN_MM_MODE == "x3":
        a_lo = (a - a_hi.astype(F32)).astype(BF16)
        b_lo = (b - b_hi.astype(F32)).astype(BF16)
        out = out + jnp.dot(a_lo, b_hi, preferred_element_type=F32) + jnp.dot(a_hi, b_lo, preferred_element_type=F32)
    return out


def _deltanet_kernel(q_ref, k_ref, v_ref, beta_ref, g_ref, z_ref, nw_ref, o_ref, state_ref):
    c = DN_CHUNK
    t = q_ref.shape[0]
    n_chunks = t // c

    @pl.when(pl.program_id(2) == 0)
    def _():
        state_ref[...] = jnp.zeros_like(state_ref)

    gc_all = g_ref[...]
    beta_all = beta_ref[...]

    gsz = DN_GROUP
    cpg = gsz // c
    ri = lax.broadcasted_iota(jnp.int32, (gsz, gsz), 0)
    ci = lax.broadcasted_iota(jnp.int32, (gsz, gsz), 1)
    same = (ri // c) == (ci // c)
    causal = (ri >= ci) & same
    strict = (ri > ci) & same
    eye = jnp.where(ri == ci, 1.0, 0.0)
    chunk_of_col = lax.broadcasted_iota(jnp.int32, (DN_DIM, gsz), 1) // c

    heads = q_ref.shape[1] // DN_DIM
    groups = range(heads * (t // gsz))
    rows = [slice((p % (t // gsz)) * gsz, (p % (t // gsz) + 1) * gsz) for p in groups]
    cols = [slice((p // (t // gsz)) * DN_DIM, (p // (t // gsz) + 1) * DN_DIM) for p in groups]
    q = [q_ref[rows[p], cols[p]] for p in groups]
    k = [k_ref[rows[p], cols[p]] for p in groups]
    gc = [gc_all[rows[p], cols[p]] for p in groups]
    eg = [jnp.exp(x) for x in gc]
    beta = [beta_all[rows[p], cols[p]] for p in groups]
    kb = [k[gi] * beta[gi] for gi in groups]
    rhs = [jnp.concatenate([kb[gi] * eg[gi], v_ref[rows[gi], cols[gi]] * beta[gi]], axis=1) for gi in groups]
    decay = []
    for gi in groups:
        gc2 = jnp.concatenate([gc[gi]] * (gsz // DN_DIM), axis=1)
        gcr = jnp.sum(gc2 * eye, axis=0, keepdims=True)
        decay.append(jnp.exp(jnp.where(causal, gc2 - gcr, NEG_INF)))
    a2 = [_dot_nt(jnp.concatenate([kb[gi], q[gi]], axis=0).astype(BF16), k[gi].astype(BF16)) for gi in groups]
    low = [jnp.where(strict, a2[gi][:gsz] * decay[gi], 0.0) for gi in groups]
    qk = [a2[gi][gsz:] * decay[gi] for gi in groups]
    inv = [eye - x for x in low]
    pw = [_mm_dn(x, x) for x in low]
    for _ in range(int(math.log2(c)) - 2):
        r = [_mm_dn(jnp.concatenate([pw[gi], inv[gi]], axis=0), pw[gi]) for gi in groups]
        inv = [inv[gi] + r[gi][gsz:] for gi in groups]
        pw = [r[gi][:gsz] for gi in groups]
    inv = [inv[gi] + _mm_dn(inv[gi], pw[gi]) for gi in groups]
    wu = [_mm_dn(inv[gi], rhs[gi]) for gi in groups]
    lasts, res = [], []
    for gi in groups:
        last = [gc[gi][(n + 1) * c - 1:(n + 1) * c, :] for n in range(cpg)]
        lasts.append(last)
        g_last_rows = jnp.concatenate([jnp.broadcast_to(x, (c, DN_DIM)) for x in last], axis=0)
        kd_t = jnp.transpose(k[gi] * jnp.exp(g_last_rows - gc[gi]))
        xs = [jnp.where(chunk_of_col == n, kd_t, 0.0) for n in range(cpg)]
        xs += [qk[gi][n * c:(n + 1) * c, :] for n in range(cpg)]
        res.append(jnp.dot(jnp.concatenate(xs, axis=0).astype(BF16), wu[gi].astype(BF16),
                           preferred_element_type=F32))
    pre = []
    for gi in groups:
        qd = q[gi] * eg[gi]
        for n in range(cpg):
            kw = res[gi][n * DN_DIM:(n + 1) * DN_DIM, :DN_DIM]
            ku = res[gi][n * DN_DIM:(n + 1) * DN_DIM, DN_DIM:]
            base = cpg * DN_DIM + n * c
            coef = qd[n * c:(n + 1) * c, :] - res[gi][base:base + c, :DN_DIM]
            qku = res[gi][base:base + c, DN_DIM:]
            lhs = jnp.concatenate([-kw, coef], axis=0).astype(BF16)
            pre.append((lhs, ku, qku, jnp.exp(lasts[gi][n])))

    states = [state_ref[hd] for hd in range(heads)]
    nw = nw_ref[...]
    for n in range(n_chunks):
        crow = slice(n * c, (n + 1) * c)
        for hd in range(heads):
            lhs, ku, qku, gl = pre[hd * n_chunks + n]
            r = jnp.dot(lhs, states[hd].astype(BF16), preferred_element_type=F32)
            o = r[DN_DIM:] + qku
            states[hd] = states[hd] * gl + r[:DN_DIM] + ku
            ccol = slice(hd * DN_DIM, (hd + 1) * DN_DIM)
            z = z_ref[crow, ccol]
            o_ref[crow, ccol] = (_rms_rows(o, nw) * (z * jax.nn.sigmoid(z))).astype(o_ref.dtype)
    for hd in range(heads):
        state_ref[hd] = states[hd]


def _deltanet(qkv, beta, g, proj, norm_w, bsz, seq, t=1024, heads=2):
    n = qkv.shape[0]
    spb = seq // t
    hw = heads * DN_DIM
    hsteps = DN_HEADS // heads
    zoff = OFF_DZ // hw
    col = lambda off: (lambda b, h, i: (b * spb + i, off + h))
    return pl.pallas_call(
        _deltanet_kernel,
        grid=(bsz, hsteps, spb),
        in_specs=[
            pl.BlockSpec((t, hw), col(0)),
            pl.BlockSpec((t, hw), col(hsteps)),
            pl.BlockSpec((t, hw), col(2 * hsteps)),
            pl.BlockSpec((t, hw), col(0)),
            pl.BlockSpec((t, hw), col(0)),
            pl.BlockSpec((t, hw), col(zoff)),
            pl.BlockSpec((1, DN_DIM), lambda b, h, i: (0, 0)),
        ],
        out_specs=pl.BlockSpec((t, hw), col(0)),
        out_shape=jax.ShapeDtypeStruct((n, DN_HEADS * DN_DIM), BF16),
        scratch_shapes=[pltpu.VMEM((heads, DN_DIM, DN_DIM), F32)],
        compiler_params=_cparams(("arbitrary", "arbitrary", "arbitrary")),
        name="deltanet",
    )(qkv, qkv, qkv, beta, g, proj, norm_w)


def _regroup_w_in(w_in):
    d = w_in.shape[0]
    sizes = (1024, 256, 256, 256, 256, 256, 256, 24, 1024, 1024, 1024, 1024, 8, 8, 2048, 2048)
    offs = np.concatenate([[0], np.cumsum(sizes)])
    (nq, kc, vc, ksel, vsel, kwin, vwin, ngate, dq, dk, dv, dz, db, da, mga, mgb) = (
        w_in[:, offs[i]:offs[i + 1]] for i in range(len(sizes)))
    zeros = lambda w: jnp.zeros((d, w), w_in.dtype)
    ngate = jnp.transpose(ngate.reshape(d, N_GROUPS, HPG, 3), (0, 1, 3, 2)).reshape(d, N_GROUPS, 3 * HPG)
    ngate = jnp.pad(ngate, ((0, 0), (0, 0), (0, GATE_ROWS - 3 * HPG))).reshape(d, N_GROUPS * GATE_ROWS)
    cols = [dq, dk, dv, nq, dz, mga, mgb, kc, vc, ksel, vsel, kwin, vwin,
            ngate, zeros(LANES - N_GROUPS * GATE_ROWS), db, da, zeros(LANES - 16), zeros(PROJ_DIM - PROJ_USED)]
    return jnp.concatenate(cols, axis=1).astype(BF16)


def _overlap_t(n_sb, n_pieces):
    n = np.arange(N_CMP_PAD) - CMP_PAD
    j = np.arange(n_sb)
    valid = (n >= 0) & (n < n_pieces - 1)
    c0 = n * CMP_STRIDE
    s0 = j * SEL_BLOCK
    ov = (c0[None, :] < s0[:, None] + SEL_BLOCK) & (c0[None, :] + CMP_LEN > s0[:, None]) & valid[None, :]
    return jnp.asarray(ov.astype(np.float32), dtype=BF16)


def _token_mixer(h, bsz, seq, norm_mix, w_in, q_norm, k_norm, cmp_pos, cmp_w1, cmp_b1, cmp_w2, dn_conv,
                 dn_a_log, dn_dt_bias, dn_norm, w_a, w_b, w_out, bias_tiles):
    n = bsz * seq
    nqb = seq // Q_BLOCK
    npc = seq // CMP_STRIDE
    n_sb = seq // SEL_BLOCK
    assert npc + CMP_PAD <= N_CMP_PAD and nqb * SUBLANES + NEAR_ROWS <= N_CMP_PAD
    assert nqb % FAR_TILES == 0
    proj = _norm_matmul(h, norm_mix.reshape(1, -1), _regroup_w_in(w_in))

    k_norm8 = jnp.concatenate([k_norm, jnp.zeros((SUBLANES - 3, HEAD_DIM), F32)], axis=0)
    qT, kc_raw, vc_raw, ks, vsT, kw, vwT, gT = _nsa_prep(proj, q_norm.reshape(1, -1), k_norm8, bsz, seq)
    pieces = jnp.stack([kc_raw, vc_raw], axis=0).reshape(2, bsz, N_GROUPS, npc, CMP_STRIDE * HEAD_DIM)
    cmp = _compress(pieces, cmp_w1, cmp_pos.reshape(2, 1, CMP_LEN * HEAD_DIM), cmp_b1.reshape(2, 1, -1),
                    cmp_w2, k_norm8)
    cmp = jnp.pad(cmp, ((0, 0), (0, 0), (0, 0), (CMP_PAD, N_CMP_PAD - CMP_PAD - npc), (0, 0))).astype(BF16)
    kc = cmp[0]
    vcT = jnp.swapaxes(cmp[1], -1, -2)
    y_nsa = _nsa_attn(qT, gT, ks, vsT, kw, vwT, kc, vcT, _overlap_t(n_sb, npc), *bias_tiles)

    lane_pad = lambda v, off: jnp.zeros((1, LANES), F32).at[0, off:off + DN_HEADS].set(v)
    qkv, beta, g = _dn_prep(proj, dn_conv, lane_pad(dn_a_log, DN_HEADS), lane_pad(dn_dt_bias, DN_HEADS), seq)
    y_dn = _deltanet(qkv, beta, g, proj, dn_norm.reshape(1, -1), bsz, seq)

    merged = _merge(y_nsa, y_dn, w_a.astype(BF16), w_b.astype(BF16), proj)
    return _out_proj(merged, w_out.astype(BF16), h)


def kernel(x, rel_table, norm_ffn1, ffn1_gate, ffn1_up, ffn1_down, norm_mix, w_in, q_norm, k_norm, cmp_pos, cmp_w1, cmp_b1, cmp_w2, dn_conv, dn_a_log, dn_dt_bias, dn_norm, w_branch_nsa, w_branch_dn, w_out, norm_ffn2, ffn2_gate, ffn2_up, ffn2_down):
    bsz, seq, d = x.shape
    depth = w_in.shape[0]
    h = x.reshape(bsz * seq, d)
    bias_tiles = _bias_tiles(rel_table)
    for l in range(depth):
        h = _ffn(h, norm_ffn1[l].reshape(1, -1), ffn1_gate[l].astype(BF16), ffn1_up[l].astype(BF16),
                 ffn1_down[l].astype(BF16))
        h = _token_mixer(h, bsz, seq, norm_mix[l], w_in[l], q_norm[l], k_norm[l], cmp_pos[l], cmp_w1[l],
                         cmp_b1[l], cmp_w2[l], dn_conv[l], dn_a_log[l], dn_dt_bias[l], dn_norm[l],
                         w_branch_nsa[l], w_branch_dn[l], w_out[l], bias_tiles)
        h = _ffn(h, norm_ffn2[l].reshape(1, -1), ffn2_gate[l].astype(BF16), ffn2_up[l].astype(BF16),
                 ffn2_down[l].astype(BF16))
    return h.reshape(bsz, seq, d)
```

```python
import functools
import math

import numpy as np
import jax
import jax.numpy as jnp
from jax import lax
from jax.experimental import pallas as pl
from jax.experimental.pallas import tpu as pltpu

F32 = jnp.float32
BF16 = jnp.bfloat16
HI = lax.Precision.HIGHEST

D_MODEL = 2048
D_FF = 5632
N_HEADS = 8
N_GROUPS = 2
HPG = N_HEADS // N_GROUPS
HEAD_DIM = 128
CMP_LEN = 32
CMP_STRIDE = 16
CMP_HIDDEN = 128
SEL_BLOCK = 64
SEL_TOPK = 16
WINDOW = 512
Q_BLOCK = 128
DN_HEADS = 8
DN_DIM = 128
DN_CONV = 4
DN_CHUNK = 64
DN_GROUP = 256
REL_BUCKETS = 32
REL_MAX_DIST = 128
EPS = 1e-6
LOG2E = math.log2(math.e)
NEG_INF = -1e30
BIG = 1e30

LANES = 128
SUBLANES = 8
VMEM_LIMIT = 56 * 1024 * 1024

OFF_MGA = 0
OFF_MGB = 2048
OFF_DQ, OFF_DK, OFF_DV = 4096, 5120, 6144
OFF_NQ = 7168
OFF_DZ = 8192
OFF_KV6 = 9216
OFF_NGATE = 10752
OFF_DBA = 10880
PROJ_USED = 11008
PROJ_DIM = 11264

CMP_PAD = 16
N_CMP_PAD = 1152
NEAR_ROWS = 24
NEAR_KEYS = 2 * Q_BLOCK
WIN_KEYS = WINDOW + Q_BLOCK
GATE_ROWS = 16
V_ROWS = HEAD_DIM + 16
FAR_SUB = 4
FAR_TILES = 2 * FAR_SUB


def _rel_bucket_thresholds():
    n = np.arange(0, 4 * REL_MAX_DIST, dtype=np.int64)
    max_exact = REL_BUCKETS // 2
    nf = np.maximum(n, max_exact).astype(np.float32)
    large = max_exact + (np.log(nf / np.float32(max_exact)) / np.float32(math.log(REL_MAX_DIST / max_exact))
                         * np.float32(REL_BUCKETS - max_exact)).astype(np.int32)
    bucket = np.where(n < max_exact, n, np.minimum(large, REL_BUCKETS - 1))
    assert np.all(np.diff(bucket) >= 0)
    return [int(np.argmax(bucket >= b)) for b in range(REL_BUCKETS)]


REL_THRESH = _rel_bucket_thresholds()
FAR_DIST = REL_THRESH[REL_BUCKETS - 1]
assert FAR_DIST <= Q_BLOCK


def _cparams(sem, vmem_limit=VMEM_LIMIT):
    return pltpu.CompilerParams(dimension_semantics=sem, vmem_limit_bytes=vmem_limit)


def _rms_rows(x, w_row):
    ms = jnp.mean(x * x, axis=-1, keepdims=True)
    return x * lax.rsqrt(ms + EPS) * w_row


def _ffn_kernel(x_ref, nw_ref, wg_ref, wu_ref, wd_ref, o_ref, xn_ref, acc_ref):
    j = pl.program_id(1)

    @pl.when(j == 0)
    def _():
        xn_ref[...] = _rms_rows(x_ref[...], nw_ref[...]).astype(BF16)
        acc_ref[...] = jnp.zeros_like(acc_ref)

    xn = xn_ref[...]
    g = jnp.dot(xn, wg_ref[...], preferred_element_type=F32)
    u = jnp.dot(xn, wu_ref[...], preferred_element_type=F32)
    a = (g * jax.nn.sigmoid(g) * u).astype(BF16)
    acc_ref[...] += jnp.dot(a, wd_ref[...], preferred_element_type=F32)

    @pl.when(j == pl.num_programs(1) - 1)
    def _():
        o_ref[...] = x_ref[...] + 0.5 * acc_ref[...]


def _ffn(h, nw, wg, wu, wd, tm=512, tf=512):
    n, d = h.shape
    f = wg.shape[1]
    return pl.pallas_call(
        _ffn_kernel,
        grid=(n // tm, f // tf),
        in_specs=[
            pl.BlockSpec((tm, d), lambda i, j: (i, 0)),
            pl.BlockSpec((1, d), lambda i, j: (0, 0)),
            pl.BlockSpec((d, tf), lambda i, j: (0, j)),
            pl.BlockSpec((d, tf), lambda i, j: (0, j)),
            pl.BlockSpec((tf, d), lambda i, j: (j, 0)),
        ],
        out_specs=pl.BlockSpec((tm, d), lambda i, j: (i, 0)),
        out_shape=jax.ShapeDtypeStruct((n, d), F32),
        scratch_shapes=[pltpu.VMEM((tm, d), BF16), pltpu.VMEM((tm, d), F32)],
        compiler_params=_cparams(("parallel", "arbitrary")),
        name="ffn",
    )(h, nw, wg, wu, wd)


def _norm_matmul_kernel(x_ref, nw_ref, w_ref, o_ref, xn_ref):
    @pl.when(pl.program_id(1) == 0)
    def _():
        xn_ref[...] = _rms_rows(x_ref[...], nw_ref[...]).astype(BF16)

    o_ref[...] = jnp.dot(xn_ref[...], w_ref[...], preferred_element_type=F32)


def _norm_matmul(h, nw, w, tm=1024, tn=1024):
    n, d = h.shape
    nout = w.shape[1]
    return pl.pallas_call(
        _norm_matmul_kernel,
        grid=(n // tm, nout // tn),
        in_specs=[
            pl.BlockSpec((tm, d), lambda i, j: (i, 0)),
            pl.BlockSpec((1, d), lambda i, j: (0, 0)),
            pl.BlockSpec((d, tn), lambda i, j: (0, j)),
        ],
        out_specs=pl.BlockSpec((tm, tn), lambda i, j: (i, j)),
        out_shape=jax.ShapeDtypeStruct((n, nout), F32),
        scratch_shapes=[pltpu.VMEM((tm, d), BF16)],
        compiler_params=_cparams(("parallel", "arbitrary")),
        name="in_proj",
    )(h, nw, w)


def _merge_out_kernel(ya_ref, yb_ref, wa_ref, wb_ref, ga_ref, gb_ref, wo_ref, h_ref, o_ref):
    a = jnp.dot(ya_ref[...], wa_ref[...], preferred_element_type=F32)
    b = jnp.dot(yb_ref[...], wb_ref[...], preferred_element_type=F32)
    merged = (jax.nn.sigmoid(ga_ref[...]) * a + jax.nn.sigmoid(gb_ref[...]) * b).astype(BF16)
    o_ref[...] = h_ref[...] + jnp.dot(merged, wo_ref[...], preferred_element_type=F32)


def _merge_out(ya, yb, wa, wb, proj, wo, h, tm=512):
    n, k = ya.shape
    d = wa.shape[1]
    resident = dict(pipeline_mode=pl.Buffered(1))
    return pl.pallas_call(
        _merge_out_kernel,
        grid=(n // tm,),
        in_specs=[
            pl.BlockSpec((tm, k), lambda i: (i, 0)),
            pl.BlockSpec((tm, k), lambda i: (i, 0)),
            pl.BlockSpec((k, d), lambda i: (0, 0), **resident),
            pl.BlockSpec((k, d), lambda i: (0, 0), **resident),
            pl.BlockSpec((tm, d), lambda i: (i, OFF_MGA // d)),
            pl.BlockSpec((tm, d), lambda i: (i, OFF_MGB // d)),
            pl.BlockSpec((d, d), lambda i: (0, 0), **resident),
            pl.BlockSpec((tm, d), lambda i: (i, 0)),
        ],
        out_specs=pl.BlockSpec((tm, d), lambda i: (i, 0)),
        out_shape=jax.ShapeDtypeStruct((n, d), F32),
        compiler_params=_cparams(("parallel",), vmem_limit=60000 * 1024),
        name="merge_out",
    )(ya, yb, wa, wb, proj, proj, wo, h)


def _nsa_prep_kernel(nq_ref, kv_ref, ng_ref, qn_ref, kn_ref,
                     qT_out, kc_out, vc_out, ks_out, vsT_out, kw_out, vwT_out, gT_out):
    tm = nq_ref.shape[0]
    scale = HEAD_DIM ** -0.5 * LOG2E
    for hh in range(N_HEADS):
        g, h = divmod(hh, HPG)
        qn = _rms_rows(nq_ref[:, hh * HEAD_DIM:(hh + 1) * HEAD_DIM], qn_ref[...]) * scale
        for qi in range(tm // Q_BLOCK):
            blk = qn[qi * Q_BLOCK:(qi + 1) * Q_BLOCK, :]
            qT_out[g, qi, :, h * Q_BLOCK:(h + 1) * Q_BLOCK] = jnp.transpose(blk).astype(BF16)
    ones_rows = jnp.where(lax.broadcasted_iota(jnp.int32, (V_ROWS - HEAD_DIM, tm), 0) == 0, 1.0, 0.0)
    for g in range(N_GROUPS):
        col = lambda kind: kv_ref[:, (kind * N_GROUPS + g) * HEAD_DIM:(kind * N_GROUPS + g + 1) * HEAD_DIM]
        kc_out[g] = col(0).astype(BF16)
        vc_out[g] = col(1).astype(BF16)
        ks_out[g] = _rms_rows(col(2), kn_ref[1:2, :]).astype(BF16)
        kw_out[g] = _rms_rows(col(4), kn_ref[2:3, :]).astype(BF16)
        for v_out, kind in ((vsT_out, 3), (vwT_out, 5)):
            v_out[g, 0:HEAD_DIM, :] = jnp.transpose(col(kind)).astype(BF16)
            v_out[g, HEAD_DIM:, :] = ones_rows.astype(BF16)
    sig = jax.nn.sigmoid(ng_ref[...])
    for qi in range(tm // Q_BLOCK):
        t = jnp.transpose(sig[qi * Q_BLOCK:(qi + 1) * Q_BLOCK, :])
        for g in range(N_GROUPS):
            gT_out[g, qi] = t[g * GATE_ROWS:(g + 1) * GATE_ROWS, :]


def _nsa_prep(proj, q_norm, k_norm, bsz, seq, tm=512):
    kvw = 6 * N_GROUPS * HEAD_DIM
    spb = seq // tm
    qpb = tm // Q_BLOCK
    nqb = seq // Q_BLOCK
    wide = HPG * Q_BLOCK
    keys = pl.BlockSpec((None, N_GROUPS, tm, HEAD_DIM), lambda i: (i // spb, 0, i % spb, 0))
    vals = pl.BlockSpec((None, N_GROUPS, V_ROWS, tm), lambda i: (i // spb, 0, 0, i % spb))
    keys_shape = jax.ShapeDtypeStruct((bsz, N_GROUPS, seq, HEAD_DIM), BF16)
    vals_shape = jax.ShapeDtypeStruct((bsz, N_GROUPS, V_ROWS, seq), BF16)
    return pl.pallas_call(
        _nsa_prep_kernel,
        grid=(bsz * spb,),
        in_specs=[
            pl.BlockSpec((tm, 1024), lambda i: (i, OFF_NQ // 1024)),
            pl.BlockSpec((tm, kvw), lambda i: (i, OFF_KV6 // kvw)),
            pl.BlockSpec((tm, LANES), lambda i: (i, OFF_NGATE // LANES)),
            pl.BlockSpec((1, HEAD_DIM), lambda i: (0, 0)),
            pl.BlockSpec((SUBLANES, HEAD_DIM), lambda i: (0, 0)),
        ],
        out_specs=[
            pl.BlockSpec((None, N_GROUPS, qpb, HEAD_DIM, wide), lambda i: (i // spb, 0, i % spb, 0, 0)),
            keys, keys, keys, vals, keys, vals,
            pl.BlockSpec((None, N_GROUPS, qpb, GATE_ROWS, Q_BLOCK), lambda i: (i // spb, 0, i % spb, 0, 0)),
        ],
        out_shape=[
            jax.ShapeDtypeStruct((bsz, N_GROUPS, nqb, HEAD_DIM, wide), BF16),
            keys_shape, keys_shape, keys_shape, vals_shape, keys_shape, vals_shape,
            jax.ShapeDtypeStruct((bsz, N_GROUPS, nqb, GATE_ROWS, Q_BLOCK), F32),
        ],
        compiler_params=_cparams(("parallel",)),
        name="nsa_prep",
    )(proj, proj, proj, q_norm, k_norm)


def _compress_kernel(p_ref, w1_ref, pos_ref, b1_ref, w2_ref, kn_ref, o_ref):
    half = CMP_STRIDE * HEAD_DIM
    p = p_ref[...]
    w1 = w1_ref[...]
    a = jnp.dot(p, w1[:half].astype(BF16), preferred_element_type=F32)
    b = jnp.dot(p, w1[half:].astype(BF16), preferred_element_type=F32)
    pos = jnp.broadcast_to(pos_ref[...], (SUBLANES, 2 * half))
    c0 = jnp.dot(pos, w1, preferred_element_type=F32, precision=HI)[0:1, :] + b1_ref[...]
    n = p.shape[0]
    hid = a + pltpu.roll(b, n - 1, 0) + c0
    hid = hid * jax.nn.sigmoid(hid)
    out = jnp.dot(hid.astype(BF16), w2_ref[...].astype(BF16), preferred_element_type=F32)
    normed = _rms_rows(out, kn_ref[0:1, :])
    out = jnp.where(pl.program_id(0) == 0, normed, out)
    row = lax.broadcasted_iota(jnp.int32, out.shape, 0)
    o_ref[...] = jnp.where(row < n - 1, out, 0.0)


def _compress(pieces, w1, pos, b1, w2, k_norm):
    _, bsz, ng, npc, width = pieces.shape
    return pl.pallas_call(
        _compress_kernel,
        grid=(2, bsz, ng),
        in_specs=[
            pl.BlockSpec((None, None, None, npc, width), lambda c, b, g: (c, b, g, 0, 0)),
            pl.BlockSpec((None, 2 * width, CMP_HIDDEN), lambda c, b, g: (c, 0, 0)),
            pl.BlockSpec((None, 1, 2 * width), lambda c, b, g: (c, 0, 0)),
            pl.BlockSpec((None, 1, CMP_HIDDEN), lambda c, b, g: (c, 0, 0)),
            pl.BlockSpec((None, CMP_HIDDEN, HEAD_DIM), lambda c, b, g: (c, 0, 0)),
            pl.BlockSpec((SUBLANES, HEAD_DIM), lambda c, b, g: (0, 0)),
        ],
        out_specs=pl.BlockSpec((None, None, None, npc, HEAD_DIM), lambda c, b, g: (c, b, g, 0, 0)),
        out_shape=jax.ShapeDtypeStruct((2, bsz, ng, npc, HEAD_DIM), F32),
        compiler_params=_cparams(("arbitrary", "arbitrary", "arbitrary")),
        name="compress",
    )(pieces, w1, pos, b1, w2, k_norm)


def _bias_kernel(tab_ref, bn_ref, bw_ref, bc_ref):
    g = pl.program_id(0)
    n_wt = WINDOW // Q_BLOCK

    def lookup(dist, head):
        v = jnp.full(dist.shape, tab_ref[0, head], F32)
        for b in range(1, REL_BUCKETS):
            v = jnp.where(dist >= REL_THRESH[b], tab_ref[b, head], v)
        return (v - tab_ref[REL_BUCKETS - 1, head]) * LOG2E

    ki = lax.broadcasted_iota(jnp.int32, (Q_BLOCK, Q_BLOCK), 0)
    qj = lax.broadcasted_iota(jnp.int32, (Q_BLOCK, Q_BLOCK), 1)
    r = lax.broadcasted_iota(jnp.int32, (NEAR_ROWS, Q_BLOCK), 0)
    qc = lax.broadcasted_iota(jnp.int32, (NEAR_ROWS, Q_BLOCK), 1)
    dist_c = qc - CMP_STRIDE * (r - CMP_PAD) - (CMP_LEN - 1)
    for h in range(HPG):
        head = g * HPG + h
        sl = slice(h * Q_BLOCK, (h + 1) * Q_BLOCK)
        diag = jnp.where(ki <= qj, lookup(qj - ki, head), NEG_INF)
        prev = lookup(qj - ki + Q_BLOCK, head)
        bn_ref[0:Q_BLOCK, sl] = prev
        bn_ref[Q_BLOCK:, sl] = diag
        bw_ref[0:Q_BLOCK, sl] = jnp.where(ki > qj, 0.0, NEG_INF)
        bw_ref[Q_BLOCK:(n_wt - 1) * Q_BLOCK, sl] = jnp.zeros(((n_wt - 2) * Q_BLOCK, Q_BLOCK), F32)
        bw_ref[(n_wt - 1) * Q_BLOCK:n_wt * Q_BLOCK, sl] = prev
        bw_ref[n_wt * Q_BLOCK:, sl] = diag
        bc_ref[:, sl] = jnp.where(dist_c >= 0, lookup(dist_c, head), NEG_INF)


def _bias_tiles(rel_table):
    wide = HPG * Q_BLOCK
    return pl.pallas_call(
        _bias_kernel,
        grid=(N_GROUPS,),
        in_specs=[pl.BlockSpec(memory_space=pltpu.SMEM)],
        out_specs=[
            pl.BlockSpec((None, NEAR_KEYS, wide), lambda g: (g, 0, 0)),
            pl.BlockSpec((None, WIN_KEYS, wide), lambda g: (g, 0, 0)),
            pl.BlockSpec((None, NEAR_ROWS, wide), lambda g: (g, 0, 0)),
        ],
        out_shape=[
            jax.ShapeDtypeStruct((N_GROUPS, NEAR_KEYS, wide), F32),
            jax.ShapeDtypeStruct((N_GROUPS, WIN_KEYS, wide), F32),
            jax.ShapeDtypeStruct((N_GROUPS, NEAR_ROWS, wide), F32),
        ],
        compiler_params=_cparams(("arbitrary",)),
        name="rel_bias",
    )(rel_table)


def _tile4(x):
    return jnp.concatenate([x] * HPG, axis=1)


def _nsa_attn_kernel(qT_ref, gT_ref, ks_ref, vsT_ref, kw_ref, vwT_ref, kc_ref, vcT_ref, ovT_ref,
                     bn_ref, bw_ref, bc_ref, o_ref, sc_ref, sel_ref, far_ref, sa_ref, sb_ref):
    qb = pl.program_id(2)
    qT = qT_ref[...]
    qpos = qb * Q_BLOCK + lax.broadcasted_iota(jnp.int32, (1, Q_BLOCK), 1)

    sc_ref[...] = jnp.dot(kc_ref[...], qT, preferred_element_type=F32)
    sc_ref[0:CMP_PAD, :] = jnp.full((CMP_PAD, HPG * Q_BLOCK), NEG_INF, F32)
    r0 = pl.multiple_of(qb * SUBLANES, SUBLANES)
    sc_ref[pl.ds(r0, NEAR_ROWS), :] += bc_ref[...]
    wide = HPG * Q_BLOCK
    rown = lax.broadcasted_iota(jnp.int32, (N_CMP_PAD, wide), 0)
    qpos4 = qb * Q_BLOCK + (lax.broadcasted_iota(jnp.int32, (1, wide), 1) & (Q_BLOCK - 1))
    s = jnp.where(rown < qb * SUBLANES + NEAR_ROWS, sc_ref[...], NEG_INF)
    m = jnp.max(s, axis=0, keepdims=True)
    p = jnp.exp2(s - m)
    l = jnp.sum(p, axis=0, keepdims=True)
    inv = jnp.where(qpos4 >= CMP_LEN - 1, 1.0 / l, 0.0)
    pn = p * inv
    o_c = jnp.dot(vcT_ref[...], pn.astype(BF16), preferred_element_type=F32)
    psum = pn[:, 0:Q_BLOCK]
    for h in range(1, HPG):
        psum = psum + pn[:, h * Q_BLOCK:(h + 1) * Q_BLOCK]
    p_hi = psum.astype(BF16)
    p_lo = (psum - p_hi.astype(F32)).astype(BF16)
    ovT = ovT_ref[...]
    imp = (jnp.dot(ovT, p_hi, preferred_element_type=F32)
           + jnp.dot(ovT, p_lo, preferred_element_type=F32))

    n_sb = imp.shape[0]
    jblk = lax.broadcasted_iota(jnp.int32, (n_sb, Q_BLOCK), 0)
    cur = qpos // SEL_BLOCK
    eligible = jblk * SEL_BLOCK <= qpos
    forced = (jblk == 0) | (jblk == cur) | (jblk == cur - 1)
    picked = -3e38
    score = jnp.where(eligible, jnp.where(forced, picked, imp), NEG_INF)
    for _ in range(max(min(SEL_TOPK, n_sb) - 3, 0)):
        top = jnp.max(score, axis=0, keepdims=True)
        idx = jnp.min(jnp.where(score == top, jblk, n_sb), axis=0, keepdims=True)
        idx = jnp.where(top > -1.0, idx, -1)
        score = jnp.where(jblk == idx, picked, score)
    selmask = jnp.where(score == picked, 0.0, NEG_INF)
    sel_ref[...] = selmask
    far_ref[...] = jnp.where(jblk < 2 * (qb - 1), selmask, NEG_INF)

    half = SEL_BLOCK

    def absent(cond):
        return jnp.where(cond, 0.0, NEG_INF)

    def sel_mask_tile(kt):
        a = jnp.broadcast_to(sel_ref[pl.ds(2 * kt, 1), :], (half, Q_BLOCK))
        b = jnp.broadcast_to(sel_ref[pl.ds(2 * kt + 1, 1), :], (half, Q_BLOCK))
        return _tile4(jnp.concatenate([a, b], axis=0))

    def key_rows(k_ref, kt, width=Q_BLOCK):
        return k_ref[pl.ds(pl.multiple_of(kt * width, width), width), :]

    def value_cols(vT_ref, kt, width=Q_BLOCK):
        return vT_ref[:, pl.ds(pl.multiple_of(kt * width, width), width)]

    def scores(k_ref, kt, width=Q_BLOCK):
        return jnp.dot(key_rows(k_ref, kt, width), qT, preferred_element_type=F32)

    def first(s, vT):
        m = jnp.max(s, axis=0, keepdims=True)
        p = jnp.exp2(s - m)
        return m, jnp.dot(vT, p.astype(BF16), preferred_element_type=F32)

    def update(state, s, vT_ref, kt, width=Q_BLOCK):
        m, acc = state
        m_new = jnp.maximum(m, jnp.max(s, axis=0, keepdims=True))
        alpha = jnp.exp2(m - m_new)
        p = jnp.exp2(s - m_new)
        acc = alpha * acc + jnp.dot(value_cols(vT_ref, kt, width), p.astype(BF16), preferred_element_type=F32)
        return m_new, acc

    def normalized(state):
        acc = state[1]
        return acc[:HEAD_DIM] * (1.0 / acc[HEAD_DIM:HEAD_DIM + 1])

    n_wt = WINDOW // Q_BLOCK
    tiles = [jnp.maximum(qb - back, 0) for back in range(n_wt, -1, -1)]
    kwin = jnp.concatenate([key_rows(kw_ref, kt) for kt in tiles], axis=0)
    sw = jnp.dot(kwin, qT, preferred_element_type=F32) + bw_ref[...]
    sw = jnp.concatenate(
        [sw[u * Q_BLOCK:(u + 1) * Q_BLOCK] + absent(qb >= n_wt - u) for u in range(n_wt)]
        + [sw[n_wt * Q_BLOCK:]], axis=0)
    o_w = normalized(first(sw, jnp.concatenate([value_cols(vwT_ref, kt) for kt in tiles], axis=1)))

    kp = jnp.maximum(qb - 1, 0)
    k2 = jnp.concatenate([key_rows(ks_ref, kp), key_rows(ks_ref, qb)], axis=0)
    near_mask = jnp.concatenate([sel_mask_tile(kp) + absent(qb >= 1), sel_mask_tile(qb)], axis=0)
    s = jnp.dot(k2, qT, preferred_element_type=F32) + bn_ref[...] + near_mask
    st = first(s, jnp.concatenate([value_cols(vsT_ref, kp), value_cols(vsT_ref, qb)], axis=1))


    sub_w = FAR_SUB * Q_BLOCK
    blocks_per_sub = sub_w // SEL_BLOCK

    def far_mask(cs):
        rows = [jnp.broadcast_to(far_ref[pl.ds(blocks_per_sub * cs + v, 1), :], (half, Q_BLOCK))
                for v in range(blocks_per_sub)]
        return _tile4(jnp.concatenate(rows, axis=0))

    last_chunk = ks_ref.shape[0] // sub_w - 1

    def far_body(i, st):
        c0 = 2 * i
        sb_ref[...] = scores(ks_ref, c0 + 1, sub_w)
        st = update(st, sa_ref[...] + far_mask(c0), vsT_ref, c0, sub_w)
        sa_ref[...] = scores(ks_ref, jnp.minimum(c0 + 2, last_chunk), sub_w)
        st = update(st, sb_ref[...] + far_mask(c0 + 1), vsT_ref, c0 + 1, sub_w)
        return st

    n_far = (jnp.maximum(qb - 1, 0) + FAR_TILES - 1) // FAR_TILES
    sa_ref[...] = scores(ks_ref, 0, sub_w)
    st = lax.fori_loop(0, n_far, far_body, st)
    o_s = normalized(st)

    for h in range(HPG):
        sl = slice(h * Q_BLOCK, (h + 1) * Q_BLOCK)
        gate = lambda c: gT_ref[c * HPG + h:c * HPG + h + 1, :]
        mix = gate(0) * o_c[:, sl] + gate(1) * o_s[:, sl] + gate(2) * o_w[:, sl]
        o_ref[:, h * HEAD_DIM:(h + 1) * HEAD_DIM] = jnp.transpose(mix).astype(o_ref.dtype)


def _nsa_attn(qT, gT, ks, vsT, kw, vwT, kc, vcT, ovT, bn, bw, bc):
    bsz, ng, nqb, _, wide = qT.shape
    seq = ks.shape[2]
    n_sb = ovT.shape[0]
    big = lambda b, g, q: (b, g, 0, 0)
    blk = lambda b, g, q: (b, g, q, 0, 0)
    grp = lambda b, g, q: (g, 0, 0)
    single = dict(pipeline_mode=pl.Buffered(1))
    return pl.pallas_call(
        _nsa_attn_kernel,
        grid=(bsz, ng, nqb),
        in_specs=[
            pl.BlockSpec((None, None, None, HEAD_DIM, wide), blk),
            pl.BlockSpec((None, None, None, GATE_ROWS, Q_BLOCK), blk),
            pl.BlockSpec((None, None, seq, HEAD_DIM), big, **single),
            pl.BlockSpec((None, None, V_ROWS, seq), big, **single),
            pl.BlockSpec((None, None, seq, HEAD_DIM), big, **single),
            pl.BlockSpec((None, None, V_ROWS, seq), big, **single),
            pl.BlockSpec((None, None, N_CMP_PAD, HEAD_DIM), big),
            pl.BlockSpec((None, None, HEAD_DIM, N_CMP_PAD), big),
            pl.BlockSpec((n_sb, N_CMP_PAD), lambda b, g, q: (0, 0)),
            pl.BlockSpec((None, NEAR_KEYS, wide), grp),
            pl.BlockSpec((None, WIN_KEYS, wide), grp),
            pl.BlockSpec((None, NEAR_ROWS, wide), grp),
        ],
        out_specs=pl.BlockSpec((Q_BLOCK, HPG * HEAD_DIM), lambda b, g, q: (b * nqb + q, g)),
        out_shape=jax.ShapeDtypeStruct((bsz * seq, ng * HPG * HEAD_DIM), BF16),
        scratch_shapes=[pltpu.VMEM((N_CMP_PAD, wide), F32), pltpu.VMEM((n_sb, Q_BLOCK), F32),
                        pltpu.VMEM((n_sb, Q_BLOCK), F32),
                        pltpu.VMEM((FAR_SUB * Q_BLOCK, wide), F32), pltpu.VMEM((FAR_SUB * Q_BLOCK, wide), F32)],
        compiler_params=_cparams(("arbitrary", "arbitrary", "arbitrary")),
        name="nsa_attn",
    )(qT, gT, ks, vsT, kw, vwT, kc, vcT, ovT, bn, bw, bc)


def _dot_exact01(a, b, left):
    x = b if left else a
    x1 = x.astype(BF16)
    r1 = x - x1.astype(F32)
    x2 = r1.astype(BF16)
    x3 = (r1 - x2.astype(F32)).astype(BF16)
    mm = (lambda p: jnp.dot(a, p, preferred_element_type=F32)) if left else (
        lambda p: jnp.dot(p, b, preferred_element_type=F32))
    return mm(x1) + mm(x2) + mm(x3)


def _dn_prep_kernel(x_ref, halo_ref, w_ref, dba_ref, alog_ref, dtb_ref, o_ref, beta_ref, g_ref, buf_ref,
                    *, blocks_per_seq):
    i = pl.program_id(0)
    j = pl.program_id(1)
    t = x_ref.shape[0]
    first_of_seq = (i % blocks_per_seq) == 0
    buf_ref[0:SUBLANES, :] = jnp.where(first_of_seq, 0.0, halo_ref[...])
    buf_ref[SUBLANES:, :] = x_ref[...]
    scale = jnp.where(j == 0, DN_DIM ** -0.5, 1.0)
    for h in range(DN_HEADS):
        sl = slice(h * DN_DIM, (h + 1) * DN_DIM)
        y = w_ref[DN_CONV - 1:DN_CONV, sl] * x_ref[:, sl]
        for back in range(1, DN_CONV):
            y = y + w_ref[DN_CONV - 1 - back:DN_CONV - back, sl] * buf_ref[pl.ds(SUBLANES - back, t), sl]
        y = y * jax.nn.sigmoid(y)
        nrm = y * lax.rsqrt(jnp.sum(y * y, axis=-1, keepdims=True) + EPS) * scale
        o_ref[:, sl] = jnp.where(j < 2, nrm, y)

    @pl.when(j == 0)
    def _():
        dba = dba_ref[...]
        beta = jax.nn.sigmoid(dba)
        z = dba + dtb_ref[...]
        softplus = jnp.maximum(z, 0.0) + jnp.log1p(jnp.exp(-jnp.abs(z)))
        g = -jnp.exp(alog_ref[...]) * softplus
        rt = lax.broadcasted_iota(jnp.int32, (t, t), 0)
        ct = lax.broadcasted_iota(jnp.int32, (t, t), 1)
        blocktri = jnp.where((rt >= ct) & (rt // DN_CHUNK == ct // DN_CHUNK), 1.0, 0.0).astype(BF16)
        g = _dot_exact01(blocktri, g, left=True)
        wide = DN_HEADS * DN_DIM
        src = lax.broadcasted_iota(jnp.int32, (LANES, wide), 0)
        head = lax.broadcasted_iota(jnp.int32, (LANES, wide), 1) // DN_DIM
        beta_ref[...] = _dot_exact01(beta, jnp.where(src == head, 1.0, 0.0).astype(BF16), left=False)
        g_ref[...] = _dot_exact01(g, jnp.where(src == head + DN_HEADS, 1.0, 0.0).astype(BF16), left=False)


def _dn_prep(proj, conv_w, alog_row, dtb_row, seq, t=512):
    n = proj.shape[0]
    width = DN_HEADS * DN_DIM
    hb = t // SUBLANES
    return pl.pallas_call(
        functools.partial(_dn_prep_kernel, blocks_per_seq=seq // t),
        grid=(n // t, 3),
        in_specs=[
            pl.BlockSpec((t, width), lambda i, j: (i, OFF_DQ // width + j)),
            pl.BlockSpec((SUBLANES, width), lambda i, j: (jnp.maximum(i * hb - 1, 0), OFF_DQ // width + j)),
            pl.BlockSpec((DN_CONV, width), lambda i, j: (0, j)),
            pl.BlockSpec((t, LANES), lambda i, j: (i, OFF_DBA // LANES)),
            pl.BlockSpec((1, LANES), lambda i, j: (0, 0)),
            pl.BlockSpec((1, LANES), lambda i, j: (0, 0)),
        ],
        out_specs=[
            pl.BlockSpec((t, width), lambda i, j: (i, j)),
            pl.BlockSpec((t, width), lambda i, j: (i, 0)),
            pl.BlockSpec((t, width), lambda i, j: (i, 0)),
        ],
        out_shape=[
            jax.ShapeDtypeStruct((n, 3 * width), F32),
            jax.ShapeDtypeStruct((n, width), F32),
            jax.ShapeDtypeStruct((n, width), F32),
        ],
        scratch_shapes=[pltpu.VMEM((t + SUBLANES, width), F32)],
        compiler_params=_cparams(("arbitrary", "arbitrary")),
        name="dn_prep",
    )(proj, proj, conv_w, proj, alog_row, dtb_row)


def _dot_nt(a, b, precision=None):
    return lax.dot_general(a, b, (((1,), (1,)), ((), ())), preferred_element_type=F32, precision=precision)


def _dot_tn(a, b, precision=None):
    return lax.dot_general(a, b, (((0,), (0,)), ((), ())), preferred_element_type=F32, precision=precision)


DN_MM_MODE = "bf16"


def _mm_dn(a, b):
    if DN_MM_MODE == "hi":
        return jnp.dot(a, b, preferred_element_type=F32, precision=HI)
    a_hi = a.astype(BF16)
    b_hi = b.astype(BF16)
    out = jnp.dot(a_hi, b_hi, preferred_element_type=F32)
    if DN_MM_MODE == "x3":
        a_lo = (a - a_hi.astype(F32)).astype(BF16)
        b_lo = (b - b_hi.astype(F32)).astype(BF16)
        out = out + jnp.dot(a_lo, b_hi, preferred_element_type=F32) + jnp.dot(a_hi, b_lo, preferred_element_type=F32)
    return out


def _deltanet_kernel(q_ref, k_ref, v_ref, beta_ref, g_ref, z_ref, nw_ref, o_ref, state_ref):
    c = DN_CHUNK
    t = q_ref.shape[0]
    n_chunks = t // c

    @pl.when(pl.program_id(2) == 0)
    def _():
        state_ref[...] = jnp.zeros_like(state_ref)

    gc_all = g_ref[...]
    beta_all = beta_ref[...]

    gsz = DN_GROUP
    cpg = gsz // c
    ri = lax.broadcasted_iota(jnp.int32, (gsz, gsz), 0)
    ci = lax.broadcasted_iota(jnp.int32, (gsz, gsz), 1)
    same = (ri // c) == (ci // c)
    causal = (ri >= ci) & same
    strict = (ri > ci) & same
    eye = jnp.where(ri == ci, 1.0, 0.0)
    chunk_of_col = lax.broadcasted_iota(jnp.int32, (DN_DIM, gsz), 1) // c

    heads = q_ref.shape[1] // DN_DIM
    groups = range(heads * (t // gsz))
    rows = [slice((p % (t // gsz)) * gsz, (p % (t // gsz) + 1) * gsz) for p in groups]
    cols = [slice((p // (t // gsz)) * DN_DIM, (p // (t // gsz) + 1) * DN_DIM) for p in groups]
    q = [q_ref[rows[p], cols[p]] for p in groups]
    k = [k_ref[rows[p], cols[p]] for p in groups]
    gc = [gc_all[rows[p], cols[p]] for p in groups]
    eg = [jnp.exp(x) for x in gc]
    beta = [beta_all[rows[p], cols[p]] for p in groups]
    kb = [k[gi] * beta[gi] for gi in groups]
    rhs = [jnp.concatenate([kb[gi] * eg[gi], v_ref[rows[gi], cols[gi]] * beta[gi]], axis=1) for gi in groups]
    decay = []
    for gi in groups:
        gc2 = jnp.concatenate([gc[gi]] * (gsz // DN_DIM), axis=1)
        gcr = jnp.sum(gc2 * eye, axis=0, keepdims=True)
        decay.append(jnp.exp(jnp.where(causal, gc2 - gcr, NEG_INF)))
    a2 = [_dot_nt(jnp.concatenate([kb[gi], q[gi]], axis=0).astype(BF16), k[gi].astype(BF16)) for gi in groups]
    low = [jnp.where(strict, a2[gi][:gsz] * decay[gi], 0.0) for gi in groups]
    qk = [a2[gi][gsz:] * decay[gi] for gi in groups]
    inv = [eye - x for x in low]
    pw = [_mm_dn(x, x) for x in low]
    for _ in range(int(math.log2(c)) - 2):
        r = [_mm_dn(jnp.concatenate([pw[gi], inv[gi]], axis=0), pw[gi]) for gi in groups]
        inv = [inv[gi] + r[gi][gsz:] for gi in groups]
        pw = [r[gi][:gsz] for gi in groups]
    inv = [inv[gi] + _mm_dn(inv[gi], pw[gi]) for gi in groups]
    wu = [_mm_dn(inv[gi], rhs[gi]) for gi in groups]
    lasts, res = [], []
    for gi in groups:
        last = [gc[gi][(n + 1) * c - 1:(n + 1) * c, :] for n in range(cpg)]
        lasts.append(last)
        g_last_rows = jnp.concatenate([jnp.broadcast_to(x, (c, DN_DIM)) for x in last], axis=0)
        kd_t = jnp.transpose(k[gi] * jnp.exp(g_last_rows - gc[gi]))
        xs = [jnp.where(chunk_of_col == n, kd_t, 0.0) for n in range(cpg)]
        xs += [qk[gi][n * c:(n + 1) * c, :] for n in range(cpg)]
        res.append(jnp.dot(jnp.concatenate(xs, axis=0).astype(BF16), wu[gi].astype(BF16),
                           preferred_element_type=F32))
    pre = []
    for gi in groups:
        qd = q[gi] * eg[gi]
        for n in range(cpg):
            kw = res[gi][n * DN_DIM:(n + 1) * DN_DIM, :DN_DIM]
            ku = res[gi][n * DN_DIM:(n + 1) * DN_DIM, DN_DIM:]
            base = cpg * DN_DIM + n * c
            coef = qd[n * c:(n + 1) * c, :] - res[gi][base:base + c, :DN_DIM]
            qku = res[gi][base:base + c, DN_DIM:]
            lhs = jnp.concatenate([-kw, coef], axis=0).astype(BF16)
            pre.append((lhs, ku, qku, jnp.exp(lasts[gi][n])))

    states = [state_ref[hd] for hd in range(heads)]
    nw = nw_ref[...]
    for n in range(n_chunks):
        crow = slice(n * c, (n + 1) * c)
        for hd in range(heads):
            lhs, ku, qku, gl = pre[hd * n_chunks + n]
            r = jnp.dot(lhs, states[hd].astype(BF16), preferred_element_type=F32)
            o = r[DN_DIM:] + qku
            states[hd] = states[hd] * gl + r[:DN_DIM] + ku
            ccol = slice(hd * DN_DIM, (hd + 1) * DN_DIM)
            z = z_ref[crow, ccol]
            o_ref[crow, ccol] = (_rms_rows(o, nw) * (z * jax.nn.sigmoid(z))).astype(o_ref.dtype)
    for hd in range(heads):
        state_ref[hd] = states[hd]


def _deltanet(qkv, beta, g, proj, norm_w, bsz, seq, t=1024, heads=2):
    n = qkv.shape[0]
    spb = seq // t
    hw = heads * DN_DIM
    hsteps = DN_HEADS // heads
    zoff = OFF_DZ // hw
    col = lambda off: (lambda b, h, i: (b * spb + i, off + h))
    return pl.pallas_call(
        _deltanet_kernel,
        grid=(bsz, hsteps, spb),
        in_specs=[
            pl.BlockSpec((t, hw), col(0)),
            pl.BlockSpec((t, hw), col(hsteps)),
            pl.BlockSpec((t, hw), col(2 * hsteps)),
            pl.BlockSpec((t, hw), col(0)),
            pl.BlockSpec((t, hw), col(0)),
            pl.BlockSpec((t, hw), col(zoff)),
            pl.BlockSpec((1, DN_DIM), lambda b, h, i: (0, 0)),
        ],
        out_specs=pl.BlockSpec((t, hw), col(0)),
        out_shape=jax.ShapeDtypeStruct((n, DN_HEADS * DN_DIM), BF16),
        scratch_shapes=[pltpu.VMEM((heads, DN_DIM, DN_DIM), F32)],
        compiler_params=_cparams(("arbitrary", "arbitrary", "arbitrary")),
        name="deltanet",
    )(qkv, qkv, qkv, beta, g, proj, norm_w)


def _regroup_w_in(w_in):
    d = w_in.shape[0]
    sizes = (1024, 256, 256, 256, 256, 256, 256, 24, 1024, 1024, 1024, 1024, 8, 8, 2048, 2048)
    offs = np.concatenate([[0], np.cumsum(sizes)])
    (nq, kc, vc, ksel, vsel, kwin, vwin, ngate, dq, dk, dv, dz, db, da, mga, mgb) = (
        w_in[:, offs[i]:offs[i + 1]] for i in range(len(sizes)))
    zeros = lambda w: jnp.zeros((d, w), w_in.dtype)
    ngate = jnp.transpose(ngate.reshape(d, N_GROUPS, HPG, 3), (0, 1, 3, 2)).reshape(d, N_GROUPS, 3 * HPG)
    ngate = jnp.pad(ngate, ((0, 0), (0, 0), (0, GATE_ROWS - 3 * HPG))).reshape(d, N_GROUPS * GATE_ROWS)
    cols = [mga, mgb, dq, dk, dv, nq, dz, kc, vc, ksel, vsel, kwin, vwin,
            ngate, zeros(LANES - N_GROUPS * GATE_ROWS), db, da, zeros(LANES - 16), zeros(PROJ_DIM - PROJ_USED)]
    return jnp.concatenate(cols, axis=1).astype(BF16)


def _overlap_t(n_sb, n_pieces):
    n = np.arange(N_CMP_PAD) - CMP_PAD
    j = np.arange(n_sb)
    valid = (n >= 0) & (n < n_pieces - 1)
    c0 = n * CMP_STRIDE
    s0 = j * SEL_BLOCK
    ov = (c0[None, :] < s0[:, None] + SEL_BLOCK) & (c0[None, :] + CMP_LEN > s0[:, None]) & valid[None, :]
    return jnp.asarray(ov.astype(np.float32), dtype=BF16)


def _token_mixer(h, bsz, seq, norm_mix, w_in, q_norm, k_norm, cmp_pos, cmp_w1, cmp_b1, cmp_w2, dn_conv,
                 dn_a_log, dn_dt_bias, dn_norm, w_a, w_b, w_out, bias_tiles):
    n = bsz * seq
    nqb = seq // Q_BLOCK
    npc = seq // CMP_STRIDE
    n_sb = seq // SEL_BLOCK
    assert npc + CMP_PAD <= N_CMP_PAD and nqb * SUBLANES + NEAR_ROWS <= N_CMP_PAD
    assert nqb % FAR_TILES == 0
    proj = _norm_matmul(h, norm_mix.reshape(1, -1), _regroup_w_in(w_in))

    k_norm8 = jnp.concatenate([k_norm, jnp.zeros((SUBLANES - 3, HEAD_DIM), F32)], axis=0)
    qT, kc_raw, vc_raw, ks, vsT, kw, vwT, gT = _nsa_prep(proj, q_norm.reshape(1, -1), k_norm8, bsz, seq)
    pieces = jnp.stack([kc_raw, vc_raw], axis=0).reshape(2, bsz, N_GROUPS, npc, CMP_STRIDE * HEAD_DIM)
    cmp = _compress(pieces, cmp_w1, cmp_pos.reshape(2, 1, CMP_LEN * HEAD_DIM), cmp_b1.reshape(2, 1, -1),
                    cmp_w2, k_norm8)
    cmp = jnp.pad(cmp, ((0, 0), (0, 0), (0, 0), (CMP_PAD, N_CMP_PAD - CMP_PAD - npc), (0, 0))).astype(BF16)
    kc = cmp[0]
    vcT = jnp.swapaxes(cmp[1], -1, -2)
    y_nsa = _nsa_attn(qT, gT, ks, vsT, kw, vwT, kc, vcT, _overlap_t(n_sb, npc), *bias_tiles)

    lane_pad = lambda v, off: jnp.zeros((1, LANES), F32).at[0, off:off + DN_HEADS].set(v)
    qkv, beta, g = _dn_prep(proj, dn_conv, lane_pad(dn_a_log, DN_HEADS), lane_pad(dn_dt_bias, DN_HEADS), seq)
    y_dn = _deltanet(qkv, beta, g, proj, dn_norm.reshape(1, -1), bsz, seq)

    return _merge_out(y_nsa, y_dn, w_a.astype(BF16), w_b.astype(BF16), proj, w_out.astype(BF16), h)


def kernel(x, rel_table, norm_ffn1, ffn1_gate, ffn1_up, ffn1_down, norm_mix, w_in, q_norm, k_norm, cmp_pos, cmp_w1, cmp_b1, cmp_w2, dn_conv, dn_a_log, dn_dt_bias, dn_norm, w_branch_nsa, w_branch_dn, w_out, norm_ffn2, ffn2_gate, ffn2_up, ffn2_down):
    bsz, seq, d = x.shape
    depth = w_in.shape[0]
    h = x.reshape(bsz * seq, d)
    bias_tiles = _bias_tiles(rel_table)
    for l in range(depth):
        h = _ffn(h, norm_ffn1[l].reshape(1, -1), ffn1_gate[l].astype(BF16), ffn1_up[l].astype(BF16),
                 ffn1_down[l].astype(BF16))
        h = _token_mixer(h, bsz, seq, norm_mix[l], w_in[l], q_norm[l], k_norm[l], cmp_pos[l], cmp_w1[l],
                         cmp_b1[l], cmp_w2[l], dn_conv[l], dn_a_log[l], dn_dt_bias[l], dn_norm[l],
                         w_branch_nsa[l], w_branch_dn[l], w_out[l], bias_tiles)
        h = _ffn(h, norm_ffn2[l].reshape(1, -1), ffn2_gate[l].astype(BF16), ffn2_up[l].astype(BF16),
                 ffn2_down[l].astype(BF16))
    return h.reshape(bsz, seq, d)
```

```python
import functools
import math

import numpy as np
import jax
import jax.numpy as jnp
from jax import lax
from jax.experimental import pallas as pl
from jax.experimental.pallas import tpu as pltpu

F32 = jnp.float32
BF16 = jnp.bfloat16
HI = lax.Precision.HIGHEST

D_MODEL = 2048
D_FF = 5632
N_HEADS = 8
N_GROUPS = 2
HPG = N_HEADS // N_GROUPS
HEAD_DIM = 128
CMP_LEN = 32
CMP_STRIDE = 16
CMP_HIDDEN = 128
SEL_BLOCK = 64
SEL_TOPK = 16
WINDOW = 512
Q_BLOCK = 128
DN_HEADS = 8
DN_DIM = 128
DN_CONV = 4
DN_CHUNK = 64
DN_GROUP = 256
REL_BUCKETS = 32
REL_MAX_DIST = 128
EPS = 1e-6
LOG2E = math.log2(math.e)
NEG_INF = -1e30
BIG = 1e30

LANES = 128
SUBLANES = 8
VMEM_LIMIT = 56 * 1024 * 1024

OFF_MGA = 0
OFF_MGB = 2048
OFF_DQ, OFF_DK, OFF_DV = 4096, 5120, 6144
OFF_NQ = 7168
OFF_DZ = 8192
OFF_KV6 = 9216
OFF_NGATE = 10752
OFF_DBA = 10880
PROJ_USED = 11008
PROJ_DIM = 11264

CMP_PAD = 16
N_CMP_PAD = 1152
NEAR_ROWS = 24
NEAR_KEYS = 2 * Q_BLOCK
WIN_KEYS = WINDOW + Q_BLOCK
GATE_ROWS = 16
V_ROWS = HEAD_DIM + 16
FAR_SUB = 4
FAR_TILES = 2 * FAR_SUB


def _rel_bucket_thresholds():
    n = np.arange(0, 4 * REL_MAX_DIST, dtype=np.int64)
    max_exact = REL_BUCKETS // 2
    nf = np.maximum(n, max_exact).astype(np.float32)
    large = max_exact + (np.log(nf / np.float32(max_exact)) / np.float32(math.log(REL_MAX_DIST / max_exact))
                         * np.float32(REL_BUCKETS - max_exact)).astype(np.int32)
    bucket = np.where(n < max_exact, n, np.minimum(large, REL_BUCKETS - 1))
    assert np.all(np.diff(bucket) >= 0)
    return [int(np.argmax(bucket >= b)) for b in range(REL_BUCKETS)]


REL_THRESH = _rel_bucket_thresholds()
FAR_DIST = REL_THRESH[REL_BUCKETS - 1]
assert FAR_DIST <= Q_BLOCK


def _cparams(sem, vmem_limit=VMEM_LIMIT):
    return pltpu.CompilerParams(dimension_semantics=sem, vmem_limit_bytes=vmem_limit)


def _rms_rows(x, w_row):
    ms = jnp.mean(x * x, axis=-1, keepdims=True)
    return x * lax.rsqrt(ms + EPS) * w_row


def _ffn_kernel(x_ref, nw_ref, wg_ref, wu_ref, wd_ref, o_ref, xn_ref, acc_ref):
    j = pl.program_id(1)

    @pl.when(j == 0)
    def _():
        xn_ref[...] = _rms_rows(x_ref[...], nw_ref[...]).astype(BF16)
        acc_ref[...] = jnp.zeros_like(acc_ref)

    xn = xn_ref[...]
    g = jnp.dot(xn, wg_ref[...], preferred_element_type=F32)
    u = jnp.dot(xn, wu_ref[...], preferred_element_type=F32)
    a = (g * jax.nn.sigmoid(g) * u).astype(BF16)
    acc_ref[...] += jnp.dot(a, wd_ref[...], preferred_element_type=F32)

    @pl.when(j == pl.num_programs(1) - 1)
    def _():
        o_ref[...] = x_ref[...] + 0.5 * acc_ref[...]


def _ffn(h, nw, wg, wu, wd, tm=512, tf=512):
    n, d = h.shape
    f = wg.shape[1]
    return pl.pallas_call(
        _ffn_kernel,
        grid=(n // tm, f // tf),
        in_specs=[
            pl.BlockSpec((tm, d), lambda i, j: (i, 0)),
            pl.BlockSpec((1, d), lambda i, j: (0, 0)),
            pl.BlockSpec((d, tf), lambda i, j: (0, j)),
            pl.BlockSpec((d, tf), lambda i, j: (0, j)),
            pl.BlockSpec((tf, d), lambda i, j: (j, 0)),
        ],
        out_specs=pl.BlockSpec((tm, d), lambda i, j: (i, 0)),
        out_shape=jax.ShapeDtypeStruct((n, d), F32),
        scratch_shapes=[pltpu.VMEM((tm, d), BF16), pltpu.VMEM((tm, d), F32)],
        compiler_params=_cparams(("parallel", "arbitrary")),
        name="ffn",
    )(h, nw, wg, wu, wd)


def _norm_matmul_kernel(x_ref, nw_ref, w_ref, o_ref, xn_ref):
    @pl.when(pl.program_id(1) == 0)
    def _():
        xn_ref[...] = _rms_rows(x_ref[...], nw_ref[...]).astype(BF16)

    o_ref[...] = jnp.dot(xn_ref[...], w_ref[...], preferred_element_type=F32)


def _norm_matmul(h, nw, w, tm=1024, tn=1024):
    n, d = h.shape
    nout = w.shape[1]
    return pl.pallas_call(
        _norm_matmul_kernel,
        grid=(n // tm, nout // tn),
        in_specs=[
            pl.BlockSpec((tm, d), lambda i, j: (i, 0)),
            pl.BlockSpec((1, d), lambda i, j: (0, 0)),
            pl.BlockSpec((d, tn), lambda i, j: (0, j)),
        ],
        out_specs=pl.BlockSpec((tm, tn), lambda i, j: (i, j)),
        out_shape=jax.ShapeDtypeStruct((n, nout), F32),
        scratch_shapes=[pltpu.VMEM((tm, d), BF16)],
        compiler_params=_cparams(("parallel", "arbitrary")),
        name="in_proj",
    )(h, nw, w)


def _merge_out_kernel(ya_ref, yb_ref, wa_ref, wb_ref, ga_ref, gb_ref, wo_ref, h_ref, o_ref):
    a = jnp.dot(ya_ref[...], wa_ref[...], preferred_element_type=F32)
    b = jnp.dot(yb_ref[...], wb_ref[...], preferred_element_type=F32)
    merged = (jax.nn.sigmoid(ga_ref[...]) * a + jax.nn.sigmoid(gb_ref[...]) * b).astype(BF16)
    o_ref[...] = h_ref[...] + jnp.dot(merged, wo_ref[...], preferred_element_type=F32)


def _merge_out(ya, yb, wa, wb, proj, wo, h, tm=512):
    n, k = ya.shape
    d = wa.shape[1]
    resident = dict(pipeline_mode=pl.Buffered(1))
    return pl.pallas_call(
        _merge_out_kernel,
        grid=(n // tm,),
        in_specs=[
            pl.BlockSpec((tm, k), lambda i: (i, 0)),
            pl.BlockSpec((tm, k), lambda i: (i, 0)),
            pl.BlockSpec((k, d), lambda i: (0, 0), **resident),
            pl.BlockSpec((k, d), lambda i: (0, 0), **resident),
            pl.BlockSpec((tm, d), lambda i: (i, OFF_MGA // d)),
            pl.BlockSpec((tm, d), lambda i: (i, OFF_MGB // d)),
            pl.BlockSpec((d, d), lambda i: (0, 0), **resident),
            pl.BlockSpec((tm, d), lambda i: (i, 0)),
        ],
        out_specs=pl.BlockSpec((tm, d), lambda i: (i, 0)),
        out_shape=jax.ShapeDtypeStruct((n, d), F32),
        compiler_params=_cparams(("parallel",), vmem_limit=60000 * 1024),
        name="merge_out",
    )(ya, yb, wa, wb, proj, proj, wo, h)


def _nsa_prep_kernel(nq_ref, kv_ref, ng_ref, qn_ref, kn_ref,
                     qT_out, kc_out, vc_out, ks_out, vsT_out, kw_out, vwT_out, gT_out):
    tm = nq_ref.shape[0]
    scale = HEAD_DIM ** -0.5 * LOG2E
    for hh in range(N_HEADS):
        g, h = divmod(hh, HPG)
        qn = _rms_rows(nq_ref[:, hh * HEAD_DIM:(hh + 1) * HEAD_DIM], qn_ref[...]) * scale
        for qi in range(tm // Q_BLOCK):
            blk = qn[qi * Q_BLOCK:(qi + 1) * Q_BLOCK, :]
            qT_out[g, qi, :, h * Q_BLOCK:(h + 1) * Q_BLOCK] = jnp.transpose(blk).astype(BF16)
    ones_rows = jnp.where(lax.broadcasted_iota(jnp.int32, (V_ROWS - HEAD_DIM, tm), 0) == 0, 1.0, 0.0)
    for g in range(N_GROUPS):
        col = lambda kind: kv_ref[:, (kind * N_GROUPS + g) * HEAD_DIM:(kind * N_GROUPS + g + 1) * HEAD_DIM]
        kc_out[g] = col(0).astype(BF16)
        vc_out[g] = col(1).astype(BF16)
        ks_out[g, :, 0:HEAD_DIM] = _rms_rows(col(2), kn_ref[1:2, :]).astype(BF16)
        blk_of_row = (lax.broadcasted_iota(jnp.int32, (tm, HEAD_DIM), 0) % (FAR_SUB * Q_BLOCK)) // SEL_BLOCK
        ks_out[g, :, HEAD_DIM:] = jnp.where(
            lax.broadcasted_iota(jnp.int32, (tm, HEAD_DIM), 1) == blk_of_row, 1.0, 0.0).astype(BF16)
        kw_out[g] = _rms_rows(col(4), kn_ref[2:3, :]).astype(BF16)
        for v_out, kind in ((vsT_out, 3), (vwT_out, 5)):
            v_out[g, 0:HEAD_DIM, :] = jnp.transpose(col(kind)).astype(BF16)
            v_out[g, HEAD_DIM:, :] = ones_rows.astype(BF16)
    sig = jax.nn.sigmoid(ng_ref[...])
    for qi in range(tm // Q_BLOCK):
        t = jnp.transpose(sig[qi * Q_BLOCK:(qi + 1) * Q_BLOCK, :])
        for g in range(N_GROUPS):
            gT_out[g, qi] = t[g * GATE_ROWS:(g + 1) * GATE_ROWS, :]


def _nsa_prep(proj, q_norm, k_norm, bsz, seq, tm=512):
    kvw = 6 * N_GROUPS * HEAD_DIM
    spb = seq // tm
    qpb = tm // Q_BLOCK
    nqb = seq // Q_BLOCK
    wide = HPG * Q_BLOCK
    keys = pl.BlockSpec((None, N_GROUPS, tm, HEAD_DIM), lambda i: (i // spb, 0, i % spb, 0))
    keys_aug = pl.BlockSpec((None, N_GROUPS, tm, 2 * HEAD_DIM), lambda i: (i // spb, 0, i % spb, 0))
    vals = pl.BlockSpec((None, N_GROUPS, V_ROWS, tm), lambda i: (i // spb, 0, 0, i % spb))
    keys_shape = jax.ShapeDtypeStruct((bsz, N_GROUPS, seq, HEAD_DIM), BF16)
    vals_shape = jax.ShapeDtypeStruct((bsz, N_GROUPS, V_ROWS, seq), BF16)
    return pl.pallas_call(
        _nsa_prep_kernel,
        grid=(bsz * spb,),
        in_specs=[
            pl.BlockSpec((tm, 1024), lambda i: (i, OFF_NQ // 1024)),
            pl.BlockSpec((tm, kvw), lambda i: (i, OFF_KV6 // kvw)),
            pl.BlockSpec((tm, LANES), lambda i: (i, OFF_NGATE // LANES)),
            pl.BlockSpec((1, HEAD_DIM), lambda i: (0, 0)),
            pl.BlockSpec((SUBLANES, HEAD_DIM), lambda i: (0, 0)),
        ],
        out_specs=[
            pl.BlockSpec((None, N_GROUPS, qpb, HEAD_DIM, wide), lambda i: (i // spb, 0, i % spb, 0, 0)),
            keys, keys, keys_aug, vals, keys, vals,
            pl.BlockSpec((None, N_GROUPS, qpb, GATE_ROWS, Q_BLOCK), lambda i: (i // spb, 0, i % spb, 0, 0)),
        ],
        out_shape=[
            jax.ShapeDtypeStruct((bsz, N_GROUPS, nqb, HEAD_DIM, wide), BF16),
            keys_shape, keys_shape, jax.ShapeDtypeStruct((bsz, N_GROUPS, seq, 2 * HEAD_DIM), BF16), vals_shape,
            keys_shape, vals_shape,
            jax.ShapeDtypeStruct((bsz, N_GROUPS, nqb, GATE_ROWS, Q_BLOCK), F32),
        ],
        compiler_params=_cparams(("parallel",)),
        name="nsa_prep",
    )(proj, proj, proj, q_norm, k_norm)


def _compress_kernel(p_ref, w1_ref, pos_ref, b1_ref, w2_ref, kn_ref, o_ref):
    half = CMP_STRIDE * HEAD_DIM
    p = p_ref[...]
    w1 = w1_ref[...]
    a = jnp.dot(p, w1[:half].astype(BF16), preferred_element_type=F32)
    b = jnp.dot(p, w1[half:].astype(BF16), preferred_element_type=F32)
    pos = jnp.broadcast_to(pos_ref[...], (SUBLANES, 2 * half))
    c0 = jnp.dot(pos, w1, preferred_element_type=F32, precision=HI)[0:1, :] + b1_ref[...]
    n = p.shape[0]
    hid = a + pltpu.roll(b, n - 1, 0) + c0
    hid = hid * jax.nn.sigmoid(hid)
    out = jnp.dot(hid.astype(BF16), w2_ref[...].astype(BF16), preferred_element_type=F32)
    normed = _rms_rows(out, kn_ref[0:1, :])
    out = jnp.where(pl.program_id(0) == 0, normed, out)
    row = lax.broadcasted_iota(jnp.int32, out.shape, 0)
    o_ref[...] = jnp.where(row < n - 1, out, 0.0)


def _compress(pieces, w1, pos, b1, w2, k_norm):
    _, bsz, ng, npc, width = pieces.shape
    return pl.pallas_call(
        _compress_kernel,
        grid=(2, bsz, ng),
        in_specs=[
            pl.BlockSpec((None, None, None, npc, width), lambda c, b, g: (c, b, g, 0, 0)),
            pl.BlockSpec((None, 2 * width, CMP_HIDDEN), lambda c, b, g: (c, 0, 0)),
            pl.BlockSpec((None, 1, 2 * width), lambda c, b, g: (c, 0, 0)),
            pl.BlockSpec((None, 1, CMP_HIDDEN), lambda c, b, g: (c, 0, 0)),
            pl.BlockSpec((None, CMP_HIDDEN, HEAD_DIM), lambda c, b, g: (c, 0, 0)),
            pl.BlockSpec((SUBLANES, HEAD_DIM), lambda c, b, g: (0, 0)),
        ],
        out_specs=pl.BlockSpec((None, None, None, npc, HEAD_DIM), lambda c, b, g: (c, b, g, 0, 0)),
        out_shape=jax.ShapeDtypeStruct((2, bsz, ng, npc, HEAD_DIM), F32),
        compiler_params=_cparams(("arbitrary", "arbitrary", "arbitrary")),
        name="compress",
    )(pieces, w1, pos, b1, w2, k_norm)


def _bias_kernel(tab_ref, bn_ref, bw_ref, bc_ref):
    g = pl.program_id(0)
    n_wt = WINDOW // Q_BLOCK

    def lookup(dist, head):
        v = jnp.full(dist.shape, tab_ref[0, head], F32)
        for b in range(1, REL_BUCKETS):
            v = jnp.where(dist >= REL_THRESH[b], tab_ref[b, head], v)
        return (v - tab_ref[REL_BUCKETS - 1, head]) * LOG2E

    ki = lax.broadcasted_iota(jnp.int32, (Q_BLOCK, Q_BLOCK), 0)
    qj = lax.broadcasted_iota(jnp.int32, (Q_BLOCK, Q_BLOCK), 1)
    r = lax.broadcasted_iota(jnp.int32, (NEAR_ROWS, Q_BLOCK), 0)
    qc = lax.broadcasted_iota(jnp.int32, (NEAR_ROWS, Q_BLOCK), 1)
    dist_c = qc - CMP_STRIDE * (r - CMP_PAD) - (CMP_LEN - 1)
    for h in range(HPG):
        head = g * HPG + h
        sl = slice(h * Q_BLOCK, (h + 1) * Q_BLOCK)
        diag = jnp.where(ki <= qj, lookup(qj - ki, head), NEG_INF)
        prev = lookup(qj - ki + Q_BLOCK, head)
        bn_ref[0:Q_BLOCK, sl] = prev
        bn_ref[Q_BLOCK:, sl] = diag
        bw_ref[0:Q_BLOCK, sl] = jnp.where(ki > qj, 0.0, NEG_INF)
        bw_ref[Q_BLOCK:(n_wt - 1) * Q_BLOCK, sl] = jnp.zeros(((n_wt - 2) * Q_BLOCK, Q_BLOCK), F32)
        bw_ref[(n_wt - 1) * Q_BLOCK:n_wt * Q_BLOCK, sl] = prev
        bw_ref[n_wt * Q_BLOCK:, sl] = diag
        bc_ref[:, sl] = jnp.where(dist_c >= 0, lookup(dist_c, head), NEG_INF)


def _bias_tiles(rel_table):
    wide = HPG * Q_BLOCK
    return pl.pallas_call(
        _bias_kernel,
        grid=(N_GROUPS,),
        in_specs=[pl.BlockSpec(memory_space=pltpu.SMEM)],
        out_specs=[
            pl.BlockSpec((None, NEAR_KEYS, wide), lambda g: (g, 0, 0)),
            pl.BlockSpec((None, WIN_KEYS, wide), lambda g: (g, 0, 0)),
            pl.BlockSpec((None, NEAR_ROWS, wide), lambda g: (g, 0, 0)),
        ],
        out_shape=[
            jax.ShapeDtypeStruct((N_GROUPS, NEAR_KEYS, wide), F32),
            jax.ShapeDtypeStruct((N_GROUPS, WIN_KEYS, wide), F32),
            jax.ShapeDtypeStruct((N_GROUPS, NEAR_ROWS, wide), F32),
        ],
        compiler_params=_cparams(("arbitrary",)),
        name="rel_bias",
    )(rel_table)


def _tile4(x):
    return jnp.concatenate([x] * HPG, axis=1)


def _nsa_attn_kernel(qT_ref, gT_ref, ks_ref, vsT_ref, kw_ref, vwT_ref, kc_ref, vcT_ref, ovT_ref,
                     bn_ref, bw_ref, bc_ref, o_ref, sc_ref, sel_ref, far_ref, sa_ref, sb_ref,
                     ra_ref, rb_ref):
    qb = pl.program_id(2)
    qT = qT_ref[...]
    qpos = qb * Q_BLOCK + lax.broadcasted_iota(jnp.int32, (1, Q_BLOCK), 1)

    sc_ref[...] = jnp.dot(kc_ref[...], qT, preferred_element_type=F32)
    sc_ref[0:CMP_PAD, :] = jnp.full((CMP_PAD, HPG * Q_BLOCK), NEG_INF, F32)
    r0 = pl.multiple_of(qb * SUBLANES, SUBLANES)
    sc_ref[pl.ds(r0, NEAR_ROWS), :] += bc_ref[...]
    wide = HPG * Q_BLOCK
    rown = lax.broadcasted_iota(jnp.int32, (N_CMP_PAD, wide), 0)
    qpos4 = qb * Q_BLOCK + (lax.broadcasted_iota(jnp.int32, (1, wide), 1) & (Q_BLOCK - 1))
    s = jnp.where(rown < qb * SUBLANES + NEAR_ROWS, sc_ref[...], NEG_INF)
    m = jnp.max(s, axis=0, keepdims=True)
    p = jnp.exp2(s - m)
    l = jnp.sum(p, axis=0, keepdims=True)
    inv = jnp.where(qpos4 >= CMP_LEN - 1, 1.0 / l, 0.0)
    pn = p * inv
    o_c = jnp.dot(vcT_ref[...], pn.astype(BF16), preferred_element_type=F32)
    psum = pn[:, 0:Q_BLOCK]
    for h in range(1, HPG):
        psum = psum + pn[:, h * Q_BLOCK:(h + 1) * Q_BLOCK]
    p_hi = psum.astype(BF16)
    p_lo = (psum - p_hi.astype(F32)).astype(BF16)
    ovT = ovT_ref[...]
    imp = (jnp.dot(ovT, p_hi, preferred_element_type=F32)
           + jnp.dot(ovT, p_lo, preferred_element_type=F32))

    n_sb = imp.shape[0]
    jblk = lax.broadcasted_iota(jnp.int32, (n_sb, Q_BLOCK), 0)
    cur = qpos // SEL_BLOCK
    eligible = jblk * SEL_BLOCK <= qpos
    forced = (jblk == 0) | (jblk == cur) | (jblk == cur - 1)
    picked = -3e38
    score = jnp.where(eligible, jnp.where(forced, picked, imp), NEG_INF)
    for _ in range(max(min(SEL_TOPK, n_sb) - 3, 0)):
        top = jnp.max(score, axis=0, keepdims=True)
        idx = jnp.min(jnp.where(score == top, jblk, n_sb), axis=0, keepdims=True)
        idx = jnp.where(top > -1.0, idx, -1)
        score = jnp.where(jblk == idx, picked, score)
    selmask = jnp.where(score == picked, 0.0, NEG_INF)
    sel_ref[...] = selmask
    far_ref[...] = jnp.where(jblk < 2 * (qb - 1), selmask, NEG_INF)

    half = SEL_BLOCK

    def absent(cond):
        return jnp.where(cond, 0.0, NEG_INF)

    def sel_mask_tile(kt):
        a = jnp.broadcast_to(sel_ref[pl.ds(2 * kt, 1), :], (half, Q_BLOCK))
        b = jnp.broadcast_to(sel_ref[pl.ds(2 * kt + 1, 1), :], (half, Q_BLOCK))
        return _tile4(jnp.concatenate([a, b], axis=0))

    def key_rows(k_ref, kt, width=Q_BLOCK):
        return k_ref[pl.ds(pl.multiple_of(kt * width, width), width), :]

    def value_cols(vT_ref, kt, width=Q_BLOCK):
        return vT_ref[:, pl.ds(pl.multiple_of(kt * width, width), width)]

    def first(s, vT):
        m = jnp.max(s, axis=0, keepdims=True)
        p = jnp.exp2(s - m)
        return m, jnp.dot(vT, p.astype(BF16), preferred_element_type=F32)

    def update(state, s, vT_ref, kt, width=Q_BLOCK):
        m, acc = state
        m_new = jnp.maximum(m, jnp.max(s, axis=0, keepdims=True))
        alpha = jnp.exp2(m - m_new)
        p = jnp.exp2(s - m_new)
        acc = alpha * acc + jnp.dot(value_cols(vT_ref, kt, width), p.astype(BF16), preferred_element_type=F32)
        return m_new, acc

    def normalized(state):
        acc = state[1]
        return acc[:HEAD_DIM] * (1.0 / acc[HEAD_DIM:HEAD_DIM + 1])

    n_wt = WINDOW // Q_BLOCK
    tiles = [jnp.maximum(qb - back, 0) for back in range(n_wt, -1, -1)]
    kwin = jnp.concatenate([key_rows(kw_ref, kt) for kt in tiles], axis=0)
    sw = jnp.dot(kwin, qT, preferred_element_type=F32) + bw_ref[...]
    sw = jnp.concatenate(
        [sw[u * Q_BLOCK:(u + 1) * Q_BLOCK] + absent(qb >= n_wt - u) for u in range(n_wt)]
        + [sw[n_wt * Q_BLOCK:]], axis=0)
    o_w = normalized(first(sw, jnp.concatenate([value_cols(vwT_ref, kt) for kt in tiles], axis=1)))

    kp = jnp.maximum(qb - 1, 0)
    k2 = jnp.concatenate([key_rows(ks_ref, kp), key_rows(ks_ref, qb)], axis=0)[:, 0:HEAD_DIM]
    near_mask = jnp.concatenate([sel_mask_tile(kp) + absent(qb >= 1), sel_mask_tile(qb)], axis=0)
    s = jnp.dot(k2, qT, preferred_element_type=F32) + bn_ref[...] + near_mask
    st = first(s, jnp.concatenate([value_cols(vsT_ref, kp), value_cols(vsT_ref, qb)], axis=1))


    sub_w = FAR_SUB * Q_BLOCK
    blocks_per_sub = sub_w // SEL_BLOCK

    assert blocks_per_sub <= SUBLANES
    for r_ref in (ra_ref, rb_ref):
        r_ref[0:HEAD_DIM, :] = qT
        r_ref[HEAD_DIM:, :] = jnp.zeros((HEAD_DIM, wide), BF16)
    zero_rows = jnp.zeros((2 * SUBLANES - blocks_per_sub, wide), F32)

    def far_scores(cs, r_ref):
        rows = far_ref[pl.ds(pl.multiple_of(blocks_per_sub * cs, blocks_per_sub), blocks_per_sub), :]
        r_ref[HEAD_DIM:HEAD_DIM + 2 * SUBLANES, :] = jnp.concatenate([_tile4(rows), zero_rows], axis=0).astype(BF16)
        k = ks_ref[pl.ds(pl.multiple_of(cs * sub_w, sub_w), sub_w), :]
        return jnp.dot(k, r_ref[...], preferred_element_type=F32)

    last_chunk = ks_ref.shape[0] // sub_w - 1

    def far_body(i, st):
        c0 = 2 * i
        sb_ref[...] = far_scores(c0 + 1, rb_ref)
        st = update(st, sa_ref[...], vsT_ref, c0, sub_w)
        sa_ref[...] = far_scores(jnp.minimum(c0 + 2, last_chunk), ra_ref)
        st = update(st, sb_ref[...], vsT_ref, c0 + 1, sub_w)
        return st

    n_far = (jnp.maximum(qb - 1, 0) + FAR_TILES - 1) // FAR_TILES
    sa_ref[...] = far_scores(0, ra_ref)
    st = lax.fori_loop(0, n_far, far_body, st)
    o_s = normalized(st)

    for h in range(HPG):
        sl = slice(h * Q_BLOCK, (h + 1) * Q_BLOCK)
        gate = lambda c: gT_ref[c * HPG + h:c * HPG + h + 1, :]
        mix = gate(0) * o_c[:, sl] + gate(1) * o_s[:, sl] + gate(2) * o_w[:, sl]
        o_ref[:, h * HEAD_DIM:(h + 1) * HEAD_DIM] = jnp.transpose(mix).astype(o_ref.dtype)


def _nsa_attn(qT, gT, ks, vsT, kw, vwT, kc, vcT, ovT, bn, bw, bc):
    bsz, ng, nqb, _, wide = qT.shape
    seq = ks.shape[2]
    n_sb = ovT.shape[0]
    big = lambda b, g, q: (b, g, 0, 0)
    blk = lambda b, g, q: (b, g, q, 0, 0)
    grp = lambda b, g, q: (g, 0, 0)
    single = dict(pipeline_mode=pl.Buffered(1))
    return pl.pallas_call(
        _nsa_attn_kernel,
        grid=(bsz, ng, nqb),
        in_specs=[
            pl.BlockSpec((None, None, None, HEAD_DIM, wide), blk),
            pl.BlockSpec((None, None, None, GATE_ROWS, Q_BLOCK), blk),
            pl.BlockSpec((None, None, seq, 2 * HEAD_DIM), big, **single),
            pl.BlockSpec((None, None, V_ROWS, seq), big, **single),
            pl.BlockSpec((None, None, seq, HEAD_DIM), big, **single),
            pl.BlockSpec((None, None, V_ROWS, seq), big, **single),
            pl.BlockSpec((None, None, N_CMP_PAD, HEAD_DIM), big),
            pl.BlockSpec((None, None, HEAD_DIM, N_CMP_PAD), big),
            pl.BlockSpec((n_sb, N_CMP_PAD), lambda b, g, q: (0, 0)),
            pl.BlockSpec((None, NEAR_KEYS, wide), grp),
            pl.BlockSpec((None, WIN_KEYS, wide), grp),
            pl.BlockSpec((None, NEAR_ROWS, wide), grp),
        ],
        out_specs=pl.BlockSpec((Q_BLOCK, HPG * HEAD_DIM), lambda b, g, q: (b * nqb + q, g)),
        out_shape=jax.ShapeDtypeStruct((bsz * seq, ng * HPG * HEAD_DIM), BF16),
        scratch_shapes=[pltpu.VMEM((N_CMP_PAD, wide), F32), pltpu.VMEM((n_sb, Q_BLOCK), F32),
                        pltpu.VMEM((n_sb, Q_BLOCK), F32),
                        pltpu.VMEM((FAR_SUB * Q_BLOCK, wide), F32), pltpu.VMEM((FAR_SUB * Q_BLOCK, wide), F32),
                        pltpu.VMEM((2 * HEAD_DIM, wide), BF16), pltpu.VMEM((2 * HEAD_DIM, wide), BF16)],
        compiler_params=_cparams(("arbitrary", "arbitrary", "arbitrary")),
        name="nsa_attn",
    )(qT, gT, ks, vsT, kw, vwT, kc, vcT, ovT, bn, bw, bc)


def _dot_exact01(a, b, left):
    x = b if left else a
    x1 = x.astype(BF16)
    r1 = x - x1.astype(F32)
    x2 = r1.astype(BF16)
    x3 = (r1 - x2.astype(F32)).astype(BF16)
    mm = (lambda p: jnp.dot(a, p, preferred_element_type=F32)) if left else (
        lambda p: jnp.dot(p, b, preferred_element_type=F32))
    return mm(x1) + mm(x2) + mm(x3)


def _dn_prep_kernel(x_ref, halo_ref, w_ref, dba_ref, alog_ref, dtb_ref, o_ref, beta_ref, g_ref, buf_ref,
                    *, blocks_per_seq):
    i = pl.program_id(0)
    j = pl.program_id(1)
    t = x_ref.shape[0]
    first_of_seq = (i % blocks_per_seq) == 0
    buf_ref[0:SUBLANES, :] = jnp.where(first_of_seq, 0.0, halo_ref[...])
    buf_ref[SUBLANES:, :] = x_ref[...]
    scale = jnp.where(j == 0, DN_DIM ** -0.5, 1.0)
    for h in range(DN_HEADS):
        sl = slice(h * DN_DIM, (h + 1) * DN_DIM)
        y = w_ref[DN_CONV - 1:DN_CONV, sl] * x_ref[:, sl]
        for back in range(1, DN_CONV):
            y = y + w_ref[DN_CONV - 1 - back:DN_CONV - back, sl] * buf_ref[pl.ds(SUBLANES - back, t), sl]
        y = y * jax.nn.sigmoid(y)
        nrm = y * lax.rsqrt(jnp.sum(y * y, axis=-1, keepdims=True) + EPS) * scale
        o_ref[:, sl] = jnp.where(j < 2, nrm, y)

    @pl.when(j == 0)
    def _():
        dba = dba_ref[...]
        beta = jax.nn.sigmoid(dba)
        z = dba + dtb_ref[...]
        softplus = jnp.maximum(z, 0.0) + jnp.log1p(jnp.exp(-jnp.abs(z)))
        g = -jnp.exp(alog_ref[...]) * softplus
        rt = lax.broadcasted_iota(jnp.int32, (t, t), 0)
        ct = lax.broadcasted_iota(jnp.int32, (t, t), 1)
        blocktri = jnp.where((rt >= ct) & (rt // DN_CHUNK == ct // DN_CHUNK), 1.0, 0.0).astype(BF16)
        g = _dot_exact01(blocktri, g, left=True)
        wide = DN_HEADS * DN_DIM
        src = lax.broadcasted_iota(jnp.int32, (LANES, wide), 0)
        head = lax.broadcasted_iota(jnp.int32, (LANES, wide), 1) // DN_DIM
        beta_ref[...] = _dot_exact01(beta, jnp.where(src == head, 1.0, 0.0).astype(BF16), left=False)
        g_ref[...] = _dot_exact01(g, jnp.where(src == head + DN_HEADS, 1.0, 0.0).astype(BF16), left=False)


def _dn_prep(proj, conv_w, alog_row, dtb_row, seq, t=512):
    n = proj.shape[0]
    width = DN_HEADS * DN_DIM
    hb = t // SUBLANES
    return pl.pallas_call(
        functools.partial(_dn_prep_kernel, blocks_per_seq=seq // t),
        grid=(n // t, 3),
        in_specs=[
            pl.BlockSpec((t, width), lambda i, j: (i, OFF_DQ // width + j)),
            pl.BlockSpec((SUBLANES, width), lambda i, j: (jnp.maximum(i * hb - 1, 0), OFF_DQ // width + j)),
            pl.BlockSpec((DN_CONV, width), lambda i, j: (0, j)),
            pl.BlockSpec((t, LANES), lambda i, j: (i, OFF_DBA // LANES)),
            pl.BlockSpec((1, LANES), lambda i, j: (0, 0)),
            pl.BlockSpec((1, LANES), lambda i, j: (0, 0)),
        ],
        out_specs=[
            pl.BlockSpec((t, width), lambda i, j: (i, j)),
            pl.BlockSpec((t, width), lambda i, j: (i, 0)),
            pl.BlockSpec((t, width), lambda i, j: (i, 0)),
        ],
        out_shape=[
            jax.ShapeDtypeStruct((n, 3 * width), F32),
            jax.ShapeDtypeStruct((n, width), F32),
            jax.ShapeDtypeStruct((n, width), F32),
        ],
        scratch_shapes=[pltpu.VMEM((t + SUBLANES, width), F32)],
        compiler_params=_cparams(("arbitrary", "arbitrary")),
        name="dn_prep",
    )(proj, proj, conv_w, proj, alog_row, dtb_row)


def _dot_nt(a, b, precision=None):
    return lax.dot_general(a, b, (((1,), (1,)), ((), ())), preferred_element_type=F32, precision=precision)


def _dot_tn(a, b, precision=None):
    return lax.dot_general(a, b, (((0,), (0,)), ((), ())), preferred_element_type=F32, precision=precision)


DN_MM_MODE = "bf16"


def _mm_dn(a, b):
    if DN_MM_MODE == "hi":
        return jnp.dot(a, b, preferred_element_type=F32, precision=HI)
    a_hi = a.astype(BF16)
    b_hi = b.astype(BF16)
    out = jnp.dot(a_hi, b_hi, preferred_element_type=F32)
    if DN_MM_MODE == "x3":
        a_lo = (a - a_hi.astype(F32)).astype(BF16)
        b_lo = (b - b_hi.astype(F32)).astype(BF16)
        out = out + jnp.dot(a_lo, b_hi, preferred_element_type=F32) + jnp.dot(a_hi, b_lo, preferred_element_type=F32)
    return out


def _deltanet_kernel(q_ref, k_ref, v_ref, beta_ref, g_ref, z_ref, nw_ref, o_ref, state_ref):
    c = DN_CHUNK
    t = q_ref.shape[0]
    n_chunks = t // c

    @pl.when(pl.program_id(2) == 0)
    def _():
        state_ref[...] = jnp.zeros_like(state_ref)

    gc_all = g_ref[...]
    beta_all = beta_ref[...]

    gsz = DN_GROUP
    cpg = gsz // c
    ri = lax.broadcasted_iota(jnp.int32, (gsz, gsz), 0)
    ci = lax.broadcasted_iota(jnp.int32, (gsz, gsz), 1)
    same = (ri // c) == (ci // c)
    causal = (ri >= ci) & same
    strict = (ri > ci) & same
    eye = jnp.where(ri == ci, 1.0, 0.0)
    chunk_of_col = lax.broadcasted_iota(jnp.int32, (DN_DIM, gsz), 1) // c

    heads = q_ref.shape[1] // DN_DIM
    groups = range(heads * (t // gsz))
    rows = [slice((p % (t // gsz)) * gsz, (p % (t // gsz) + 1) * gsz) for p in groups]
    cols = [slice((p // (t // gsz)) * DN_DIM, (p // (t // gsz) + 1) * DN_DIM) for p in groups]
    q = [q_ref[rows[p], cols[p]] for p in groups]
    k = [k_ref[rows[p], cols[p]] for p in groups]
    gc = [gc_all[rows[p], cols[p]] for p in groups]
    eg = [jnp.exp(x) for x in gc]
    beta = [beta_all[rows[p], cols[p]] for p in groups]
    kb = [k[gi] * beta[gi] for gi in groups]
    rhs = [jnp.concatenate([kb[gi] * eg[gi], v_ref[rows[gi], cols[gi]] * beta[gi]], axis=1) for gi in groups]
    decay = []
    for gi in groups:
        gc2 = jnp.concatenate([gc[gi]] * (gsz // DN_DIM), axis=1)
        gcr = jnp.sum(gc2 * eye, axis=0, keepdims=True)
        decay.append(jnp.exp(jnp.where(causal, gc2 - gcr, NEG_INF)))
    a2 = [_dot_nt(jnp.concatenate([kb[gi], q[gi]], axis=0).astype(BF16), k[gi].astype(BF16)) for gi in groups]
    low = [jnp.where(strict, a2[gi][:gsz] * decay[gi], 0.0) for gi in groups]
    qk = [a2[gi][gsz:] * decay[gi] for gi in groups]
    inv = [eye - x for x in low]
    pw = [_mm_dn(x, x) for x in low]
    for _ in range(int(math.log2(c)) - 2):
        r = [_mm_dn(jnp.concatenate([pw[gi], inv[gi]], axis=0), pw[gi]) for gi in groups]
        inv = [inv[gi] + r[gi][gsz:] for gi in groups]
        pw = [r[gi][:gsz] for gi in groups]
    inv = [inv[gi] + _mm_dn(inv[gi], pw[gi]) for gi in groups]
    wu = [_mm_dn(inv[gi], rhs[gi]) for gi in groups]
    lasts, res = [], []
    for gi in groups:
        last = [gc[gi][(n + 1) * c - 1:(n + 1) * c, :] for n in range(cpg)]
        lasts.append(last)
        g_last_rows = jnp.concatenate([jnp.broadcast_to(x, (c, DN_DIM)) for x in last], axis=0)
        kd_t = jnp.transpose(k[gi] * jnp.exp(g_last_rows - gc[gi]))
        xs = [jnp.where(chunk_of_col == n, kd_t, 0.0) for n in range(cpg)]
        xs += [qk[gi][n * c:(n + 1) * c, :] for n in range(cpg)]
        res.append(jnp.dot(jnp.concatenate(xs, axis=0).astype(BF16), wu[gi].astype(BF16),
                           preferred_element_type=F32))
    pre = []
    for gi in groups:
        qd = q[gi] * eg[gi]
        for n in range(cpg):
            kw = res[gi][n * DN_DIM:(n + 1) * DN_DIM, :DN_DIM]
            ku = res[gi][n * DN_DIM:(n + 1) * DN_DIM, DN_DIM:]
            base = cpg * DN_DIM + n * c
            coef = qd[n * c:(n + 1) * c, :] - res[gi][base:base + c, :DN_DIM]
            qku = res[gi][base:base + c, DN_DIM:]
            lhs = jnp.concatenate([-kw, coef], axis=0).astype(BF16)
            pre.append((lhs, ku, qku, jnp.exp(lasts[gi][n])))

    states = [state_ref[hd] for hd in range(heads)]
    nw = nw_ref[...]
    for n in range(n_chunks):
        crow = slice(n * c, (n + 1) * c)
        for hd in range(heads):
            lhs, ku, qku, gl = pre[hd * n_chunks + n]
            r = jnp.dot(lhs, states[hd].astype(BF16), preferred_element_type=F32)
            o = r[DN_DIM:] + qku
            states[hd] = states[hd] * gl + r[:DN_DIM] + ku
            ccol = slice(hd * DN_DIM, (hd + 1) * DN_DIM)
            z = z_ref[crow, ccol]
            o_ref[crow, ccol] = (_rms_rows(o, nw) * (z * jax.nn.sigmoid(z))).astype(o_ref.dtype)
    for hd in range(heads):
        state_ref[hd] = states[hd]


def _deltanet(qkv, beta, g, proj, norm_w, bsz, seq, t=1024, heads=2):
    n = qkv.shape[0]
    spb = seq // t
    hw = heads * DN_DIM
    hsteps = DN_HEADS // heads
    zoff = OFF_DZ // hw
    col = lambda off: (lambda b, h, i: (b * spb + i, off + h))
    return pl.pallas_call(
        _deltanet_kernel,
        grid=(bsz, hsteps, spb),
        in_specs=[
            pl.BlockSpec((t, hw), col(0)),
            pl.BlockSpec((t, hw), col(hsteps)),
            pl.BlockSpec((t, hw), col(2 * hsteps)),
            pl.BlockSpec((t, hw), col(0)),
            pl.BlockSpec((t, hw), col(0)),
            pl.BlockSpec((t, hw), col(zoff)),
            pl.BlockSpec((1, DN_DIM), lambda b, h, i: (0, 0)),
        ],
        out_specs=pl.BlockSpec((t, hw), col(0)),
        out_shape=jax.ShapeDtypeStruct((n, DN_HEADS * DN_DIM), BF16),
        scratch_shapes=[pltpu.VMEM((heads, DN_DIM, DN_DIM), F32)],
        compiler_params=_cparams(("arbitrary", "arbitrary", "arbitrary")),
        name="deltanet",
    )(qkv, qkv, qkv, beta, g, proj, norm_w)


def _regroup_w_in(w_in):
    d = w_in.shape[0]
    sizes = (1024, 256, 256, 256, 256, 256, 256, 24, 1024, 1024, 1024, 1024, 8, 8, 2048, 2048)
    offs = np.concatenate([[0], np.cumsum(sizes)])
    (nq, kc, vc, ksel, vsel, kwin, vwin, ngate, dq, dk, dv, dz, db, da, mga, mgb) = (
        w_in[:, offs[i]:offs[i + 1]] for i in range(len(sizes)))
    zeros = lambda w: jnp.zeros((d, w), w_in.dtype)
    ngate = jnp.transpose(ngate.reshape(d, N_GROUPS, HPG, 3), (0, 1, 3, 2)).reshape(d, N_GROUPS, 3 * HPG)
    ngate = jnp.pad(ngate, ((0, 0), (0, 0), (0, GATE_ROWS - 3 * HPG))).reshape(d, N_GROUPS * GATE_ROWS)
    cols = [mga, mgb, dq, dk, dv, nq, dz, kc, vc, ksel, vsel, kwin, vwin,
            ngate, zeros(LANES - N_GROUPS * GATE_ROWS), db, da, zeros(LANES - 16), zeros(PROJ_DIM - PROJ_USED)]
    return jnp.concatenate(cols, axis=1).astype(BF16)


def _overlap_t(n_sb, n_pieces):
    n = np.arange(N_CMP_PAD) - CMP_PAD
    j = np.arange(n_sb)
    valid = (n >= 0) & (n < n_pieces - 1)
    c0 = n * CMP_STRIDE
    s0 = j * SEL_BLOCK
    ov = (c0[None, :] < s0[:, None] + SEL_BLOCK) & (c0[None, :] + CMP_LEN > s0[:, None]) & valid[None, :]
    return jnp.asarray(ov.astype(np.float32), dtype=BF16)


def _token_mixer(h, bsz, seq, norm_mix, w_in, q_norm, k_norm, cmp_pos, cmp_w1, cmp_b1, cmp_w2, dn_conv,
                 dn_a_log, dn_dt_bias, dn_norm, w_a, w_b, w_out, bias_tiles):
    n = bsz * seq
    nqb = seq // Q_BLOCK
    npc = seq // CMP_STRIDE
    n_sb = seq // SEL_BLOCK
    assert npc + CMP_PAD <= N_CMP_PAD and nqb * SUBLANES + NEAR_ROWS <= N_CMP_PAD
    assert nqb % FAR_TILES == 0
    proj = _norm_matmul(h, norm_mix.reshape(1, -1), _regroup_w_in(w_in))

    k_norm8 = jnp.concatenate([k_norm, jnp.zeros((SUBLANES - 3, HEAD_DIM), F32)], axis=0)
    qT, kc_raw, vc_raw, ks, vsT, kw, vwT, gT = _nsa_prep(proj, q_norm.reshape(1, -1), k_norm8, bsz, seq)
    pieces = jnp.stack([kc_raw, vc_raw], axis=0).reshape(2, bsz, N_GROUPS, npc, CMP_STRIDE * HEAD_DIM)
    cmp = _compress(pieces, cmp_w1, cmp_pos.reshape(2, 1, CMP_LEN * HEAD_DIM), cmp_b1.reshape(2, 1, -1),
                    cmp_w2, k_norm8)
    cmp = jnp.pad(cmp, ((0, 0), (0, 0), (0, 0), (CMP_PAD, N_CMP_PAD - CMP_PAD - npc), (0, 0))).astype(BF16)
    kc = cmp[0]
    vcT = jnp.swapaxes(cmp[1], -1, -2)
    y_nsa = _nsa_attn(qT, gT, ks, vsT, kw, vwT, kc, vcT, _overlap_t(n_sb, npc), *bias_tiles)

    lane_pad = lambda v, off: jnp.zeros((1, LANES), F32).at[0, off:off + DN_HEADS].set(v)
    qkv, beta, g = _dn_prep(proj, dn_conv, lane_pad(dn_a_log, DN_HEADS), lane_pad(dn_dt_bias, DN_HEADS), seq)
    y_dn = _deltanet(qkv, beta, g, proj, dn_norm.reshape(1, -1), bsz, seq)

    return _merge_out(y_nsa, y_dn, w_a.astype(BF16), w_b.astype(BF16), proj, w_out.astype(BF16), h)


def kernel(x, rel_table, norm_ffn1, ffn1_gate, ffn1_up, ffn1_down, norm_mix, w_in, q_norm, k_norm, cmp_pos, cmp_w1, cmp_b1, cmp_w2, dn_conv, dn_a_log, dn_dt_bias, dn_norm, w_branch_nsa, w_branch_dn, w_out, norm_ffn2, ffn2_gate, ffn2_up, ffn2_down):
    bsz, seq, d = x.shape
    depth = w_in.shape[0]
    h = x.reshape(bsz * seq, d)
    bias_tiles = _bias_tiles(rel_table)
    for l in range(depth):
        h = _ffn(h, norm_ffn1[l].reshape(1, -1), ffn1_gate[l].astype(BF16), ffn1_up[l].astype(BF16),
                 ffn1_down[l].astype(BF16))
        h = _token_mixer(h, bsz, seq, norm_mix[l], w_in[l], q_norm[l], k_norm[l], cmp_pos[l], cmp_w1[l],
                         cmp_b1[l], cmp_w2[l], dn_conv[l], dn_a_log[l], dn_dt_bias[l], dn_norm[l],
                         w_branch_nsa[l], w_branch_dn[l], w_out[l], bias_tiles)
        h = _ffn(h, norm_ffn2[l].reshape(1, -1), ffn2_gate[l].astype(BF16), ffn2_up[l].astype(BF16),
                 ffn2_down[l].astype(BF16))
    return h.reshape(bsz, seq, d)
```

```python
import functools
import math

import numpy as np
import jax
import jax.numpy as jnp
from jax import lax
from jax.experimental import pallas as pl
from jax.experimental.pallas import tpu as pltpu

F32 = jnp.float32
BF16 = jnp.bfloat16
HI = lax.Precision.HIGHEST

D_MODEL = 2048
D_FF = 5632
N_HEADS = 8
N_GROUPS = 2
HPG = N_HEADS // N_GROUPS
HEAD_DIM = 128
CMP_LEN = 32
CMP_STRIDE = 16
CMP_HIDDEN = 128
SEL_BLOCK = 64
SEL_TOPK = 16
WINDOW = 512
Q_BLOCK = 128
DN_HEADS = 8
DN_DIM = 128
DN_CONV = 4
DN_CHUNK = 64
DN_GROUP = 256
REL_BUCKETS = 32
REL_MAX_DIST = 128
EPS = 1e-6
LOG2E = math.log2(math.e)
NEG_INF = -1e30
BIG = 1e30

LANES = 128
SUBLANES = 8
VMEM_LIMIT = 56 * 1024 * 1024

OFF_MGA = 0
OFF_MGB = 2048
OFF_DQ, OFF_DK, OFF_DV = 4096, 5120, 6144
OFF_NQ = 7168
OFF_DZ = 8192
OFF_KV6 = 9216
OFF_NGATE = 10752
OFF_DBA = 10880
PROJ_USED = 11008
PROJ_DIM = 11264

CMP_PAD = 16
N_CMP_PAD = 1152
NEAR_ROWS = 24
NEAR_KEYS = 2 * Q_BLOCK
WIN_KEYS = WINDOW + Q_BLOCK
GATE_ROWS = 16
V_ROWS = HEAD_DIM + 16
FAR_SUB = 4
FAR_TILES = 2 * FAR_SUB


def _rel_bucket_thresholds():
    n = np.arange(0, 4 * REL_MAX_DIST, dtype=np.int64)
    max_exact = REL_BUCKETS // 2
    nf = np.maximum(n, max_exact).astype(np.float32)
    large = max_exact + (np.log(nf / np.float32(max_exact)) / np.float32(math.log(REL_MAX_DIST / max_exact))
                         * np.float32(REL_BUCKETS - max_exact)).astype(np.int32)
    bucket = np.where(n < max_exact, n, np.minimum(large, REL_BUCKETS - 1))
    assert np.all(np.diff(bucket) >= 0)
    return [int(np.argmax(bucket >= b)) for b in range(REL_BUCKETS)]


REL_THRESH = _rel_bucket_thresholds()
FAR_DIST = REL_THRESH[REL_BUCKETS - 1]
assert FAR_DIST <= Q_BLOCK


def _cparams(sem, vmem_limit=VMEM_LIMIT):
    return pltpu.CompilerParams(dimension_semantics=sem, vmem_limit_bytes=vmem_limit)


def _rms_rows(x, w_row):
    ms = jnp.mean(x * x, axis=-1, keepdims=True)
    return x * lax.rsqrt(ms + EPS) * w_row


def _ffn_kernel(x_ref, nw_ref, wg_ref, wu_ref, wd_ref, o_ref, xn_ref, acc_ref):
    j = pl.program_id(1)

    @pl.when(j == 0)
    def _():
        xn_ref[...] = _rms_rows(x_ref[...], nw_ref[...]).astype(BF16)
        acc_ref[...] = jnp.zeros_like(acc_ref)

    xn = xn_ref[...]
    g = jnp.dot(xn, wg_ref[...], preferred_element_type=F32)
    u = jnp.dot(xn, wu_ref[...], preferred_element_type=F32)
    a = (g * jax.nn.sigmoid(g) * u).astype(BF16)
    acc_ref[...] += jnp.dot(a, wd_ref[...], preferred_element_type=F32)

    @pl.when(j == pl.num_programs(1) - 1)
    def _():
        o_ref[...] = x_ref[...] + 0.5 * acc_ref[...]


def _ffn(h, nw, wg, wu, wd, tm=512, tf=512):
    n, d = h.shape
    f = wg.shape[1]
    return pl.pallas_call(
        _ffn_kernel,
        grid=(n // tm, f // tf),
        in_specs=[
            pl.BlockSpec((tm, d), lambda i, j: (i, 0)),
            pl.BlockSpec((1, d), lambda i, j: (0, 0)),
            pl.BlockSpec((d, tf), lambda i, j: (0, j)),
            pl.BlockSpec((d, tf), lambda i, j: (0, j)),
            pl.BlockSpec((tf, d), lambda i, j: (j, 0)),
        ],
        out_specs=pl.BlockSpec((tm, d), lambda i, j: (i, 0)),
        out_shape=jax.ShapeDtypeStruct((n, d), F32),
        scratch_shapes=[pltpu.VMEM((tm, d), BF16), pltpu.VMEM((tm, d), F32)],
        compiler_params=_cparams(("parallel", "arbitrary")),
        name="ffn",
    )(h, nw, wg, wu, wd)


def _norm_matmul_kernel(x_ref, nw_ref, w_ref, o_ref, xn_ref):
    @pl.when(pl.program_id(1) == 0)
    def _():
        xn_ref[...] = _rms_rows(x_ref[...], nw_ref[...]).astype(BF16)

    o_ref[...] = jnp.dot(xn_ref[...], w_ref[...], preferred_element_type=F32)


def _norm_matmul(h, nw, w, tm=1024, tn=1024):
    n, d = h.shape
    nout = w.shape[1]
    return pl.pallas_call(
        _norm_matmul_kernel,
        grid=(n // tm, nout // tn),
        in_specs=[
            pl.BlockSpec((tm, d), lambda i, j: (i, 0)),
            pl.BlockSpec((1, d), lambda i, j: (0, 0)),
            pl.BlockSpec((d, tn), lambda i, j: (0, j)),
        ],
        out_specs=pl.BlockSpec((tm, tn), lambda i, j: (i, j)),
        out_shape=jax.ShapeDtypeStruct((n, nout), F32),
        scratch_shapes=[pltpu.VMEM((tm, d), BF16)],
        compiler_params=_cparams(("parallel", "arbitrary")),
        name="in_proj",
    )(h, nw, w)


def _merge_out_kernel(ya_ref, yb_ref, wa_ref, wb_ref, ga_ref, gb_ref, wo_ref, h_ref, o_ref):
    a = jnp.dot(ya_ref[...], wa_ref[...], preferred_element_type=F32)
    b = jnp.dot(yb_ref[...], wb_ref[...], preferred_element_type=F32)
    merged = (jax.nn.sigmoid(ga_ref[...]) * a + jax.nn.sigmoid(gb_ref[...]) * b).astype(BF16)
    o_ref[...] = h_ref[...] + jnp.dot(merged, wo_ref[...], preferred_element_type=F32)


def _merge_out(ya, yb, wa, wb, proj, wo, h, tm=512):
    n, k = ya.shape
    d = wa.shape[1]
    resident = dict(pipeline_mode=pl.Buffered(1))
    return pl.pallas_call(
        _merge_out_kernel,
        grid=(n // tm,),
        in_specs=[
            pl.BlockSpec((tm, k), lambda i: (i, 0)),
            pl.BlockSpec((tm, k), lambda i: (i, 0)),
            pl.BlockSpec((k, d), lambda i: (0, 0), **resident),
            pl.BlockSpec((k, d), lambda i: (0, 0), **resident),
            pl.BlockSpec((tm, d), lambda i: (i, OFF_MGA // d)),
            pl.BlockSpec((tm, d), lambda i: (i, OFF_MGB // d)),
            pl.BlockSpec((d, d), lambda i: (0, 0), **resident),
            pl.BlockSpec((tm, d), lambda i: (i, 0)),
        ],
        out_specs=pl.BlockSpec((tm, d), lambda i: (i, 0)),
        out_shape=jax.ShapeDtypeStruct((n, d), F32),
        compiler_params=_cparams(("parallel",), vmem_limit=60000 * 1024),
        name="merge_out",
    )(ya, yb, wa, wb, proj, proj, wo, h)


def _nsa_prep_kernel(nq_ref, kv_ref, ng_ref, qn_ref, kn_ref,
                     qT_out, kc_out, vc_out, ks_out, vsT_out, kw_out, vwT_out, gT_out):
    tm = nq_ref.shape[0]
    scale = HEAD_DIM ** -0.5 * LOG2E
    for hh in range(N_HEADS):
        g, h = divmod(hh, HPG)
        qn = _rms_rows(nq_ref[:, hh * HEAD_DIM:(hh + 1) * HEAD_DIM], qn_ref[...]) * scale
        for qi in range(tm // Q_BLOCK):
            blk = qn[qi * Q_BLOCK:(qi + 1) * Q_BLOCK, :]
            qT_out[g, qi, :, h * Q_BLOCK:(h + 1) * Q_BLOCK] = jnp.transpose(blk).astype(BF16)
    ones_rows = jnp.where(lax.broadcasted_iota(jnp.int32, (V_ROWS - HEAD_DIM, tm), 0) == 0, 1.0, 0.0)
    for g in range(N_GROUPS):
        col = lambda kind: kv_ref[:, (kind * N_GROUPS + g) * HEAD_DIM:(kind * N_GROUPS + g + 1) * HEAD_DIM]
        kc_out[g] = col(0).astype(BF16)
        vc_out[g] = col(1).astype(BF16)
        ks_out[g, :, 0:HEAD_DIM] = _rms_rows(col(2), kn_ref[1:2, :]).astype(BF16)
        blk_of_row = (lax.broadcasted_iota(jnp.int32, (tm, HEAD_DIM), 0) % (FAR_SUB * Q_BLOCK)) // SEL_BLOCK
        ks_out[g, :, HEAD_DIM:] = jnp.where(
            lax.broadcasted_iota(jnp.int32, (tm, HEAD_DIM), 1) == blk_of_row, 1.0, 0.0).astype(BF16)
        kw_out[g] = _rms_rows(col(4), kn_ref[2:3, :]).astype(BF16)
        for v_out, kind in ((vsT_out, 3), (vwT_out, 5)):
            v_out[g, 0:HEAD_DIM, :] = jnp.transpose(col(kind)).astype(BF16)
            v_out[g, HEAD_DIM:, :] = ones_rows.astype(BF16)
    sig = jax.nn.sigmoid(ng_ref[...])
    for qi in range(tm // Q_BLOCK):
        t = jnp.transpose(sig[qi * Q_BLOCK:(qi + 1) * Q_BLOCK, :])
        for g in range(N_GROUPS):
            gT_out[g, qi] = t[g * GATE_ROWS:(g + 1) * GATE_ROWS, :]


def _nsa_prep(proj, q_norm, k_norm, bsz, seq, tm=512):
    kvw = 6 * N_GROUPS * HEAD_DIM
    spb = seq // tm
    qpb = tm // Q_BLOCK
    nqb = seq // Q_BLOCK
    wide = HPG * Q_BLOCK
    keys = pl.BlockSpec((None, N_GROUPS, tm, HEAD_DIM), lambda i: (i // spb, 0, i % spb, 0))
    keys_aug = pl.BlockSpec((None, N_GROUPS, tm, 2 * HEAD_DIM), lambda i: (i // spb, 0, i % spb, 0))
    vals = pl.BlockSpec((None, N_GROUPS, V_ROWS, tm), lambda i: (i // spb, 0, 0, i % spb))
    keys_shape = jax.ShapeDtypeStruct((bsz, N_GROUPS, seq, HEAD_DIM), BF16)
    vals_shape = jax.ShapeDtypeStruct((bsz, N_GROUPS, V_ROWS, seq), BF16)
    return pl.pallas_call(
        _nsa_prep_kernel,
        grid=(bsz * spb,),
        in_specs=[
            pl.BlockSpec((tm, 1024), lambda i: (i, OFF_NQ // 1024)),
            pl.BlockSpec((tm, kvw), lambda i: (i, OFF_KV6 // kvw)),
            pl.BlockSpec((tm, LANES), lambda i: (i, OFF_NGATE // LANES)),
            pl.BlockSpec((1, HEAD_DIM), lambda i: (0, 0)),
            pl.BlockSpec((SUBLANES, HEAD_DIM), lambda i: (0, 0)),
        ],
        out_specs=[
            pl.BlockSpec((None, N_GROUPS, qpb, HEAD_DIM, wide), lambda i: (i // spb, 0, i % spb, 0, 0)),
            keys, keys, keys_aug, vals, keys, vals,
            pl.BlockSpec((None, N_GROUPS, qpb, GATE_ROWS, Q_BLOCK), lambda i: (i // spb, 0, i % spb, 0, 0)),
        ],
        out_shape=[
            jax.ShapeDtypeStruct((bsz, N_GROUPS, nqb, HEAD_DIM, wide), BF16),
            keys_shape, keys_shape, jax.ShapeDtypeStruct((bsz, N_GROUPS, seq, 2 * HEAD_DIM), BF16), vals_shape,
            keys_shape, vals_shape,
            jax.ShapeDtypeStruct((bsz, N_GROUPS, nqb, GATE_ROWS, Q_BLOCK), F32),
        ],
        compiler_params=_cparams(("parallel",)),
        name="nsa_prep",
    )(proj, proj, proj, q_norm, k_norm)


def _compress_kernel(p_ref, w1_ref, pos_ref, b1_ref, w2_ref, kn_ref, o_ref):
    half = CMP_STRIDE * HEAD_DIM
    p = p_ref[...]
    w1 = w1_ref[...]
    a = jnp.dot(p, w1[:half].astype(BF16), preferred_element_type=F32)
    b = jnp.dot(p, w1[half:].astype(BF16), preferred_element_type=F32)
    pos = jnp.broadcast_to(pos_ref[...], (SUBLANES, 2 * half))
    c0 = jnp.dot(pos, w1, preferred_element_type=F32, precision=HI)[0:1, :] + b1_ref[...]
    n = p.shape[0]
    hid = a + pltpu.roll(b, n - 1, 0) + c0
    hid = hid * jax.nn.sigmoid(hid)
    out = jnp.dot(hid.astype(BF16), w2_ref[...].astype(BF16), preferred_element_type=F32)
    normed = _rms_rows(out, kn_ref[0:1, :])
    out = jnp.where(pl.program_id(0) == 0, normed, out)
    row = lax.broadcasted_iota(jnp.int32, out.shape, 0)
    o_ref[...] = jnp.where(row < n - 1, out, 0.0)


def _compress(pieces, w1, pos, b1, w2, k_norm):
    _, bsz, ng, npc, width = pieces.shape
    return pl.pallas_call(
        _compress_kernel,
        grid=(2, bsz, ng),
        in_specs=[
            pl.BlockSpec((None, None, None, npc, width), lambda c, b, g: (c, b, g, 0, 0)),
            pl.BlockSpec((None, 2 * width, CMP_HIDDEN), lambda c, b, g: (c, 0, 0)),
            pl.BlockSpec((None, 1, 2 * width), lambda c, b, g: (c, 0, 0)),
            pl.BlockSpec((None, 1, CMP_HIDDEN), lambda c, b, g: (c, 0, 0)),
            pl.BlockSpec((None, CMP_HIDDEN, HEAD_DIM), lambda c, b, g: (c, 0, 0)),
            pl.BlockSpec((SUBLANES, HEAD_DIM), lambda c, b, g: (0, 0)),
        ],
        out_specs=pl.BlockSpec((None, None, None, npc, HEAD_DIM), lambda c, b, g: (c, b, g, 0, 0)),
        out_shape=jax.ShapeDtypeStruct((2, bsz, ng, npc, HEAD_DIM), F32),
        compiler_params=_cparams(("arbitrary", "arbitrary", "arbitrary")),
        name="compress",
    )(pieces, w1, pos, b1, w2, k_norm)


def _bias_kernel(tab_ref, bn_ref, bw_ref, bc_ref):
    g = pl.program_id(0)
    n_wt = WINDOW // Q_BLOCK

    def lookup(dist, head):
        v = jnp.full(dist.shape, tab_ref[0, head], F32)
        for b in range(1, REL_BUCKETS):
            v = jnp.where(dist >= REL_THRESH[b], tab_ref[b, head], v)
        return (v - tab_ref[REL_BUCKETS - 1, head]) * LOG2E

    ki = lax.broadcasted_iota(jnp.int32, (Q_BLOCK, Q_BLOCK), 0)
    qj = lax.broadcasted_iota(jnp.int32, (Q_BLOCK, Q_BLOCK), 1)
    r = lax.broadcasted_iota(jnp.int32, (NEAR_ROWS, Q_BLOCK), 0)
    qc = lax.broadcasted_iota(jnp.int32, (NEAR_ROWS, Q_BLOCK), 1)
    dist_c = qc - CMP_STRIDE * (r - CMP_PAD) - (CMP_LEN - 1)
    for h in range(HPG):
        head = g * HPG + h
        sl = slice(h * Q_BLOCK, (h + 1) * Q_BLOCK)
        diag = jnp.where(ki <= qj, lookup(qj - ki, head), NEG_INF)
        prev = lookup(qj - ki + Q_BLOCK, head)
        bn_ref[0:Q_BLOCK, sl] = prev
        bn_ref[Q_BLOCK:, sl] = diag
        bw_ref[0:Q_BLOCK, sl] = jnp.where(ki > qj, 0.0, NEG_INF)
        bw_ref[Q_BLOCK:(n_wt - 1) * Q_BLOCK, sl] = jnp.zeros(((n_wt - 2) * Q_BLOCK, Q_BLOCK), F32)
        bw_ref[(n_wt - 1) * Q_BLOCK:n_wt * Q_BLOCK, sl] = prev
        bw_ref[n_wt * Q_BLOCK:, sl] = diag
        bc_ref[:, sl] = jnp.where(dist_c >= 0, lookup(dist_c, head), NEG_INF)


def _bias_tiles(rel_table):
    wide = HPG * Q_BLOCK
    return pl.pallas_call(
        _bias_kernel,
        grid=(N_GROUPS,),
        in_specs=[pl.BlockSpec(memory_space=pltpu.SMEM)],
        out_specs=[
            pl.BlockSpec((None, NEAR_KEYS, wide), lambda g: (g, 0, 0)),
            pl.BlockSpec((None, WIN_KEYS, wide), lambda g: (g, 0, 0)),
            pl.BlockSpec((None, NEAR_ROWS, wide), lambda g: (g, 0, 0)),
        ],
        out_shape=[
            jax.ShapeDtypeStruct((N_GROUPS, NEAR_KEYS, wide), F32),
            jax.ShapeDtypeStruct((N_GROUPS, WIN_KEYS, wide), F32),
            jax.ShapeDtypeStruct((N_GROUPS, NEAR_ROWS, wide), F32),
        ],
        compiler_params=_cparams(("arbitrary",)),
        name="rel_bias",
    )(rel_table)


def _tile4(x):
    return jnp.concatenate([x] * HPG, axis=1)


def _nsa_attn_kernel(qT_ref, gT_ref, ks_ref, vsT_ref, kw_ref, vwT_ref, kc_ref, vcT_ref, ovT_ref,
                     bn_ref, bw_ref, bc_ref, o_ref, sc_ref, sel_ref, far_ref, sa_ref, sb_ref,
                     ra_ref, rb_ref):
    qb = pl.program_id(2)
    qT = qT_ref[...]
    qpos = qb * Q_BLOCK + lax.broadcasted_iota(jnp.int32, (1, Q_BLOCK), 1)

    sc_ref[...] = jnp.dot(kc_ref[...], qT, preferred_element_type=F32)
    sc_ref[0:CMP_PAD, :] = jnp.full((CMP_PAD, HPG * Q_BLOCK), NEG_INF, F32)
    r0 = pl.multiple_of(qb * SUBLANES, SUBLANES)
    sc_ref[pl.ds(r0, NEAR_ROWS), :] += bc_ref[...]
    wide = HPG * Q_BLOCK
    rown = lax.broadcasted_iota(jnp.int32, (N_CMP_PAD, wide), 0)
    qpos4 = qb * Q_BLOCK + (lax.broadcasted_iota(jnp.int32, (1, wide), 1) & (Q_BLOCK - 1))
    s = jnp.where(rown < qb * SUBLANES + NEAR_ROWS, sc_ref[...], NEG_INF)
    m = jnp.max(s, axis=0, keepdims=True)
    p = jnp.exp2(s - m)
    l = jnp.sum(p, axis=0, keepdims=True)
    inv = jnp.where(qpos4 >= CMP_LEN - 1, 1.0 / l, 0.0)
    pn = p * inv
    o_c = jnp.dot(vcT_ref[...], pn.astype(BF16), preferred_element_type=F32)
    psum = pn[:, 0:Q_BLOCK]
    for h in range(1, HPG):
        psum = psum + pn[:, h * Q_BLOCK:(h + 1) * Q_BLOCK]
    p_hi = psum.astype(BF16)
    p_lo = (psum - p_hi.astype(F32)).astype(BF16)
    ovT = ovT_ref[...]
    imp = (jnp.dot(ovT, p_hi, preferred_element_type=F32)
           + jnp.dot(ovT, p_lo, preferred_element_type=F32))

    n_sb = imp.shape[0]
    jblk = lax.broadcasted_iota(jnp.int32, (n_sb, Q_BLOCK), 0)
    cur = qpos // SEL_BLOCK
    eligible = jblk * SEL_BLOCK <= qpos
    forced = (jblk == 0) | (jblk == cur) | (jblk == cur - 1)
    picked = -3e38
    score = jnp.where(eligible, jnp.where(forced, picked, imp), NEG_INF)
    for _ in range(max(min(SEL_TOPK, n_sb) - 3, 0)):
        top = jnp.max(score, axis=0, keepdims=True)
        idx = jnp.min(jnp.where(score == top, jblk, n_sb), axis=0, keepdims=True)
        idx = jnp.where(top > -1.0, idx, -1)
        score = jnp.where(jblk == idx, picked, score)
    selmask = jnp.where(score == picked, 0.0, NEG_INF)
    sel_ref[...] = selmask
    far_ref[...] = jnp.where(jblk < 2 * (qb - 1), selmask, NEG_INF)

    half = SEL_BLOCK

    def absent(cond):
        return jnp.where(cond, 0.0, NEG_INF)

    def sel_mask_tile(kt):
        a = jnp.broadcast_to(sel_ref[pl.ds(2 * kt, 1), :], (half, Q_BLOCK))
        b = jnp.broadcast_to(sel_ref[pl.ds(2 * kt + 1, 1), :], (half, Q_BLOCK))
        return _tile4(jnp.concatenate([a, b], axis=0))

    def key_rows(k_ref, kt, width=Q_BLOCK):
        return k_ref[pl.ds(pl.multiple_of(kt * width, width), width), :]

    def value_cols(vT_ref, kt, width=Q_BLOCK):
        return vT_ref[:, pl.ds(pl.multiple_of(kt * width, width), width)]

    def first(s, vT):
        m = jnp.max(s, axis=0, keepdims=True)
        p = jnp.exp2(s - m)
        return m, jnp.dot(vT, p.astype(BF16), preferred_element_type=F32)

    def update(state, s, vT_ref, kt, width=Q_BLOCK):
        m, acc = state
        m_new = jnp.maximum(m, jnp.max(s, axis=0, keepdims=True))
        alpha = jnp.exp2(m - m_new)
        p = jnp.exp2(s - m_new)
        acc = alpha * acc + jnp.dot(value_cols(vT_ref, kt, width), p.astype(BF16), preferred_element_type=F32)
        return m_new, acc

    def normalized(state):
        acc = state[1]
        return acc[:HEAD_DIM] * (1.0 / acc[HEAD_DIM:HEAD_DIM + 1])

    n_wt = WINDOW // Q_BLOCK
    tiles = [jnp.maximum(qb - back, 0) for back in range(n_wt, -1, -1)]
    kwin = jnp.concatenate([key_rows(kw_ref, kt) for kt in tiles], axis=0)
    sw = jnp.dot(kwin, qT, preferred_element_type=F32) + bw_ref[...]
    sw = jnp.concatenate(
        [sw[u * Q_BLOCK:(u + 1) * Q_BLOCK] + absent(qb >= n_wt - u) for u in range(n_wt)]
        + [sw[n_wt * Q_BLOCK:]], axis=0)
    o_w = normalized(first(sw, jnp.concatenate([value_cols(vwT_ref, kt) for kt in tiles], axis=1)))

    kp = jnp.maximum(qb - 1, 0)
    k2 = jnp.concatenate([key_rows(ks_ref, kp), key_rows(ks_ref, qb)], axis=0)[:, 0:HEAD_DIM]
    near_mask = jnp.concatenate([sel_mask_tile(kp) + absent(qb >= 1), sel_mask_tile(qb)], axis=0)
    s = jnp.dot(k2, qT, preferred_element_type=F32) + bn_ref[...] + near_mask
    st = first(s, jnp.concatenate([value_cols(vsT_ref, kp), value_cols(vsT_ref, qb)], axis=1))


    sub_w = FAR_SUB * Q_BLOCK
    blocks_per_sub = sub_w // SEL_BLOCK

    assert blocks_per_sub <= SUBLANES
    for r_ref in (ra_ref, rb_ref):
        r_ref[0:HEAD_DIM, :] = qT
        r_ref[HEAD_DIM:, :] = jnp.zeros((HEAD_DIM, wide), BF16)
    zero_rows = jnp.zeros((2 * SUBLANES - blocks_per_sub, wide), F32)

    def far_scores(cs, r_ref):
        rows = far_ref[pl.ds(pl.multiple_of(blocks_per_sub * cs, blocks_per_sub), blocks_per_sub), :]
        r_ref[HEAD_DIM:HEAD_DIM + 2 * SUBLANES, :] = jnp.concatenate([_tile4(rows), zero_rows], axis=0).astype(BF16)
        k = ks_ref[pl.ds(pl.multiple_of(cs * sub_w, sub_w), sub_w), :]
        return jnp.dot(k, r_ref[...], preferred_element_type=F32)

    last_chunk = ks_ref.shape[0] // sub_w - 1

    def far_body(i, st):
        c0 = 2 * i
        sb_ref[...] = far_scores(c0 + 1, rb_ref)
        st = update(st, sa_ref[...], vsT_ref, c0, sub_w)
        sa_ref[...] = far_scores(jnp.minimum(c0 + 2, last_chunk), ra_ref)
        st = update(st, sb_ref[...], vsT_ref, c0 + 1, sub_w)
        return st

    n_far = (jnp.maximum(qb - 1, 0) + FAR_TILES - 1) // FAR_TILES
    sa_ref[...] = far_scores(0, ra_ref)
    st = lax.fori_loop(0, n_far, far_body, st)
    o_s = normalized(st)

    for h in range(HPG):
        sl = slice(h * Q_BLOCK, (h + 1) * Q_BLOCK)
        gate = lambda c: gT_ref[c * HPG + h:c * HPG + h + 1, :]
        mix = gate(0) * o_c[:, sl] + gate(1) * o_s[:, sl] + gate(2) * o_w[:, sl]
        o_ref[:, h * HEAD_DIM:(h + 1) * HEAD_DIM] = jnp.transpose(mix).astype(o_ref.dtype)


def _nsa_attn(qT, gT, ks, vsT, kw, vwT, kc, vcT, ovT, bn, bw, bc):
    bsz, ng, nqb, _, wide = qT.shape
    seq = ks.shape[2]
    n_sb = ovT.shape[0]
    big = lambda b, g, q: (b, g, 0, 0)
    blk = lambda b, g, q: (b, g, q, 0, 0)
    grp = lambda b, g, q: (g, 0, 0)
    single = dict(pipeline_mode=pl.Buffered(1))
    return pl.pallas_call(
        _nsa_attn_kernel,
        grid=(bsz, ng, nqb),
        in_specs=[
            pl.BlockSpec((None, None, None, HEAD_DIM, wide), blk),
            pl.BlockSpec((None, None, None, GATE_ROWS, Q_BLOCK), blk),
            pl.BlockSpec((None, None, seq, 2 * HEAD_DIM), big, **single),
            pl.BlockSpec((None, None, V_ROWS, seq), big, **single),
            pl.BlockSpec((None, None, seq, HEAD_DIM), big, **single),
            pl.BlockSpec((None, None, V_ROWS, seq), big, **single),
            pl.BlockSpec((None, None, N_CMP_PAD, HEAD_DIM), big),
            pl.BlockSpec((None, None, HEAD_DIM, N_CMP_PAD), big),
            pl.BlockSpec((n_sb, N_CMP_PAD), lambda b, g, q: (0, 0)),
            pl.BlockSpec((None, NEAR_KEYS, wide), grp),
            pl.BlockSpec((None, WIN_KEYS, wide), grp),
            pl.BlockSpec((None, NEAR_ROWS, wide), grp),
        ],
        out_specs=pl.BlockSpec((Q_BLOCK, HPG * HEAD_DIM), lambda b, g, q: (b * nqb + q, g)),
        out_shape=jax.ShapeDtypeStruct((bsz * seq, ng * HPG * HEAD_DIM), BF16),
        scratch_shapes=[pltpu.VMEM((N_CMP_PAD, wide), F32), pltpu.VMEM((n_sb, Q_BLOCK), F32),
                        pltpu.VMEM((n_sb, Q_BLOCK), F32),
                        pltpu.VMEM((FAR_SUB * Q_BLOCK, wide), F32), pltpu.VMEM((FAR_SUB * Q_BLOCK, wide), F32),
                        pltpu.VMEM((2 * HEAD_DIM, wide), BF16), pltpu.VMEM((2 * HEAD_DIM, wide), BF16)],
        compiler_params=_cparams(("arbitrary", "arbitrary", "arbitrary")),
        name="nsa_attn",
    )(qT, gT, ks, vsT, kw, vwT, kc, vcT, ovT, bn, bw, bc)


def _dot_exact01(a, b, left):
    x = b if left else a
    x1 = x.astype(BF16)
    r1 = x - x1.astype(F32)
    x2 = r1.astype(BF16)
    x3 = (r1 - x2.astype(F32)).astype(BF16)
    mm = (lambda p: jnp.dot(a, p, preferred_element_type=F32)) if left else (
        lambda p: jnp.dot(p, b, preferred_element_type=F32))
    return mm(x1) + mm(x2) + mm(x3)


def _dn_prep_kernel(x_ref, halo_ref, w_ref, dba_ref, alog_ref, dtb_ref, o_ref, beta_ref, g_ref, buf_ref,
                    *, blocks_per_seq):
    i = pl.program_id(0)
    j = pl.program_id(1)
    t = x_ref.shape[0]
    first_of_seq = (i % blocks_per_seq) == 0
    buf_ref[0:SUBLANES, :] = jnp.where(first_of_seq, 0.0, halo_ref[...])
    buf_ref[SUBLANES:, :] = x_ref[...]
    scale = jnp.where(j == 0, DN_DIM ** -0.5, 1.0)
    for h in range(DN_HEADS):
        sl = slice(h * DN_DIM, (h + 1) * DN_DIM)
        y = w_ref[DN_CONV - 1:DN_CONV, sl] * x_ref[:, sl]
        for back in range(1, DN_CONV):
            y = y + w_ref[DN_CONV - 1 - back:DN_CONV - back, sl] * buf_ref[pl.ds(SUBLANES - back, t), sl]
        y = y * jax.nn.sigmoid(y)
        nrm = y * lax.rsqrt(jnp.sum(y * y, axis=-1, keepdims=True) + EPS) * scale
        o_ref[:, sl] = jnp.where(j < 2, nrm, y)

    @pl.when(j == 0)
    def _():
        dba = dba_ref[...]
        beta = jax.nn.sigmoid(dba)
        z = dba + dtb_ref[...]
        softplus = jnp.maximum(z, 0.0) + jnp.log1p(jnp.exp(-jnp.abs(z)))
        g = -jnp.exp(alog_ref[...]) * softplus
        rt = lax.broadcasted_iota(jnp.int32, (t, t), 0)
        ct = lax.broadcasted_iota(jnp.int32, (t, t), 1)
        blocktri = jnp.where((rt >= ct) & (rt // DN_CHUNK == ct // DN_CHUNK), 1.0, 0.0).astype(BF16)
        g = _dot_exact01(blocktri, g, left=True)
        wide = DN_HEADS * DN_DIM
        src = lax.broadcasted_iota(jnp.int32, (LANES, wide), 0)
        head = lax.broadcasted_iota(jnp.int32, (LANES, wide), 1) // DN_DIM
        beta_ref[...] = _dot_exact01(beta, jnp.where(src == head, 1.0, 0.0).astype(BF16), left=False)
        g_ref[...] = _dot_exact01(g, jnp.where(src == head + DN_HEADS, 1.0, 0.0).astype(BF16), left=False)


def _dn_prep(proj, conv_w, alog_row, dtb_row, seq, t=512):
    n = proj.shape[0]
    width = DN_HEADS * DN_DIM
    hb = t // SUBLANES
    return pl.pallas_call(
        functools.partial(_dn_prep_kernel, blocks_per_seq=seq // t),
        grid=(n // t, 3),
        in_specs=[
            pl.BlockSpec((t, width), lambda i, j: (i, OFF_DQ // width + j)),
            pl.BlockSpec((SUBLANES, width), lambda i, j: (jnp.maximum(i * hb - 1, 0), OFF_DQ // width + j)),
            pl.BlockSpec((DN_CONV, width), lambda i, j: (0, j)),
            pl.BlockSpec((t, LANES), lambda i, j: (i, OFF_DBA // LANES)),
            pl.BlockSpec((1, LANES), lambda i, j: (0, 0)),
            pl.BlockSpec((1, LANES), lambda i, j: (0, 0)),
        ],
        out_specs=[
            pl.BlockSpec((t, width), lambda i, j: (i, j)),
            pl.BlockSpec((t, width), lambda i, j: (i, 0)),
            pl.BlockSpec((t, width), lambda i, j: (i, 0)),
        ],
        out_shape=[
            jax.ShapeDtypeStruct((n, 3 * width), F32),
            jax.ShapeDtypeStruct((n, width), F32),
            jax.ShapeDtypeStruct((n, width), F32),
        ],
        scratch_shapes=[pltpu.VMEM((t + SUBLANES, width), F32)],
        compiler_params=_cparams(("arbitrary", "arbitrary")),
        name="dn_prep",
    )(proj, proj, conv_w, proj, alog_row, dtb_row)


def _dot_nt(a, b, precision=None):
    return lax.dot_general(a, b, (((1,), (1,)), ((), ())), preferred_element_type=F32, precision=precision)


def _dot_tn(a, b, precision=None):
    return lax.dot_general(a, b, (((0,), (0,)), ((), ())), preferred_element_type=F32, precision=precision)


DN_MM_MODE = "bf16"


def _mm_dn(a, b):
    if DN_MM_MODE == "hi":
        return jnp.dot(a, b, preferred_element_type=F32, precision=HI)
    a_hi = a.astype(BF16)
    b_hi = b.astype(BF16)
    out = jnp.dot(a_hi, b_hi, preferred_element_type=F32)
    if DN_MM_MODE == "x3":
        a_lo = (a - a_hi.astype(F32)).astype(BF16)
        b_lo = (b - b_hi.astype(F32)).astype(BF16)
        out = out + jnp.dot(a_lo, b_hi, preferred_element_type=F32) + jnp.dot(a_hi, b_lo, preferred_element_type=F32)
    return out


def _deltanet_kernel(q_ref, k_ref, v_ref, beta_ref, g_ref, z_ref, nw_ref, o_ref, state_ref):
    c = DN_CHUNK
    t = q_ref.shape[0]
    n_chunks = t // c

    @pl.when(pl.program_id(2) == 0)
    def _():
        state_ref[...] = jnp.zeros_like(state_ref)

    gc_all = g_ref[...]
    beta_all = beta_ref[...]

    gsz = DN_GROUP
    cpg = gsz // c
    ri = lax.broadcasted_iota(jnp.int32, (gsz, gsz), 0)
    ci = lax.broadcasted_iota(jnp.int32, (gsz, gsz), 1)
    same = (ri // c) == (ci // c)
    causal = (ri >= ci) & same
    strict = (ri > ci) & same
    eye = jnp.where(ri == ci, 1.0, 0.0)
    chunk_of_col = lax.broadcasted_iota(jnp.int32, (DN_DIM, gsz), 1) // c

    heads = q_ref.shape[1] // DN_DIM
    groups = range(heads * (t // gsz))
    rows = [slice((p % (t // gsz)) * gsz, (p % (t // gsz) + 1) * gsz) for p in groups]
    cols = [slice((p // (t // gsz)) * DN_DIM, (p // (t // gsz) + 1) * DN_DIM) for p in groups]
    q = [q_ref[rows[p], cols[p]] for p in groups]
    k = [k_ref[rows[p], cols[p]] for p in groups]
    gc = [gc_all[rows[p], cols[p]] for p in groups]
    eg = [jnp.exp(x) for x in gc]
    beta = [beta_all[rows[p], cols[p]] for p in groups]
    kb = [k[gi] * beta[gi] for gi in groups]
    rhs = [jnp.concatenate([kb[gi] * eg[gi], v_ref[rows[gi], cols[gi]] * beta[gi]], axis=1) for gi in groups]
    decay = []
    for gi in groups:
        gc2 = jnp.concatenate([gc[gi]] * (gsz // DN_DIM), axis=1)
        gcr = jnp.sum(gc2 * eye, axis=0, keepdims=True)
        decay.append(jnp.exp(jnp.where(causal, gc2 - gcr, NEG_INF)))
    a2 = [_dot_nt(jnp.concatenate([kb[gi], q[gi]], axis=0).astype(BF16), k[gi].astype(BF16)) for gi in groups]
    low = [jnp.where(strict, a2[gi][:gsz] * decay[gi], 0.0) for gi in groups]
    qk = [a2[gi][gsz:] * decay[gi] for gi in groups]
    size = 2
    pair = ((ri // size) == (ci // size)) & (ri > ci)
    inv = [eye - jnp.where(pair, x, 0.0) for x in low]
    while size < c:
        size *= 2
        pair = ((ri // size) == (ci // size)) & ((ri // (size // 2)) != (ci // (size // 2))) & (ri > ci)
        sub = [jnp.where(pair, x, 0.0) for x in low]
        y = [_mm_dn(sub[gi], inv[gi]) for gi in groups]
        inv = [inv[gi] - _mm_dn(inv[gi], y[gi]) for gi in groups]
    wu = [_mm_dn(inv[gi], rhs[gi]) for gi in groups]
    lasts, res = [], []
    for gi in groups:
        last = [gc[gi][(n + 1) * c - 1:(n + 1) * c, :] for n in range(cpg)]
        lasts.append(last)
        g_last_rows = jnp.concatenate([jnp.broadcast_to(x, (c, DN_DIM)) for x in last], axis=0)
        kd_t = jnp.transpose(k[gi] * jnp.exp(g_last_rows - gc[gi]))
        xs = [jnp.where(chunk_of_col == n, kd_t, 0.0) for n in range(cpg)]
        xs += [qk[gi][n * c:(n + 1) * c, :] for n in range(cpg)]
        res.append(jnp.dot(jnp.concatenate(xs, axis=0).astype(BF16), wu[gi].astype(BF16),
                           preferred_element_type=F32))
    pre = []
    for gi in groups:
        qd = q[gi] * eg[gi]
        for n in range(cpg):
            kw = res[gi][n * DN_DIM:(n + 1) * DN_DIM, :DN_DIM]
            ku = res[gi][n * DN_DIM:(n + 1) * DN_DIM, DN_DIM:]
            base = cpg * DN_DIM + n * c
            coef = qd[n * c:(n + 1) * c, :] - res[gi][base:base + c, :DN_DIM]
            qku = res[gi][base:base + c, DN_DIM:]
            lhs = jnp.concatenate([-kw, coef], axis=0).astype(BF16)
            pre.append((lhs, ku, qku, jnp.exp(lasts[gi][n])))

    states = [state_ref[hd] for hd in range(heads)]
    nw = nw_ref[...]
    for n in range(n_chunks):
        crow = slice(n * c, (n + 1) * c)
        for hd in range(heads):
            lhs, ku, qku, gl = pre[hd * n_chunks + n]
            r = jnp.dot(lhs, states[hd].astype(BF16), preferred_element_type=F32)
            o = r[DN_DIM:] + qku
            states[hd] = states[hd] * gl + r[:DN_DIM] + ku
            ccol = slice(hd * DN_DIM, (hd + 1) * DN_DIM)
            z = z_ref[crow, ccol]
            o_ref[crow, ccol] = (_rms_rows(o, nw) * (z * jax.nn.sigmoid(z))).astype(o_ref.dtype)
    for hd in range(heads):
        state_ref[hd] = states[hd]


def _deltanet(qkv, beta, g, proj, norm_w, bsz, seq, t=1024, heads=2):
    n = qkv.shape[0]
    spb = seq // t
    hw = heads * DN_DIM
    hsteps = DN_HEADS // heads
    zoff = OFF_DZ // hw
    col = lambda off: (lambda b, h, i: (b * spb + i, off + h))
    return pl.pallas_call(
        _deltanet_kernel,
        grid=(bsz, hsteps, spb),
        in_specs=[
            pl.BlockSpec((t, hw), col(0)),
            pl.BlockSpec((t, hw), col(hsteps)),
            pl.BlockSpec((t, hw), col(2 * hsteps)),
            pl.BlockSpec((t, hw), col(0)),
            pl.BlockSpec((t, hw), col(0)),
            pl.BlockSpec((t, hw), col(zoff)),
            pl.BlockSpec((1, DN_DIM), lambda b, h, i: (0, 0)),
        ],
        out_specs=pl.BlockSpec((t, hw), col(0)),
        out_shape=jax.ShapeDtypeStruct((n, DN_HEADS * DN_DIM), BF16),
        scratch_shapes=[pltpu.VMEM((heads, DN_DIM, DN_DIM), F32)],
        compiler_params=_cparams(("arbitrary", "arbitrary", "arbitrary")),
        name="deltanet",
    )(qkv, qkv, qkv, beta, g, proj, norm_w)


def _regroup_w_in(w_in):
    d = w_in.shape[0]
    sizes = (1024, 256, 256, 256, 256, 256, 256, 24, 1024, 1024, 1024, 1024, 8, 8, 2048, 2048)
    offs = np.concatenate([[0], np.cumsum(sizes)])
    (nq, kc, vc, ksel, vsel, kwin, vwin, ngate, dq, dk, dv, dz, db, da, mga, mgb) = (
        w_in[:, offs[i]:offs[i + 1]] for i in range(len(sizes)))
    zeros = lambda w: jnp.zeros((d, w), w_in.dtype)
    ngate = jnp.transpose(ngate.reshape(d, N_GROUPS, HPG, 3), (0, 1, 3, 2)).reshape(d, N_GROUPS, 3 * HPG)
    ngate = jnp.pad(ngate, ((0, 0), (0, 0), (0, GATE_ROWS - 3 * HPG))).reshape(d, N_GROUPS * GATE_ROWS)
    cols = [mga, mgb, dq, dk, dv, nq, dz, kc, vc, ksel, vsel, kwin, vwin,
            ngate, zeros(LANES - N_GROUPS * GATE_ROWS), db, da, zeros(LANES - 16), zeros(PROJ_DIM - PROJ_USED)]
    return jnp.concatenate(cols, axis=1).astype(BF16)


def _overlap_t(n_sb, n_pieces):
    n = np.arange(N_CMP_PAD) - CMP_PAD
    j = np.arange(n_sb)
    valid = (n >= 0) & (n < n_pieces - 1)
    c0 = n * CMP_STRIDE
    s0 = j * SEL_BLOCK
    ov = (c0[None, :] < s0[:, None] + SEL_BLOCK) & (c0[None, :] + CMP_LEN > s0[:, None]) & valid[None, :]
    return jnp.asarray(ov.astype(np.float32), dtype=BF16)


def _token_mixer(h, bsz, seq, norm_mix, w_in, q_norm, k_norm, cmp_pos, cmp_w1, cmp_b1, cmp_w2, dn_conv,
                 dn_a_log, dn_dt_bias, dn_norm, w_a, w_b, w_out, bias_tiles):
    n = bsz * seq
    nqb = seq // Q_BLOCK
    npc = seq // CMP_STRIDE
    n_sb = seq // SEL_BLOCK
    assert npc + CMP_PAD <= N_CMP_PAD and nqb * SUBLANES + NEAR_ROWS <= N_CMP_PAD
    assert nqb % FAR_TILES == 0
    proj = _norm_matmul(h, norm_mix.reshape(1, -1), _regroup_w_in(w_in))

    k_norm8 = jnp.concatenate([k_norm, jnp.zeros((SUBLANES - 3, HEAD_DIM), F32)], axis=0)
    qT, kc_raw, vc_raw, ks, vsT, kw, vwT, gT = _nsa_prep(proj, q_norm.reshape(1, -1), k_norm8, bsz, seq)
    pieces = jnp.stack([kc_raw, vc_raw], axis=0).reshape(2, bsz, N_GROUPS, npc, CMP_STRIDE * HEAD_DIM)
    cmp = _compress(pieces, cmp_w1, cmp_pos.reshape(2, 1, CMP_LEN * HEAD_DIM), cmp_b1.reshape(2, 1, -1),
                    cmp_w2, k_norm8)
    cmp = jnp.pad(cmp, ((0, 0), (0, 0), (0, 0), (CMP_PAD, N_CMP_PAD - CMP_PAD - npc), (0, 0))).astype(BF16)
    kc = cmp[0]
    vcT = jnp.swapaxes(cmp[1], -1, -2)
    y_nsa = _nsa_attn(qT, gT, ks, vsT, kw, vwT, kc, vcT, _overlap_t(n_sb, npc), *bias_tiles)

    lane_pad = lambda v, off: jnp.zeros((1, LANES), F32).at[0, off:off + DN_HEADS].set(v)
    qkv, beta, g = _dn_prep(proj, dn_conv, lane_pad(dn_a_log, DN_HEADS), lane_pad(dn_dt_bias, DN_HEADS), seq)
    y_dn = _deltanet(qkv, beta, g, proj, dn_norm.reshape(1, -1), bsz, seq)

    return _merge_out(y_nsa, y_dn, w_a.astype(BF16), w_b.astype(BF16), proj, w_out.astype(BF16), h)


def kernel(x, rel_table, norm_ffn1, ffn1_gate, ffn1_up, ffn1_down, norm_mix, w_in, q_norm, k_norm, cmp_pos, cmp_w1, cmp_b1, cmp_w2, dn_conv, dn_a_log, dn_dt_bias, dn_norm, w_branch_nsa, w_branch_dn, w_out, norm_ffn2, ffn2_gate, ffn2_up, ffn2_down):
    bsz, seq, d = x.shape
    depth = w_in.shape[0]
    h = x.reshape(bsz * seq, d)
    bias_tiles = _bias_tiles(rel_table)
    for l in range(depth):
        h = _ffn(h, norm_ffn1[l].reshape(1, -1), ffn1_gate[l].astype(BF16), ffn1_up[l].astype(BF16),
                 ffn1_down[l].astype(BF16))
        h = _token_mixer(h, bsz, seq, norm_mix[l], w_in[l], q_norm[l], k_norm[l], cmp_pos[l], cmp_w1[l],
                         cmp_b1[l], cmp_w2[l], dn_conv[l], dn_a_log[l], dn_dt_bias[l], dn_norm[l],
                         w_branch_nsa[l], w_branch_dn[l], w_out[l], bias_tiles)
        h = _ffn(h, norm_ffn2[l].reshape(1, -1), ffn2_gate[l].astype(BF16), ffn2_up[l].astype(BF16),
                 ffn2_down[l].astype(BF16))
    return h.reshape(bsz, seq, d)
```

```python
import functools
import math

import numpy as np
import jax
import jax.numpy as jnp
from jax import lax
from jax.experimental import pallas as pl
from jax.experimental.pallas import tpu as pltpu

F32 = jnp.float32
BF16 = jnp.bfloat16
HI = lax.Precision.HIGHEST

D_MODEL = 2048
D_FF = 5632
N_HEADS = 8
N_GROUPS = 2
HPG = N_HEADS // N_GROUPS
HEAD_DIM = 128
CMP_LEN = 32
CMP_STRIDE = 16
CMP_HIDDEN = 128
SEL_BLOCK = 64
SEL_TOPK = 16
WINDOW = 512
Q_BLOCK = 128
DN_HEADS = 8
DN_DIM = 128
DN_CONV = 4
DN_CHUNK = 64
DN_GROUP = 256
REL_BUCKETS = 32
REL_MAX_DIST = 128
EPS = 1e-6
LOG2E = math.log2(math.e)
NEG_INF = -1e30
BIG = 1e30

LANES = 128
SUBLANES = 8
VMEM_LIMIT = 56 * 1024 * 1024

OFF_MGA = 0
OFF_MGB = 2048
OFF_DQ, OFF_DK, OFF_DV = 4096, 5120, 6144
OFF_NQ = 7168
OFF_DZ = 8192
OFF_KV6 = 9216
OFF_NGATE = 10752
OFF_DBA = 10880
PROJ_USED = 11008
PROJ_DIM = 11264

CMP_PAD = 16
N_CMP_PAD = 1152
NEAR_ROWS = 24
NEAR_KEYS = 2 * Q_BLOCK
WIN_KEYS = WINDOW + Q_BLOCK
GATE_ROWS = 16
V_ROWS = HEAD_DIM + 16
FAR_SUB = 4
FAR_PAIRS = 2
FAR_TILES = 2 * FAR_PAIRS * FAR_SUB


def _rel_bucket_thresholds():
    n = np.arange(0, 4 * REL_MAX_DIST, dtype=np.int64)
    max_exact = REL_BUCKETS // 2
    nf = np.maximum(n, max_exact).astype(np.float32)
    large = max_exact + (np.log(nf / np.float32(max_exact)) / np.float32(math.log(REL_MAX_DIST / max_exact))
                         * np.float32(REL_BUCKETS - max_exact)).astype(np.int32)
    bucket = np.where(n < max_exact, n, np.minimum(large, REL_BUCKETS - 1))
    assert np.all(np.diff(bucket) >= 0)
    return [int(np.argmax(bucket >= b)) for b in range(REL_BUCKETS)]


REL_THRESH = _rel_bucket_thresholds()
FAR_DIST = REL_THRESH[REL_BUCKETS - 1]
assert FAR_DIST <= Q_BLOCK


def _cparams(sem, vmem_limit=VMEM_LIMIT):
    return pltpu.CompilerParams(dimension_semantics=sem, vmem_limit_bytes=vmem_limit)


def _rms_rows(x, w_row):
    ms = jnp.mean(x * x, axis=-1, keepdims=True)
    return x * lax.rsqrt(ms + EPS) * w_row


def _ffn_kernel(x_ref, nw_ref, wg_ref, wu_ref, wd_ref, o_ref, xn_ref, acc_ref):
    j = pl.program_id(1)

    @pl.when(j == 0)
    def _():
        xn_ref[...] = _rms_rows(x_ref[...], nw_ref[...]).astype(BF16)
        acc_ref[...] = jnp.zeros_like(acc_ref)

    xn = xn_ref[...]
    g = jnp.dot(xn, wg_ref[...], preferred_element_type=F32)
    u = jnp.dot(xn, wu_ref[...], preferred_element_type=F32)
    a = (g * jax.nn.sigmoid(g) * u).astype(BF16)
    acc_ref[...] += jnp.dot(a, wd_ref[...], preferred_element_type=F32)

    @pl.when(j == pl.num_programs(1) - 1)
    def _():
        o_ref[...] = x_ref[...] + 0.5 * acc_ref[...]


def _ffn(h, nw, wg, wu, wd, tm=512, tf=512):
    n, d = h.shape
    f = wg.shape[1]
    return pl.pallas_call(
        _ffn_kernel,
        grid=(n // tm, f // tf),
        in_specs=[
            pl.BlockSpec((tm, d), lambda i, j: (i, 0)),
            pl.BlockSpec((1, d), lambda i, j: (0, 0)),
            pl.BlockSpec((d, tf), lambda i, j: (0, j)),
            pl.BlockSpec((d, tf), lambda i, j: (0, j)),
            pl.BlockSpec((tf, d), lambda i, j: (j, 0)),
        ],
        out_specs=pl.BlockSpec((tm, d), lambda i, j: (i, 0)),
        out_shape=jax.ShapeDtypeStruct((n, d), F32),
        scratch_shapes=[pltpu.VMEM((tm, d), BF16), pltpu.VMEM((tm, d), F32)],
        compiler_params=_cparams(("parallel", "arbitrary")),
        name="ffn",
    )(h, nw, wg, wu, wd)


def _norm_matmul_kernel(x_ref, nw_ref, w_ref, o_ref, xn_ref):
    @pl.when(pl.program_id(1) == 0)
    def _():
        xn_ref[...] = _rms_rows(x_ref[...], nw_ref[...]).astype(BF16)

    o_ref[...] = jnp.dot(xn_ref[...], w_ref[...], preferred_element_type=F32)


def _norm_matmul(h, nw, w, tm=1024, tn=1024):
    n, d = h.shape
    nout = w.shape[1]
    return pl.pallas_call(
        _norm_matmul_kernel,
        grid=(n // tm, nout // tn),
        in_specs=[
            pl.BlockSpec((tm, d), lambda i, j: (i, 0)),
            pl.BlockSpec((1, d), lambda i, j: (0, 0)),
            pl.BlockSpec((d, tn), lambda i, j: (0, j)),
        ],
        out_specs=pl.BlockSpec((tm, tn), lambda i, j: (i, j)),
        out_shape=jax.ShapeDtypeStruct((n, nout), F32),
        scratch_shapes=[pltpu.VMEM((tm, d), BF16)],
        compiler_params=_cparams(("parallel", "arbitrary")),
        name="in_proj",
    )(h, nw, w)


def _merge_out_kernel(ya_ref, yb_ref, wa_ref, wb_ref, ga_ref, gb_ref, wo_ref, h_ref, o_ref):
    a = jnp.dot(ya_ref[...], wa_ref[...], preferred_element_type=F32)
    b = jnp.dot(yb_ref[...], wb_ref[...], preferred_element_type=F32)
    merged = (jax.nn.sigmoid(ga_ref[...]) * a + jax.nn.sigmoid(gb_ref[...]) * b).astype(BF16)
    o_ref[...] = h_ref[...] + jnp.dot(merged, wo_ref[...], preferred_element_type=F32)


def _merge_out(ya, yb, wa, wb, proj, wo, h, tm=512):
    n, k = ya.shape
    d = wa.shape[1]
    resident = dict(pipeline_mode=pl.Buffered(1))
    return pl.pallas_call(
        _merge_out_kernel,
        grid=(n // tm,),
        in_specs=[
            pl.BlockSpec((tm, k), lambda i: (i, 0)),
            pl.BlockSpec((tm, k), lambda i: (i, 0)),
            pl.BlockSpec((k, d), lambda i: (0, 0), **resident),
            pl.BlockSpec((k, d), lambda i: (0, 0), **resident),
            pl.BlockSpec((tm, d), lambda i: (i, OFF_MGA // d)),
            pl.BlockSpec((tm, d), lambda i: (i, OFF_MGB // d)),
            pl.BlockSpec((d, d), lambda i: (0, 0), **resident),
            pl.BlockSpec((tm, d), lambda i: (i, 0)),
        ],
        out_specs=pl.BlockSpec((tm, d), lambda i: (i, 0)),
        out_shape=jax.ShapeDtypeStruct((n, d), F32),
        compiler_params=_cparams(("parallel",), vmem_limit=60000 * 1024),
        name="merge_out",
    )(ya, yb, wa, wb, proj, proj, wo, h)


def _nsa_prep_kernel(nq_ref, kv_ref, ng_ref, qn_ref, kn_ref,
                     qT_out, kc_out, vc_out, ks_out, vsT_out, kw_out, vwT_out, gT_out):
    tm = nq_ref.shape[0]
    scale = HEAD_DIM ** -0.5 * LOG2E
    for hh in range(N_HEADS):
        g, h = divmod(hh, HPG)
        qn = _rms_rows(nq_ref[:, hh * HEAD_DIM:(hh + 1) * HEAD_DIM], qn_ref[...]) * scale
        for qi in range(tm // Q_BLOCK):
            blk = qn[qi * Q_BLOCK:(qi + 1) * Q_BLOCK, :]
            qT_out[g, qi, :, h * Q_BLOCK:(h + 1) * Q_BLOCK] = jnp.transpose(blk).astype(BF16)
    ones_rows = jnp.where(lax.broadcasted_iota(jnp.int32, (V_ROWS - HEAD_DIM, tm), 0) == 0, 1.0, 0.0)
    for g in range(N_GROUPS):
        col = lambda kind: kv_ref[:, (kind * N_GROUPS + g) * HEAD_DIM:(kind * N_GROUPS + g + 1) * HEAD_DIM]
        kc_out[g] = col(0).astype(BF16)
        vc_out[g] = col(1).astype(BF16)
        ks_out[g, :, 0:HEAD_DIM] = _rms_rows(col(2), kn_ref[1:2, :]).astype(BF16)
        tok = pl.program_id(0) * tm + lax.broadcasted_iota(jnp.int32, (tm, HEAD_DIM), 0)
        blk_of_row = (tok % (FAR_SUB * Q_BLOCK)) // SEL_BLOCK
        ks_out[g, :, HEAD_DIM:] = jnp.where(
            lax.broadcasted_iota(jnp.int32, (tm, HEAD_DIM), 1) == blk_of_row, 1.0, 0.0).astype(BF16)
        kw_out[g] = _rms_rows(col(4), kn_ref[2:3, :]).astype(BF16)
        for v_out, kind in ((vsT_out, 3), (vwT_out, 5)):
            v_out[g, 0:HEAD_DIM, :] = jnp.transpose(col(kind)).astype(BF16)
            v_out[g, HEAD_DIM:, :] = ones_rows.astype(BF16)
    sig = jax.nn.sigmoid(ng_ref[...])
    for qi in range(tm // Q_BLOCK):
        t = jnp.transpose(sig[qi * Q_BLOCK:(qi + 1) * Q_BLOCK, :])
        for g in range(N_GROUPS):
            gT_out[g, qi] = t[g * GATE_ROWS:(g + 1) * GATE_ROWS, :]


def _nsa_prep(proj, q_norm, k_norm, bsz, seq, tm=512):
    kvw = 6 * N_GROUPS * HEAD_DIM
    spb = seq // tm
    qpb = tm // Q_BLOCK
    nqb = seq // Q_BLOCK
    wide = HPG * Q_BLOCK
    keys = pl.BlockSpec((None, N_GROUPS, tm, HEAD_DIM), lambda i: (i // spb, 0, i % spb, 0))
    keys_aug = pl.BlockSpec((None, N_GROUPS, tm, 2 * HEAD_DIM), lambda i: (i // spb, 0, i % spb, 0))
    vals = pl.BlockSpec((None, N_GROUPS, V_ROWS, tm), lambda i: (i // spb, 0, 0, i % spb))
    keys_shape = jax.ShapeDtypeStruct((bsz, N_GROUPS, seq, HEAD_DIM), BF16)
    vals_shape = jax.ShapeDtypeStruct((bsz, N_GROUPS, V_ROWS, seq), BF16)
    return pl.pallas_call(
        _nsa_prep_kernel,
        grid=(bsz * spb,),
        in_specs=[
            pl.BlockSpec((tm, 1024), lambda i: (i, OFF_NQ // 1024)),
            pl.BlockSpec((tm, kvw), lambda i: (i, OFF_KV6 // kvw)),
            pl.BlockSpec((tm, LANES), lambda i: (i, OFF_NGATE // LANES)),
            pl.BlockSpec((1, HEAD_DIM), lambda i: (0, 0)),
            pl.BlockSpec((SUBLANES, HEAD_DIM), lambda i: (0, 0)),
        ],
        out_specs=[
            pl.BlockSpec((None, N_GROUPS, qpb, HEAD_DIM, wide), lambda i: (i // spb, 0, i % spb, 0, 0)),
            keys, keys, keys_aug, vals, keys, vals,
            pl.BlockSpec((None, N_GROUPS, qpb, GATE_ROWS, Q_BLOCK), lambda i: (i // spb, 0, i % spb, 0, 0)),
        ],
        out_shape=[
            jax.ShapeDtypeStruct((bsz, N_GROUPS, nqb, HEAD_DIM, wide), BF16),
            keys_shape, keys_shape, jax.ShapeDtypeStruct((bsz, N_GROUPS, seq, 2 * HEAD_DIM), BF16), vals_shape,
            keys_shape, vals_shape,
            jax.ShapeDtypeStruct((bsz, N_GROUPS, nqb, GATE_ROWS, Q_BLOCK), F32),
        ],
        compiler_params=_cparams(("parallel",)),
        name="nsa_prep",
    )(proj, proj, proj, q_norm, k_norm)


def _compress_kernel(p_ref, w1_ref, pos_ref, b1_ref, w2_ref, kn_ref, o_ref):
    half = CMP_STRIDE * HEAD_DIM
    p = p_ref[...]
    w1 = w1_ref[...]
    a = jnp.dot(p, w1[:half].astype(BF16), preferred_element_type=F32)
    b = jnp.dot(p, w1[half:].astype(BF16), preferred_element_type=F32)
    pos = jnp.broadcast_to(pos_ref[...], (SUBLANES, 2 * half))
    c0 = jnp.dot(pos, w1, preferred_element_type=F32, precision=HI)[0:1, :] + b1_ref[...]
    n = p.shape[0]
    hid = a + pltpu.roll(b, n - 1, 0) + c0
    hid = hid * jax.nn.sigmoid(hid)
    out = jnp.dot(hid.astype(BF16), w2_ref[...].astype(BF16), preferred_element_type=F32)
    normed = _rms_rows(out, kn_ref[0:1, :])
    out = jnp.where(pl.program_id(0) == 0, normed, out)
    row = lax.broadcasted_iota(jnp.int32, out.shape, 0)
    o_ref[...] = jnp.where(row < n - 1, out, 0.0)


def _compress(pieces, w1, pos, b1, w2, k_norm):
    _, bsz, ng, npc, width = pieces.shape
    return pl.pallas_call(
        _compress_kernel,
        grid=(2, bsz, ng),
        in_specs=[
            pl.BlockSpec((None, None, None, npc, width), lambda c, b, g: (c, b, g, 0, 0)),
            pl.BlockSpec((None, 2 * width, CMP_HIDDEN), lambda c, b, g: (c, 0, 0)),
            pl.BlockSpec((None, 1, 2 * width), lambda c, b, g: (c, 0, 0)),
            pl.BlockSpec((None, 1, CMP_HIDDEN), lambda c, b, g: (c, 0, 0)),
            pl.BlockSpec((None, CMP_HIDDEN, HEAD_DIM), lambda c, b, g: (c, 0, 0)),
            pl.BlockSpec((SUBLANES, HEAD_DIM), lambda c, b, g: (0, 0)),
        ],
        out_specs=pl.BlockSpec((None, None, None, npc, HEAD_DIM), lambda c, b, g: (c, b, g, 0, 0)),
        out_shape=jax.ShapeDtypeStruct((2, bsz, ng, npc, HEAD_DIM), F32),
        compiler_params=_cparams(("arbitrary", "arbitrary", "arbitrary")),
        name="compress",
    )(pieces, w1, pos, b1, w2, k_norm)


def _bias_kernel(tab_ref, bn_ref, bw_ref, bc_ref):
    g = pl.program_id(0)
    n_wt = WINDOW // Q_BLOCK

    def lookup(dist, head):
        v = jnp.full(dist.shape, tab_ref[0, head], F32)
        for b in range(1, REL_BUCKETS):
            v = jnp.where(dist >= REL_THRESH[b], tab_ref[b, head], v)
        return (v - tab_ref[REL_BUCKETS - 1, head]) * LOG2E

    ki = lax.broadcasted_iota(jnp.int32, (Q_BLOCK, Q_BLOCK), 0)
    qj = lax.broadcasted_iota(jnp.int32, (Q_BLOCK, Q_BLOCK), 1)
    r = lax.broadcasted_iota(jnp.int32, (NEAR_ROWS, Q_BLOCK), 0)
    qc = lax.broadcasted_iota(jnp.int32, (NEAR_ROWS, Q_BLOCK), 1)
    dist_c = qc - CMP_STRIDE * (r - CMP_PAD) - (CMP_LEN - 1)
    for h in range(HPG):
        head = g * HPG + h
        sl = slice(h * Q_BLOCK, (h + 1) * Q_BLOCK)
        diag = jnp.where(ki <= qj, lookup(qj - ki, head), NEG_INF)
        prev = lookup(qj - ki + Q_BLOCK, head)
        bn_ref[0:Q_BLOCK, sl] = prev
        bn_ref[Q_BLOCK:, sl] = diag
        bw_ref[0:Q_BLOCK, sl] = jnp.where(ki > qj, 0.0, NEG_INF)
        bw_ref[Q_BLOCK:(n_wt - 1) * Q_BLOCK, sl] = jnp.zeros(((n_wt - 2) * Q_BLOCK, Q_BLOCK), F32)
        bw_ref[(n_wt - 1) * Q_BLOCK:n_wt * Q_BLOCK, sl] = prev
        bw_ref[n_wt * Q_BLOCK:, sl] = diag
        bc_ref[:, sl] = jnp.where(dist_c >= 0, lookup(dist_c, head), NEG_INF)


def _bias_tiles(rel_table):
    wide = HPG * Q_BLOCK
    return pl.pallas_call(
        _bias_kernel,
        grid=(N_GROUPS,),
        in_specs=[pl.BlockSpec(memory_space=pltpu.SMEM)],
        out_specs=[
            pl.BlockSpec((None, NEAR_KEYS, wide), lambda g: (g, 0, 0)),
            pl.BlockSpec((None, WIN_KEYS, wide), lambda g: (g, 0, 0)),
            pl.BlockSpec((None, NEAR_ROWS, wide), lambda g: (g, 0, 0)),
        ],
        out_shape=[
            jax.ShapeDtypeStruct((N_GROUPS, NEAR_KEYS, wide), F32),
            jax.ShapeDtypeStruct((N_GROUPS, WIN_KEYS, wide), F32),
            jax.ShapeDtypeStruct((N_GROUPS, NEAR_ROWS, wide), F32),
        ],
        compiler_params=_cparams(("arbitrary",)),
        name="rel_bias",
    )(rel_table)


def _tile4(x):
    return jnp.concatenate([x] * HPG, axis=1)


def _nsa_attn_kernel(qT_ref, gT_ref, ks_ref, vsT_ref, kw_ref, vwT_ref, kc_ref, vcT_ref, ovT_ref,
                     bn_ref, bw_ref, bc_ref, o_ref, sc_ref, sel_ref, far_ref, sa_ref, sb_ref,
                     ra_ref, rb_ref):
    qb = pl.program_id(2)
    qT = qT_ref[...]
    qpos = qb * Q_BLOCK + lax.broadcasted_iota(jnp.int32, (1, Q_BLOCK), 1)

    sc_ref[...] = jnp.dot(kc_ref[...], qT, preferred_element_type=F32)
    sc_ref[0:CMP_PAD, :] = jnp.full((CMP_PAD, HPG * Q_BLOCK), NEG_INF, F32)
    r0 = pl.multiple_of(qb * SUBLANES, SUBLANES)
    sc_ref[pl.ds(r0, NEAR_ROWS), :] += bc_ref[...]
    wide = HPG * Q_BLOCK
    rown = lax.broadcasted_iota(jnp.int32, (N_CMP_PAD, wide), 0)
    qpos4 = qb * Q_BLOCK + (lax.broadcasted_iota(jnp.int32, (1, wide), 1) & (Q_BLOCK - 1))
    s = jnp.where(rown < qb * SUBLANES + NEAR_ROWS, sc_ref[...], NEG_INF)
    m = jnp.max(s, axis=0, keepdims=True)
    p = jnp.exp2(s - m)
    l = jnp.sum(p, axis=0, keepdims=True)
    inv = jnp.where(qpos4 >= CMP_LEN - 1, 1.0 / l, 0.0)
    pn = p * inv
    o_c = jnp.dot(vcT_ref[...], pn.astype(BF16), preferred_element_type=F32)
    psum = pn[:, 0:Q_BLOCK]
    for h in range(1, HPG):
        psum = psum + pn[:, h * Q_BLOCK:(h + 1) * Q_BLOCK]
    p_hi = psum.astype(BF16)
    p_lo = (psum - p_hi.astype(F32)).astype(BF16)
    ovT = ovT_ref[...]
    imp = (jnp.dot(ovT, p_hi, preferred_element_type=F32)
           + jnp.dot(ovT, p_lo, preferred_element_type=F32))

    n_sb = imp.shape[0]
    jblk = lax.broadcasted_iota(jnp.int32, (n_sb, Q_BLOCK), 0)
    cur = qpos // SEL_BLOCK
    eligible = jblk * SEL_BLOCK <= qpos
    forced = (jblk == 0) | (jblk == cur) | (jblk == cur - 1)
    picked = -3e38
    score = jnp.where(eligible, jnp.where(forced, picked, imp), NEG_INF)
    for _ in range(max(min(SEL_TOPK, n_sb) - 3, 0)):
        top = jnp.max(score, axis=0, keepdims=True)
        idx = jnp.min(jnp.where(score == top, jblk, n_sb), axis=0, keepdims=True)
        idx = jnp.where(top > -1.0, idx, -1)
        score = jnp.where(jblk == idx, picked, score)
    selmask = jnp.where(score == picked, 0.0, NEG_INF)
    sel_ref[...] = selmask
    far_ref[...] = jnp.where(jblk < 2 * (qb - 1), selmask, NEG_INF)

    half = SEL_BLOCK

    def absent(cond):
        return jnp.where(cond, 0.0, NEG_INF)

    def sel_mask_tile(kt):
        a = jnp.broadcast_to(sel_ref[pl.ds(2 * kt, 1), :], (half, Q_BLOCK))
        b = jnp.broadcast_to(sel_ref[pl.ds(2 * kt + 1, 1), :], (half, Q_BLOCK))
        return _tile4(jnp.concatenate([a, b], axis=0))

    def key_rows(k_ref, kt, width=Q_BLOCK):
        return k_ref[pl.ds(pl.multiple_of(kt * width, width), width), :]

    def value_cols(vT_ref, kt, width=Q_BLOCK):
        return vT_ref[:, pl.ds(pl.multiple_of(kt * width, width), width)]

    def first(s, vT):
        m = jnp.max(s, axis=0, keepdims=True)
        p = jnp.exp2(s - m)
        return m, jnp.dot(vT, p.astype(BF16), preferred_element_type=F32)

    def update(state, s, vT_ref, kt, width=Q_BLOCK):
        m, acc = state
        m_new = jnp.maximum(m, jnp.max(s, axis=0, keepdims=True))
        alpha = jnp.exp2(m - m_new)
        p = jnp.exp2(s - m_new)
        acc = alpha * acc + jnp.dot(value_cols(vT_ref, kt, width), p.astype(BF16), preferred_element_type=F32)
        return m_new, acc

    def normalized(state):
        acc = state[1]
        return acc[:HEAD_DIM] * (1.0 / acc[HEAD_DIM:HEAD_DIM + 1])

    n_wt = WINDOW // Q_BLOCK
    tiles = [jnp.maximum(qb - back, 0) for back in range(n_wt, -1, -1)]
    kwin = jnp.concatenate([key_rows(kw_ref, kt) for kt in tiles], axis=0)
    sw = jnp.dot(kwin, qT, preferred_element_type=F32) + bw_ref[...]
    sw = jnp.concatenate(
        [sw[u * Q_BLOCK:(u + 1) * Q_BLOCK] + absent(qb >= n_wt - u) for u in range(n_wt)]
        + [sw[n_wt * Q_BLOCK:]], axis=0)
    o_w = normalized(first(sw, jnp.concatenate([value_cols(vwT_ref, kt) for kt in tiles], axis=1)))

    kp = jnp.maximum(qb - 1, 0)
    k2 = jnp.concatenate([key_rows(ks_ref, kp), key_rows(ks_ref, qb)], axis=0)[:, 0:HEAD_DIM]
    near_mask = jnp.concatenate([sel_mask_tile(kp) + absent(qb >= 1), sel_mask_tile(qb)], axis=0)
    s = jnp.dot(k2, qT, preferred_element_type=F32) + bn_ref[...] + near_mask
    st = first(s, jnp.concatenate([value_cols(vsT_ref, kp), value_cols(vsT_ref, qb)], axis=1))


    sub_w = FAR_SUB * Q_BLOCK
    blocks_per_sub = sub_w // SEL_BLOCK

    bf16_rows = 2 * SUBLANES
    mask_rows = -(-blocks_per_sub // bf16_rows) * bf16_rows
    assert mask_rows <= HEAD_DIM
    for r_ref in (ra_ref, rb_ref):
        r_ref[0:HEAD_DIM, :] = qT
        r_ref[HEAD_DIM:, :] = jnp.zeros((HEAD_DIM, wide), BF16)

    def far_scores(cs, r_ref):
        rows = _tile4(far_ref[pl.ds(pl.multiple_of(blocks_per_sub * cs, blocks_per_sub), blocks_per_sub), :])
        if mask_rows > blocks_per_sub:
            rows = jnp.concatenate([rows, jnp.zeros((mask_rows - blocks_per_sub, wide), F32)], axis=0)
        r_ref[HEAD_DIM:HEAD_DIM + mask_rows, :] = rows.astype(BF16)
        k = ks_ref[pl.ds(pl.multiple_of(cs * sub_w, sub_w), sub_w), :]
        return jnp.dot(k, r_ref[...], preferred_element_type=F32)

    last_chunk = ks_ref.shape[0] // sub_w - 1

    def pair_step(c0, st):
        sb_ref[...] = far_scores(c0 + 1, rb_ref)
        st = update(st, sa_ref[...], vsT_ref, c0, sub_w)
        sa_ref[...] = far_scores(jnp.minimum(c0 + 2, last_chunk), ra_ref)
        return update(st, sb_ref[...], vsT_ref, c0 + 1, sub_w)

    def far_body(i, st):
        for pair in range(FAR_PAIRS):
            st = pair_step(2 * (FAR_PAIRS * i + pair), st)
        return st

    far_tiles = jnp.maximum(qb - 1, 0)
    n_full = far_tiles // FAR_TILES
    n_pairs = (far_tiles + 2 * FAR_SUB - 1) // (2 * FAR_SUB)
    sa_ref[...] = far_scores(0, ra_ref)
    st = lax.fori_loop(0, n_full, far_body, st)
    st = lax.fori_loop(FAR_PAIRS * n_full, n_pairs, lambda j, st: pair_step(2 * j, st), st)
    o_s = normalized(st)

    for h in range(HPG):
        sl = slice(h * Q_BLOCK, (h + 1) * Q_BLOCK)
        gate = lambda c: gT_ref[c * HPG + h:c * HPG + h + 1, :]
        mix = gate(0) * o_c[:, sl] + gate(1) * o_s[:, sl] + gate(2) * o_w[:, sl]
        o_ref[:, h * HEAD_DIM:(h + 1) * HEAD_DIM] = jnp.transpose(mix).astype(o_ref.dtype)


def _nsa_attn(qT, gT, ks, vsT, kw, vwT, kc, vcT, ovT, bn, bw, bc):
    bsz, ng, nqb, _, wide = qT.shape
    seq = ks.shape[2]
    n_sb = ovT.shape[0]
    big = lambda b, g, q: (b, g, 0, 0)
    blk = lambda b, g, q: (b, g, q, 0, 0)
    grp = lambda b, g, q: (g, 0, 0)
    single = dict(pipeline_mode=pl.Buffered(1))
    return pl.pallas_call(
        _nsa_attn_kernel,
        grid=(bsz, ng, nqb),
        in_specs=[
            pl.BlockSpec((None, None, None, HEAD_DIM, wide), blk),
            pl.BlockSpec((None, None, None, GATE_ROWS, Q_BLOCK), blk),
            pl.BlockSpec((None, None, seq, 2 * HEAD_DIM), big, **single),
            pl.BlockSpec((None, None, V_ROWS, seq), big, **single),
            pl.BlockSpec((None, None, seq, HEAD_DIM), big, **single),
            pl.BlockSpec((None, None, V_ROWS, seq), big, **single),
            pl.BlockSpec((None, None, N_CMP_PAD, HEAD_DIM), big),
            pl.BlockSpec((None, None, HEAD_DIM, N_CMP_PAD), big),
            pl.BlockSpec((n_sb, N_CMP_PAD), lambda b, g, q: (0, 0)),
            pl.BlockSpec((None, NEAR_KEYS, wide), grp),
            pl.BlockSpec((None, WIN_KEYS, wide), grp),
            pl.BlockSpec((None, NEAR_ROWS, wide), grp),
        ],
        out_specs=pl.BlockSpec((Q_BLOCK, HPG * HEAD_DIM), lambda b, g, q: (b * nqb + q, g)),
        out_shape=jax.ShapeDtypeStruct((bsz * seq, ng * HPG * HEAD_DIM), BF16),
        scratch_shapes=[pltpu.VMEM((N_CMP_PAD, wide), F32), pltpu.VMEM((n_sb, Q_BLOCK), F32),
                        pltpu.VMEM((n_sb, Q_BLOCK), F32),
                        pltpu.VMEM((FAR_SUB * Q_BLOCK, wide), F32), pltpu.VMEM((FAR_SUB * Q_BLOCK, wide), F32),
                        pltpu.VMEM((2 * HEAD_DIM, wide), BF16), pltpu.VMEM((2 * HEAD_DIM, wide), BF16)],
        compiler_params=_cparams(("arbitrary", "arbitrary", "arbitrary")),
        name="nsa_attn",
    )(qT, gT, ks, vsT, kw, vwT, kc, vcT, ovT, bn, bw, bc)


def _dot_exact01(a, b, left):
    x = b if left else a
    x1 = x.astype(BF16)
    r1 = x - x1.astype(F32)
    x2 = r1.astype(BF16)
    x3 = (r1 - x2.astype(F32)).astype(BF16)
    mm = (lambda p: jnp.dot(a, p, preferred_element_type=F32)) if left else (
        lambda p: jnp.dot(p, b, preferred_element_type=F32))
    return mm(x1) + mm(x2) + mm(x3)


def _dn_prep_kernel(x_ref, halo_ref, w_ref, dba_ref, alog_ref, dtb_ref, o_ref, beta_ref, g_ref, buf_ref,
                    *, blocks_per_seq):
    i = pl.program_id(0)
    j = pl.program_id(1)
    t = x_ref.shape[0]
    first_of_seq = (i % blocks_per_seq) == 0
    buf_ref[0:SUBLANES, :] = jnp.where(first_of_seq, 0.0, halo_ref[...])
    buf_ref[SUBLANES:, :] = x_ref[...]
    scale = jnp.where(j == 0, DN_DIM ** -0.5, 1.0)
    for h in range(DN_HEADS):
        sl = slice(h * DN_DIM, (h + 1) * DN_DIM)
        y = w_ref[DN_CONV - 1:DN_CONV, sl] * x_ref[:, sl]
        for back in range(1, DN_CONV):
            y = y + w_ref[DN_CONV - 1 - back:DN_CONV - back, sl] * buf_ref[pl.ds(SUBLANES - back, t), sl]
        y = y * jax.nn.sigmoid(y)
        nrm = y * lax.rsqrt(jnp.sum(y * y, axis=-1, keepdims=True) + EPS) * scale
        o_ref[:, sl] = jnp.where(j < 2, nrm, y)

    @pl.when(j == 0)
    def _():
        dba = dba_ref[...]
        beta = jax.nn.sigmoid(dba)
        z = dba + dtb_ref[...]
        softplus = jnp.maximum(z, 0.0) + jnp.log1p(jnp.exp(-jnp.abs(z)))
        g = -jnp.exp(alog_ref[...]) * softplus
        rt = lax.broadcasted_iota(jnp.int32, (t, t), 0)
        ct = lax.broadcasted_iota(jnp.int32, (t, t), 1)
        blocktri = jnp.where((rt >= ct) & (rt // DN_CHUNK == ct // DN_CHUNK), 1.0, 0.0).astype(BF16)
        g = _dot_exact01(blocktri, g, left=True)
        wide = DN_HEADS * DN_DIM
        src = lax.broadcasted_iota(jnp.int32, (LANES, wide), 0)
        head = lax.broadcasted_iota(jnp.int32, (LANES, wide), 1) // DN_DIM
        beta_ref[...] = _dot_exact01(beta, jnp.where(src == head, 1.0, 0.0).astype(BF16), left=False)
        g_ref[...] = _dot_exact01(g, jnp.where(src == head + DN_HEADS, 1.0, 0.0).astype(BF16), left=False)


def _dn_prep(proj, conv_w, alog_row, dtb_row, seq, t=512):
    n = proj.shape[0]
    width = DN_HEADS * DN_DIM
    hb = t // SUBLANES
    return pl.pallas_call(
        functools.partial(_dn_prep_kernel, blocks_per_seq=seq // t),
        grid=(n // t, 3),
        in_specs=[
            pl.BlockSpec((t, width), lambda i, j: (i, OFF_DQ // width + j)),
            pl.BlockSpec((SUBLANES, width), lambda i, j: (jnp.maximum(i * hb - 1, 0), OFF_DQ // width + j)),
            pl.BlockSpec((DN_CONV, width), lambda i, j: (0, j)),
            pl.BlockSpec((t, LANES), lambda i, j: (i, OFF_DBA // LANES)),
            pl.BlockSpec((1, LANES), lambda i, j: (0, 0)),
            pl.BlockSpec((1, LANES), lambda i, j: (0, 0)),
        ],
        out_specs=[
            pl.BlockSpec((t, width), lambda i, j: (i, j)),
            pl.BlockSpec((t, width), lambda i, j: (i, 0)),
            pl.BlockSpec((t, width), lambda i, j: (i, 0)),
        ],
        out_shape=[
            jax.ShapeDtypeStruct((n, 3 * width), F32),
            jax.ShapeDtypeStruct((n, width), F32),
            jax.ShapeDtypeStruct((n, width), F32),
        ],
        scratch_shapes=[pltpu.VMEM((t + SUBLANES, width), F32)],
        compiler_params=_cparams(("arbitrary", "arbitrary")),
        name="dn_prep",
    )(proj, proj, conv_w, proj, alog_row, dtb_row)


def _dot_nt(a, b, precision=None):
    return lax.dot_general(a, b, (((1,), (1,)), ((), ())), preferred_element_type=F32, precision=precision)


def _dot_tn(a, b, precision=None):
    return lax.dot_general(a, b, (((0,), (0,)), ((), ())), preferred_element_type=F32, precision=precision)


DN_MM_MODE = "bf16"


def _mm_dn(a, b):
    if DN_MM_MODE == "hi":
        return jnp.dot(a, b, preferred_element_type=F32, precision=HI)
    a_hi = a.astype(BF16)
    b_hi = b.astype(BF16)
    out = jnp.dot(a_hi, b_hi, preferred_element_type=F32)
    if DN_MM_MODE == "x3":
        a_lo = (a - a_hi.astype(F32)).astype(BF16)
        b_lo = (b - b_hi.astype(F32)).astype(BF16)
        out = out + jnp.dot(a_lo, b_hi, preferred_element_type=F32) + jnp.dot(a_hi, b_lo, preferred_element_type=F32)
    return out


def _deltanet_kernel(q_ref, k_ref, v_ref, beta_ref, g_ref, z_ref, nw_ref, o_ref, state_ref):
    c = DN_CHUNK
    t = q_ref.shape[0]
    n_chunks = t // c

    @pl.when(pl.program_id(2) == 0)
    def _():
        state_ref[...] = jnp.zeros_like(state_ref)

    gc_all = g_ref[...]
    beta_all = beta_ref[...]

    gsz = DN_GROUP
    cpg = gsz // c
    ri = lax.broadcasted_iota(jnp.int32, (gsz, gsz), 0)
    ci = lax.broadcasted_iota(jnp.int32, (gsz, gsz), 1)
    same = (ri // c) == (ci // c)
    causal = (ri >= ci) & same
    strict = (ri > ci) & same
    eye = jnp.where(ri == ci, 1.0, 0.0)
    chunk_of_col = lax.broadcasted_iota(jnp.int32, (DN_DIM, gsz), 1) // c

    heads = q_ref.shape[1] // DN_DIM
    groups = range(heads * (t // gsz))
    rows = [slice((p % (t // gsz)) * gsz, (p % (t // gsz) + 1) * gsz) for p in groups]
    cols = [slice((p // (t // gsz)) * DN_DIM, (p // (t // gsz) + 1) * DN_DIM) for p in groups]
    q = [q_ref[rows[p], cols[p]] for p in groups]
    k = [k_ref[rows[p], cols[p]] for p in groups]
    gc = [gc_all[rows[p], cols[p]] for p in groups]
    eg = [jnp.exp(x) for x in gc]
    beta = [beta_all[rows[p], cols[p]] for p in groups]
    kb = [k[gi] * beta[gi] for gi in groups]
    rhs = [jnp.concatenate([kb[gi] * eg[gi], v_ref[rows[gi], cols[gi]] * beta[gi]], axis=1) for gi in groups]
    decay = []
    for gi in groups:
        gc2 = jnp.concatenate([gc[gi]] * (gsz // DN_DIM), axis=1)
        gcr = jnp.sum(gc2 * eye, axis=0, keepdims=True)
        decay.append(jnp.exp(jnp.where(causal, gc2 - gcr, NEG_INF)))
    a2 = [_dot_nt(jnp.concatenate([kb[gi], q[gi]], axis=0).astype(BF16), k[gi].astype(BF16)) for gi in groups]
    low = [jnp.where(strict, a2[gi][:gsz] * decay[gi], 0.0) for gi in groups]
    qk = [a2[gi][gsz:] * decay[gi] for gi in groups]
    size = 2
    pair = ((ri // size) == (ci // size)) & (ri > ci)
    inv = [eye - jnp.where(pair, x, 0.0) for x in low]
    while size < c:
        size *= 2
        pair = ((ri // size) == (ci // size)) & ((ri // (size // 2)) != (ci // (size // 2))) & (ri > ci)
        sub = [jnp.where(pair, x, 0.0) for x in low]
        y = [_mm_dn(sub[gi], inv[gi]) for gi in groups]
        inv = [inv[gi] - _mm_dn(inv[gi], y[gi]) for gi in groups]
    wu = [_mm_dn(inv[gi], rhs[gi]) for gi in groups]
    lasts, res = [], []
    for gi in groups:
        last = [gc[gi][(n + 1) * c - 1:(n + 1) * c, :] for n in range(cpg)]
        lasts.append(last)
        g_last_rows = jnp.concatenate([jnp.broadcast_to(x, (c, DN_DIM)) for x in last], axis=0)
        kd_t = jnp.transpose(k[gi] * jnp.exp(g_last_rows - gc[gi]))
        xs = [jnp.where(chunk_of_col == n, kd_t, 0.0) for n in range(cpg)]
        xs += [qk[gi][n * c:(n + 1) * c, :] for n in range(cpg)]
        res.append(jnp.dot(jnp.concatenate(xs, axis=0).astype(BF16), wu[gi].astype(BF16),
                           preferred_element_type=F32))
    pre = []
    for gi in groups:
        qd = q[gi] * eg[gi]
        for n in range(cpg):
            kw = res[gi][n * DN_DIM:(n + 1) * DN_DIM, :DN_DIM]
            ku = res[gi][n * DN_DIM:(n + 1) * DN_DIM, DN_DIM:]
            base = cpg * DN_DIM + n * c
            coef = qd[n * c:(n + 1) * c, :] - res[gi][base:base + c, :DN_DIM]
            qku = res[gi][base:base + c, DN_DIM:]
            lhs = jnp.concatenate([-kw, coef], axis=0).astype(BF16)
            pre.append((lhs, ku, qku, jnp.exp(lasts[gi][n])))

    states = [state_ref[hd] for hd in range(heads)]
    nw = nw_ref[...]
    for n in range(n_chunks):
        crow = slice(n * c, (n + 1) * c)
        for hd in range(heads):
            lhs, ku, qku, gl = pre[hd * n_chunks + n]
            r = jnp.dot(lhs, states[hd].astype(BF16), preferred_element_type=F32)
            o = r[DN_DIM:] + qku
            states[hd] = states[hd] * gl + r[:DN_DIM] + ku
            ccol = slice(hd * DN_DIM, (hd + 1) * DN_DIM)
            z = z_ref[crow, ccol]
            o_ref[crow, ccol] = (_rms_rows(o, nw) * (z * jax.nn.sigmoid(z))).astype(o_ref.dtype)
    for hd in range(heads):
        state_ref[hd] = states[hd]


def _deltanet(qkv, beta, g, proj, norm_w, bsz, seq, t=1024, heads=2):
    n = qkv.shape[0]
    spb = seq // t
    hw = heads * DN_DIM
    hsteps = DN_HEADS // heads
    zoff = OFF_DZ // hw
    col = lambda off: (lambda b, h, i: (b * spb + i, off + h))
    return pl.pallas_call(
        _deltanet_kernel,
        grid=(bsz, hsteps, spb),
        in_specs=[
            pl.BlockSpec((t, hw), col(0)),
            pl.BlockSpec((t, hw), col(hsteps)),
            pl.BlockSpec((t, hw), col(2 * hsteps)),
            pl.BlockSpec((t, hw), col(0)),
            pl.BlockSpec((t, hw), col(0)),
            pl.BlockSpec((t, hw), col(zoff)),
            pl.BlockSpec((1, DN_DIM), lambda b, h, i: (0, 0)),
        ],
        out_specs=pl.BlockSpec((t, hw), col(0)),
        out_shape=jax.ShapeDtypeStruct((n, DN_HEADS * DN_DIM), BF16),
        scratch_shapes=[pltpu.VMEM((heads, DN_DIM, DN_DIM), F32)],
        compiler_params=_cparams(("arbitrary", "arbitrary", "arbitrary")),
        name="deltanet",
    )(qkv, qkv, qkv, beta, g, proj, norm_w)


def _regroup_w_in(w_in):
    d = w_in.shape[0]
    sizes = (1024, 256, 256, 256, 256, 256, 256, 24, 1024, 1024, 1024, 1024, 8, 8, 2048, 2048)
    offs = np.concatenate([[0], np.cumsum(sizes)])
    (nq, kc, vc, ksel, vsel, kwin, vwin, ngate, dq, dk, dv, dz, db, da, mga, mgb) = (
        w_in[:, offs[i]:offs[i + 1]] for i in range(len(sizes)))
    zeros = lambda w: jnp.zeros((d, w), w_in.dtype)
    ngate = jnp.transpose(ngate.reshape(d, N_GROUPS, HPG, 3), (0, 1, 3, 2)).reshape(d, N_GROUPS, 3 * HPG)
    ngate = jnp.pad(ngate, ((0, 0), (0, 0), (0, GATE_ROWS - 3 * HPG))).reshape(d, N_GROUPS * GATE_ROWS)
    cols = [mga, mgb, dq, dk, dv, nq, dz, kc, vc, ksel, vsel, kwin, vwin,
            ngate, zeros(LANES - N_GROUPS * GATE_ROWS), db, da, zeros(LANES - 16), zeros(PROJ_DIM - PROJ_USED)]
    return jnp.concatenate(cols, axis=1).astype(BF16)


def _overlap_t(n_sb, n_pieces):
    n = np.arange(N_CMP_PAD) - CMP_PAD
    j = np.arange(n_sb)
    valid = (n >= 0) & (n < n_pieces - 1)
    c0 = n * CMP_STRIDE
    s0 = j * SEL_BLOCK
    ov = (c0[None, :] < s0[:, None] + SEL_BLOCK) & (c0[None, :] + CMP_LEN > s0[:, None]) & valid[None, :]
    return jnp.asarray(ov.astype(np.float32), dtype=BF16)


def _token_mixer(h, bsz, seq, norm_mix, w_in, q_norm, k_norm, cmp_pos, cmp_w1, cmp_b1, cmp_w2, dn_conv,
                 dn_a_log, dn_dt_bias, dn_norm, w_a, w_b, w_out, bias_tiles):
    n = bsz * seq
    nqb = seq // Q_BLOCK
    npc = seq // CMP_STRIDE
    n_sb = seq // SEL_BLOCK
    assert npc + CMP_PAD <= N_CMP_PAD and nqb * SUBLANES + NEAR_ROWS <= N_CMP_PAD
    assert nqb % FAR_TILES == 0
    proj = _norm_matmul(h, norm_mix.reshape(1, -1), _regroup_w_in(w_in))

    k_norm8 = jnp.concatenate([k_norm, jnp.zeros((SUBLANES - 3, HEAD_DIM), F32)], axis=0)
    qT, kc_raw, vc_raw, ks, vsT, kw, vwT, gT = _nsa_prep(proj, q_norm.reshape(1, -1), k_norm8, bsz, seq)
    pieces = jnp.stack([kc_raw, vc_raw], axis=0).reshape(2, bsz, N_GROUPS, npc, CMP_STRIDE * HEAD_DIM)
    cmp = _compress(pieces, cmp_w1, cmp_pos.reshape(2, 1, CMP_LEN * HEAD_DIM), cmp_b1.reshape(2, 1, -1),
                    cmp_w2, k_norm8)
    cmp = jnp.pad(cmp, ((0, 0), (0, 0), (0, 0), (CMP_PAD, N_CMP_PAD - CMP_PAD - npc), (0, 0))).astype(BF16)
    kc = cmp[0]
    vcT = jnp.swapaxes(cmp[1], -1, -2)
    y_nsa = _nsa_attn(qT, gT, ks, vsT, kw, vwT, kc, vcT, _overlap_t(n_sb, npc), *bias_tiles)

    lane_pad = lambda v, off: jnp.zeros((1, LANES), F32).at[0, off:off + DN_HEADS].set(v)
    qkv, beta, g = _dn_prep(proj, dn_conv, lane_pad(dn_a_log, DN_HEADS), lane_pad(dn_dt_bias, DN_HEADS), seq)
    y_dn = _deltanet(qkv, beta, g, proj, dn_norm.reshape(1, -1), bsz, seq)

    return _merge_out(y_nsa, y_dn, w_a.astype(BF16), w_b.astype(BF16), proj, w_out.astype(BF16), h)


def kernel(x, rel_table, norm_ffn1, ffn1_gate, ffn1_up, ffn1_down, norm_mix, w_in, q_norm, k_norm, cmp_pos, cmp_w1, cmp_b1, cmp_w2, dn_conv, dn_a_log, dn_dt_bias, dn_norm, w_branch_nsa, w_branch_dn, w_out, norm_ffn2, ffn2_gate, ffn2_up, ffn2_down):
    bsz, seq, d = x.shape
    depth = w_in.shape[0]
    h = x.reshape(bsz * seq, d)
    bias_tiles = _bias_tiles(rel_table)
    for l in range(depth):
        h = _ffn(h, norm_ffn1[l].reshape(1, -1), ffn1_gate[l].astype(BF16), ffn1_up[l].astype(BF16),
                 ffn1_down[l].astype(BF16))
        h = _token_mixer(h, bsz, seq, norm_mix[l], w_in[l], q_norm[l], k_norm[l], cmp_pos[l], cmp_w1[l],
                         cmp_b1[l], cmp_w2[l], dn_conv[l], dn_a_log[l], dn_dt_bias[l], dn_norm[l],
                         w_branch_nsa[l], w_branch_dn[l], w_out[l], bias_tiles)
        h = _ffn(h, norm_ffn2[l].reshape(1, -1), ffn2_gate[l].astype(BF16), ffn2_up[l].astype(BF16),
                 ffn2_down[l].astype(BF16))
    return h.reshape(bsz, seq, d)
```

```python
import functools
import math

import numpy as np
import jax
import jax.numpy as jnp
from jax import lax
from jax.experimental import pallas as pl
from jax.experimental.pallas import tpu as pltpu

F32 = jnp.float32
BF16 = jnp.bfloat16
HI = lax.Precision.HIGHEST

D_MODEL = 2048
D_FF = 5632
N_HEADS = 8
N_GROUPS = 2
HPG = N_HEADS // N_GROUPS
HEAD_DIM = 128
CMP_LEN = 32
CMP_STRIDE = 16
CMP_HIDDEN = 128
SEL_BLOCK = 64
SEL_TOPK = 16
WINDOW = 512
Q_BLOCK = 128
DN_HEADS = 8
DN_DIM = 128
DN_CONV = 4
DN_CHUNK = 64
DN_GROUP = 256
REL_BUCKETS = 32
REL_MAX_DIST = 128
EPS = 1e-6
LOG2E = math.log2(math.e)
NEG_INF = -1e30
BIG = 1e30

LANES = 128
SUBLANES = 8
VMEM_LIMIT = 56 * 1024 * 1024

OFF_MGA = 0
OFF_MGB = 2048
OFF_DQ, OFF_DK, OFF_DV = 4096, 5120, 6144
OFF_NQ = 7168
OFF_DZ = 8192
OFF_KV6 = 9216
OFF_NGATE = 10752
OFF_DBA = 10880
PROJ_USED = 11008
PROJ_DIM = 11264

CMP_PAD = 16
N_CMP_PAD = 1152
NEAR_ROWS = 24
NEAR_KEYS = 2 * Q_BLOCK
WIN_KEYS = WINDOW + Q_BLOCK
GATE_ROWS = 16
V_ROWS = HEAD_DIM + 16
FAR_SUB = 4
FAR_PAIRS = 4
FAR_TILES = 2 * FAR_PAIRS * FAR_SUB


def _rel_bucket_thresholds():
    n = np.arange(0, 4 * REL_MAX_DIST, dtype=np.int64)
    max_exact = REL_BUCKETS // 2
    nf = np.maximum(n, max_exact).astype(np.float32)
    large = max_exact + (np.log(nf / np.float32(max_exact)) / np.float32(math.log(REL_MAX_DIST / max_exact))
                         * np.float32(REL_BUCKETS - max_exact)).astype(np.int32)
    bucket = np.where(n < max_exact, n, np.minimum(large, REL_BUCKETS - 1))
    assert np.all(np.diff(bucket) >= 0)
    return [int(np.argmax(bucket >= b)) for b in range(REL_BUCKETS)]


REL_THRESH = _rel_bucket_thresholds()
FAR_DIST = REL_THRESH[REL_BUCKETS - 1]
assert FAR_DIST <= Q_BLOCK


def _cparams(sem, vmem_limit=VMEM_LIMIT):
    return pltpu.CompilerParams(dimension_semantics=sem, vmem_limit_bytes=vmem_limit)


def _rms_rows(x, w_row):
    ms = jnp.mean(x * x, axis=-1, keepdims=True)
    return x * lax.rsqrt(ms + EPS) * w_row


def _ffn_kernel(x_ref, nw_ref, wg_ref, wu_ref, wd_ref, o_ref, xn_ref, acc_ref):
    j = pl.program_id(1)

    @pl.when(j == 0)
    def _():
        xn_ref[...] = _rms_rows(x_ref[...], nw_ref[...]).astype(BF16)
        acc_ref[...] = jnp.zeros_like(acc_ref)

    xn = xn_ref[...]
    g = jnp.dot(xn, wg_ref[...], preferred_element_type=F32)
    u = jnp.dot(xn, wu_ref[...], preferred_element_type=F32)
    a = (g * jax.nn.sigmoid(g) * u).astype(BF16)
    acc_ref[...] += jnp.dot(a, wd_ref[...], preferred_element_type=F32)

    @pl.when(j == pl.num_programs(1) - 1)
    def _():
        o_ref[...] = x_ref[...] + 0.5 * acc_ref[...]


def _ffn(h, nw, wg, wu, wd, tm=512, tf=512):
    n, d = h.shape
    f = wg.shape[1]
    return pl.pallas_call(
        _ffn_kernel,
        grid=(n // tm, f // tf),
        in_specs=[
            pl.BlockSpec((tm, d), lambda i, j: (i, 0)),
            pl.BlockSpec((1, d), lambda i, j: (0, 0)),
            pl.BlockSpec((d, tf), lambda i, j: (0, j)),
            pl.BlockSpec((d, tf), lambda i, j: (0, j)),
            pl.BlockSpec((tf, d), lambda i, j: (j, 0)),
        ],
        out_specs=pl.BlockSpec((tm, d), lambda i, j: (i, 0)),
        out_shape=jax.ShapeDtypeStruct((n, d), F32),
        scratch_shapes=[pltpu.VMEM((tm, d), BF16), pltpu.VMEM((tm, d), F32)],
        compiler_params=_cparams(("parallel", "arbitrary")),
        name="ffn",
    )(h, nw, wg, wu, wd)


def _norm_matmul_kernel(x_ref, nw_ref, w_ref, o_ref, xn_ref):
    @pl.when(pl.program_id(1) == 0)
    def _():
        xn_ref[...] = _rms_rows(x_ref[...], nw_ref[...]).astype(BF16)

    o_ref[...] = jnp.dot(xn_ref[...], w_ref[...], preferred_element_type=F32)


def _norm_matmul(h, nw, w, tm=1024, tn=1024):
    n, d = h.shape
    nout = w.shape[1]
    return pl.pallas_call(
        _norm_matmul_kernel,
        grid=(n // tm, nout // tn),
        in_specs=[
            pl.BlockSpec((tm, d), lambda i, j: (i, 0)),
            pl.BlockSpec((1, d), lambda i, j: (0, 0)),
            pl.BlockSpec((d, tn), lambda i, j: (0, j)),
        ],
        out_specs=pl.BlockSpec((tm, tn), lambda i, j: (i, j)),
        out_shape=jax.ShapeDtypeStruct((n, nout), F32),
        scratch_shapes=[pltpu.VMEM((tm, d), BF16)],
        compiler_params=_cparams(("parallel", "arbitrary")),
        name="in_proj",
    )(h, nw, w)


def _merge_out_kernel(ya_ref, yb_ref, wa_ref, wb_ref, ga_ref, gb_ref, wo_ref, h_ref, o_ref):
    a = jnp.dot(ya_ref[...], wa_ref[...], preferred_element_type=F32)
    b = jnp.dot(yb_ref[...], wb_ref[...], preferred_element_type=F32)
    merged = (jax.nn.sigmoid(ga_ref[...]) * a + jax.nn.sigmoid(gb_ref[...]) * b).astype(BF16)
    o_ref[...] = h_ref[...] + jnp.dot(merged, wo_ref[...], preferred_element_type=F32)


def _merge_out(ya, yb, wa, wb, proj, wo, h, tm=512):
    n, k = ya.shape
    d = wa.shape[1]
    resident = dict(pipeline_mode=pl.Buffered(1))
    return pl.pallas_call(
        _merge_out_kernel,
        grid=(n // tm,),
        in_specs=[
            pl.BlockSpec((tm, k), lambda i: (i, 0)),
            pl.BlockSpec((tm, k), lambda i: (i, 0)),
            pl.BlockSpec((k, d), lambda i: (0, 0), **resident),
            pl.BlockSpec((k, d), lambda i: (0, 0), **resident),
            pl.BlockSpec((tm, d), lambda i: (i, OFF_MGA // d)),
            pl.BlockSpec((tm, d), lambda i: (i, OFF_MGB // d)),
            pl.BlockSpec((d, d), lambda i: (0, 0), **resident),
            pl.BlockSpec((tm, d), lambda i: (i, 0)),
        ],
        out_specs=pl.BlockSpec((tm, d), lambda i: (i, 0)),
        out_shape=jax.ShapeDtypeStruct((n, d), F32),
        compiler_params=_cparams(("parallel",), vmem_limit=60000 * 1024),
        name="merge_out",
    )(ya, yb, wa, wb, proj, proj, wo, h)


def _nsa_prep_kernel(nq_ref, kv_ref, ng_ref, qn_ref, kn_ref,
                     qT_out, kc_out, vc_out, ks_out, vsT_out, kw_out, vwT_out, gT_out):
    tm = nq_ref.shape[0]
    scale = HEAD_DIM ** -0.5 * LOG2E
    for hh in range(N_HEADS):
        g, h = divmod(hh, HPG)
        qn = _rms_rows(nq_ref[:, hh * HEAD_DIM:(hh + 1) * HEAD_DIM], qn_ref[...]) * scale
        for qi in range(tm // Q_BLOCK):
            blk = qn[qi * Q_BLOCK:(qi + 1) * Q_BLOCK, :]
            qT_out[g, qi, :, h * Q_BLOCK:(h + 1) * Q_BLOCK] = jnp.transpose(blk).astype(BF16)
    ones_rows = jnp.where(lax.broadcasted_iota(jnp.int32, (V_ROWS - HEAD_DIM, tm), 0) == 0, 1.0, 0.0)
    for g in range(N_GROUPS):
        col = lambda kind: kv_ref[:, (kind * N_GROUPS + g) * HEAD_DIM:(kind * N_GROUPS + g + 1) * HEAD_DIM]
        kc_out[g] = col(0).astype(BF16)
        vc_out[g] = col(1).astype(BF16)
        ks_out[g, :, 0:HEAD_DIM] = _rms_rows(col(2), kn_ref[1:2, :]).astype(BF16)
        tok = pl.program_id(0) * tm + lax.broadcasted_iota(jnp.int32, (tm, HEAD_DIM), 0)
        blk_of_row = (tok % (FAR_SUB * Q_BLOCK)) // SEL_BLOCK
        ks_out[g, :, HEAD_DIM:] = jnp.where(
            lax.broadcasted_iota(jnp.int32, (tm, HEAD_DIM), 1) == blk_of_row, 1.0, 0.0).astype(BF16)
        kw_out[g] = _rms_rows(col(4), kn_ref[2:3, :]).astype(BF16)
        for v_out, kind in ((vsT_out, 3), (vwT_out, 5)):
            v_out[g, 0:HEAD_DIM, :] = jnp.transpose(col(kind)).astype(BF16)
            v_out[g, HEAD_DIM:, :] = ones_rows.astype(BF16)
    sig = jax.nn.sigmoid(ng_ref[...])
    for qi in range(tm // Q_BLOCK):
        t = jnp.transpose(sig[qi * Q_BLOCK:(qi + 1) * Q_BLOCK, :])
        for g in range(N_GROUPS):
            gT_out[g, qi] = t[g * GATE_ROWS:(g + 1) * GATE_ROWS, :]


def _nsa_prep(proj, q_norm, k_norm, bsz, seq, tm=512):
    kvw = 6 * N_GROUPS * HEAD_DIM
    spb = seq // tm
    qpb = tm // Q_BLOCK
    nqb = seq // Q_BLOCK
    wide = HPG * Q_BLOCK
    keys = pl.BlockSpec((None, N_GROUPS, tm, HEAD_DIM), lambda i: (i // spb, 0, i % spb, 0))
    keys_aug = pl.BlockSpec((None, N_GROUPS, tm, 2 * HEAD_DIM), lambda i: (i // spb, 0, i % spb, 0))
    vals = pl.BlockSpec((None, N_GROUPS, V_ROWS, tm), lambda i: (i // spb, 0, 0, i % spb))
    keys_shape = jax.ShapeDtypeStruct((bsz, N_GROUPS, seq, HEAD_DIM), BF16)
    vals_shape = jax.ShapeDtypeStruct((bsz, N_GROUPS, V_ROWS, seq), BF16)
    return pl.pallas_call(
        _nsa_prep_kernel,
        grid=(bsz * spb,),
        in_specs=[
            pl.BlockSpec((tm, 1024), lambda i: (i, OFF_NQ // 1024)),
            pl.BlockSpec((tm, kvw), lambda i: (i, OFF_KV6 // kvw)),
            pl.BlockSpec((tm, LANES), lambda i: (i, OFF_NGATE // LANES)),
            pl.BlockSpec((1, HEAD_DIM), lambda i: (0, 0)),
            pl.BlockSpec((SUBLANES, HEAD_DIM), lambda i: (0, 0)),
        ],
        out_specs=[
            pl.BlockSpec((None, N_GROUPS, qpb, HEAD_DIM, wide), lambda i: (i // spb, 0, i % spb, 0, 0)),
            keys, keys, keys_aug, vals, keys, vals,
            pl.BlockSpec((None, N_GROUPS, qpb, GATE_ROWS, Q_BLOCK), lambda i: (i // spb, 0, i % spb, 0, 0)),
        ],
        out_shape=[
            jax.ShapeDtypeStruct((bsz, N_GROUPS, nqb, HEAD_DIM, wide), BF16),
            keys_shape, keys_shape, jax.ShapeDtypeStruct((bsz, N_GROUPS, seq, 2 * HEAD_DIM), BF16), vals_shape,
            keys_shape, vals_shape,
            jax.ShapeDtypeStruct((bsz, N_GROUPS, nqb, GATE_ROWS, Q_BLOCK), F32),
        ],
        compiler_params=_cparams(("parallel",)),
        name="nsa_prep",
    )(proj, proj, proj, q_norm, k_norm)


def _compress_kernel(p_ref, w1_ref, pos_ref, b1_ref, w2_ref, kn_ref, o_ref):
    half = CMP_STRIDE * HEAD_DIM
    p = p_ref[...]
    w1 = w1_ref[...]
    a = jnp.dot(p, w1[:half].astype(BF16), preferred_element_type=F32)
    b = jnp.dot(p, w1[half:].astype(BF16), preferred_element_type=F32)
    pos = jnp.broadcast_to(pos_ref[...], (SUBLANES, 2 * half))
    c0 = jnp.dot(pos, w1, preferred_element_type=F32, precision=HI)[0:1, :] + b1_ref[...]
    n = p.shape[0]
    hid = a + pltpu.roll(b, n - 1, 0) + c0
    hid = hid * jax.nn.sigmoid(hid)
    out = jnp.dot(hid.astype(BF16), w2_ref[...].astype(BF16), preferred_element_type=F32)
    normed = _rms_rows(out, kn_ref[0:1, :])
    out = jnp.where(pl.program_id(0) == 0, normed, out)
    row = lax.broadcasted_iota(jnp.int32, out.shape, 0)
    o_ref[...] = jnp.where(row < n - 1, out, 0.0)


def _compress(pieces, w1, pos, b1, w2, k_norm):
    _, bsz, ng, npc, width = pieces.shape
    return pl.pallas_call(
        _compress_kernel,
        grid=(2, bsz, ng),
        in_specs=[
            pl.BlockSpec((None, None, None, npc, width), lambda c, b, g: (c, b, g, 0, 0)),
            pl.BlockSpec((None, 2 * width, CMP_HIDDEN), lambda c, b, g: (c, 0, 0)),
            pl.BlockSpec((None, 1, 2 * width), lambda c, b, g: (c, 0, 0)),
            pl.BlockSpec((None, 1, CMP_HIDDEN), lambda c, b, g: (c, 0, 0)),
            pl.BlockSpec((None, CMP_HIDDEN, HEAD_DIM), lambda c, b, g: (c, 0, 0)),
            pl.BlockSpec((SUBLANES, HEAD_DIM), lambda c, b, g: (0, 0)),
        ],
        out_specs=pl.BlockSpec((None, None, None, npc, HEAD_DIM), lambda c, b, g: (c, b, g, 0, 0)),
        out_shape=jax.ShapeDtypeStruct((2, bsz, ng, npc, HEAD_DIM), F32),
        compiler_params=_cparams(("arbitrary", "arbitrary", "arbitrary")),
        name="compress",
    )(pieces, w1, pos, b1, w2, k_norm)


def _bias_kernel(tab_ref, bn_ref, bw_ref, bc_ref):
    g = pl.program_id(0)
    n_wt = WINDOW // Q_BLOCK

    def lookup(dist, head):
        v = jnp.full(dist.shape, tab_ref[0, head], F32)
        for b in range(1, REL_BUCKETS):
            v = jnp.where(dist >= REL_THRESH[b], tab_ref[b, head], v)
        return (v - tab_ref[REL_BUCKETS - 1, head]) * LOG2E

    ki = lax.broadcasted_iota(jnp.int32, (Q_BLOCK, Q_BLOCK), 0)
    qj = lax.broadcasted_iota(jnp.int32, (Q_BLOCK, Q_BLOCK), 1)
    r = lax.broadcasted_iota(jnp.int32, (NEAR_ROWS, Q_BLOCK), 0)
    qc = lax.broadcasted_iota(jnp.int32, (NEAR_ROWS, Q_BLOCK), 1)
    dist_c = qc - CMP_STRIDE * (r - CMP_PAD) - (CMP_LEN - 1)
    for h in range(HPG):
        head = g * HPG + h
        sl = slice(h * Q_BLOCK, (h + 1) * Q_BLOCK)
        diag = jnp.where(ki <= qj, lookup(qj - ki, head), NEG_INF)
        prev = lookup(qj - ki + Q_BLOCK, head)
        bn_ref[0:Q_BLOCK, sl] = prev
        bn_ref[Q_BLOCK:, sl] = diag
        bw_ref[0:Q_BLOCK, sl] = jnp.where(ki > qj, 0.0, NEG_INF)
        bw_ref[Q_BLOCK:(n_wt - 1) * Q_BLOCK, sl] = jnp.zeros(((n_wt - 2) * Q_BLOCK, Q_BLOCK), F32)
        bw_ref[(n_wt - 1) * Q_BLOCK:n_wt * Q_BLOCK, sl] = prev
        bw_ref[n_wt * Q_BLOCK:, sl] = diag
        bc_ref[:, sl] = jnp.where(dist_c >= 0, lookup(dist_c, head), NEG_INF)


def _bias_tiles(rel_table):
    wide = HPG * Q_BLOCK
    return pl.pallas_call(
        _bias_kernel,
        grid=(N_GROUPS,),
        in_specs=[pl.BlockSpec(memory_space=pltpu.SMEM)],
        out_specs=[
            pl.BlockSpec((None, NEAR_KEYS, wide), lambda g: (g, 0, 0)),
            pl.BlockSpec((None, WIN_KEYS, wide), lambda g: (g, 0, 0)),
            pl.BlockSpec((None, NEAR_ROWS, wide), lambda g: (g, 0, 0)),
        ],
        out_shape=[
            jax.ShapeDtypeStruct((N_GROUPS, NEAR_KEYS, wide), F32),
            jax.ShapeDtypeStruct((N_GROUPS, WIN_KEYS, wide), F32),
            jax.ShapeDtypeStruct((N_GROUPS, NEAR_ROWS, wide), F32),
        ],
        compiler_params=_cparams(("arbitrary",)),
        name="rel_bias",
    )(rel_table)


def _tile4(x):
    return jnp.concatenate([x] * HPG, axis=1)


def _nsa_attn_kernel(qT_ref, gT_ref, ks_ref, vsT_ref, kw_ref, vwT_ref, kc_ref, vcT_ref, ovT_ref,
                     bn_ref, bw_ref, bc_ref, o_ref, sc_ref, sel_ref, far_ref, sa_ref, sb_ref,
                     ra_ref, rb_ref):
    qb = pl.program_id(2)
    qT = qT_ref[...]
    qpos = qb * Q_BLOCK + lax.broadcasted_iota(jnp.int32, (1, Q_BLOCK), 1)

    sc_ref[...] = jnp.dot(kc_ref[...], qT, preferred_element_type=F32)
    sc_ref[0:CMP_PAD, :] = jnp.full((CMP_PAD, HPG * Q_BLOCK), NEG_INF, F32)
    r0 = pl.multiple_of(qb * SUBLANES, SUBLANES)
    sc_ref[pl.ds(r0, NEAR_ROWS), :] += bc_ref[...]
    wide = HPG * Q_BLOCK
    rown = lax.broadcasted_iota(jnp.int32, (N_CMP_PAD, wide), 0)
    qpos4 = qb * Q_BLOCK + (lax.broadcasted_iota(jnp.int32, (1, wide), 1) & (Q_BLOCK - 1))
    s = jnp.where(rown < qb * SUBLANES + NEAR_ROWS, sc_ref[...], NEG_INF)
    m = jnp.max(s, axis=0, keepdims=True)
    p = jnp.exp2(s - m)
    l = jnp.sum(p, axis=0, keepdims=True)
    inv = jnp.where(qpos4 >= CMP_LEN - 1, 1.0 / l, 0.0)
    pn = p * inv
    o_c = jnp.dot(vcT_ref[...], pn.astype(BF16), preferred_element_type=F32)
    psum = pn[:, 0:Q_BLOCK]
    for h in range(1, HPG):
        psum = psum + pn[:, h * Q_BLOCK:(h + 1) * Q_BLOCK]
    p_hi = psum.astype(BF16)
    p_lo = (psum - p_hi.astype(F32)).astype(BF16)
    ovT = ovT_ref[...]
    imp = (jnp.dot(ovT, p_hi, preferred_element_type=F32)
           + jnp.dot(ovT, p_lo, preferred_element_type=F32))

    n_sb = imp.shape[0]
    jblk = lax.broadcasted_iota(jnp.int32, (n_sb, Q_BLOCK), 0)
    cur = qpos // SEL_BLOCK
    eligible = jblk * SEL_BLOCK <= qpos
    forced = (jblk == 0) | (jblk == cur) | (jblk == cur - 1)
    picked = -3e38
    score = jnp.where(eligible, jnp.where(forced, picked, imp), NEG_INF)
    for _ in range(max(min(SEL_TOPK, n_sb) - 3, 0)):
        top = jnp.max(score, axis=0, keepdims=True)
        idx = jnp.min(jnp.where(score == top, jblk, n_sb), axis=0, keepdims=True)
        idx = jnp.where(top > -1.0, idx, -1)
        score = jnp.where(jblk == idx, picked, score)
    selmask = jnp.where(score == picked, 0.0, NEG_INF)
    sel_ref[...] = selmask
    far_ref[...] = jnp.where(jblk < 2 * (qb - 1), selmask, NEG_INF)

    half = SEL_BLOCK

    def absent(cond):
        return jnp.where(cond, 0.0, NEG_INF)

    def sel_mask_tile(kt):
        a = jnp.broadcast_to(sel_ref[pl.ds(2 * kt, 1), :], (half, Q_BLOCK))
        b = jnp.broadcast_to(sel_ref[pl.ds(2 * kt + 1, 1), :], (half, Q_BLOCK))
        return _tile4(jnp.concatenate([a, b], axis=0))

    def key_rows(k_ref, kt, width=Q_BLOCK):
        return k_ref[pl.ds(pl.multiple_of(kt * width, width), width), :]

    def value_cols(vT_ref, kt, width=Q_BLOCK):
        return vT_ref[:, pl.ds(pl.multiple_of(kt * width, width), width)]

    def first(s, vT):
        m = jnp.max(s, axis=0, keepdims=True)
        p = jnp.exp2(s - m)
        return m, jnp.dot(vT, p.astype(BF16), preferred_element_type=F32)

    def update(state, s, vT_ref, kt, width=Q_BLOCK):
        m, acc = state
        m_new = jnp.maximum(m, jnp.max(s, axis=0, keepdims=True))
        alpha = jnp.exp2(m - m_new)
        p = jnp.exp2(s - m_new)
        acc = alpha * acc + jnp.dot(value_cols(vT_ref, kt, width), p.astype(BF16), preferred_element_type=F32)
        return m_new, acc

    def normalized(state):
        acc = state[1]
        return acc[:HEAD_DIM] * (1.0 / acc[HEAD_DIM:HEAD_DIM + 1])

    n_wt = WINDOW // Q_BLOCK
    tiles = [jnp.maximum(qb - back, 0) for back in range(n_wt, -1, -1)]
    kwin = jnp.concatenate([key_rows(kw_ref, kt) for kt in tiles], axis=0)
    sw = jnp.dot(kwin, qT, preferred_element_type=F32) + bw_ref[...]
    sw = jnp.concatenate(
        [sw[u * Q_BLOCK:(u + 1) * Q_BLOCK] + absent(qb >= n_wt - u) for u in range(n_wt)]
        + [sw[n_wt * Q_BLOCK:]], axis=0)
    o_w = normalized(first(sw, jnp.concatenate([value_cols(vwT_ref, kt) for kt in tiles], axis=1)))

    kp = jnp.maximum(qb - 1, 0)
    k2 = jnp.concatenate([key_rows(ks_ref, kp), key_rows(ks_ref, qb)], axis=0)[:, 0:HEAD_DIM]
    near_mask = jnp.concatenate([sel_mask_tile(kp) + absent(qb >= 1), sel_mask_tile(qb)], axis=0)
    s = jnp.dot(k2, qT, preferred_element_type=F32) + bn_ref[...] + near_mask
    st = first(s, jnp.concatenate([value_cols(vsT_ref, kp), value_cols(vsT_ref, qb)], axis=1))


    sub_w = FAR_SUB * Q_BLOCK
    blocks_per_sub = sub_w // SEL_BLOCK

    bf16_rows = 2 * SUBLANES
    mask_rows = -(-blocks_per_sub // bf16_rows) * bf16_rows
    assert mask_rows <= HEAD_DIM
    for r_ref in (ra_ref, rb_ref):
        r_ref[0:HEAD_DIM, :] = qT
        r_ref[HEAD_DIM:, :] = jnp.zeros((HEAD_DIM, wide), BF16)

    def far_scores(cs, r_ref):
        rows = _tile4(far_ref[pl.ds(pl.multiple_of(blocks_per_sub * cs, blocks_per_sub), blocks_per_sub), :])
        if mask_rows > blocks_per_sub:
            rows = jnp.concatenate([rows, jnp.zeros((mask_rows - blocks_per_sub, wide), F32)], axis=0)
        r_ref[HEAD_DIM:HEAD_DIM + mask_rows, :] = rows.astype(BF16)
        k = ks_ref[pl.ds(pl.multiple_of(cs * sub_w, sub_w), sub_w), :]
        return jnp.dot(k, r_ref[...], preferred_element_type=F32)

    last_chunk = ks_ref.shape[0] // sub_w - 1

    def pair_step(c0, st):
        sb_ref[...] = far_scores(c0 + 1, rb_ref)
        st = update(st, sa_ref[...], vsT_ref, c0, sub_w)
        sa_ref[...] = far_scores(jnp.minimum(c0 + 2, last_chunk), ra_ref)
        return update(st, sb_ref[...], vsT_ref, c0 + 1, sub_w)

    def far_body(i, st):
        for pair in range(FAR_PAIRS):
            st = pair_step(2 * (FAR_PAIRS * i + pair), st)
        return st

    far_tiles = jnp.maximum(qb - 1, 0)
    n_full = far_tiles // FAR_TILES
    n_pairs = (far_tiles + 2 * FAR_SUB - 1) // (2 * FAR_SUB)
    sa_ref[...] = far_scores(0, ra_ref)
    st = lax.fori_loop(0, n_full, far_body, st)
    st = lax.fori_loop(FAR_PAIRS * n_full, n_pairs, lambda j, st: pair_step(2 * j, st), st)
    o_s = normalized(st)

    for h in range(HPG):
        sl = slice(h * Q_BLOCK, (h + 1) * Q_BLOCK)
        gate = lambda c: gT_ref[c * HPG + h:c * HPG + h + 1, :]
        mix = gate(0) * o_c[:, sl] + gate(1) * o_s[:, sl] + gate(2) * o_w[:, sl]
        o_ref[:, h * HEAD_DIM:(h + 1) * HEAD_DIM] = jnp.transpose(mix).astype(o_ref.dtype)


def _nsa_attn(qT, gT, ks, vsT, kw, vwT, kc, vcT, ovT, bn, bw, bc):
    bsz, ng, nqb, _, wide = qT.shape
    seq = ks.shape[2]
    n_sb = ovT.shape[0]
    big = lambda b, g, q: (b, g, 0, 0)
    blk = lambda b, g, q: (b, g, q, 0, 0)
    grp = lambda b, g, q: (g, 0, 0)
    single = dict(pipeline_mode=pl.Buffered(1))
    return pl.pallas_call(
        _nsa_attn_kernel,
        grid=(bsz, ng, nqb),
        in_specs=[
            pl.BlockSpec((None, None, None, HEAD_DIM, wide), blk),
            pl.BlockSpec((None, None, None, GATE_ROWS, Q_BLOCK), blk),
            pl.BlockSpec((None, None, seq, 2 * HEAD_DIM), big, **single),
            pl.BlockSpec((None, None, V_ROWS, seq), big, **single),
            pl.BlockSpec((None, None, seq, HEAD_DIM), big, **single),
            pl.BlockSpec((None, None, V_ROWS, seq), big, **single),
            pl.BlockSpec((None, None, N_CMP_PAD, HEAD_DIM), big),
            pl.BlockSpec((None, None, HEAD_DIM, N_CMP_PAD), big),
            pl.BlockSpec((n_sb, N_CMP_PAD), lambda b, g, q: (0, 0)),
            pl.BlockSpec((None, NEAR_KEYS, wide), grp),
            pl.BlockSpec((None, WIN_KEYS, wide), grp),
            pl.BlockSpec((None, NEAR_ROWS, wide), grp),
        ],
        out_specs=pl.BlockSpec((Q_BLOCK, HPG * HEAD_DIM), lambda b, g, q: (b * nqb + q, g)),
        out_shape=jax.ShapeDtypeStruct((bsz * seq, ng * HPG * HEAD_DIM), BF16),
        scratch_shapes=[pltpu.VMEM((N_CMP_PAD, wide), F32), pltpu.VMEM((n_sb, Q_BLOCK), F32),
                        pltpu.VMEM((n_sb, Q_BLOCK), F32),
                        pltpu.VMEM((FAR_SUB * Q_BLOCK, wide), F32), pltpu.VMEM((FAR_SUB * Q_BLOCK, wide), F32),
                        pltpu.VMEM((2 * HEAD_DIM, wide), BF16), pltpu.VMEM((2 * HEAD_DIM, wide), BF16)],
        compiler_params=_cparams(("arbitrary", "arbitrary", "arbitrary")),
        name="nsa_attn",
    )(qT, gT, ks, vsT, kw, vwT, kc, vcT, ovT, bn, bw, bc)


def _dot_exact01(a, b, left):
    x = b if left else a
    x1 = x.astype(BF16)
    r1 = x - x1.astype(F32)
    x2 = r1.astype(BF16)
    x3 = (r1 - x2.astype(F32)).astype(BF16)
    mm = (lambda p: jnp.dot(a, p, preferred_element_type=F32)) if left else (
        lambda p: jnp.dot(p, b, preferred_element_type=F32))
    return mm(x1) + mm(x2) + mm(x3)


def _dn_prep_kernel(x_ref, halo_ref, w_ref, dba_ref, alog_ref, dtb_ref, o_ref, beta_ref, g_ref, buf_ref,
                    *, blocks_per_seq):
    i = pl.program_id(0)
    j = pl.program_id(1)
    t = x_ref.shape[0]
    first_of_seq = (i % blocks_per_seq) == 0
    buf_ref[0:SUBLANES, :] = jnp.where(first_of_seq, 0.0, halo_ref[...])
    buf_ref[SUBLANES:, :] = x_ref[...]
    scale = jnp.where(j == 0, DN_DIM ** -0.5, 1.0)
    for h in range(DN_HEADS):
        sl = slice(h * DN_DIM, (h + 1) * DN_DIM)
        y = w_ref[DN_CONV - 1:DN_CONV, sl] * x_ref[:, sl]
        for back in range(1, DN_CONV):
            y = y + w_ref[DN_CONV - 1 - back:DN_CONV - back, sl] * buf_ref[pl.ds(SUBLANES - back, t), sl]
        y = y * jax.nn.sigmoid(y)
        nrm = y * lax.rsqrt(jnp.sum(y * y, axis=-1, keepdims=True) + EPS) * scale
        o_ref[:, sl] = jnp.where(j < 2, nrm, y)

    @pl.when(j == 0)
    def _():
        dba = dba_ref[...]
        beta = jax.nn.sigmoid(dba)
        z = dba + dtb_ref[...]
        softplus = jnp.maximum(z, 0.0) + jnp.log1p(jnp.exp(-jnp.abs(z)))
        g = -jnp.exp(alog_ref[...]) * softplus
        rt = lax.broadcasted_iota(jnp.int32, (t, t), 0)
        ct = lax.broadcasted_iota(jnp.int32, (t, t), 1)
        blocktri = jnp.where((rt >= ct) & (rt // DN_CHUNK == ct // DN_CHUNK), 1.0, 0.0).astype(BF16)
        g = _dot_exact01(blocktri, g, left=True)
        wide = DN_HEADS * DN_DIM
        src = lax.broadcasted_iota(jnp.int32, (LANES, wide), 0)
        head = lax.broadcasted_iota(jnp.int32, (LANES, wide), 1) // DN_DIM
        beta_ref[...] = _dot_exact01(beta, jnp.where(src == head, 1.0, 0.0).astype(BF16), left=False)
        g_ref[...] = _dot_exact01(g, jnp.where(src == head + DN_HEADS, 1.0, 0.0).astype(BF16), left=False)


def _dn_prep(proj, conv_w, alog_row, dtb_row, seq, t=512):
    n = proj.shape[0]
    width = DN_HEADS * DN_DIM
    hb = t // SUBLANES
    return pl.pallas_call(
        functools.partial(_dn_prep_kernel, blocks_per_seq=seq // t),
        grid=(n // t, 3),
        in_specs=[
            pl.BlockSpec((t, width), lambda i, j: (i, OFF_DQ // width + j)),
            pl.BlockSpec((SUBLANES, width), lambda i, j: (jnp.maximum(i * hb - 1, 0), OFF_DQ // width + j)),
            pl.BlockSpec((DN_CONV, width), lambda i, j: (0, j)),
            pl.BlockSpec((t, LANES), lambda i, j: (i, OFF_DBA // LANES)),
            pl.BlockSpec((1, LANES), lambda i, j: (0, 0)),
            pl.BlockSpec((1, LANES), lambda i, j: (0, 0)),
        ],
        out_specs=[
            pl.BlockSpec((t, width), lambda i, j: (i, j)),
            pl.BlockSpec((t, width), lambda i, j: (i, 0)),
            pl.BlockSpec((t, width), lambda i, j: (i, 0)),
        ],
        out_shape=[
            jax.ShapeDtypeStruct((n, 3 * width), F32),
            jax.ShapeDtypeStruct((n, width), F32),
            jax.ShapeDtypeStruct((n, width), F32),
        ],
        scratch_shapes=[pltpu.VMEM((t + SUBLANES, width), F32)],
        compiler_params=_cparams(("arbitrary", "arbitrary")),
        name="dn_prep",
    )(proj, proj, conv_w, proj, alog_row, dtb_row)


def _dot_nt(a, b, precision=None):
    return lax.dot_general(a, b, (((1,), (1,)), ((), ())), preferred_element_type=F32, precision=precision)


def _dot_tn(a, b, precision=None):
    return lax.dot_general(a, b, (((0,), (0,)), ((), ())), preferred_element_type=F32, precision=precision)


DN_MM_MODE = "bf16"


def _mm_dn(a, b):
    if DN_MM_MODE == "hi":
        return jnp.dot(a, b, preferred_element_type=F32, precision=HI)
    a_hi = a.astype(BF16)
    b_hi = b.astype(BF16)
    out = jnp.dot(a_hi, b_hi, preferred_element_type=F32)
    if DN_MM_MODE == "x3":
        a_lo = (a - a_hi.astype(F32)).astype(BF16)
        b_lo = (b - b_hi.astype(F32)).astype(BF16)
        out = out + jnp.dot(a_lo, b_hi, preferred_element_type=F32) + jnp.dot(a_hi, b_lo, preferred_element_type=F32)
    return out


def _deltanet_kernel(q_ref, k_ref, v_ref, beta_ref, g_ref, z_ref, nw_ref, o_ref, state_ref):
    c = DN_CHUNK
    t = q_ref.shape[0]
    n_chunks = t // c

    @pl.when(pl.program_id(2) == 0)
    def _():
        state_ref[...] = jnp.zeros_like(state_ref)

    gc_all = g_ref[...]
    beta_all = beta_ref[...]

    gsz = DN_GROUP
    cpg = gsz // c
    ri = lax.broadcasted_iota(jnp.int32, (gsz, gsz), 0)
    ci = lax.broadcasted_iota(jnp.int32, (gsz, gsz), 1)
    same = (ri // c) == (ci // c)
    causal = (ri >= ci) & same
    strict = (ri > ci) & same
    eye = jnp.where(ri == ci, 1.0, 0.0)
    chunk_of_col = lax.broadcasted_iota(jnp.int32, (DN_DIM, gsz), 1) // c

    heads = q_ref.shape[1] // DN_DIM
    groups = range(heads * (t // gsz))
    rows = [slice((p % (t // gsz)) * gsz, (p % (t // gsz) + 1) * gsz) for p in groups]
    cols = [slice((p // (t // gsz)) * DN_DIM, (p // (t // gsz) + 1) * DN_DIM) for p in groups]
    q = [q_ref[rows[p], cols[p]] for p in groups]
    k = [k_ref[rows[p], cols[p]] for p in groups]
    gc = [gc_all[rows[p], cols[p]] for p in groups]
    eg = [jnp.exp(x) for x in gc]
    beta = [beta_all[rows[p], cols[p]] for p in groups]
    kb = [k[gi] * beta[gi] for gi in groups]
    rhs = [jnp.concatenate([kb[gi] * eg[gi], v_ref[rows[gi], cols[gi]] * beta[gi]], axis=1) for gi in groups]
    decay = []
    for gi in groups:
        gc2 = jnp.concatenate([gc[gi]] * (gsz // DN_DIM), axis=1)
        gcr = jnp.sum(gc2 * eye, axis=0, keepdims=True)
        decay.append(jnp.exp(jnp.where(causal, gc2 - gcr, NEG_INF)))
    a2 = [_dot_nt(jnp.concatenate([kb[gi], q[gi]], axis=0).astype(BF16), k[gi].astype(BF16)) for gi in groups]
    low = [jnp.where(strict, a2[gi][:gsz] * decay[gi], 0.0) for gi in groups]
    qk = [a2[gi][gsz:] * decay[gi] for gi in groups]
    size = 2
    pair = ((ri // size) == (ci // size)) & (ri > ci)
    inv = [eye - jnp.where(pair, x, 0.0) for x in low]
    while size < c:
        size *= 2
        pair = ((ri // size) == (ci // size)) & ((ri // (size // 2)) != (ci // (size // 2))) & (ri > ci)
        sub = [jnp.where(pair, x, 0.0) for x in low]
        y = [_mm_dn(sub[gi], inv[gi]) for gi in groups]
        inv = [inv[gi] - _mm_dn(inv[gi], y[gi]) for gi in groups]
    wu = [_mm_dn(inv[gi], rhs[gi]) for gi in groups]
    lasts, res = [], []
    for gi in groups:
        last = [gc[gi][(n + 1) * c - 1:(n + 1) * c, :] for n in range(cpg)]
        lasts.append(last)
        g_last_rows = jnp.concatenate([jnp.broadcast_to(x, (c, DN_DIM)) for x in last], axis=0)
        kd_t = jnp.transpose(k[gi] * jnp.exp(g_last_rows - gc[gi]))
        xs = [jnp.where(chunk_of_col == n, kd_t, 0.0) for n in range(cpg)]
        xs += [qk[gi][n * c:(n + 1) * c, :] for n in range(cpg)]
        res.append(jnp.dot(jnp.concatenate(xs, axis=0).astype(BF16), wu[gi].astype(BF16),
                           preferred_element_type=F32))
    pre = []
    for gi in groups:
        qd = q[gi] * eg[gi]
        for n in range(cpg):
            kw = res[gi][n * DN_DIM:(n + 1) * DN_DIM, :DN_DIM]
            ku = res[gi][n * DN_DIM:(n + 1) * DN_DIM, DN_DIM:]
            base = cpg * DN_DIM + n * c
            coef = qd[n * c:(n + 1) * c, :] - res[gi][base:base + c, :DN_DIM]
            qku = res[gi][base:base + c, DN_DIM:]
            lhs = jnp.concatenate([-kw, coef], axis=0).astype(BF16)
            pre.append((lhs, ku, qku, jnp.exp(lasts[gi][n])))

    states = [state_ref[hd] for hd in range(heads)]
    nw = nw_ref[...]
    for n in range(n_chunks):
        crow = slice(n * c, (n + 1) * c)
        for hd in range(heads):
            lhs, ku, qku, gl = pre[hd * n_chunks + n]
            r = jnp.dot(lhs, states[hd].astype(BF16), preferred_element_type=F32)
            o = r[DN_DIM:] + qku
            states[hd] = states[hd] * gl + r[:DN_DIM] + ku
            ccol = slice(hd * DN_DIM, (hd + 1) * DN_DIM)
            z = z_ref[crow, ccol]
            o_ref[crow, ccol] = (_rms_rows(o, nw) * (z * jax.nn.sigmoid(z))).astype(o_ref.dtype)
    for hd in range(heads):
        state_ref[hd] = states[hd]


def _deltanet(qkv, beta, g, proj, norm_w, bsz, seq, t=1024, heads=2):
    n = qkv.shape[0]
    spb = seq // t
    hw = heads * DN_DIM
    hsteps = DN_HEADS // heads
    zoff = OFF_DZ // hw
    col = lambda off: (lambda b, h, i: (b * spb + i, off + h))
    return pl.pallas_call(
        _deltanet_kernel,
        grid=(bsz, hsteps, spb),
        in_specs=[
            pl.BlockSpec((t, hw), col(0)),
            pl.BlockSpec((t, hw), col(hsteps)),
            pl.BlockSpec((t, hw), col(2 * hsteps)),
            pl.BlockSpec((t, hw), col(0)),
            pl.BlockSpec((t, hw), col(0)),
            pl.BlockSpec((t, hw), col(zoff)),
            pl.BlockSpec((1, DN_DIM), lambda b, h, i: (0, 0)),
        ],
        out_specs=pl.BlockSpec((t, hw), col(0)),
        out_shape=jax.ShapeDtypeStruct((n, DN_HEADS * DN_DIM), BF16),
        scratch_shapes=[pltpu.VMEM((heads, DN_DIM, DN_DIM), F32)],
        compiler_params=_cparams(("arbitrary", "arbitrary", "arbitrary")),
        name="deltanet",
    )(qkv, qkv, qkv, beta, g, proj, norm_w)


def _regroup_w_in(w_in):
    d = w_in.shape[0]
    sizes = (1024, 256, 256, 256, 256, 256, 256, 24, 1024, 1024, 1024, 1024, 8, 8, 2048, 2048)
    offs = np.concatenate([[0], np.cumsum(sizes)])
    (nq, kc, vc, ksel, vsel, kwin, vwin, ngate, dq, dk, dv, dz, db, da, mga, mgb) = (
        w_in[:, offs[i]:offs[i + 1]] for i in range(len(sizes)))
    zeros = lambda w: jnp.zeros((d, w), w_in.dtype)
    ngate = jnp.transpose(ngate.reshape(d, N_GROUPS, HPG, 3), (0, 1, 3, 2)).reshape(d, N_GROUPS, 3 * HPG)
    ngate = jnp.pad(ngate, ((0, 0), (0, 0), (0, GATE_ROWS - 3 * HPG))).reshape(d, N_GROUPS * GATE_ROWS)
    cols = [mga, mgb, dq, dk, dv, nq, dz, kc, vc, ksel, vsel, kwin, vwin,
            ngate, zeros(LANES - N_GROUPS * GATE_ROWS), db, da, zeros(LANES - 16), zeros(PROJ_DIM - PROJ_USED)]
    return jnp.concatenate(cols, axis=1).astype(BF16)


def _overlap_t(n_sb, n_pieces):
    n = np.arange(N_CMP_PAD) - CMP_PAD
    j = np.arange(n_sb)
    valid = (n >= 0) & (n < n_pieces - 1)
    c0 = n * CMP_STRIDE
    s0 = j * SEL_BLOCK
    ov = (c0[None, :] < s0[:, None] + SEL_BLOCK) & (c0[None, :] + CMP_LEN > s0[:, None]) & valid[None, :]
    return jnp.asarray(ov.astype(np.float32), dtype=BF16)


def _token_mixer(h, bsz, seq, norm_mix, w_in, q_norm, k_norm, cmp_pos, cmp_w1, cmp_b1, cmp_w2, dn_conv,
                 dn_a_log, dn_dt_bias, dn_norm, w_a, w_b, w_out, bias_tiles):
    n = bsz * seq
    nqb = seq // Q_BLOCK
    npc = seq // CMP_STRIDE
    n_sb = seq // SEL_BLOCK
    assert npc + CMP_PAD <= N_CMP_PAD and nqb * SUBLANES + NEAR_ROWS <= N_CMP_PAD
    assert nqb % (2 * FAR_SUB) == 0
    proj = _norm_matmul(h, norm_mix.reshape(1, -1), _regroup_w_in(w_in))

    k_norm8 = jnp.concatenate([k_norm, jnp.zeros((SUBLANES - 3, HEAD_DIM), F32)], axis=0)
    qT, kc_raw, vc_raw, ks, vsT, kw, vwT, gT = _nsa_prep(proj, q_norm.reshape(1, -1), k_norm8, bsz, seq)
    pieces = jnp.stack([kc_raw, vc_raw], axis=0).reshape(2, bsz, N_GROUPS, npc, CMP_STRIDE * HEAD_DIM)
    cmp = _compress(pieces, cmp_w1, cmp_pos.reshape(2, 1, CMP_LEN * HEAD_DIM), cmp_b1.reshape(2, 1, -1),
                    cmp_w2, k_norm8)
    cmp = jnp.pad(cmp, ((0, 0), (0, 0), (0, 0), (CMP_PAD, N_CMP_PAD - CMP_PAD - npc), (0, 0))).astype(BF16)
    kc = cmp[0]
    vcT = jnp.swapaxes(cmp[1], -1, -2)
    y_nsa = _nsa_attn(qT, gT, ks, vsT, kw, vwT, kc, vcT, _overlap_t(n_sb, npc), *bias_tiles)

    lane_pad = lambda v, off: jnp.zeros((1, LANES), F32).at[0, off:off + DN_HEADS].set(v)
    qkv, beta, g = _dn_prep(proj, dn_conv, lane_pad(dn_a_log, DN_HEADS), lane_pad(dn_dt_bias, DN_HEADS), seq)
    y_dn = _deltanet(qkv, beta, g, proj, dn_norm.reshape(1, -1), bsz, seq)

    return _merge_out(y_nsa, y_dn, w_a.astype(BF16), w_b.astype(BF16), proj, w_out.astype(BF16), h)


def kernel(x, rel_table, norm_ffn1, ffn1_gate, ffn1_up, ffn1_down, norm_mix, w_in, q_norm, k_norm, cmp_pos, cmp_w1, cmp_b1, cmp_w2, dn_conv, dn_a_log, dn_dt_bias, dn_norm, w_branch_nsa, w_branch_dn, w_out, norm_ffn2, ffn2_gate, ffn2_up, ffn2_down):
    bsz, seq, d = x.shape
    depth = w_in.shape[0]
    h = x.reshape(bsz * seq, d)
    bias_tiles = _bias_tiles(rel_table)
    for l in range(depth):
        h = _ffn(h, norm_ffn1[l].reshape(1, -1), ffn1_gate[l].astype(BF16), ffn1_up[l].astype(BF16),
                 ffn1_down[l].astype(BF16))
        h = _token_mixer(h, bsz, seq, norm_mix[l], w_in[l], q_norm[l], k_norm[l], cmp_pos[l], cmp_w1[l],
                         cmp_b1[l], cmp_w2[l], dn_conv[l], dn_a_log[l], dn_dt_bias[l], dn_norm[l],
                         w_branch_nsa[l], w_branch_dn[l], w_out[l], bias_tiles)
        h = _ffn(h, norm_ffn2[l].reshape(1, -1), ffn2_gate[l].astype(BF16), ffn2_up[l].astype(BF16),
                 ffn2_down[l].astype(BF16))
    return h.reshape(bsz, seq, d)
```

```python
import functools
import math

import numpy as np
import jax
import jax.numpy as jnp
from jax import lax
from jax.experimental import pallas as pl
from jax.experimental.pallas import tpu as pltpu

F32 = jnp.float32
BF16 = jnp.bfloat16
HI = lax.Precision.HIGHEST

D_MODEL = 2048
D_FF = 5632
N_HEADS = 8
N_GROUPS = 2
HPG = N_HEADS // N_GROUPS
HEAD_DIM = 128
CMP_LEN = 32
CMP_STRIDE = 16
CMP_HIDDEN = 128
SEL_BLOCK = 64
SEL_TOPK = 16
WINDOW = 512
Q_BLOCK = 128
DN_HEADS = 8
DN_DIM = 128
DN_CONV = 4
DN_CHUNK = 64
DN_GROUP = 256
REL_BUCKETS = 32
REL_MAX_DIST = 128
EPS = 1e-6
LOG2E = math.log2(math.e)
NEG_INF = -1e30
BIG = 1e30

LANES = 128
SUBLANES = 8
VMEM_LIMIT = 56 * 1024 * 1024

OFF_MGA = 0
OFF_MGB = 2048
OFF_DQ, OFF_DK, OFF_DV = 4096, 5120, 6144
OFF_NQ = 7168
OFF_DZ = 8192
OFF_KV6 = 9216
OFF_NGATE = 10752
OFF_DBA = 10880
PROJ_USED = 11008
PROJ_DIM = 11264

CMP_PAD = 16
N_CMP_PAD = 1152
NEAR_ROWS = 24
NEAR_KEYS = 2 * Q_BLOCK
WIN_KEYS = WINDOW + Q_BLOCK
GATE_ROWS = 16
V_ROWS = HEAD_DIM + 16
FAR_SUB = 4
FAR_PAIRS = 4
FAR_TILES = 2 * FAR_PAIRS * FAR_SUB
QB_PER_STEP = 2


def _rel_bucket_thresholds():
    n = np.arange(0, 4 * REL_MAX_DIST, dtype=np.int64)
    max_exact = REL_BUCKETS // 2
    nf = np.maximum(n, max_exact).astype(np.float32)
    large = max_exact + (np.log(nf / np.float32(max_exact)) / np.float32(math.log(REL_MAX_DIST / max_exact))
                         * np.float32(REL_BUCKETS - max_exact)).astype(np.int32)
    bucket = np.where(n < max_exact, n, np.minimum(large, REL_BUCKETS - 1))
    assert np.all(np.diff(bucket) >= 0)
    return [int(np.argmax(bucket >= b)) for b in range(REL_BUCKETS)]


REL_THRESH = _rel_bucket_thresholds()
FAR_DIST = REL_THRESH[REL_BUCKETS - 1]
assert FAR_DIST <= Q_BLOCK


def _cparams(sem, vmem_limit=VMEM_LIMIT):
    return pltpu.CompilerParams(dimension_semantics=sem, vmem_limit_bytes=vmem_limit)


def _rms_rows(x, w_row):
    ms = jnp.mean(x * x, axis=-1, keepdims=True)
    return x * lax.rsqrt(ms + EPS) * w_row


def _ffn_kernel(x_ref, nw_ref, wg_ref, wu_ref, wd_ref, o_ref, xn_ref, acc_ref):
    j = pl.program_id(1)

    @pl.when(j == 0)
    def _():
        xn_ref[...] = _rms_rows(x_ref[...], nw_ref[...]).astype(BF16)
        acc_ref[...] = jnp.zeros_like(acc_ref)

    xn = xn_ref[...]
    g = jnp.dot(xn, wg_ref[...], preferred_element_type=F32)
    u = jnp.dot(xn, wu_ref[...], preferred_element_type=F32)
    a = (g * jax.nn.sigmoid(g) * u).astype(BF16)
    acc_ref[...] += jnp.dot(a, wd_ref[...], preferred_element_type=F32)

    @pl.when(j == pl.num_programs(1) - 1)
    def _():
        o_ref[...] = x_ref[...] + 0.5 * acc_ref[...]


def _ffn(h, nw, wg, wu, wd, tm=512, tf=512):
    n, d = h.shape
    f = wg.shape[1]
    return pl.pallas_call(
        _ffn_kernel,
        grid=(n // tm, f // tf),
        in_specs=[
            pl.BlockSpec((tm, d), lambda i, j: (i, 0)),
            pl.BlockSpec((1, d), lambda i, j: (0, 0)),
            pl.BlockSpec((d, tf), lambda i, j: (0, j)),
            pl.BlockSpec((d, tf), lambda i, j: (0, j)),
            pl.BlockSpec((tf, d), lambda i, j: (j, 0)),
        ],
        out_specs=pl.BlockSpec((tm, d), lambda i, j: (i, 0)),
        out_shape=jax.ShapeDtypeStruct((n, d), F32),
        scratch_shapes=[pltpu.VMEM((tm, d), BF16), pltpu.VMEM((tm, d), F32)],
        compiler_params=_cparams(("parallel", "arbitrary")),
        name="ffn",
    )(h, nw, wg, wu, wd)


def _norm_matmul_kernel(x_ref, nw_ref, w_ref, o_ref, xn_ref):
    @pl.when(pl.program_id(1) == 0)
    def _():
        xn_ref[...] = _rms_rows(x_ref[...], nw_ref[...]).astype(BF16)

    o_ref[...] = jnp.dot(xn_ref[...], w_ref[...], preferred_element_type=F32)


def _norm_matmul(h, nw, w, tm=1024, tn=1024):
    n, d = h.shape
    nout = w.shape[1]
    return pl.pallas_call(
        _norm_matmul_kernel,
        grid=(n // tm, nout // tn),
        in_specs=[
            pl.BlockSpec((tm, d), lambda i, j: (i, 0)),
            pl.BlockSpec((1, d), lambda i, j: (0, 0)),
            pl.BlockSpec((d, tn), lambda i, j: (0, j)),
        ],
        out_specs=pl.BlockSpec((tm, tn), lambda i, j: (i, j)),
        out_shape=jax.ShapeDtypeStruct((n, nout), F32),
        scratch_shapes=[pltpu.VMEM((tm, d), BF16)],
        compiler_params=_cparams(("parallel", "arbitrary")),
        name="in_proj",
    )(h, nw, w)


def _merge_out_kernel(ya_ref, yb_ref, wa_ref, wb_ref, ga_ref, gb_ref, wo_ref, h_ref, o_ref):
    a = jnp.dot(ya_ref[...], wa_ref[...], preferred_element_type=F32)
    b = jnp.dot(yb_ref[...], wb_ref[...], preferred_element_type=F32)
    merged = (jax.nn.sigmoid(ga_ref[...]) * a + jax.nn.sigmoid(gb_ref[...]) * b).astype(BF16)
    o_ref[...] = h_ref[...] + jnp.dot(merged, wo_ref[...], preferred_element_type=F32)


def _merge_out(ya, yb, wa, wb, proj, wo, h, tm=512):
    n, k = ya.shape
    d = wa.shape[1]
    resident = dict(pipeline_mode=pl.Buffered(1))
    return pl.pallas_call(
        _merge_out_kernel,
        grid=(n // tm,),
        in_specs=[
            pl.BlockSpec((tm, k), lambda i: (i, 0)),
            pl.BlockSpec((tm, k), lambda i: (i, 0)),
            pl.BlockSpec((k, d), lambda i: (0, 0), **resident),
            pl.BlockSpec((k, d), lambda i: (0, 0), **resident),
            pl.BlockSpec((tm, d), lambda i: (i, OFF_MGA // d)),
            pl.BlockSpec((tm, d), lambda i: (i, OFF_MGB // d)),
            pl.BlockSpec((d, d), lambda i: (0, 0), **resident),
            pl.BlockSpec((tm, d), lambda i: (i, 0)),
        ],
        out_specs=pl.BlockSpec((tm, d), lambda i: (i, 0)),
        out_shape=jax.ShapeDtypeStruct((n, d), F32),
        compiler_params=_cparams(("parallel",), vmem_limit=60000 * 1024),
        name="merge_out",
    )(ya, yb, wa, wb, proj, proj, wo, h)


def _nsa_prep_kernel(nq_ref, kv_ref, ng_ref, qn_ref, kn_ref,
                     qT_out, kc_out, vc_out, ks_out, vsT_out, kw_out, vwT_out, gT_out):
    tm = nq_ref.shape[0]
    scale = HEAD_DIM ** -0.5 * LOG2E
    for hh in range(N_HEADS):
        g, h = divmod(hh, HPG)
        qn = _rms_rows(nq_ref[:, hh * HEAD_DIM:(hh + 1) * HEAD_DIM], qn_ref[...]) * scale
        for qi in range(tm // Q_BLOCK):
            blk = qn[qi * Q_BLOCK:(qi + 1) * Q_BLOCK, :]
            qT_out[g, qi, :, h * Q_BLOCK:(h + 1) * Q_BLOCK] = jnp.transpose(blk).astype(BF16)
    ones_rows = jnp.where(lax.broadcasted_iota(jnp.int32, (V_ROWS - HEAD_DIM, tm), 0) == 0, 1.0, 0.0)
    for g in range(N_GROUPS):
        col = lambda kind: kv_ref[:, (kind * N_GROUPS + g) * HEAD_DIM:(kind * N_GROUPS + g + 1) * HEAD_DIM]
        kc_out[g] = col(0).astype(BF16)
        vc_out[g] = col(1).astype(BF16)
        ks_out[g, :, 0:HEAD_DIM] = _rms_rows(col(2), kn_ref[1:2, :]).astype(BF16)
        tok = pl.program_id(0) * tm + lax.broadcasted_iota(jnp.int32, (tm, HEAD_DIM), 0)
        blk_of_row = (tok % (FAR_SUB * Q_BLOCK)) // SEL_BLOCK
        ks_out[g, :, HEAD_DIM:] = jnp.where(
            lax.broadcasted_iota(jnp.int32, (tm, HEAD_DIM), 1) == blk_of_row, 1.0, 0.0).astype(BF16)
        kw_out[g] = _rms_rows(col(4), kn_ref[2:3, :]).astype(BF16)
        for v_out, kind in ((vsT_out, 3), (vwT_out, 5)):
            v_out[g, 0:HEAD_DIM, :] = jnp.transpose(col(kind)).astype(BF16)
            v_out[g, HEAD_DIM:, :] = ones_rows.astype(BF16)
    sig = jax.nn.sigmoid(ng_ref[...])
    for qi in range(tm // Q_BLOCK):
        t = jnp.transpose(sig[qi * Q_BLOCK:(qi + 1) * Q_BLOCK, :])
        for g in range(N_GROUPS):
            gT_out[g, qi] = t[g * GATE_ROWS:(g + 1) * GATE_ROWS, :]


def _nsa_prep(proj, q_norm, k_norm, bsz, seq, tm=512):
    kvw = 6 * N_GROUPS * HEAD_DIM
    spb = seq // tm
    qpb = tm // Q_BLOCK
    nqb = seq // Q_BLOCK
    wide = HPG * Q_BLOCK
    keys = pl.BlockSpec((None, N_GROUPS, tm, HEAD_DIM), lambda i: (i // spb, 0, i % spb, 0))
    keys_aug = pl.BlockSpec((None, N_GROUPS, tm, 2 * HEAD_DIM), lambda i: (i // spb, 0, i % spb, 0))
    vals = pl.BlockSpec((None, N_GROUPS, V_ROWS, tm), lambda i: (i // spb, 0, 0, i % spb))
    keys_shape = jax.ShapeDtypeStruct((bsz, N_GROUPS, seq, HEAD_DIM), BF16)
    vals_shape = jax.ShapeDtypeStruct((bsz, N_GROUPS, V_ROWS, seq), BF16)
    return pl.pallas_call(
        _nsa_prep_kernel,
        grid=(bsz * spb,),
        in_specs=[
            pl.BlockSpec((tm, 1024), lambda i: (i, OFF_NQ // 1024)),
            pl.BlockSpec((tm, kvw), lambda i: (i, OFF_KV6 // kvw)),
            pl.BlockSpec((tm, LANES), lambda i: (i, OFF_NGATE // LANES)),
            pl.BlockSpec((1, HEAD_DIM), lambda i: (0, 0)),
            pl.BlockSpec((SUBLANES, HEAD_DIM), lambda i: (0, 0)),
        ],
        out_specs=[
            pl.BlockSpec((None, N_GROUPS, qpb, HEAD_DIM, wide), lambda i: (i // spb, 0, i % spb, 0, 0)),
            keys, keys, keys_aug, vals, keys, vals,
            pl.BlockSpec((None, N_GROUPS, qpb, GATE_ROWS, Q_BLOCK), lambda i: (i // spb, 0, i % spb, 0, 0)),
        ],
        out_shape=[
            jax.ShapeDtypeStruct((bsz, N_GROUPS, nqb, HEAD_DIM, wide), BF16),
            keys_shape, keys_shape, jax.ShapeDtypeStruct((bsz, N_GROUPS, seq, 2 * HEAD_DIM), BF16), vals_shape,
            keys_shape, vals_shape,
            jax.ShapeDtypeStruct((bsz, N_GROUPS, nqb, GATE_ROWS, Q_BLOCK), F32),
        ],
        compiler_params=_cparams(("parallel",)),
        name="nsa_prep",
    )(proj, proj, proj, q_norm, k_norm)


def _compress_kernel(p_ref, w1_ref, pos_ref, b1_ref, w2_ref, kn_ref, o_ref):
    half = CMP_STRIDE * HEAD_DIM
    p = p_ref[...]
    w1 = w1_ref[...]
    a = jnp.dot(p, w1[:half].astype(BF16), preferred_element_type=F32)
    b = jnp.dot(p, w1[half:].astype(BF16), preferred_element_type=F32)
    pos = jnp.broadcast_to(pos_ref[...], (SUBLANES, 2 * half))
    c0 = jnp.dot(pos, w1, preferred_element_type=F32, precision=HI)[0:1, :] + b1_ref[...]
    n = p.shape[0]
    hid = a + pltpu.roll(b, n - 1, 0) + c0
    hid = hid * jax.nn.sigmoid(hid)
    out = jnp.dot(hid.astype(BF16), w2_ref[...].astype(BF16), preferred_element_type=F32)
    normed = _rms_rows(out, kn_ref[0:1, :])
    out = jnp.where(pl.program_id(0) == 0, normed, out)
    row = lax.broadcasted_iota(jnp.int32, out.shape, 0)
    o_ref[...] = jnp.where(row < n - 1, out, 0.0)


def _compress(pieces, w1, pos, b1, w2, k_norm):
    _, bsz, ng, npc, width = pieces.shape
    return pl.pallas_call(
        _compress_kernel,
        grid=(2, bsz, ng),
        in_specs=[
            pl.BlockSpec((None, None, None, npc, width), lambda c, b, g: (c, b, g, 0, 0)),
            pl.BlockSpec((None, 2 * width, CMP_HIDDEN), lambda c, b, g: (c, 0, 0)),
            pl.BlockSpec((None, 1, 2 * width), lambda c, b, g: (c, 0, 0)),
            pl.BlockSpec((None, 1, CMP_HIDDEN), lambda c, b, g: (c, 0, 0)),
            pl.BlockSpec((None, CMP_HIDDEN, HEAD_DIM), lambda c, b, g: (c, 0, 0)),
            pl.BlockSpec((SUBLANES, HEAD_DIM), lambda c, b, g: (0, 0)),
        ],
        out_specs=pl.BlockSpec((None, None, None, npc, HEAD_DIM), lambda c, b, g: (c, b, g, 0, 0)),
        out_shape=jax.ShapeDtypeStruct((2, bsz, ng, npc, HEAD_DIM), F32),
        compiler_params=_cparams(("arbitrary", "arbitrary", "arbitrary")),
        name="compress",
    )(pieces, w1, pos, b1, w2, k_norm)


def _bias_kernel(tab_ref, bn_ref, bw_ref, bc_ref):
    g = pl.program_id(0)
    n_wt = WINDOW // Q_BLOCK

    def lookup(dist, head):
        v = jnp.full(dist.shape, tab_ref[0, head], F32)
        for b in range(1, REL_BUCKETS):
            v = jnp.where(dist >= REL_THRESH[b], tab_ref[b, head], v)
        return (v - tab_ref[REL_BUCKETS - 1, head]) * LOG2E

    ki = lax.broadcasted_iota(jnp.int32, (Q_BLOCK, Q_BLOCK), 0)
    qj = lax.broadcasted_iota(jnp.int32, (Q_BLOCK, Q_BLOCK), 1)
    r = lax.broadcasted_iota(jnp.int32, (NEAR_ROWS, Q_BLOCK), 0)
    qc = lax.broadcasted_iota(jnp.int32, (NEAR_ROWS, Q_BLOCK), 1)
    dist_c = qc - CMP_STRIDE * (r - CMP_PAD) - (CMP_LEN - 1)
    for h in range(HPG):
        head = g * HPG + h
        sl = slice(h * Q_BLOCK, (h + 1) * Q_BLOCK)
        diag = jnp.where(ki <= qj, lookup(qj - ki, head), NEG_INF)
        prev = lookup(qj - ki + Q_BLOCK, head)
        bn_ref[0:Q_BLOCK, sl] = prev
        bn_ref[Q_BLOCK:, sl] = diag
        bw_ref[0:Q_BLOCK, sl] = jnp.where(ki > qj, 0.0, NEG_INF)
        bw_ref[Q_BLOCK:(n_wt - 1) * Q_BLOCK, sl] = jnp.zeros(((n_wt - 2) * Q_BLOCK, Q_BLOCK), F32)
        bw_ref[(n_wt - 1) * Q_BLOCK:n_wt * Q_BLOCK, sl] = prev
        bw_ref[n_wt * Q_BLOCK:, sl] = diag
        bc_ref[:, sl] = jnp.where(dist_c >= 0, lookup(dist_c, head), NEG_INF)


def _bias_tiles(rel_table):
    wide = HPG * Q_BLOCK
    return pl.pallas_call(
        _bias_kernel,
        grid=(N_GROUPS,),
        in_specs=[pl.BlockSpec(memory_space=pltpu.SMEM)],
        out_specs=[
            pl.BlockSpec((None, NEAR_KEYS, wide), lambda g: (g, 0, 0)),
            pl.BlockSpec((None, WIN_KEYS, wide), lambda g: (g, 0, 0)),
            pl.BlockSpec((None, NEAR_ROWS, wide), lambda g: (g, 0, 0)),
        ],
        out_shape=[
            jax.ShapeDtypeStruct((N_GROUPS, NEAR_KEYS, wide), F32),
            jax.ShapeDtypeStruct((N_GROUPS, WIN_KEYS, wide), F32),
            jax.ShapeDtypeStruct((N_GROUPS, NEAR_ROWS, wide), F32),
        ],
        compiler_params=_cparams(("arbitrary",)),
        name="rel_bias",
    )(rel_table)


def _tile4(x):
    return jnp.concatenate([x] * HPG, axis=1)


def _nsa_attn_kernel(qT_ref, gT_ref, ks_ref, vsT_ref, kw_ref, vwT_ref, kc_ref, vcT_ref, ovT_ref,
                     bn_ref, bw_ref, bc_ref, o_ref, sc_ref, sel_ref, far_ref, sa_ref, sb_ref,
                     ra_ref, rb_ref):
    step = pl.program_id(2)
    wide = HPG * Q_BLOCK
    half = SEL_BLOCK
    n_wt = WINDOW // Q_BLOCK
    sub_w = FAR_SUB * Q_BLOCK
    blocks_per_sub = sub_w // SEL_BLOCK
    bf16_rows = 2 * SUBLANES
    mask_rows = -(-blocks_per_sub // bf16_rows) * bf16_rows
    assert mask_rows <= HEAD_DIM
    last_chunk = ks_ref.shape[0] // sub_w - 1

    def absent(cond):
        return jnp.where(cond, 0.0, NEG_INF)

    def key_rows(k_ref, kt, width=Q_BLOCK):
        return k_ref[pl.ds(pl.multiple_of(kt * width, width), width), :]

    def value_cols(vT_ref, kt, width=Q_BLOCK):
        return vT_ref[:, pl.ds(pl.multiple_of(kt * width, width), width)]

    def first(s, vT):
        m = jnp.max(s, axis=0, keepdims=True)
        p = jnp.exp2(s - m)
        return m, jnp.dot(vT, p.astype(BF16), preferred_element_type=F32)

    def update(state, s, vT_ref, kt, width=Q_BLOCK):
        m, acc = state
        m_new = jnp.maximum(m, jnp.max(s, axis=0, keepdims=True))
        alpha = jnp.exp2(m - m_new)
        p = jnp.exp2(s - m_new)
        acc = alpha * acc + jnp.dot(value_cols(vT_ref, kt, width), p.astype(BF16), preferred_element_type=F32)
        return m_new, acc

    def normalized(state):
        acc = state[1]
        return acc[:HEAD_DIM] * (1.0 / acc[HEAD_DIM:HEAD_DIM + 1])

    def straight(u):
        qb = step * QB_PER_STEP + u
        sc, sel, far = sc_ref.at[u], sel_ref.at[u], far_ref.at[u]
        qT = qT_ref[u]
        qpos = qb * Q_BLOCK + lax.broadcasted_iota(jnp.int32, (1, Q_BLOCK), 1)

        sc[...] = jnp.dot(kc_ref[...], qT, preferred_element_type=F32)
        sc[0:CMP_PAD, :] = jnp.full((CMP_PAD, wide), NEG_INF, F32)
        r0 = pl.multiple_of(qb * SUBLANES, SUBLANES)
        sc[pl.ds(r0, NEAR_ROWS), :] += bc_ref[...]
        rown = lax.broadcasted_iota(jnp.int32, (N_CMP_PAD, wide), 0)
        qpos4 = qb * Q_BLOCK + (lax.broadcasted_iota(jnp.int32, (1, wide), 1) & (Q_BLOCK - 1))
        s = jnp.where(rown < qb * SUBLANES + NEAR_ROWS, sc[...], NEG_INF)
        m = jnp.max(s, axis=0, keepdims=True)
        p = jnp.exp2(s - m)
        l = jnp.sum(p, axis=0, keepdims=True)
        inv = jnp.where(qpos4 >= CMP_LEN - 1, 1.0 / l, 0.0)
        pn = p * inv
        o_c = jnp.dot(vcT_ref[...], pn.astype(BF16), preferred_element_type=F32)
        psum = pn[:, 0:Q_BLOCK]
        for h in range(1, HPG):
            psum = psum + pn[:, h * Q_BLOCK:(h + 1) * Q_BLOCK]
        p_hi = psum.astype(BF16)
        p_lo = (psum - p_hi.astype(F32)).astype(BF16)
        ovT = ovT_ref[...]
        imp = (jnp.dot(ovT, p_hi, preferred_element_type=F32)
               + jnp.dot(ovT, p_lo, preferred_element_type=F32))

        n_sb = imp.shape[0]
        jblk = lax.broadcasted_iota(jnp.int32, (n_sb, Q_BLOCK), 0)
        cur = qpos // SEL_BLOCK
        eligible = jblk * SEL_BLOCK <= qpos
        forced = (jblk == 0) | (jblk == cur) | (jblk == cur - 1)
        picked = -3e38
        score = jnp.where(eligible, jnp.where(forced, picked, imp), NEG_INF)
        for _ in range(max(min(SEL_TOPK, n_sb) - 3, 0)):
            top = jnp.max(score, axis=0, keepdims=True)
            idx = jnp.min(jnp.where(score == top, jblk, n_sb), axis=0, keepdims=True)
            idx = jnp.where(top > -1.0, idx, -1)
            score = jnp.where(jblk == idx, picked, score)
        selmask = jnp.where(score == picked, 0.0, NEG_INF)
        sel[...] = selmask
        far[...] = jnp.where(jblk < 2 * (qb - 1), selmask, NEG_INF)

        def sel_mask_tile(kt):
            a = jnp.broadcast_to(sel[pl.ds(2 * kt, 1), :], (half, Q_BLOCK))
            b = jnp.broadcast_to(sel[pl.ds(2 * kt + 1, 1), :], (half, Q_BLOCK))
            return _tile4(jnp.concatenate([a, b], axis=0))

        tiles = [jnp.maximum(qb - back, 0) for back in range(n_wt, -1, -1)]
        kwin = jnp.concatenate([key_rows(kw_ref, kt) for kt in tiles], axis=0)
        sw = jnp.dot(kwin, qT, preferred_element_type=F32) + bw_ref[...]
        sw = jnp.concatenate(
            [sw[t * Q_BLOCK:(t + 1) * Q_BLOCK] + absent(qb >= n_wt - t) for t in range(n_wt)]
            + [sw[n_wt * Q_BLOCK:]], axis=0)
        o_w = normalized(first(sw, jnp.concatenate([value_cols(vwT_ref, kt) for kt in tiles], axis=1)))

        kp = jnp.maximum(qb - 1, 0)
        k2 = jnp.concatenate([key_rows(ks_ref, kp), key_rows(ks_ref, qb)], axis=0)[:, 0:HEAD_DIM]
        near_mask = jnp.concatenate([sel_mask_tile(kp) + absent(qb >= 1), sel_mask_tile(qb)], axis=0)
        s = jnp.dot(k2, qT, preferred_element_type=F32) + bn_ref[...] + near_mask
        st = first(s, jnp.concatenate([value_cols(vsT_ref, kp), value_cols(vsT_ref, qb)], axis=1))

        for r_ref in (ra_ref.at[u], rb_ref.at[u]):
            r_ref[0:HEAD_DIM, :] = qT
            r_ref[HEAD_DIM:, :] = jnp.zeros((HEAD_DIM, wide), BF16)
        return o_c, o_w, st

    def far_scores(u, cs, r_ref):
        rows = _tile4(far_ref[u, pl.ds(pl.multiple_of(blocks_per_sub * cs, blocks_per_sub), blocks_per_sub), :])
        if mask_rows > blocks_per_sub:
            rows = jnp.concatenate([rows, jnp.zeros((mask_rows - blocks_per_sub, wide), F32)], axis=0)
        r_ref[u, HEAD_DIM:HEAD_DIM + mask_rows, :] = rows.astype(BF16)
        k = ks_ref[pl.ds(pl.multiple_of(cs * sub_w, sub_w), sub_w), :]
        return jnp.dot(k, r_ref[u], preferred_element_type=F32)

    blocks = range(QB_PER_STEP)
    pre = [straight(u) for u in blocks]

    def pair_step(c0, sts):
        sts = list(sts)
        for u in blocks:
            sb_ref[u] = far_scores(u, c0 + 1, rb_ref)
        for u in blocks:
            sts[u] = update(sts[u], sa_ref[u], vsT_ref, c0, sub_w)
        for u in blocks:
            sa_ref[u] = far_scores(u, jnp.minimum(c0 + 2, last_chunk), ra_ref)
        for u in blocks:
            sts[u] = update(sts[u], sb_ref[u], vsT_ref, c0 + 1, sub_w)
        return tuple(sts)

    def far_body(i, sts):
        for pair in range(FAR_PAIRS):
            sts = pair_step(2 * (FAR_PAIRS * i + pair), sts)
        return sts

    far_tiles = jnp.maximum(step * QB_PER_STEP + QB_PER_STEP - 2, 0)
    n_full = far_tiles // FAR_TILES
    n_pairs = (far_tiles + 2 * FAR_SUB - 1) // (2 * FAR_SUB)
    for u in blocks:
        sa_ref[u] = far_scores(u, 0, ra_ref)
    sts = tuple(pre[u][2] for u in blocks)
    sts = lax.fori_loop(0, n_full, far_body, sts)
    sts = lax.fori_loop(FAR_PAIRS * n_full, n_pairs, lambda j, sts: pair_step(2 * j, sts), sts)

    for u in blocks:
        o_c, o_w, _ = pre[u]
        o_s = normalized(sts[u])
        for h in range(HPG):
            sl = slice(h * Q_BLOCK, (h + 1) * Q_BLOCK)
            gate = lambda c: gT_ref[u, c * HPG + h:c * HPG + h + 1, :]
            mix = gate(0) * o_c[:, sl] + gate(1) * o_s[:, sl] + gate(2) * o_w[:, sl]
            o_ref[u * Q_BLOCK:(u + 1) * Q_BLOCK, h * HEAD_DIM:(h + 1) * HEAD_DIM] = (
                jnp.transpose(mix).astype(o_ref.dtype))


def _nsa_attn(qT, gT, ks, vsT, kw, vwT, kc, vcT, ovT, bn, bw, bc):
    bsz, ng, nqb, _, wide = qT.shape
    seq = ks.shape[2]
    n_sb = ovT.shape[0]
    qs = QB_PER_STEP
    steps = nqb // qs
    big = lambda b, g, q: (b, g, 0, 0)
    blk = lambda b, g, q: (b, g, q, 0, 0)
    grp = lambda b, g, q: (g, 0, 0)
    single = dict(pipeline_mode=pl.Buffered(1))
    return pl.pallas_call(
        _nsa_attn_kernel,
        grid=(bsz, ng, steps),
        in_specs=[
            pl.BlockSpec((None, None, qs, HEAD_DIM, wide), blk),
            pl.BlockSpec((None, None, qs, GATE_ROWS, Q_BLOCK), blk),
            pl.BlockSpec((None, None, seq, 2 * HEAD_DIM), big, **single),
            pl.BlockSpec((None, None, V_ROWS, seq), big, **single),
            pl.BlockSpec((None, None, seq, HEAD_DIM), big, **single),
            pl.BlockSpec((None, None, V_ROWS, seq), big, **single),
            pl.BlockSpec((None, None, N_CMP_PAD, HEAD_DIM), big),
            pl.BlockSpec((None, None, HEAD_DIM, N_CMP_PAD), big),
            pl.BlockSpec((n_sb, N_CMP_PAD), lambda b, g, q: (0, 0)),
            pl.BlockSpec((None, NEAR_KEYS, wide), grp),
            pl.BlockSpec((None, WIN_KEYS, wide), grp),
            pl.BlockSpec((None, NEAR_ROWS, wide), grp),
        ],
        out_specs=pl.BlockSpec((qs * Q_BLOCK, HPG * HEAD_DIM), lambda b, g, q: (b * steps + q, g)),
        out_shape=jax.ShapeDtypeStruct((bsz * seq, ng * HPG * HEAD_DIM), BF16),
        scratch_shapes=[pltpu.VMEM((qs, N_CMP_PAD, wide), F32), pltpu.VMEM((qs, n_sb, Q_BLOCK), F32),
                        pltpu.VMEM((qs, n_sb, Q_BLOCK), F32),
                        pltpu.VMEM((qs, FAR_SUB * Q_BLOCK, wide), F32),
                        pltpu.VMEM((qs, FAR_SUB * Q_BLOCK, wide), F32),
                        pltpu.VMEM((qs, 2 * HEAD_DIM, wide), BF16), pltpu.VMEM((qs, 2 * HEAD_DIM, wide), BF16)],
        compiler_params=_cparams(("arbitrary", "arbitrary", "arbitrary")),
        name="nsa_attn",
    )(qT, gT, ks, vsT, kw, vwT, kc, vcT, ovT, bn, bw, bc)


def _dot_exact01(a, b, left):
    x = b if left else a
    x1 = x.astype(BF16)
    r1 = x - x1.astype(F32)
    x2 = r1.astype(BF16)
    x3 = (r1 - x2.astype(F32)).astype(BF16)
    mm = (lambda p: jnp.dot(a, p, preferred_element_type=F32)) if left else (
        lambda p: jnp.dot(p, b, preferred_element_type=F32))
    return mm(x1) + mm(x2) + mm(x3)


def _dn_prep_kernel(x_ref, halo_ref, w_ref, dba_ref, alog_ref, dtb_ref, o_ref, beta_ref, g_ref, buf_ref,
                    *, blocks_per_seq):
    i = pl.program_id(0)
    j = pl.program_id(1)
    t = x_ref.shape[0]
    first_of_seq = (i % blocks_per_seq) == 0
    buf_ref[0:SUBLANES, :] = jnp.where(first_of_seq, 0.0, halo_ref[...])
    buf_ref[SUBLANES:, :] = x_ref[...]
    scale = jnp.where(j == 0, DN_DIM ** -0.5, 1.0)
    for h in range(DN_HEADS):
        sl = slice(h * DN_DIM, (h + 1) * DN_DIM)
        y = w_ref[DN_CONV - 1:DN_CONV, sl] * x_ref[:, sl]
        for back in range(1, DN_CONV):
            y = y + w_ref[DN_CONV - 1 - back:DN_CONV - back, sl] * buf_ref[pl.ds(SUBLANES - back, t), sl]
        y = y * jax.nn.sigmoid(y)
        nrm = y * lax.rsqrt(jnp.sum(y * y, axis=-1, keepdims=True) + EPS) * scale
        o_ref[:, sl] = jnp.where(j < 2, nrm, y)

    @pl.when(j == 0)
    def _():
        dba = dba_ref[...]
        beta = jax.nn.sigmoid(dba)
        z = dba + dtb_ref[...]
        softplus = jnp.maximum(z, 0.0) + jnp.log1p(jnp.exp(-jnp.abs(z)))
        g = -jnp.exp(alog_ref[...]) * softplus
        rt = lax.broadcasted_iota(jnp.int32, (t, t), 0)
        ct = lax.broadcasted_iota(jnp.int32, (t, t), 1)
        blocktri = jnp.where((rt >= ct) & (rt // DN_CHUNK == ct // DN_CHUNK), 1.0, 0.0).astype(BF16)
        g = _dot_exact01(blocktri, g, left=True)
        wide = DN_HEADS * DN_DIM
        src = lax.broadcasted_iota(jnp.int32, (LANES, wide), 0)
        head = lax.broadcasted_iota(jnp.int32, (LANES, wide), 1) // DN_DIM
        beta_ref[...] = _dot_exact01(beta, jnp.where(src == head, 1.0, 0.0).astype(BF16), left=False)
        g_ref[...] = _dot_exact01(g, jnp.where(src == head + DN_HEADS, 1.0, 0.0).astype(BF16), left=False)


def _dn_prep(proj, conv_w, alog_row, dtb_row, seq, t=512):
    n = proj.shape[0]
    width = DN_HEADS * DN_DIM
    hb = t // SUBLANES
    return pl.pallas_call(
        functools.partial(_dn_prep_kernel, blocks_per_seq=seq // t),
        grid=(n // t, 3),
        in_specs=[
            pl.BlockSpec((t, width), lambda i, j: (i, OFF_DQ // width + j)),
            pl.BlockSpec((SUBLANES, width), lambda i, j: (jnp.maximum(i * hb - 1, 0), OFF_DQ // width + j)),
            pl.BlockSpec((DN_CONV, width), lambda i, j: (0, j)),
            pl.BlockSpec((t, LANES), lambda i, j: (i, OFF_DBA // LANES)),
            pl.BlockSpec((1, LANES), lambda i, j: (0, 0)),
            pl.BlockSpec((1, LANES), lambda i, j: (0, 0)),
        ],
        out_specs=[
            pl.BlockSpec((t, width), lambda i, j: (i, j)),
            pl.BlockSpec((t, width), lambda i, j: (i, 0)),
            pl.BlockSpec((t, width), lambda i, j: (i, 0)),
        ],
        out_shape=[
            jax.ShapeDtypeStruct((n, 3 * width), F32),
            jax.ShapeDtypeStruct((n, width), F32),
            jax.ShapeDtypeStruct((n, width), F32),
        ],
        scratch_shapes=[pltpu.VMEM((t + SUBLANES, width), F32)],
        compiler_params=_cparams(("arbitrary", "arbitrary")),
        name="dn_prep",
    )(proj, proj, conv_w, proj, alog_row, dtb_row)


def _dot_nt(a, b, precision=None):
    return lax.dot_general(a, b, (((1,), (1,)), ((), ())), preferred_element_type=F32, precision=precision)


def _dot_tn(a, b, precision=None):
    return lax.dot_general(a, b, (((0,), (0,)), ((), ())), preferred_element_type=F32, precision=precision)


DN_MM_MODE = "bf16"


def _mm_dn(a, b):
    if DN_MM_MODE == "hi":
        return jnp.dot(a, b, preferred_element_type=F32, precision=HI)
    a_hi = a.astype(BF16)
    b_hi = b.astype(BF16)
    out = jnp.dot(a_hi, b_hi, preferred_element_type=F32)
    if DN_MM_MODE == "x3":
        a_lo = (a - a_hi.astype(F32)).astype(BF16)
        b_lo = (b - b_hi.astype(F32)).astype(BF16)
        out = out + jnp.dot(a_lo, b_hi, preferred_element_type=F32) + jnp.dot(a_hi, b_lo, preferred_element_type=F32)
    return out


def _deltanet_kernel(q_ref, k_ref, v_ref, beta_ref, g_ref, z_ref, nw_ref, o_ref, state_ref):
    c = DN_CHUNK
    t = q_ref.shape[0]
    n_chunks = t // c

    @pl.when(pl.program_id(2) == 0)
    def _():
        state_ref[...] = jnp.zeros_like(state_ref)

    gc_all = g_ref[...]
    beta_all = beta_ref[...]

    gsz = DN_GROUP
    cpg = gsz // c
    ri = lax.broadcasted_iota(jnp.int32, (gsz, gsz), 0)
    ci = lax.broadcasted_iota(jnp.int32, (gsz, gsz), 1)
    same = (ri // c) == (ci // c)
    causal = (ri >= ci) & same
    strict = (ri > ci) & same
    eye = jnp.where(ri == ci, 1.0, 0.0)
    chunk_of_col = lax.broadcasted_iota(jnp.int32, (DN_DIM, gsz), 1) // c

    heads = q_ref.shape[1] // DN_DIM
    groups = range(heads * (t // gsz))
    rows = [slice((p % (t // gsz)) * gsz, (p % (t // gsz) + 1) * gsz) for p in groups]
    cols = [slice((p // (t // gsz)) * DN_DIM, (p // (t // gsz) + 1) * DN_DIM) for p in groups]
    q = [q_ref[rows[p], cols[p]] for p in groups]
    k = [k_ref[rows[p], cols[p]] for p in groups]
    gc = [gc_all[rows[p], cols[p]] for p in groups]
    eg = [jnp.exp(x) for x in gc]
    beta = [beta_all[rows[p], cols[p]] for p in groups]
    kb = [k[gi] * beta[gi] for gi in groups]
    rhs = [jnp.concatenate([kb[gi] * eg[gi], v_ref[rows[gi], cols[gi]] * beta[gi]], axis=1) for gi in groups]
    decay = []
    for gi in groups:
        gc2 = jnp.concatenate([gc[gi]] * (gsz // DN_DIM), axis=1)
        gcr = jnp.sum(gc2 * eye, axis=0, keepdims=True)
        decay.append(jnp.exp(jnp.where(causal, gc2 - gcr, NEG_INF)))
    a2 = [_dot_nt(jnp.concatenate([kb[gi], q[gi]], axis=0).astype(BF16), k[gi].astype(BF16)) for gi in groups]
    low = [jnp.where(strict, a2[gi][:gsz] * decay[gi], 0.0) for gi in groups]
    qk = [a2[gi][gsz:] * decay[gi] for gi in groups]
    size = 2
    pair = ((ri // size) == (ci // size)) & (ri > ci)
    inv = [eye - jnp.where(pair, x, 0.0) for x in low]
    while size < c:
        size *= 2
        pair = ((ri // size) == (ci // size)) & ((ri // (size // 2)) != (ci // (size // 2))) & (ri > ci)
        sub = [jnp.where(pair, x, 0.0) for x in low]
        y = [_mm_dn(sub[gi], inv[gi]) for gi in groups]
        inv = [inv[gi] - _mm_dn(inv[gi], y[gi]) for gi in groups]
    wu = [_mm_dn(inv[gi], rhs[gi]) for gi in groups]
    lasts, res = [], []
    for gi in groups:
        last = [gc[gi][(n + 1) * c - 1:(n + 1) * c, :] for n in range(cpg)]
        lasts.append(last)
        g_last_rows = jnp.concatenate([jnp.broadcast_to(x, (c, DN_DIM)) for x in last], axis=0)
        kd_t = jnp.transpose(k[gi] * jnp.exp(g_last_rows - gc[gi]))
        xs = [jnp.where(chunk_of_col == n, kd_t, 0.0) for n in range(cpg)]
        xs += [qk[gi][n * c:(n + 1) * c, :] for n in range(cpg)]
        res.append(jnp.dot(jnp.concatenate(xs, axis=0).astype(BF16), wu[gi].astype(BF16),
                           preferred_element_type=F32))
    pre = []
    for gi in groups:
        qd = q[gi] * eg[gi]
        for n in range(cpg):
            kw = res[gi][n * DN_DIM:(n + 1) * DN_DIM, :DN_DIM]
            ku = res[gi][n * DN_DIM:(n + 1) * DN_DIM, DN_DIM:]
            base = cpg * DN_DIM + n * c
            coef = qd[n * c:(n + 1) * c, :] - res[gi][base:base + c, :DN_DIM]
            qku = res[gi][base:base + c, DN_DIM:]
            lhs = jnp.concatenate([-kw, coef], axis=0).astype(BF16)
            pre.append((lhs, ku, qku, jnp.exp(lasts[gi][n])))

    states = [state_ref[hd] for hd in range(heads)]
    nw = nw_ref[...]
    for n in range(n_chunks):
        crow = slice(n * c, (n + 1) * c)
        for hd in range(heads):
            lhs, ku, qku, gl = pre[hd * n_chunks + n]
            r = jnp.dot(lhs, states[hd].astype(BF16), preferred_element_type=F32)
            o = r[DN_DIM:] + qku
            states[hd] = states[hd] * gl + r[:DN_DIM] + ku
            ccol = slice(hd * DN_DIM, (hd + 1) * DN_DIM)
            z = z_ref[crow, ccol]
            o_ref[crow, ccol] = (_rms_rows(o, nw) * (z * jax.nn.sigmoid(z))).astype(o_ref.dtype)
    for hd in range(heads):
        state_ref[hd] = states[hd]


def _deltanet(qkv, beta, g, proj, norm_w, bsz, seq, t=1024, heads=2):
    n = qkv.shape[0]
    spb = seq // t
    hw = heads * DN_DIM
    hsteps = DN_HEADS // heads
    zoff = OFF_DZ // hw
    col = lambda off: (lambda b, h, i: (b * spb + i, off + h))
    return pl.pallas_call(
        _deltanet_kernel,
        grid=(bsz, hsteps, spb),
        in_specs=[
            pl.BlockSpec((t, hw), col(0)),
            pl.BlockSpec((t, hw), col(hsteps)),
            pl.BlockSpec((t, hw), col(2 * hsteps)),
            pl.BlockSpec((t, hw), col(0)),
            pl.BlockSpec((t, hw), col(0)),
            pl.BlockSpec((t, hw), col(zoff)),
            pl.BlockSpec((1, DN_DIM), lambda b, h, i: (0, 0)),
        ],
        out_specs=pl.BlockSpec((t, hw), col(0)),
        out_shape=jax.ShapeDtypeStruct((n, DN_HEADS * DN_DIM), BF16),
        scratch_shapes=[pltpu.VMEM((heads, DN_DIM, DN_DIM), F32)],
        compiler_params=_cparams(("arbitrary", "arbitrary", "arbitrary")),
        name="deltanet",
    )(qkv, qkv, qkv, beta, g, proj, norm_w)


def _regroup_w_in(w_in):
    d = w_in.shape[0]
    sizes = (1024, 256, 256, 256, 256, 256, 256, 24, 1024, 1024, 1024, 1024, 8, 8, 2048, 2048)
    offs = np.concatenate([[0], np.cumsum(sizes)])
    (nq, kc, vc, ksel, vsel, kwin, vwin, ngate, dq, dk, dv, dz, db, da, mga, mgb) = (
        w_in[:, offs[i]:offs[i + 1]] for i in range(len(sizes)))
    zeros = lambda w: jnp.zeros((d, w), w_in.dtype)
    ngate = jnp.transpose(ngate.reshape(d, N_GROUPS, HPG, 3), (0, 1, 3, 2)).reshape(d, N_GROUPS, 3 * HPG)
    ngate = jnp.pad(ngate, ((0, 0), (0, 0), (0, GATE_ROWS - 3 * HPG))).reshape(d, N_GROUPS * GATE_ROWS)
    cols = [mga, mgb, dq, dk, dv, nq, dz, kc, vc, ksel, vsel, kwin, vwin,
            ngate, zeros(LANES - N_GROUPS * GATE_ROWS), db, da, zeros(LANES - 16), zeros(PROJ_DIM - PROJ_USED)]
    return jnp.concatenate(cols, axis=1).astype(BF16)


def _overlap_t(n_sb, n_pieces):
    n = np.arange(N_CMP_PAD) - CMP_PAD
    j = np.arange(n_sb)
    valid = (n >= 0) & (n < n_pieces - 1)
    c0 = n * CMP_STRIDE
    s0 = j * SEL_BLOCK
    ov = (c0[None, :] < s0[:, None] + SEL_BLOCK) & (c0[None, :] + CMP_LEN > s0[:, None]) & valid[None, :]
    return jnp.asarray(ov.astype(np.float32), dtype=BF16)


def _token_mixer(h, bsz, seq, norm_mix, w_in, q_norm, k_norm, cmp_pos, cmp_w1, cmp_b1, cmp_w2, dn_conv,
                 dn_a_log, dn_dt_bias, dn_norm, w_a, w_b, w_out, bias_tiles):
    n = bsz * seq
    nqb = seq // Q_BLOCK
    npc = seq // CMP_STRIDE
    n_sb = seq // SEL_BLOCK
    assert npc + CMP_PAD <= N_CMP_PAD and nqb * SUBLANES + NEAR_ROWS <= N_CMP_PAD
    assert nqb % (2 * FAR_SUB) == 0
    assert nqb % QB_PER_STEP == 0
    proj = _norm_matmul(h, norm_mix.reshape(1, -1), _regroup_w_in(w_in))

    k_norm8 = jnp.concatenate([k_norm, jnp.zeros((SUBLANES - 3, HEAD_DIM), F32)], axis=0)
    qT, kc_raw, vc_raw, ks, vsT, kw, vwT, gT = _nsa_prep(proj, q_norm.reshape(1, -1), k_norm8, bsz, seq)
    pieces = jnp.stack([kc_raw, vc_raw], axis=0).reshape(2, bsz, N_GROUPS, npc, CMP_STRIDE * HEAD_DIM)
    cmp = _compress(pieces, cmp_w1, cmp_pos.reshape(2, 1, CMP_LEN * HEAD_DIM), cmp_b1.reshape(2, 1, -1),
                    cmp_w2, k_norm8)
    cmp = jnp.pad(cmp, ((0, 0), (0, 0), (0, 0), (CMP_PAD, N_CMP_PAD - CMP_PAD - npc), (0, 0))).astype(BF16)
    kc = cmp[0]
    vcT = jnp.swapaxes(cmp[1], -1, -2)
    y_nsa = _nsa_attn(qT, gT, ks, vsT, kw, vwT, kc, vcT, _overlap_t(n_sb, npc), *bias_tiles)

    lane_pad = lambda v, off: jnp.zeros((1, LANES), F32).at[0, off:off + DN_HEADS].set(v)
    qkv, beta, g = _dn_prep(proj, dn_conv, lane_pad(dn_a_log, DN_HEADS), lane_pad(dn_dt_bias, DN_HEADS), seq)
    y_dn = _deltanet(qkv, beta, g, proj, dn_norm.reshape(1, -1), bsz, seq)

    return _merge_out(y_nsa, y_dn, w_a.astype(BF16), w_b.astype(BF16), proj, w_out.astype(BF16), h)


def kernel(x, rel_table, norm_ffn1, ffn1_gate, ffn1_up, ffn1_down, norm_mix, w_in, q_norm, k_norm, cmp_pos, cmp_w1, cmp_b1, cmp_w2, dn_conv, dn_a_log, dn_dt_bias, dn_norm, w_branch_nsa, w_branch_dn, w_out, norm_ffn2, ffn2_gate, ffn2_up, ffn2_down):
    bsz, seq, d = x.shape
    depth = w_in.shape[0]
    h = x.reshape(bsz * seq, d)
    bias_tiles = _bias_tiles(rel_table)
    for l in range(depth):
        h = _ffn(h, norm_ffn1[l].reshape(1, -1), ffn1_gate[l].astype(BF16), ffn1_up[l].astype(BF16),
                 ffn1_down[l].astype(BF16))
        h = _token_mixer(h, bsz, seq, norm_mix[l], w_in[l], q_norm[l], k_norm[l], cmp_pos[l], cmp_w1[l],
                         cmp_b1[l], cmp_w2[l], dn_conv[l], dn_a_log[l], dn_dt_bias[l], dn_norm[l],
                         w_branch_nsa[l], w_branch_dn[l], w_out[l], bias_tiles)
        h = _ffn(h, norm_ffn2[l].reshape(1, -1), ffn2_gate[l].astype(BF16), ffn2_up[l].astype(BF16),
                 ffn2_down[l].astype(BF16))
    return h.reshape(bsz, seq, d)
```

```python
import functools
import math

import numpy as np
import jax
import jax.numpy as jnp
from jax import lax
from jax.experimental import pallas as pl
from jax.experimental.pallas import tpu as pltpu

F32 = jnp.float32
BF16 = jnp.bfloat16
HI = lax.Precision.HIGHEST

N_HEADS = 8
N_GROUPS = 2
HPG = N_HEADS // N_GROUPS
HEAD_DIM = 128
CMP_LEN = 32
CMP_STRIDE = 16
CMP_HIDDEN = 128
SEL_BLOCK = 64
SEL_TOPK = 16
WINDOW = 512
Q_BLOCK = 128
DN_HEADS = 8
DN_DIM = 128
DN_CONV = 4
DN_CHUNK = 64
DN_GROUP = 256
REL_BUCKETS = 32
REL_MAX_DIST = 128
EPS = 1e-6
LOG2E = math.log2(math.e)
NEG_INF = -1e30

LANES = 128
SUBLANES = 8
VMEM_LIMIT = 56 * 1024 * 1024

OFF_MGA = 0
OFF_MGB = 2048
OFF_DQ, OFF_DK, OFF_DV = 4096, 5120, 6144
OFF_NQ = 7168
OFF_DZ = 8192
OFF_KV6 = 9216
OFF_NGATE = 10752
OFF_DBA = 10880
PROJ_USED = 11008
PROJ_DIM = 11264

CMP_PAD = 16
N_CMP_PAD = 1152
NEAR_ROWS = 24
NEAR_KEYS = 2 * Q_BLOCK
WIN_KEYS = WINDOW + Q_BLOCK
GATE_ROWS = 16
V_ROWS = HEAD_DIM + 16
FAR_SUB = 4
FAR_PAIRS = 4
FAR_TILES = 2 * FAR_PAIRS * FAR_SUB
QB_PER_STEP = 2


def _rel_bucket_thresholds():
    n = np.arange(0, 4 * REL_MAX_DIST, dtype=np.int64)
    max_exact = REL_BUCKETS // 2
    nf = np.maximum(n, max_exact).astype(np.float32)
    large = max_exact + (np.log(nf / np.float32(max_exact)) / np.float32(math.log(REL_MAX_DIST / max_exact))
                         * np.float32(REL_BUCKETS - max_exact)).astype(np.int32)
    bucket = np.where(n < max_exact, n, np.minimum(large, REL_BUCKETS - 1))
    assert np.all(np.diff(bucket) >= 0)
    return [int(np.argmax(bucket >= b)) for b in range(REL_BUCKETS)]


REL_THRESH = _rel_bucket_thresholds()
FAR_DIST = REL_THRESH[REL_BUCKETS - 1]
assert FAR_DIST <= Q_BLOCK


def _cparams(sem, vmem_limit=VMEM_LIMIT):
    return pltpu.CompilerParams(dimension_semantics=sem, vmem_limit_bytes=vmem_limit)


def _rms_rows(x, w_row):
    ms = jnp.mean(x * x, axis=-1, keepdims=True)
    return x * lax.rsqrt(ms + EPS) * w_row


def _ffn_kernel(x_ref, nw_ref, wg_ref, wu_ref, wd_ref, o_ref, xn_ref, acc_ref):
    j = pl.program_id(1)

    @pl.when(j == 0)
    def _():
        xn_ref[...] = _rms_rows(x_ref[...], nw_ref[...]).astype(BF16)
        acc_ref[...] = jnp.zeros_like(acc_ref)

    xn = xn_ref[...]
    g = jnp.dot(xn, wg_ref[...], preferred_element_type=F32)
    u = jnp.dot(xn, wu_ref[...], preferred_element_type=F32)
    a = (g * jax.nn.sigmoid(g) * u).astype(BF16)
    acc_ref[...] += jnp.dot(a, wd_ref[...], preferred_element_type=F32)

    @pl.when(j == pl.num_programs(1) - 1)
    def _():
        o_ref[...] = x_ref[...] + 0.5 * acc_ref[...]


def _ffn(h, nw, wg, wu, wd, tm=512, tf=512):
    n, d = h.shape
    f = wg.shape[1]
    return pl.pallas_call(
        _ffn_kernel,
        grid=(n // tm, f // tf),
        in_specs=[
            pl.BlockSpec((tm, d), lambda i, j: (i, 0)),
            pl.BlockSpec((1, d), lambda i, j: (0, 0)),
            pl.BlockSpec((d, tf), lambda i, j: (0, j)),
            pl.BlockSpec((d, tf), lambda i, j: (0, j)),
            pl.BlockSpec((tf, d), lambda i, j: (j, 0)),
        ],
        out_specs=pl.BlockSpec((tm, d), lambda i, j: (i, 0)),
        out_shape=jax.ShapeDtypeStruct((n, d), F32),
        scratch_shapes=[pltpu.VMEM((tm, d), BF16), pltpu.VMEM((tm, d), F32)],
        compiler_params=_cparams(("parallel", "arbitrary")),
        name="ffn",
    )(h, nw, wg, wu, wd)


def _norm_matmul_kernel(x_ref, nw_ref, w_ref, o_ref, xn_ref):
    @pl.when(pl.program_id(1) == 0)
    def _():
        xn_ref[...] = _rms_rows(x_ref[...], nw_ref[...]).astype(BF16)

    o_ref[...] = jnp.dot(xn_ref[...], w_ref[...], preferred_element_type=F32)


def _norm_matmul(h, nw, w, tm=1024, tn=1024):
    n, d = h.shape
    nout = w.shape[1]
    return pl.pallas_call(
        _norm_matmul_kernel,
        grid=(n // tm, nout // tn),
        in_specs=[
            pl.BlockSpec((tm, d), lambda i, j: (i, 0)),
            pl.BlockSpec((1, d), lambda i, j: (0, 0)),
            pl.BlockSpec((d, tn), lambda i, j: (0, j)),
        ],
        out_specs=pl.BlockSpec((tm, tn), lambda i, j: (i, j)),
        out_shape=jax.ShapeDtypeStruct((n, nout), F32),
        scratch_shapes=[pltpu.VMEM((tm, d), BF16)],
        compiler_params=_cparams(("parallel", "arbitrary")),
        name="in_proj",
    )(h, nw, w)


def _merge_out_kernel(ya_ref, yb_ref, wa_ref, wb_ref, ga_ref, gb_ref, wo_ref, h_ref, o_ref):
    a = jnp.dot(ya_ref[...], wa_ref[...], preferred_element_type=F32)
    b = jnp.dot(yb_ref[...], wb_ref[...], preferred_element_type=F32)
    merged = (jax.nn.sigmoid(ga_ref[...]) * a + jax.nn.sigmoid(gb_ref[...]) * b).astype(BF16)
    o_ref[...] = h_ref[...] + jnp.dot(merged, wo_ref[...], preferred_element_type=F32)


def _merge_out(ya, yb, wa, wb, proj, wo, h, tm=512):
    n, k = ya.shape
    d = wa.shape[1]
    resident = dict(pipeline_mode=pl.Buffered(1))
    return pl.pallas_call(
        _merge_out_kernel,
        grid=(n // tm,),
        in_specs=[
            pl.BlockSpec((tm, k), lambda i: (i, 0)),
            pl.BlockSpec((tm, k), lambda i: (i, 0)),
            pl.BlockSpec((k, d), lambda i: (0, 0), **resident),
            pl.BlockSpec((k, d), lambda i: (0, 0), **resident),
            pl.BlockSpec((tm, d), lambda i: (i, OFF_MGA // d)),
            pl.BlockSpec((tm, d), lambda i: (i, OFF_MGB // d)),
            pl.BlockSpec((d, d), lambda i: (0, 0), **resident),
            pl.BlockSpec((tm, d), lambda i: (i, 0)),
        ],
        out_specs=pl.BlockSpec((tm, d), lambda i: (i, 0)),
        out_shape=jax.ShapeDtypeStruct((n, d), F32),
        compiler_params=_cparams(("parallel",), vmem_limit=60000 * 1024),
        name="merge_out",
    )(ya, yb, wa, wb, proj, proj, wo, h)


def _nsa_prep_kernel(nq_ref, kv_ref, ng_ref, qn_ref, kn_ref,
                     qT_out, kc_out, vc_out, ks_out, vsT_out, kw_out, vwT_out, gT_out):
    tm = nq_ref.shape[0]
    scale = HEAD_DIM ** -0.5 * LOG2E
    for hh in range(N_HEADS):
        g, h = divmod(hh, HPG)
        qn = _rms_rows(nq_ref[:, hh * HEAD_DIM:(hh + 1) * HEAD_DIM], qn_ref[...]) * scale
        for qi in range(tm // Q_BLOCK):
            blk = qn[qi * Q_BLOCK:(qi + 1) * Q_BLOCK, :]
            qT_out[g, qi, :, h * Q_BLOCK:(h + 1) * Q_BLOCK] = jnp.transpose(blk).astype(BF16)
    ones_rows = jnp.where(lax.broadcasted_iota(jnp.int32, (V_ROWS - HEAD_DIM, tm), 0) == 0, 1.0, 0.0)
    for g in range(N_GROUPS):
        col = lambda kind: kv_ref[:, (kind * N_GROUPS + g) * HEAD_DIM:(kind * N_GROUPS + g + 1) * HEAD_DIM]
        kc_out[g] = col(0).astype(BF16)
        vc_out[g] = col(1).astype(BF16)
        ks_out[g, :, 0:HEAD_DIM] = _rms_rows(col(2), kn_ref[1:2, :]).astype(BF16)
        tok = pl.program_id(0) * tm + lax.broadcasted_iota(jnp.int32, (tm, HEAD_DIM), 0)
        blk_of_row = (tok % (FAR_SUB * Q_BLOCK)) // SEL_BLOCK
        ks_out[g, :, HEAD_DIM:] = jnp.where(
            lax.broadcasted_iota(jnp.int32, (tm, HEAD_DIM), 1) == blk_of_row, 1.0, 0.0).astype(BF16)
        kw_out[g] = _rms_rows(col(4), kn_ref[2:3, :]).astype(BF16)
        for v_out, kind in ((vsT_out, 3), (vwT_out, 5)):
            v_out[g, 0:HEAD_DIM, :] = jnp.transpose(col(kind)).astype(BF16)
            v_out[g, HEAD_DIM:, :] = ones_rows.astype(BF16)
    sig = jax.nn.sigmoid(ng_ref[...])
    for qi in range(tm // Q_BLOCK):
        t = jnp.transpose(sig[qi * Q_BLOCK:(qi + 1) * Q_BLOCK, :])
        for g in range(N_GROUPS):
            gT_out[g, qi] = t[g * GATE_ROWS:(g + 1) * GATE_ROWS, :]


def _nsa_prep(proj, q_norm, k_norm, bsz, seq, tm=512):
    kvw = 6 * N_GROUPS * HEAD_DIM
    spb = seq // tm
    qpb = tm // Q_BLOCK
    nqb = seq // Q_BLOCK
    wide = HPG * Q_BLOCK
    keys = pl.BlockSpec((None, N_GROUPS, tm, HEAD_DIM), lambda i: (i // spb, 0, i % spb, 0))
    keys_aug = pl.BlockSpec((None, N_GROUPS, tm, 2 * HEAD_DIM), lambda i: (i // spb, 0, i % spb, 0))
    vals = pl.BlockSpec((None, N_GROUPS, V_ROWS, tm), lambda i: (i // spb, 0, 0, i % spb))
    keys_shape = jax.ShapeDtypeStruct((bsz, N_GROUPS, seq, HEAD_DIM), BF16)
    vals_shape = jax.ShapeDtypeStruct((bsz, N_GROUPS, V_ROWS, seq), BF16)
    return pl.pallas_call(
        _nsa_prep_kernel,
        grid=(bsz * spb,),
        in_specs=[
            pl.BlockSpec((tm, N_HEADS * HEAD_DIM), lambda i: (i, OFF_NQ // (N_HEADS * HEAD_DIM))),
            pl.BlockSpec((tm, kvw), lambda i: (i, OFF_KV6 // kvw)),
            pl.BlockSpec((tm, LANES), lambda i: (i, OFF_NGATE // LANES)),
            pl.BlockSpec((1, HEAD_DIM), lambda i: (0, 0)),
            pl.BlockSpec((SUBLANES, HEAD_DIM), lambda i: (0, 0)),
        ],
        out_specs=[
            pl.BlockSpec((None, N_GROUPS, qpb, HEAD_DIM, wide), lambda i: (i // spb, 0, i % spb, 0, 0)),
            keys, keys, keys_aug, vals, keys, vals,
            pl.BlockSpec((None, N_GROUPS, qpb, GATE_ROWS, Q_BLOCK), lambda i: (i // spb, 0, i % spb, 0, 0)),
        ],
        out_shape=[
            jax.ShapeDtypeStruct((bsz, N_GROUPS, nqb, HEAD_DIM, wide), BF16),
            keys_shape, keys_shape, jax.ShapeDtypeStruct((bsz, N_GROUPS, seq, 2 * HEAD_DIM), BF16), vals_shape,
            keys_shape, vals_shape,
            jax.ShapeDtypeStruct((bsz, N_GROUPS, nqb, GATE_ROWS, Q_BLOCK), F32),
        ],
        compiler_params=_cparams(("parallel",)),
        name="nsa_prep",
    )(proj, proj, proj, q_norm, k_norm)


def _compress_kernel(p_ref, w1_ref, pos_ref, b1_ref, w2_ref, kn_ref, o_ref):
    half = CMP_STRIDE * HEAD_DIM
    p = p_ref[...]
    w1 = w1_ref[...]
    a = jnp.dot(p, w1[:half].astype(BF16), preferred_element_type=F32)
    b = jnp.dot(p, w1[half:].astype(BF16), preferred_element_type=F32)
    pos = jnp.broadcast_to(pos_ref[...], (SUBLANES, 2 * half))
    c0 = jnp.dot(pos, w1, preferred_element_type=F32, precision=HI)[0:1, :] + b1_ref[...]
    n = p.shape[0]
    hid = a + pltpu.roll(b, n - 1, 0) + c0
    hid = hid * jax.nn.sigmoid(hid)
    out = jnp.dot(hid.astype(BF16), w2_ref[...].astype(BF16), preferred_element_type=F32)
    normed = _rms_rows(out, kn_ref[0:1, :])
    out = jnp.where(pl.program_id(0) == 0, normed, out)
    row = lax.broadcasted_iota(jnp.int32, out.shape, 0)
    o_ref[...] = jnp.where(row < n - 1, out, 0.0)


def _compress(pieces, w1, pos, b1, w2, k_norm):
    _, bsz, ng, npc, width = pieces.shape
    return pl.pallas_call(
        _compress_kernel,
        grid=(2, bsz, ng),
        in_specs=[
            pl.BlockSpec((None, None, None, npc, width), lambda c, b, g: (c, b, g, 0, 0)),
            pl.BlockSpec((None, 2 * width, CMP_HIDDEN), lambda c, b, g: (c, 0, 0)),
            pl.BlockSpec((None, 1, 2 * width), lambda c, b, g: (c, 0, 0)),
            pl.BlockSpec((None, 1, CMP_HIDDEN), lambda c, b, g: (c, 0, 0)),
            pl.BlockSpec((None, CMP_HIDDEN, HEAD_DIM), lambda c, b, g: (c, 0, 0)),
            pl.BlockSpec((SUBLANES, HEAD_DIM), lambda c, b, g: (0, 0)),
        ],
        out_specs=pl.BlockSpec((None, None, None, npc, HEAD_DIM), lambda c, b, g: (c, b, g, 0, 0)),
        out_shape=jax.ShapeDtypeStruct((2, bsz, ng, npc, HEAD_DIM), F32),
        compiler_params=_cparams(("arbitrary", "arbitrary", "arbitrary")),
        name="compress",
    )(pieces, w1, pos, b1, w2, k_norm)


def _bias_kernel(tab_ref, bn_ref, bw_ref, bc_ref):
    g = pl.program_id(0)
    n_wt = WINDOW // Q_BLOCK

    def lookup(dist, head):
        v = jnp.full(dist.shape, tab_ref[0, head], F32)
        for b in range(1, REL_BUCKETS):
            v = jnp.where(dist >= REL_THRESH[b], tab_ref[b, head], v)
        return (v - tab_ref[REL_BUCKETS - 1, head]) * LOG2E

    ki = lax.broadcasted_iota(jnp.int32, (Q_BLOCK, Q_BLOCK), 0)
    qj = lax.broadcasted_iota(jnp.int32, (Q_BLOCK, Q_BLOCK), 1)
    r = lax.broadcasted_iota(jnp.int32, (NEAR_ROWS, Q_BLOCK), 0)
    qc = lax.broadcasted_iota(jnp.int32, (NEAR_ROWS, Q_BLOCK), 1)
    dist_c = qc - CMP_STRIDE * (r - CMP_PAD) - (CMP_LEN - 1)
    for h in range(HPG):
        head = g * HPG + h
        sl = slice(h * Q_BLOCK, (h + 1) * Q_BLOCK)
        diag = jnp.where(ki <= qj, lookup(qj - ki, head), NEG_INF)
        prev = lookup(qj - ki + Q_BLOCK, head)
        bn_ref[0:Q_BLOCK, sl] = prev
        bn_ref[Q_BLOCK:, sl] = diag
        bw_ref[0:Q_BLOCK, sl] = jnp.where(ki > qj, 0.0, NEG_INF)
        bw_ref[Q_BLOCK:(n_wt - 1) * Q_BLOCK, sl] = jnp.zeros(((n_wt - 2) * Q_BLOCK, Q_BLOCK), F32)
        bw_ref[(n_wt - 1) * Q_BLOCK:n_wt * Q_BLOCK, sl] = prev
        bw_ref[n_wt * Q_BLOCK:, sl] = diag
        bc_ref[:, sl] = jnp.where(dist_c >= 0, lookup(dist_c, head), NEG_INF)


def _bias_tiles(rel_table):
    wide = HPG * Q_BLOCK
    return pl.pallas_call(
        _bias_kernel,
        grid=(N_GROUPS,),
        in_specs=[pl.BlockSpec(memory_space=pltpu.SMEM)],
        out_specs=[
            pl.BlockSpec((None, NEAR_KEYS, wide), lambda g: (g, 0, 0)),
            pl.BlockSpec((None, WIN_KEYS, wide), lambda g: (g, 0, 0)),
            pl.BlockSpec((None, NEAR_ROWS, wide), lambda g: (g, 0, 0)),
        ],
        out_shape=[
            jax.ShapeDtypeStruct((N_GROUPS, NEAR_KEYS, wide), F32),
            jax.ShapeDtypeStruct((N_GROUPS, WIN_KEYS, wide), F32),
            jax.ShapeDtypeStruct((N_GROUPS, NEAR_ROWS, wide), F32),
        ],
        compiler_params=_cparams(("arbitrary",)),
        name="rel_bias",
    )(rel_table)


def _tile4(x):
    return jnp.concatenate([x] * HPG, axis=1)


def _nsa_attn_kernel(qT_ref, gT_ref, ks_ref, vsT_ref, kw_ref, vwT_ref, kc_ref, vcT_ref, ovT_ref,
                     bn_ref, bw_ref, bc_ref, o_ref, sc_ref, sel_ref, far_ref, sa_ref, sb_ref,
                     ra_ref, rb_ref):
    step = pl.program_id(2)
    wide = HPG * Q_BLOCK
    half = SEL_BLOCK
    n_wt = WINDOW // Q_BLOCK
    sub_w = FAR_SUB * Q_BLOCK
    blocks_per_sub = sub_w // SEL_BLOCK
    bf16_rows = 2 * SUBLANES
    mask_rows = -(-blocks_per_sub // bf16_rows) * bf16_rows
    assert mask_rows <= HEAD_DIM
    last_chunk = ks_ref.shape[0] // sub_w - 1

    def absent(cond):
        return jnp.where(cond, 0.0, NEG_INF)

    def key_rows(k_ref, kt, width=Q_BLOCK):
        return k_ref[pl.ds(pl.multiple_of(kt * width, width), width), :]

    def value_cols(vT_ref, kt, width=Q_BLOCK):
        return vT_ref[:, pl.ds(pl.multiple_of(kt * width, width), width)]

    def first(s, vT):
        m = jnp.max(s, axis=0, keepdims=True)
        p = jnp.exp2(s - m)
        return m, jnp.dot(vT, p.astype(BF16), preferred_element_type=F32)

    def update(state, s, vT_ref, kt, width=Q_BLOCK):
        m, acc = state
        m_new = jnp.maximum(m, jnp.max(s, axis=0, keepdims=True))
        alpha = jnp.exp2(m - m_new)
        p = jnp.exp2(s - m_new)
        acc = alpha * acc + jnp.dot(value_cols(vT_ref, kt, width), p.astype(BF16), preferred_element_type=F32)
        return m_new, acc

    def normalized(state):
        acc = state[1]
        return acc[:HEAD_DIM] * (1.0 / acc[HEAD_DIM:HEAD_DIM + 1])

    def straight(u):
        qb = step * QB_PER_STEP + u
        sc, sel, far = sc_ref.at[u], sel_ref.at[u], far_ref.at[u]
        qT = qT_ref[u]
        qpos = qb * Q_BLOCK + lax.broadcasted_iota(jnp.int32, (1, Q_BLOCK), 1)

        sc[...] = jnp.dot(kc_ref[...], qT, preferred_element_type=F32)
        sc[0:CMP_PAD, :] = jnp.full((CMP_PAD, wide), NEG_INF, F32)
        r0 = pl.multiple_of(qb * SUBLANES, SUBLANES)
        sc[pl.ds(r0, NEAR_ROWS), :] += bc_ref[...]
        rown = lax.broadcasted_iota(jnp.int32, (N_CMP_PAD, wide), 0)
        qpos4 = qb * Q_BLOCK + (lax.broadcasted_iota(jnp.int32, (1, wide), 1) & (Q_BLOCK - 1))
        s = jnp.where(rown < qb * SUBLANES + NEAR_ROWS, sc[...], NEG_INF)
        m = jnp.max(s, axis=0, keepdims=True)
        p = jnp.exp2(s - m)
        l = jnp.sum(p, axis=0, keepdims=True)
        inv = jnp.where(qpos4 >= CMP_LEN - 1, 1.0 / l, 0.0)
        pn = p * inv
        o_c = jnp.dot(vcT_ref[...], pn.astype(BF16), preferred_element_type=F32)
        psum = pn[:, 0:Q_BLOCK]
        for h in range(1, HPG):
            psum = psum + pn[:, h * Q_BLOCK:(h + 1) * Q_BLOCK]
        p_hi = psum.astype(BF16)
        p_lo = (psum - p_hi.astype(F32)).astype(BF16)
        ovT = ovT_ref[...]
        imp = (jnp.dot(ovT, p_hi, preferred_element_type=F32)
               + jnp.dot(ovT, p_lo, preferred_element_type=F32))

        n_sb = imp.shape[0]
        jblk = lax.broadcasted_iota(jnp.int32, (n_sb, Q_BLOCK), 0)
        cur = qpos // SEL_BLOCK
        eligible = jblk * SEL_BLOCK <= qpos
        forced = (jblk == 0) | (jblk == cur) | (jblk == cur - 1)
        picked = -3e38
        score = jnp.where(eligible, jnp.where(forced, picked, imp), NEG_INF)
        for _ in range(max(min(SEL_TOPK, n_sb) - 3, 0)):
            top = jnp.max(score, axis=0, keepdims=True)
            idx = jnp.min(jnp.where(score == top, jblk, n_sb), axis=0, keepdims=True)
            idx = jnp.where(top > -1.0, idx, -1)
            score = jnp.where(jblk == idx, picked, score)
        selmask = jnp.where(score == picked, 0.0, NEG_INF)
        sel[...] = selmask
        far[...] = jnp.where(jblk < 2 * (qb - 1), selmask, NEG_INF)

        def sel_mask_tile(kt):
            a = jnp.broadcast_to(sel[pl.ds(2 * kt, 1), :], (half, Q_BLOCK))
            b = jnp.broadcast_to(sel[pl.ds(2 * kt + 1, 1), :], (half, Q_BLOCK))
            return _tile4(jnp.concatenate([a, b], axis=0))

        tiles = [jnp.maximum(qb - back, 0) for back in range(n_wt, -1, -1)]
        kwin = jnp.concatenate([key_rows(kw_ref, kt) for kt in tiles], axis=0)
        sw = jnp.dot(kwin, qT, preferred_element_type=F32) + bw_ref[...]
        sw = jnp.concatenate(
            [sw[t * Q_BLOCK:(t + 1) * Q_BLOCK] + absent(qb >= n_wt - t) for t in range(n_wt)]
            + [sw[n_wt * Q_BLOCK:]], axis=0)
        o_w = normalized(first(sw, jnp.concatenate([value_cols(vwT_ref, kt) for kt in tiles], axis=1)))

        kp = jnp.maximum(qb - 1, 0)
        k2 = jnp.concatenate([key_rows(ks_ref, kp), key_rows(ks_ref, qb)], axis=0)[:, 0:HEAD_DIM]
        near_mask = jnp.concatenate([sel_mask_tile(kp) + absent(qb >= 1), sel_mask_tile(qb)], axis=0)
        s = jnp.dot(k2, qT, preferred_element_type=F32) + bn_ref[...] + near_mask
        st = first(s, jnp.concatenate([value_cols(vsT_ref, kp), value_cols(vsT_ref, qb)], axis=1))

        for r_ref in (ra_ref.at[u], rb_ref.at[u]):
            r_ref[0:HEAD_DIM, :] = qT
            r_ref[HEAD_DIM:, :] = jnp.zeros((HEAD_DIM, wide), BF16)
        return o_c, o_w, st

    def far_scores(u, cs, r_ref):
        rows = _tile4(far_ref[u, pl.ds(pl.multiple_of(blocks_per_sub * cs, blocks_per_sub), blocks_per_sub), :])
        if mask_rows > blocks_per_sub:
            rows = jnp.concatenate([rows, jnp.zeros((mask_rows - blocks_per_sub, wide), F32)], axis=0)
        r_ref[u, HEAD_DIM:HEAD_DIM + mask_rows, :] = rows.astype(BF16)
        k = ks_ref[pl.ds(pl.multiple_of(cs * sub_w, sub_w), sub_w), :]
        return jnp.dot(k, r_ref[u], preferred_element_type=F32)

    blocks = range(QB_PER_STEP)
    pre = [straight(u) for u in blocks]

    def pair_step(c0, sts):
        sts = list(sts)
        for u in blocks:
            sb_ref[u] = far_scores(u, c0 + 1, rb_ref)
        for u in blocks:
            sts[u] = update(sts[u], sa_ref[u], vsT_ref, c0, sub_w)
        for u in blocks:
            sa_ref[u] = far_scores(u, jnp.minimum(c0 + 2, last_chunk), ra_ref)
        for u in blocks:
            sts[u] = update(sts[u], sb_ref[u], vsT_ref, c0 + 1, sub_w)
        return tuple(sts)

    def far_body(i, sts):
        for pair in range(FAR_PAIRS):
            sts = pair_step(2 * (FAR_PAIRS * i + pair), sts)
        return sts

    far_tiles = jnp.maximum(step * QB_PER_STEP + QB_PER_STEP - 2, 0)
    n_full = far_tiles // FAR_TILES
    n_pairs = (far_tiles + 2 * FAR_SUB - 1) // (2 * FAR_SUB)
    for u in blocks:
        sa_ref[u] = far_scores(u, 0, ra_ref)
    sts = tuple(pre[u][2] for u in blocks)
    sts = lax.fori_loop(0, n_full, far_body, sts)
    sts = lax.fori_loop(FAR_PAIRS * n_full, n_pairs, lambda j, sts: pair_step(2 * j, sts), sts)

    for u in blocks:
        o_c, o_w, _ = pre[u]
        o_s = normalized(sts[u])
        for h in range(HPG):
            sl = slice(h * Q_BLOCK, (h + 1) * Q_BLOCK)
            gate = lambda c: gT_ref[u, c * HPG + h:c * HPG + h + 1, :]
            mix = gate(0) * o_c[:, sl] + gate(1) * o_s[:, sl] + gate(2) * o_w[:, sl]
            o_ref[u * Q_BLOCK:(u + 1) * Q_BLOCK, h * HEAD_DIM:(h + 1) * HEAD_DIM] = (
                jnp.transpose(mix).astype(o_ref.dtype))


def _nsa_attn(qT, gT, ks, vsT, kw, vwT, kc, vcT, ovT, bn, bw, bc):
    bsz, ng, nqb, _, wide = qT.shape
    seq = ks.shape[2]
    n_sb = ovT.shape[0]
    qs = QB_PER_STEP
    steps = nqb // qs
    big = lambda b, g, q: (b, g, 0, 0)
    blk = lambda b, g, q: (b, g, q, 0, 0)
    grp = lambda b, g, q: (g, 0, 0)
    single = dict(pipeline_mode=pl.Buffered(1))
    return pl.pallas_call(
        _nsa_attn_kernel,
        grid=(bsz, ng, steps),
        in_specs=[
            pl.BlockSpec((None, None, qs, HEAD_DIM, wide), blk),
            pl.BlockSpec((None, None, qs, GATE_ROWS, Q_BLOCK), blk),
            pl.BlockSpec((None, None, seq, 2 * HEAD_DIM), big, **single),
            pl.BlockSpec((None, None, V_ROWS, seq), big, **single),
            pl.BlockSpec((None, None, seq, HEAD_DIM), big, **single),
            pl.BlockSpec((None, None, V_ROWS, seq), big, **single),
            pl.BlockSpec((None, None, N_CMP_PAD, HEAD_DIM), big),
            pl.BlockSpec((None, None, HEAD_DIM, N_CMP_PAD), big),
            pl.BlockSpec((n_sb, N_CMP_PAD), lambda b, g, q: (0, 0)),
            pl.BlockSpec((None, NEAR_KEYS, wide), grp),
            pl.BlockSpec((None, WIN_KEYS, wide), grp),
            pl.BlockSpec((None, NEAR_ROWS, wide), grp),
        ],
        out_specs=pl.BlockSpec((qs * Q_BLOCK, HPG * HEAD_DIM), lambda b, g, q: (b * steps + q, g)),
        out_shape=jax.ShapeDtypeStruct((bsz * seq, ng * HPG * HEAD_DIM), BF16),
        scratch_shapes=[pltpu.VMEM((qs, N_CMP_PAD, wide), F32), pltpu.VMEM((qs, n_sb, Q_BLOCK), F32),
                        pltpu.VMEM((qs, n_sb, Q_BLOCK), F32),
                        pltpu.VMEM((qs, FAR_SUB * Q_BLOCK, wide), F32),
                        pltpu.VMEM((qs, FAR_SUB * Q_BLOCK, wide), F32),
                        pltpu.VMEM((qs, 2 * HEAD_DIM, wide), BF16), pltpu.VMEM((qs, 2 * HEAD_DIM, wide), BF16)],
        compiler_params=_cparams(("arbitrary", "arbitrary", "arbitrary")),
        name="nsa_attn",
    )(qT, gT, ks, vsT, kw, vwT, kc, vcT, ovT, bn, bw, bc)


def _dot_exact01(a, b, left):
    x = b if left else a
    x1 = x.astype(BF16)
    r1 = x - x1.astype(F32)
    x2 = r1.astype(BF16)
    x3 = (r1 - x2.astype(F32)).astype(BF16)
    mm = (lambda p: jnp.dot(a, p, preferred_element_type=F32)) if left else (
        lambda p: jnp.dot(p, b, preferred_element_type=F32))
    return mm(x1) + mm(x2) + mm(x3)


def _dn_prep_kernel(x_ref, halo_ref, w_ref, dba_ref, alog_ref, dtb_ref, o_ref, beta_ref, g_ref, buf_ref,
                    *, blocks_per_seq):
    i = pl.program_id(0)
    j = pl.program_id(1)
    t = x_ref.shape[0]
    first_of_seq = (i % blocks_per_seq) == 0
    buf_ref[0:SUBLANES, :] = jnp.where(first_of_seq, 0.0, halo_ref[...])
    buf_ref[SUBLANES:, :] = x_ref[...]
    scale = jnp.where(j == 0, DN_DIM ** -0.5, 1.0)
    for h in range(DN_HEADS):
        sl = slice(h * DN_DIM, (h + 1) * DN_DIM)
        y = w_ref[DN_CONV - 1:DN_CONV, sl] * x_ref[:, sl]
        for back in range(1, DN_CONV):
            y = y + w_ref[DN_CONV - 1 - back:DN_CONV - back, sl] * buf_ref[pl.ds(SUBLANES - back, t), sl]
        y = y * jax.nn.sigmoid(y)
        nrm = y * lax.rsqrt(jnp.sum(y * y, axis=-1, keepdims=True) + EPS) * scale
        o_ref[:, sl] = jnp.where(j < 2, nrm, y)

    @pl.when(j == 0)
    def _():
        dba = dba_ref[...]
        beta = jax.nn.sigmoid(dba)
        z = dba + dtb_ref[...]
        softplus = jnp.maximum(z, 0.0) + jnp.log1p(jnp.exp(-jnp.abs(z)))
        g = -jnp.exp(alog_ref[...]) * softplus
        rt = lax.broadcasted_iota(jnp.int32, (t, t), 0)
        ct = lax.broadcasted_iota(jnp.int32, (t, t), 1)
        blocktri = jnp.where((rt >= ct) & (rt // DN_CHUNK == ct // DN_CHUNK), 1.0, 0.0).astype(BF16)
        g = _dot_exact01(blocktri, g, left=True)
        wide = DN_HEADS * DN_DIM
        src = lax.broadcasted_iota(jnp.int32, (LANES, wide), 0)
        head = lax.broadcasted_iota(jnp.int32, (LANES, wide), 1) // DN_DIM
        beta_ref[...] = _dot_exact01(beta, jnp.where(src == head, 1.0, 0.0).astype(BF16), left=False)
        g_ref[...] = _dot_exact01(g, jnp.where(src == head + DN_HEADS, 1.0, 0.0).astype(BF16), left=False)


def _dn_prep(proj, conv_w, alog_row, dtb_row, seq, t=512):
    n = proj.shape[0]
    width = DN_HEADS * DN_DIM
    hb = t // SUBLANES
    return pl.pallas_call(
        functools.partial(_dn_prep_kernel, blocks_per_seq=seq // t),
        grid=(n // t, 3),
        in_specs=[
            pl.BlockSpec((t, width), lambda i, j: (i, OFF_DQ // width + j)),
            pl.BlockSpec((SUBLANES, width), lambda i, j: (jnp.maximum(i * hb - 1, 0), OFF_DQ // width + j)),
            pl.BlockSpec((DN_CONV, width), lambda i, j: (0, j)),
            pl.BlockSpec((t, LANES), lambda i, j: (i, OFF_DBA // LANES)),
            pl.BlockSpec((1, LANES), lambda i, j: (0, 0)),
            pl.BlockSpec((1, LANES), lambda i, j: (0, 0)),
        ],
        out_specs=[
            pl.BlockSpec((t, width), lambda i, j: (i, j)),
            pl.BlockSpec((t, width), lambda i, j: (i, 0)),
            pl.BlockSpec((t, width), lambda i, j: (i, 0)),
        ],
        out_shape=[
            jax.ShapeDtypeStruct((n, 3 * width), F32),
            jax.ShapeDtypeStruct((n, width), F32),
            jax.ShapeDtypeStruct((n, width), F32),
        ],
        scratch_shapes=[pltpu.VMEM((t + SUBLANES, width), F32)],
        compiler_params=_cparams(("arbitrary", "arbitrary")),
        name="dn_prep",
    )(proj, proj, conv_w, proj, alog_row, dtb_row)


def _dot_nt(a, b):
    return lax.dot_general(a, b, (((1,), (1,)), ((), ())), preferred_element_type=F32)


def _mm_dn(a, b):
    return jnp.dot(a.astype(BF16), b.astype(BF16), preferred_element_type=F32)


def _deltanet_kernel(q_ref, k_ref, v_ref, beta_ref, g_ref, z_ref, nw_ref, o_ref, state_ref):
    c = DN_CHUNK
    t = q_ref.shape[0]
    n_chunks = t // c

    @pl.when(pl.program_id(2) == 0)
    def _():
        state_ref[...] = jnp.zeros_like(state_ref)

    gc_all = g_ref[...]
    beta_all = beta_ref[...]

    gsz = DN_GROUP
    cpg = gsz // c
    ri = lax.broadcasted_iota(jnp.int32, (gsz, gsz), 0)
    ci = lax.broadcasted_iota(jnp.int32, (gsz, gsz), 1)
    same = (ri // c) == (ci // c)
    causal = (ri >= ci) & same
    strict = (ri > ci) & same
    eye = jnp.where(ri == ci, 1.0, 0.0)
    chunk_of_col = lax.broadcasted_iota(jnp.int32, (DN_DIM, gsz), 1) // c

    heads = q_ref.shape[1] // DN_DIM
    groups = range(heads * (t // gsz))
    rows = [slice((p % (t // gsz)) * gsz, (p % (t // gsz) + 1) * gsz) for p in groups]
    cols = [slice((p // (t // gsz)) * DN_DIM, (p // (t // gsz) + 1) * DN_DIM) for p in groups]
    q = [q_ref[rows[p], cols[p]] for p in groups]
    k = [k_ref[rows[p], cols[p]] for p in groups]
    gc = [gc_all[rows[p], cols[p]] for p in groups]
    eg = [jnp.exp(x) for x in gc]
    beta = [beta_all[rows[p], cols[p]] for p in groups]
    kb = [k[gi] * beta[gi] for gi in groups]
    rhs = [jnp.concatenate([kb[gi] * eg[gi], v_ref[rows[gi], cols[gi]] * beta[gi]], axis=1) for gi in groups]
    decay = []
    for gi in groups:
        gc2 = jnp.concatenate([gc[gi]] * (gsz // DN_DIM), axis=1)
        gcr = jnp.sum(gc2 * eye, axis=0, keepdims=True)
        decay.append(jnp.exp(jnp.where(causal, gc2 - gcr, NEG_INF)))
    a2 = [_dot_nt(jnp.concatenate([kb[gi], q[gi]], axis=0).astype(BF16), k[gi].astype(BF16)) for gi in groups]
    low = [jnp.where(strict, a2[gi][:gsz] * decay[gi], 0.0) for gi in groups]
    qk = [a2[gi][gsz:] * decay[gi] for gi in groups]
    size = 2
    pair = ((ri // size) == (ci // size)) & (ri > ci)
    inv = [eye - jnp.where(pair, x, 0.0) for x in low]
    while size < c:
        size *= 2
        pair = ((ri // size) == (ci // size)) & ((ri // (size // 2)) != (ci // (size // 2))) & (ri > ci)
        sub = [jnp.where(pair, x, 0.0) for x in low]
        y = [_mm_dn(sub[gi], inv[gi]) for gi in groups]
        inv = [inv[gi] - _mm_dn(inv[gi], y[gi]) for gi in groups]
    wu = [_mm_dn(inv[gi], rhs[gi]) for gi in groups]
    lasts, res = [], []
    for gi in groups:
        last = [gc[gi][(n + 1) * c - 1:(n + 1) * c, :] for n in range(cpg)]
        lasts.append(last)
        g_last_rows = jnp.concatenate([jnp.broadcast_to(x, (c, DN_DIM)) for x in last], axis=0)
        kd_t = jnp.transpose(k[gi] * jnp.exp(g_last_rows - gc[gi]))
        xs = [jnp.where(chunk_of_col == n, kd_t, 0.0) for n in range(cpg)]
        xs += [qk[gi][n * c:(n + 1) * c, :] for n in range(cpg)]
        res.append(jnp.dot(jnp.concatenate(xs, axis=0).astype(BF16), wu[gi].astype(BF16),
                           preferred_element_type=F32))
    pre = []
    for gi in groups:
        qd = q[gi] * eg[gi]
        for n in range(cpg):
            kw = res[gi][n * DN_DIM:(n + 1) * DN_DIM, :DN_DIM]
            ku = res[gi][n * DN_DIM:(n + 1) * DN_DIM, DN_DIM:]
            base = cpg * DN_DIM + n * c
            coef = qd[n * c:(n + 1) * c, :] - res[gi][base:base + c, :DN_DIM]
            qku = res[gi][base:base + c, DN_DIM:]
            lhs = jnp.concatenate([-kw, coef], axis=0).astype(BF16)
            pre.append((lhs, ku, qku, jnp.exp(lasts[gi][n])))

    states = [state_ref[hd] for hd in range(heads)]
    nw = nw_ref[...]
    for n in range(n_chunks):
        crow = slice(n * c, (n + 1) * c)
        for hd in range(heads):
            lhs, ku, qku, gl = pre[hd * n_chunks + n]
            r = jnp.dot(lhs, states[hd].astype(BF16), preferred_element_type=F32)
            o = r[DN_DIM:] + qku
            states[hd] = states[hd] * gl + r[:DN_DIM] + ku
            ccol = slice(hd * DN_DIM, (hd + 1) * DN_DIM)
            z = z_ref[crow, ccol]
            o_ref[crow, ccol] = (_rms_rows(o, nw) * (z * jax.nn.sigmoid(z))).astype(o_ref.dtype)
    for hd in range(heads):
        state_ref[hd] = states[hd]


def _deltanet(qkv, beta, g, proj, norm_w, bsz, seq, t=512, heads=4):
    n = qkv.shape[0]
    spb = seq // t
    hw = heads * DN_DIM
    hsteps = DN_HEADS // heads
    zoff = OFF_DZ // hw
    col = lambda off: (lambda b, h, i: (b * spb + i, off + h))
    return pl.pallas_call(
        _deltanet_kernel,
        grid=(bsz, hsteps, spb),
        in_specs=[
            pl.BlockSpec((t, hw), col(0)),
            pl.BlockSpec((t, hw), col(hsteps)),
            pl.BlockSpec((t, hw), col(2 * hsteps)),
            pl.BlockSpec((t, hw), col(0)),
            pl.BlockSpec((t, hw), col(0)),
            pl.BlockSpec((t, hw), col(zoff)),
            pl.BlockSpec((1, DN_DIM), lambda b, h, i: (0, 0)),
        ],
        out_specs=pl.BlockSpec((t, hw), col(0)),
        out_shape=jax.ShapeDtypeStruct((n, DN_HEADS * DN_DIM), BF16),
        scratch_shapes=[pltpu.VMEM((heads, DN_DIM, DN_DIM), F32)],
        compiler_params=_cparams(("arbitrary", "arbitrary", "arbitrary")),
        name="deltanet",
    )(qkv, qkv, qkv, beta, g, proj, norm_w)


def _regroup_w_in(w_in):
    d = w_in.shape[0]
    nsa_q, nsa_kv, dn = N_HEADS * HEAD_DIM, N_GROUPS * HEAD_DIM, DN_HEADS * DN_DIM
    sizes = (nsa_q,) + (nsa_kv,) * 6 + (3 * N_HEADS,) + (dn,) * 4 + (DN_HEADS,) * 2 + (d, d)
    offs = np.concatenate([[0], np.cumsum(sizes)])
    (nq, kc, vc, ksel, vsel, kwin, vwin, ngate, dq, dk, dv, dz, db, da, mga, mgb) = (
        w_in[:, offs[i]:offs[i + 1]] for i in range(len(sizes)))
    zeros = lambda w: jnp.zeros((d, w), w_in.dtype)
    ngate = jnp.transpose(ngate.reshape(d, N_GROUPS, HPG, 3), (0, 1, 3, 2)).reshape(d, N_GROUPS, 3 * HPG)
    ngate = jnp.pad(ngate, ((0, 0), (0, 0), (0, GATE_ROWS - 3 * HPG))).reshape(d, N_GROUPS * GATE_ROWS)
    cols = [mga, mgb, dq, dk, dv, nq, dz, kc, vc, ksel, vsel, kwin, vwin,
            ngate, zeros(LANES - N_GROUPS * GATE_ROWS), db, da, zeros(LANES - 2 * DN_HEADS), zeros(PROJ_DIM - PROJ_USED)]
    return jnp.concatenate(cols, axis=1).astype(BF16)


def _overlap_t(n_sb, n_pieces):
    n = np.arange(N_CMP_PAD) - CMP_PAD
    j = np.arange(n_sb)
    valid = (n >= 0) & (n < n_pieces - 1)
    c0 = n * CMP_STRIDE
    s0 = j * SEL_BLOCK
    ov = (c0[None, :] < s0[:, None] + SEL_BLOCK) & (c0[None, :] + CMP_LEN > s0[:, None]) & valid[None, :]
    return jnp.asarray(ov.astype(np.float32), dtype=BF16)


def _token_mixer(h, bsz, seq, norm_mix, w_in, q_norm, k_norm, cmp_pos, cmp_w1, cmp_b1, cmp_w2, dn_conv,
                 dn_a_log, dn_dt_bias, dn_norm, w_a, w_b, w_out, bias_tiles):
    n = bsz * seq
    nqb = seq // Q_BLOCK
    npc = seq // CMP_STRIDE
    n_sb = seq // SEL_BLOCK
    assert npc + CMP_PAD <= N_CMP_PAD and nqb * SUBLANES + NEAR_ROWS <= N_CMP_PAD
    assert nqb % (2 * FAR_SUB) == 0
    assert nqb % QB_PER_STEP == 0
    proj = _norm_matmul(h, norm_mix.reshape(1, -1), _regroup_w_in(w_in))

    k_norm8 = jnp.concatenate([k_norm, jnp.zeros((SUBLANES - 3, HEAD_DIM), F32)], axis=0)
    qT, kc_raw, vc_raw, ks, vsT, kw, vwT, gT = _nsa_prep(proj, q_norm.reshape(1, -1), k_norm8, bsz, seq)
    pieces = jnp.stack([kc_raw, vc_raw], axis=0).reshape(2, bsz, N_GROUPS, npc, CMP_STRIDE * HEAD_DIM)
    cmp = _compress(pieces, cmp_w1, cmp_pos.reshape(2, 1, CMP_LEN * HEAD_DIM), cmp_b1.reshape(2, 1, -1),
                    cmp_w2, k_norm8)
    cmp = jnp.pad(cmp, ((0, 0), (0, 0), (0, 0), (CMP_PAD, N_CMP_PAD - CMP_PAD - npc), (0, 0))).astype(BF16)
    kc = cmp[0]
    vcT = jnp.swapaxes(cmp[1], -1, -2)
    y_nsa = _nsa_attn(qT, gT, ks, vsT, kw, vwT, kc, vcT, _overlap_t(n_sb, npc), *bias_tiles)

    lane_pad = lambda v, off: jnp.zeros((1, LANES), F32).at[0, off:off + DN_HEADS].set(v)
    qkv, beta, g = _dn_prep(proj, dn_conv, lane_pad(dn_a_log, DN_HEADS), lane_pad(dn_dt_bias, DN_HEADS), seq)
    y_dn = _deltanet(qkv, beta, g, proj, dn_norm.reshape(1, -1), bsz, seq)

    return _merge_out(y_nsa, y_dn, w_a.astype(BF16), w_b.astype(BF16), proj, w_out.astype(BF16), h)


def kernel(x, rel_table, norm_ffn1, ffn1_gate, ffn1_up, ffn1_down, norm_mix, w_in, q_norm, k_norm, cmp_pos, cmp_w1, cmp_b1, cmp_w2, dn_conv, dn_a_log, dn_dt_bias, dn_norm, w_branch_nsa, w_branch_dn, w_out, norm_ffn2, ffn2_gate, ffn2_up, ffn2_down):
    bsz, seq, d = x.shape
    depth = w_in.shape[0]
    h = x.reshape(bsz * seq, d)
    bias_tiles = _bias_tiles(rel_table)
    for l in range(depth):
        h = _ffn(h, norm_ffn1[l].reshape(1, -1), ffn1_gate[l].astype(BF16), ffn1_up[l].astype(BF16),
                 ffn1_down[l].astype(BF16))
        h = _token_mixer(h, bsz, seq, norm_mix[l], w_in[l], q_norm[l], k_norm[l], cmp_pos[l], cmp_w1[l],
                         cmp_b1[l], cmp_w2[l], dn_conv[l], dn_a_log[l], dn_dt_bias[l], dn_norm[l],
                         w_branch_nsa[l], w_branch_dn[l], w_out[l], bias_tiles)
        h = _ffn(h, norm_ffn2[l].reshape(1, -1), ffn2_gate[l].astype(BF16), ffn2_up[l].astype(BF16),
                 ffn2_down[l].astype(BF16))
    return h.reshape(bsz, seq, d)
```

```python
import functools
import math

import numpy as np
import jax
import jax.numpy as jnp
from jax import lax
from jax.experimental import pallas as pl
from jax.experimental.pallas import tpu as pltpu

F32 = jnp.float32
BF16 = jnp.bfloat16
HI = lax.Precision.HIGHEST

N_HEADS = 8
N_GROUPS = 2
HPG = N_HEADS // N_GROUPS
HEAD_DIM = 128
CMP_LEN = 32
CMP_STRIDE = 16
CMP_HIDDEN = 128
SEL_BLOCK = 64
SEL_TOPK = 16
WINDOW = 512
Q_BLOCK = 128
DN_HEADS = 8
DN_DIM = 128
DN_CONV = 4
DN_CHUNK = 64
DN_GROUP = 256
REL_BUCKETS = 32
REL_MAX_DIST = 128
EPS = 1e-6
LOG2E = math.log2(math.e)
NEG_INF = -1e30

LANES = 128
SUBLANES = 8
VMEM_LIMIT = 56 * 1024 * 1024

OFF_MGA = 0
OFF_MGB = 2048
OFF_DQ, OFF_DK, OFF_DV = 4096, 5120, 6144
OFF_NQ = 7168
OFF_DZ = 8192
OFF_KV6 = 9216
OFF_NGATE = 10752
OFF_DBA = 10880
PROJ_USED = 11008
PROJ_DIM = 11264

CMP_PAD = 16
N_CMP_PAD = 1152
NEAR_ROWS = 24
NEAR_KEYS = 2 * Q_BLOCK
WIN_KEYS = WINDOW + Q_BLOCK
GATE_ROWS = 16
V_ROWS = HEAD_DIM + 16
FAR_SUB = 4
FAR_PAIRS = 4
FAR_TILES = 2 * FAR_PAIRS * FAR_SUB
QB_PER_STEP = 2


def _rel_bucket_thresholds():
    n = np.arange(0, 4 * REL_MAX_DIST, dtype=np.int64)
    max_exact = REL_BUCKETS // 2
    nf = np.maximum(n, max_exact).astype(np.float32)
    large = max_exact + (np.log(nf / np.float32(max_exact)) / np.float32(math.log(REL_MAX_DIST / max_exact))
                         * np.float32(REL_BUCKETS - max_exact)).astype(np.int32)
    bucket = np.where(n < max_exact, n, np.minimum(large, REL_BUCKETS - 1))
    assert np.all(np.diff(bucket) >= 0)
    return [int(np.argmax(bucket >= b)) for b in range(REL_BUCKETS)]


REL_THRESH = _rel_bucket_thresholds()
FAR_DIST = REL_THRESH[REL_BUCKETS - 1]
assert FAR_DIST <= Q_BLOCK


def _cparams(sem, vmem_limit=VMEM_LIMIT):
    return pltpu.CompilerParams(dimension_semantics=sem, vmem_limit_bytes=vmem_limit)


def _rms_rows(x, w_row):
    ms = jnp.mean(x * x, axis=-1, keepdims=True)
    return x * lax.rsqrt(ms + EPS) * w_row


def _ffn_kernel(x_ref, nw_ref, wg_ref, wu_ref, wd_ref, o_ref, xn_ref, acc_ref):
    j = pl.program_id(1)

    @pl.when(j == 0)
    def _():
        xn_ref[...] = _rms_rows(x_ref[...], nw_ref[...]).astype(BF16)
        acc_ref[...] = jnp.zeros_like(acc_ref)

    xn = xn_ref[...]
    g = jnp.dot(xn, wg_ref[...], preferred_element_type=F32)
    u = jnp.dot(xn, wu_ref[...], preferred_element_type=F32)
    a = (g * jax.nn.sigmoid(g) * u).astype(BF16)
    acc_ref[...] += jnp.dot(a, wd_ref[...], preferred_element_type=F32)

    @pl.when(j == pl.num_programs(1) - 1)
    def _():
        o_ref[...] = x_ref[...] + 0.5 * acc_ref[...]


def _ffn(h, nw, wg, wu, wd, tm=512, tf=512):
    n, d = h.shape
    f = wg.shape[1]
    return pl.pallas_call(
        _ffn_kernel,
        grid=(n // tm, f // tf),
        in_specs=[
            pl.BlockSpec((tm, d), lambda i, j: (i, 0)),
            pl.BlockSpec((1, d), lambda i, j: (0, 0)),
            pl.BlockSpec((d, tf), lambda i, j: (0, j)),
            pl.BlockSpec((d, tf), lambda i, j: (0, j)),
            pl.BlockSpec((tf, d), lambda i, j: (j, 0)),
        ],
        out_specs=pl.BlockSpec((tm, d), lambda i, j: (i, 0)),
        out_shape=jax.ShapeDtypeStruct((n, d), F32),
        scratch_shapes=[pltpu.VMEM((tm, d), BF16), pltpu.VMEM((tm, d), F32)],
        compiler_params=_cparams(("parallel", "arbitrary")),
        name="ffn",
    )(h, nw, wg, wu, wd)


def _norm_matmul_kernel(x_ref, nw_ref, w_ref, o_ref, xn_ref):
    @pl.when(pl.program_id(1) == 0)
    def _():
        xn_ref[...] = _rms_rows(x_ref[...], nw_ref[...]).astype(BF16)

    o_ref[...] = jnp.dot(xn_ref[...], w_ref[...], preferred_element_type=F32)


def _norm_matmul(h, nw, w, tm=1024, tn=1024):
    n, d = h.shape
    nout = w.shape[1]
    return pl.pallas_call(
        _norm_matmul_kernel,
        grid=(n // tm, nout // tn),
        in_specs=[
            pl.BlockSpec((tm, d), lambda i, j: (i, 0)),
            pl.BlockSpec((1, d), lambda i, j: (0, 0)),
            pl.BlockSpec((d, tn), lambda i, j: (0, j)),
        ],
        out_specs=pl.BlockSpec((tm, tn), lambda i, j: (i, j)),
        out_shape=jax.ShapeDtypeStruct((n, nout), F32),
        scratch_shapes=[pltpu.VMEM((tm, d), BF16)],
        compiler_params=_cparams(("parallel", "arbitrary")),
        name="in_proj",
    )(h, nw, w)


def _merge_out_kernel(ya_ref, yb_ref, wa_ref, wb_ref, ga_ref, gb_ref, wo_ref, h_ref, o_ref):
    a = jnp.dot(ya_ref[...], wa_ref[...], preferred_element_type=F32)
    b = jnp.dot(yb_ref[...], wb_ref[...], preferred_element_type=F32)
    merged = (jax.nn.sigmoid(ga_ref[...]) * a + jax.nn.sigmoid(gb_ref[...]) * b).astype(BF16)
    o_ref[...] = h_ref[...] + jnp.dot(merged, wo_ref[...], preferred_element_type=F32)


def _merge_out(ya, yb, wa, wb, proj, wo, h, tm=512):
    n, k = ya.shape
    d = wa.shape[1]
    resident = dict(pipeline_mode=pl.Buffered(1))
    return pl.pallas_call(
        _merge_out_kernel,
        grid=(n // tm,),
        in_specs=[
            pl.BlockSpec((tm, k), lambda i: (i, 0)),
            pl.BlockSpec((tm, k), lambda i: (i, 0)),
            pl.BlockSpec((k, d), lambda i: (0, 0), **resident),
            pl.BlockSpec((k, d), lambda i: (0, 0), **resident),
            pl.BlockSpec((tm, d), lambda i: (i, OFF_MGA // d)),
            pl.BlockSpec((tm, d), lambda i: (i, OFF_MGB // d)),
            pl.BlockSpec((d, d), lambda i: (0, 0), **resident),
            pl.BlockSpec((tm, d), lambda i: (i, 0)),
        ],
        out_specs=pl.BlockSpec((tm, d), lambda i: (i, 0)),
        out_shape=jax.ShapeDtypeStruct((n, d), F32),
        compiler_params=_cparams(("parallel",), vmem_limit=60000 * 1024),
        name="merge_out",
    )(ya, yb, wa, wb, proj, proj, wo, h)


def _nsa_prep_kernel(nq_ref, kv_ref, ng_ref, qn_ref, kn_ref,
                     qT_out, kc_out, vc_out, ks_out, vsT_out, kw_out, vwT_out, gT_out):
    tm = nq_ref.shape[0]
    scale = HEAD_DIM ** -0.5 * LOG2E
    for hh in range(N_HEADS):
        g, h = divmod(hh, HPG)
        qn = _rms_rows(nq_ref[:, hh * HEAD_DIM:(hh + 1) * HEAD_DIM], qn_ref[...]) * scale
        for qi in range(tm // Q_BLOCK):
            blk = qn[qi * Q_BLOCK:(qi + 1) * Q_BLOCK, :]
            qT_out[g, qi, :, h * Q_BLOCK:(h + 1) * Q_BLOCK] = jnp.transpose(blk).astype(BF16)
    ones_rows = jnp.where(lax.broadcasted_iota(jnp.int32, (V_ROWS - HEAD_DIM, tm), 0) == 0, 1.0, 0.0)
    for g in range(N_GROUPS):
        col = lambda kind: kv_ref[:, (kind * N_GROUPS + g) * HEAD_DIM:(kind * N_GROUPS + g + 1) * HEAD_DIM]
        kc_out[g] = col(0).astype(BF16)
        vc_out[g] = col(1).astype(BF16)
        ks_out[g, :, 0:HEAD_DIM] = _rms_rows(col(2), kn_ref[1:2, :]).astype(BF16)
        tok = pl.program_id(0) * tm + lax.broadcasted_iota(jnp.int32, (tm, HEAD_DIM), 0)
        blk_of_row = (tok % (FAR_SUB * Q_BLOCK)) // SEL_BLOCK
        ks_out[g, :, HEAD_DIM:] = jnp.where(
            lax.broadcasted_iota(jnp.int32, (tm, HEAD_DIM), 1) == blk_of_row, 1.0, 0.0).astype(BF16)
        kw_out[g] = _rms_rows(col(4), kn_ref[2:3, :]).astype(BF16)
        for v_out, kind in ((vsT_out, 3), (vwT_out, 5)):
            v_out[g, 0:HEAD_DIM, :] = jnp.transpose(col(kind)).astype(BF16)
            v_out[g, HEAD_DIM:, :] = ones_rows.astype(BF16)
    sig = jax.nn.sigmoid(ng_ref[...])
    for qi in range(tm // Q_BLOCK):
        t = jnp.transpose(sig[qi * Q_BLOCK:(qi + 1) * Q_BLOCK, :])
        for g in range(N_GROUPS):
            gT_out[g, qi] = t[g * GATE_ROWS:(g + 1) * GATE_ROWS, :]


def _nsa_prep(proj, q_norm, k_norm, bsz, seq, tm=512):
    kvw = 6 * N_GROUPS * HEAD_DIM
    spb = seq // tm
    qpb = tm // Q_BLOCK
    nqb = seq // Q_BLOCK
    wide = HPG * Q_BLOCK
    keys = pl.BlockSpec((None, N_GROUPS, tm, HEAD_DIM), lambda i: (i // spb, 0, i % spb, 0))
    keys_aug = pl.BlockSpec((None, N_GROUPS, tm, 2 * HEAD_DIM), lambda i: (i // spb, 0, i % spb, 0))
    vals = pl.BlockSpec((None, N_GROUPS, V_ROWS, tm), lambda i: (i // spb, 0, 0, i % spb))
    keys_shape = jax.ShapeDtypeStruct((bsz, N_GROUPS, seq, HEAD_DIM), BF16)
    vals_shape = jax.ShapeDtypeStruct((bsz, N_GROUPS, V_ROWS, seq), BF16)
    return pl.pallas_call(
        _nsa_prep_kernel,
        grid=(bsz * spb,),
        in_specs=[
            pl.BlockSpec((tm, N_HEADS * HEAD_DIM), lambda i: (i, OFF_NQ // (N_HEADS * HEAD_DIM))),
            pl.BlockSpec((tm, kvw), lambda i: (i, OFF_KV6 // kvw)),
            pl.BlockSpec((tm, LANES), lambda i: (i, OFF_NGATE // LANES)),
            pl.BlockSpec((1, HEAD_DIM), lambda i: (0, 0)),
            pl.BlockSpec((SUBLANES, HEAD_DIM), lambda i: (0, 0)),
        ],
        out_specs=[
            pl.BlockSpec((None, N_GROUPS, qpb, HEAD_DIM, wide), lambda i: (i // spb, 0, i % spb, 0, 0)),
            keys, keys, keys_aug, vals, keys, vals,
            pl.BlockSpec((None, N_GROUPS, qpb, GATE_ROWS, Q_BLOCK), lambda i: (i // spb, 0, i % spb, 0, 0)),
        ],
        out_shape=[
            jax.ShapeDtypeStruct((bsz, N_GROUPS, nqb, HEAD_DIM, wide), BF16),
            keys_shape, keys_shape, jax.ShapeDtypeStruct((bsz, N_GROUPS, seq, 2 * HEAD_DIM), BF16), vals_shape,
            keys_shape, vals_shape,
            jax.ShapeDtypeStruct((bsz, N_GROUPS, nqb, GATE_ROWS, Q_BLOCK), F32),
        ],
        compiler_params=_cparams(("parallel",)),
        name="nsa_prep",
    )(proj, proj, proj, q_norm, k_norm)


def _compress_kernel(p_ref, w1_ref, pos_ref, b1_ref, w2_ref, kn_ref, o_ref):
    half = CMP_STRIDE * HEAD_DIM
    p = p_ref[...]
    w1 = w1_ref[...]
    a = jnp.dot(p, w1[:half].astype(BF16), preferred_element_type=F32)
    b = jnp.dot(p, w1[half:].astype(BF16), preferred_element_type=F32)
    pos = jnp.broadcast_to(pos_ref[...], (SUBLANES, 2 * half))
    c0 = jnp.dot(pos, w1, preferred_element_type=F32, precision=HI)[0:1, :] + b1_ref[...]
    n = p.shape[0]
    hid = a + pltpu.roll(b, n - 1, 0) + c0
    hid = hid * jax.nn.sigmoid(hid)
    out = jnp.dot(hid.astype(BF16), w2_ref[...].astype(BF16), preferred_element_type=F32)
    normed = _rms_rows(out, kn_ref[0:1, :])
    out = jnp.where(pl.program_id(0) == 0, normed, out)
    row = lax.broadcasted_iota(jnp.int32, out.shape, 0)
    o_ref[...] = jnp.where(row < n - 1, out, 0.0)


def _compress(pieces, w1, pos, b1, w2, k_norm):
    _, bsz, ng, npc, width = pieces.shape
    return pl.pallas_call(
        _compress_kernel,
        grid=(2, bsz, ng),
        in_specs=[
            pl.BlockSpec((None, None, None, npc, width), lambda c, b, g: (c, b, g, 0, 0)),
            pl.BlockSpec((None, 2 * width, CMP_HIDDEN), lambda c, b, g: (c, 0, 0)),
            pl.BlockSpec((None, 1, 2 * width), lambda c, b, g: (c, 0, 0)),
            pl.BlockSpec((None, 1, CMP_HIDDEN), lambda c, b, g: (c, 0, 0)),
            pl.BlockSpec((None, CMP_HIDDEN, HEAD_DIM), lambda c, b, g: (c, 0, 0)),
            pl.BlockSpec((SUBLANES, HEAD_DIM), lambda c, b, g: (0, 0)),
        ],
        out_specs=pl.BlockSpec((None, None, None, npc, HEAD_DIM), lambda c, b, g: (c, b, g, 0, 0)),
        out_shape=jax.ShapeDtypeStruct((2, bsz, ng, npc, HEAD_DIM), F32),
        compiler_params=_cparams(("arbitrary", "arbitrary", "arbitrary")),
        name="compress",
    )(pieces, w1, pos, b1, w2, k_norm)


def _bias_kernel(tab_ref, bn_ref, bw_ref, bc_ref):
    g = pl.program_id(0)
    n_wt = WINDOW // Q_BLOCK

    def lookup(dist, head):
        v = jnp.full(dist.shape, tab_ref[0, head], F32)
        for b in range(1, REL_BUCKETS):
            v = jnp.where(dist >= REL_THRESH[b], tab_ref[b, head], v)
        return (v - tab_ref[REL_BUCKETS - 1, head]) * LOG2E

    ki = lax.broadcasted_iota(jnp.int32, (Q_BLOCK, Q_BLOCK), 0)
    qj = lax.broadcasted_iota(jnp.int32, (Q_BLOCK, Q_BLOCK), 1)
    r = lax.broadcasted_iota(jnp.int32, (NEAR_ROWS, Q_BLOCK), 0)
    qc = lax.broadcasted_iota(jnp.int32, (NEAR_ROWS, Q_BLOCK), 1)
    dist_c = qc - CMP_STRIDE * (r - CMP_PAD) - (CMP_LEN - 1)
    for h in range(HPG):
        head = g * HPG + h
        sl = slice(h * Q_BLOCK, (h + 1) * Q_BLOCK)
        diag = jnp.where(ki <= qj, lookup(qj - ki, head), NEG_INF)
        prev = lookup(qj - ki + Q_BLOCK, head)
        bn_ref[0:Q_BLOCK, sl] = prev
        bn_ref[Q_BLOCK:, sl] = diag
        bw_ref[0:Q_BLOCK, sl] = jnp.where(ki > qj, 0.0, NEG_INF)
        bw_ref[Q_BLOCK:(n_wt - 1) * Q_BLOCK, sl] = jnp.zeros(((n_wt - 2) * Q_BLOCK, Q_BLOCK), F32)
        bw_ref[(n_wt - 1) * Q_BLOCK:n_wt * Q_BLOCK, sl] = prev
        bw_ref[n_wt * Q_BLOCK:, sl] = diag
        bc_ref[:, sl] = jnp.where(dist_c >= 0, lookup(dist_c, head), NEG_INF)


def _bias_tiles(rel_table):
    wide = HPG * Q_BLOCK
    return pl.pallas_call(
        _bias_kernel,
        grid=(N_GROUPS,),
        in_specs=[pl.BlockSpec(memory_space=pltpu.SMEM)],
        out_specs=[
            pl.BlockSpec((None, NEAR_KEYS, wide), lambda g: (g, 0, 0)),
            pl.BlockSpec((None, WIN_KEYS, wide), lambda g: (g, 0, 0)),
            pl.BlockSpec((None, NEAR_ROWS, wide), lambda g: (g, 0, 0)),
        ],
        out_shape=[
            jax.ShapeDtypeStruct((N_GROUPS, NEAR_KEYS, wide), F32),
            jax.ShapeDtypeStruct((N_GROUPS, WIN_KEYS, wide), F32),
            jax.ShapeDtypeStruct((N_GROUPS, NEAR_ROWS, wide), F32),
        ],
        compiler_params=_cparams(("arbitrary",)),
        name="rel_bias",
    )(rel_table)


def _tile4(x):
    return jnp.concatenate([x] * HPG, axis=1)


def _nsa_attn_kernel(qT_ref, gT_ref, ks_ref, vsT_ref, kw_ref, vwT_ref, kc_ref, vcT_ref, ovT_ref,
                     bn_ref, bw_ref, bc_ref, o_ref, sc_ref, sel_ref, far_ref, sa_ref, sb_ref,
                     ra_ref, rb_ref):
    step = pl.program_id(2)
    wide = HPG * Q_BLOCK
    half = SEL_BLOCK
    n_wt = WINDOW // Q_BLOCK
    sub_w = FAR_SUB * Q_BLOCK
    blocks_per_sub = sub_w // SEL_BLOCK
    bf16_rows = 2 * SUBLANES
    mask_rows = -(-blocks_per_sub // bf16_rows) * bf16_rows
    assert mask_rows <= HEAD_DIM
    last_chunk = ks_ref.shape[0] // sub_w - 1

    def absent(cond):
        return jnp.where(cond, 0.0, NEG_INF)

    def key_rows(k_ref, kt, width=Q_BLOCK):
        return k_ref[pl.ds(pl.multiple_of(kt * width, width), width), :]

    def value_cols(vT_ref, kt, width=Q_BLOCK):
        return vT_ref[:, pl.ds(pl.multiple_of(kt * width, width), width)]

    def first(s, vT):
        m = jnp.max(s, axis=0, keepdims=True)
        p = jnp.exp2(s - m)
        return m, jnp.dot(vT, p.astype(BF16), preferred_element_type=F32)

    def update(state, s, vT_ref, kt, width=Q_BLOCK):
        m, acc = state
        m_new = jnp.maximum(m, jnp.max(s, axis=0, keepdims=True))
        alpha = jnp.exp2(m - m_new)
        p = jnp.exp2(s - m_new)
        acc = alpha * acc + jnp.dot(value_cols(vT_ref, kt, width), p.astype(BF16), preferred_element_type=F32)
        return m_new, acc

    def normalized(state):
        acc = state[1]
        return acc[:HEAD_DIM] * (1.0 / acc[HEAD_DIM:HEAD_DIM + 1])

    def straight(u):
        qb = step * QB_PER_STEP + u
        sc, sel, far = sc_ref.at[u], sel_ref.at[u], far_ref.at[u]
        qT = qT_ref[u]
        qpos = qb * Q_BLOCK + lax.broadcasted_iota(jnp.int32, (1, Q_BLOCK), 1)

        sc[...] = jnp.dot(kc_ref[...], qT, preferred_element_type=F32)
        sc[0:CMP_PAD, :] = jnp.full((CMP_PAD, wide), NEG_INF, F32)
        r0 = pl.multiple_of(qb * SUBLANES, SUBLANES)
        sc[pl.ds(r0, NEAR_ROWS), :] += bc_ref[...]
        rown = lax.broadcasted_iota(jnp.int32, (N_CMP_PAD, wide), 0)
        qpos4 = qb * Q_BLOCK + (lax.broadcasted_iota(jnp.int32, (1, wide), 1) & (Q_BLOCK - 1))
        s = jnp.where(rown < qb * SUBLANES + NEAR_ROWS, sc[...], NEG_INF)
        m = jnp.max(s, axis=0, keepdims=True)
        p = jnp.exp2(s - m)
        l = jnp.sum(p, axis=0, keepdims=True)
        inv = jnp.where(qpos4 >= CMP_LEN - 1, 1.0 / l, 0.0)
        pn = p * inv
        o_c = jnp.dot(vcT_ref[...], pn.astype(BF16), preferred_element_type=F32)
        psum = pn[:, 0:Q_BLOCK]
        for h in range(1, HPG):
            psum = psum + pn[:, h * Q_BLOCK:(h + 1) * Q_BLOCK]
        p_hi = psum.astype(BF16)
        p_lo = (psum - p_hi.astype(F32)).astype(BF16)
        imp2 = jnp.dot(ovT_ref[...], jnp.concatenate([p_hi, p_lo], axis=1), preferred_element_type=F32)
        imp = imp2[:, 0:Q_BLOCK] + imp2[:, Q_BLOCK:]

        n_sb = imp.shape[0]
        jblk = lax.broadcasted_iota(jnp.int32, (n_sb, Q_BLOCK), 0)
        cur = qpos // SEL_BLOCK
        eligible = jblk * SEL_BLOCK <= qpos
        forced = (jblk == 0) | (jblk == cur) | (jblk == cur - 1)
        picked = -3e38
        score = jnp.where(eligible, jnp.where(forced, picked, imp), NEG_INF)
        for _ in range(max(min(SEL_TOPK, n_sb) - 3, 0)):
            top = jnp.max(score, axis=0, keepdims=True)
            idx = jnp.min(jnp.where(score == top, jblk, n_sb), axis=0, keepdims=True)
            idx = jnp.where(top > -1.0, idx, -1)
            score = jnp.where(jblk == idx, picked, score)
        selmask = jnp.where(score == picked, 0.0, NEG_INF)
        sel[...] = selmask
        far[...] = jnp.where(jblk < 2 * (qb - 1), selmask, NEG_INF)

        def sel_mask_tile(kt):
            a = jnp.broadcast_to(sel[pl.ds(2 * kt, 1), :], (half, Q_BLOCK))
            b = jnp.broadcast_to(sel[pl.ds(2 * kt + 1, 1), :], (half, Q_BLOCK))
            return _tile4(jnp.concatenate([a, b], axis=0))

        tiles = [jnp.maximum(qb - back, 0) for back in range(n_wt, -1, -1)]
        kwin = jnp.concatenate([key_rows(kw_ref, kt) for kt in tiles], axis=0)
        sw = jnp.dot(kwin, qT, preferred_element_type=F32) + bw_ref[...]
        sw = jnp.concatenate(
            [sw[t * Q_BLOCK:(t + 1) * Q_BLOCK] + absent(qb >= n_wt - t) for t in range(n_wt)]
            + [sw[n_wt * Q_BLOCK:]], axis=0)
        o_w = normalized(first(sw, jnp.concatenate([value_cols(vwT_ref, kt) for kt in tiles], axis=1)))

        kp = jnp.maximum(qb - 1, 0)
        k2 = jnp.concatenate([key_rows(ks_ref, kp), key_rows(ks_ref, qb)], axis=0)[:, 0:HEAD_DIM]
        near_mask = jnp.concatenate([sel_mask_tile(kp) + absent(qb >= 1), sel_mask_tile(qb)], axis=0)
        s = jnp.dot(k2, qT, preferred_element_type=F32) + bn_ref[...] + near_mask
        st = first(s, jnp.concatenate([value_cols(vsT_ref, kp), value_cols(vsT_ref, qb)], axis=1))

        for r_ref in (ra_ref.at[u], rb_ref.at[u]):
            r_ref[0:HEAD_DIM, :] = qT
            r_ref[HEAD_DIM:, :] = jnp.zeros((HEAD_DIM, wide), BF16)
        return o_c, o_w, st

    def far_scores(u, cs, r_ref):
        rows = _tile4(far_ref[u, pl.ds(pl.multiple_of(blocks_per_sub * cs, blocks_per_sub), blocks_per_sub), :])
        if mask_rows > blocks_per_sub:
            rows = jnp.concatenate([rows, jnp.zeros((mask_rows - blocks_per_sub, wide), F32)], axis=0)
        r_ref[u, HEAD_DIM:HEAD_DIM + mask_rows, :] = rows.astype(BF16)
        k = ks_ref[pl.ds(pl.multiple_of(cs * sub_w, sub_w), sub_w), :]
        return jnp.dot(k, r_ref[u], preferred_element_type=F32)

    blocks = range(QB_PER_STEP)
    pre = [straight(u) for u in blocks]

    def pair_step(c0, sts):
        sts = list(sts)
        for u in blocks:
            sb_ref[u] = far_scores(u, c0 + 1, rb_ref)
        for u in blocks:
            sts[u] = update(sts[u], sa_ref[u], vsT_ref, c0, sub_w)
        for u in blocks:
            sa_ref[u] = far_scores(u, jnp.minimum(c0 + 2, last_chunk), ra_ref)
        for u in blocks:
            sts[u] = update(sts[u], sb_ref[u], vsT_ref, c0 + 1, sub_w)
        return tuple(sts)

    def far_body(i, sts):
        for pair in range(FAR_PAIRS):
            sts = pair_step(2 * (FAR_PAIRS * i + pair), sts)
        return sts

    far_tiles = jnp.maximum(step * QB_PER_STEP + QB_PER_STEP - 2, 0)
    n_full = far_tiles // FAR_TILES
    n_pairs = (far_tiles + 2 * FAR_SUB - 1) // (2 * FAR_SUB)
    for u in blocks:
        sa_ref[u] = far_scores(u, 0, ra_ref)
    sts = tuple(pre[u][2] for u in blocks)
    sts = lax.fori_loop(0, n_full, far_body, sts)
    sts = lax.fori_loop(FAR_PAIRS * n_full, n_pairs, lambda j, sts: pair_step(2 * j, sts), sts)

    for u in blocks:
        o_c, o_w, _ = pre[u]
        o_s = normalized(sts[u])
        for h in range(HPG):
            sl = slice(h * Q_BLOCK, (h + 1) * Q_BLOCK)
            gate = lambda c: gT_ref[u, c * HPG + h:c * HPG + h + 1, :]
            mix = gate(0) * o_c[:, sl] + gate(1) * o_s[:, sl] + gate(2) * o_w[:, sl]
            o_ref[u * Q_BLOCK:(u + 1) * Q_BLOCK, h * HEAD_DIM:(h + 1) * HEAD_DIM] = (
                jnp.transpose(mix).astype(o_ref.dtype))


def _nsa_attn(qT, gT, ks, vsT, kw, vwT, kc, vcT, ovT, bn, bw, bc):
    bsz, ng, nqb, _, wide = qT.shape
    seq = ks.shape[2]
    n_sb = ovT.shape[0]
    qs = QB_PER_STEP
    steps = nqb // qs
    big = lambda b, g, q: (b, g, 0, 0)
    blk = lambda b, g, q: (b, g, q, 0, 0)
    grp = lambda b, g, q: (g, 0, 0)
    single = dict(pipeline_mode=pl.Buffered(1))
    return pl.pallas_call(
        _nsa_attn_kernel,
        grid=(bsz, ng, steps),
        in_specs=[
            pl.BlockSpec((None, None, qs, HEAD_DIM, wide), blk),
            pl.BlockSpec((None, None, qs, GATE_ROWS, Q_BLOCK), blk),
            pl.BlockSpec((None, None, seq, 2 * HEAD_DIM), big, **single),
            pl.BlockSpec((None, None, V_ROWS, seq), big, **single),
            pl.BlockSpec((None, None, seq, HEAD_DIM), big, **single),
            pl.BlockSpec((None, None, V_ROWS, seq), big, **single),
            pl.BlockSpec((None, None, N_CMP_PAD, HEAD_DIM), big),
            pl.BlockSpec((None, None, HEAD_DIM, N_CMP_PAD), big),
            pl.BlockSpec((n_sb, N_CMP_PAD), lambda b, g, q: (0, 0)),
            pl.BlockSpec((None, NEAR_KEYS, wide), grp),
            pl.BlockSpec((None, WIN_KEYS, wide), grp),
            pl.BlockSpec((None, NEAR_ROWS, wide), grp),
        ],
        out_specs=pl.BlockSpec((qs * Q_BLOCK, HPG * HEAD_DIM), lambda b, g, q: (b * steps + q, g)),
        out_shape=jax.ShapeDtypeStruct((bsz * seq, ng * HPG * HEAD_DIM), BF16),
        scratch_shapes=[pltpu.VMEM((qs, N_CMP_PAD, wide), F32), pltpu.VMEM((qs, n_sb, Q_BLOCK), F32),
                        pltpu.VMEM((qs, n_sb, Q_BLOCK), F32),
                        pltpu.VMEM((qs, FAR_SUB * Q_BLOCK, wide), F32),
                        pltpu.VMEM((qs, FAR_SUB * Q_BLOCK, wide), F32),
                        pltpu.VMEM((qs, 2 * HEAD_DIM, wide), BF16), pltpu.VMEM((qs, 2 * HEAD_DIM, wide), BF16)],
        compiler_params=_cparams(("arbitrary", "arbitrary", "arbitrary")),
        name="nsa_attn",
    )(qT, gT, ks, vsT, kw, vwT, kc, vcT, ovT, bn, bw, bc)


def _dot_exact01(a, b, left):
    x = b if left else a
    x1 = x.astype(BF16)
    r1 = x - x1.astype(F32)
    x2 = r1.astype(BF16)
    x3 = (r1 - x2.astype(F32)).astype(BF16)
    mm = (lambda p: jnp.dot(a, p, preferred_element_type=F32)) if left else (
        lambda p: jnp.dot(p, b, preferred_element_type=F32))
    return mm(x1) + mm(x2) + mm(x3)


def _dn_prep_kernel(x_ref, halo_ref, w_ref, dba_ref, alog_ref, dtb_ref, o_ref, beta_ref, g_ref, buf_ref,
                    *, blocks_per_seq):
    i = pl.program_id(0)
    j = pl.program_id(1)
    t = x_ref.shape[0]
    first_of_seq = (i % blocks_per_seq) == 0
    buf_ref[0:SUBLANES, :] = jnp.where(first_of_seq, 0.0, halo_ref[...])
    buf_ref[SUBLANES:, :] = x_ref[...]
    scale = jnp.where(j == 0, DN_DIM ** -0.5, 1.0)
    for h in range(DN_HEADS):
        sl = slice(h * DN_DIM, (h + 1) * DN_DIM)
        y = w_ref[DN_CONV - 1:DN_CONV, sl] * x_ref[:, sl]
        for back in range(1, DN_CONV):
            y = y + w_ref[DN_CONV - 1 - back:DN_CONV - back, sl] * buf_ref[pl.ds(SUBLANES - back, t), sl]
        y = y * jax.nn.sigmoid(y)
        nrm = y * lax.rsqrt(jnp.sum(y * y, axis=-1, keepdims=True) + EPS) * scale
        o_ref[:, sl] = jnp.where(j < 2, nrm, y)

    @pl.when(j == 0)
    def _():
        dba = dba_ref[...]
        beta = jax.nn.sigmoid(dba)
        z = dba + dtb_ref[...]
        softplus = jnp.maximum(z, 0.0) + jnp.log1p(jnp.exp(-jnp.abs(z)))
        g = -jnp.exp(alog_ref[...]) * softplus
        rt = lax.broadcasted_iota(jnp.int32, (t, t), 0)
        ct = lax.broadcasted_iota(jnp.int32, (t, t), 1)
        blocktri = jnp.where((rt >= ct) & (rt // DN_CHUNK == ct // DN_CHUNK), 1.0, 0.0).astype(BF16)
        g = _dot_exact01(blocktri, g, left=True)
        wide = DN_HEADS * DN_DIM
        src = lax.broadcasted_iota(jnp.int32, (LANES, wide), 0)
        head = lax.broadcasted_iota(jnp.int32, (LANES, wide), 1) // DN_DIM
        beta_ref[...] = _dot_exact01(beta, jnp.where(src == head, 1.0, 0.0).astype(BF16), left=False)
        g_ref[...] = _dot_exact01(g, jnp.where(src == head + DN_HEADS, 1.0, 0.0).astype(BF16), left=False)


def _dn_prep(proj, conv_w, alog_row, dtb_row, seq, t=512):
    n = proj.shape[0]
    width = DN_HEADS * DN_DIM
    hb = t // SUBLANES
    return pl.pallas_call(
        functools.partial(_dn_prep_kernel, blocks_per_seq=seq // t),
        grid=(n // t, 3),
        in_specs=[
            pl.BlockSpec((t, width), lambda i, j: (i, OFF_DQ // width + j)),
            pl.BlockSpec((SUBLANES, width), lambda i, j: (jnp.maximum(i * hb - 1, 0), OFF_DQ // width + j)),
            pl.BlockSpec((DN_CONV, width), lambda i, j: (0, j)),
            pl.BlockSpec((t, LANES), lambda i, j: (i, OFF_DBA // LANES)),
            pl.BlockSpec((1, LANES), lambda i, j: (0, 0)),
            pl.BlockSpec((1, LANES), lambda i, j: (0, 0)),
        ],
        out_specs=[
            pl.BlockSpec((t, width), lambda i, j: (i, j)),
            pl.BlockSpec((t, width), lambda i, j: (i, 0)),
            pl.BlockSpec((t, width), lambda i, j: (i, 0)),
        ],
        out_shape=[
            jax.ShapeDtypeStruct((n, 3 * width), F32),
            jax.ShapeDtypeStruct((n, width), F32),
            jax.ShapeDtypeStruct((n, width), F32),
        ],
        scratch_shapes=[pltpu.VMEM((t + SUBLANES, width), F32)],
        compiler_params=_cparams(("arbitrary", "arbitrary")),
        name="dn_prep",
    )(proj, proj, conv_w, proj, alog_row, dtb_row)


def _dot_nt(a, b):
    return lax.dot_general(a, b, (((1,), (1,)), ((), ())), preferred_element_type=F32)


def _mm_dn(a, b):
    return jnp.dot(a.astype(BF16), b.astype(BF16), preferred_element_type=F32)


def _deltanet_kernel(q_ref, k_ref, v_ref, beta_ref, g_ref, z_ref, nw_ref, o_ref, state_ref):
    c = DN_CHUNK
    t = q_ref.shape[0]
    n_chunks = t // c

    @pl.when(pl.program_id(2) == 0)
    def _():
        state_ref[...] = jnp.zeros_like(state_ref)

    gc_all = g_ref[...]
    beta_all = beta_ref[...]

    gsz = DN_GROUP
    cpg = gsz // c
    ri = lax.broadcasted_iota(jnp.int32, (gsz, gsz), 0)
    ci = lax.broadcasted_iota(jnp.int32, (gsz, gsz), 1)
    same = (ri // c) == (ci // c)
    causal = (ri >= ci) & same
    strict = (ri > ci) & same
    eye = jnp.where(ri == ci, 1.0, 0.0)
    chunk_of_col = lax.broadcasted_iota(jnp.int32, (DN_DIM, gsz), 1) // c

    heads = q_ref.shape[1] // DN_DIM
    groups = range(heads * (t // gsz))
    rows = [slice((p % (t // gsz)) * gsz, (p % (t // gsz) + 1) * gsz) for p in groups]
    cols = [slice((p // (t // gsz)) * DN_DIM, (p // (t // gsz) + 1) * DN_DIM) for p in groups]
    q = [q_ref[rows[p], cols[p]] for p in groups]
    k = [k_ref[rows[p], cols[p]] for p in groups]
    gc = [gc_all[rows[p], cols[p]] for p in groups]
    eg = [jnp.exp(x) for x in gc]
    beta = [beta_all[rows[p], cols[p]] for p in groups]
    kb = [k[gi] * beta[gi] for gi in groups]
    rhs = [jnp.concatenate([kb[gi] * eg[gi], v_ref[rows[gi], cols[gi]] * beta[gi]], axis=1) for gi in groups]
    decay = []
    for gi in groups:
        gc2 = jnp.concatenate([gc[gi]] * (gsz // DN_DIM), axis=1)
        gcr = jnp.sum(gc2 * eye, axis=0, keepdims=True)
        decay.append(jnp.exp(jnp.where(causal, gc2 - gcr, NEG_INF)))
    a2 = [_dot_nt(jnp.concatenate([kb[gi], q[gi]], axis=0).astype(BF16), k[gi].astype(BF16)) for gi in groups]
    low = [jnp.where(strict, a2[gi][:gsz] * decay[gi], 0.0) for gi in groups]
    qk = [a2[gi][gsz:] * decay[gi] for gi in groups]
    size = 2
    pair = ((ri // size) == (ci // size)) & (ri > ci)
    inv = [eye - jnp.where(pair, x, 0.0) for x in low]
    while size < c:
        size *= 2
        pair = ((ri // size) == (ci // size)) & ((ri // (size // 2)) != (ci // (size // 2))) & (ri > ci)
        sub = [jnp.where(pair, x, 0.0) for x in low]
        y = [_mm_dn(sub[gi], inv[gi]) for gi in groups]
        inv = [inv[gi] - _mm_dn(inv[gi], y[gi]) for gi in groups]
    wu = [_mm_dn(inv[gi], rhs[gi]) for gi in groups]
    lasts, res = [], []
    for gi in groups:
        last = [gc[gi][(n + 1) * c - 1:(n + 1) * c, :] for n in range(cpg)]
        lasts.append(last)
        g_last_rows = jnp.concatenate([jnp.broadcast_to(x, (c, DN_DIM)) for x in last], axis=0)
        kd_t = jnp.transpose(k[gi] * jnp.exp(g_last_rows - gc[gi]))
        xs = [jnp.where(chunk_of_col == n, kd_t, 0.0) for n in range(cpg)]
        xs += [qk[gi][n * c:(n + 1) * c, :] for n in range(cpg)]
        res.append(jnp.dot(jnp.concatenate(xs, axis=0).astype(BF16), wu[gi].astype(BF16),
                           preferred_element_type=F32))
    pre = []
    for gi in groups:
        qd = q[gi] * eg[gi]
        for n in range(cpg):
            kw = res[gi][n * DN_DIM:(n + 1) * DN_DIM, :DN_DIM]
            ku = res[gi][n * DN_DIM:(n + 1) * DN_DIM, DN_DIM:]
            base = cpg * DN_DIM + n * c
            coef = qd[n * c:(n + 1) * c, :] - res[gi][base:base + c, :DN_DIM]
            qku = res[gi][base:base + c, DN_DIM:]
            lhs = jnp.concatenate([-kw, coef], axis=0).astype(BF16)
            pre.append((lhs, ku, qku, jnp.exp(lasts[gi][n])))

    states = [state_ref[hd] for hd in range(heads)]
    nw = nw_ref[...]
    for n in range(n_chunks):
        crow = slice(n * c, (n + 1) * c)
        for hd in range(heads):
            lhs, ku, qku, gl = pre[hd * n_chunks + n]
            r = jnp.dot(lhs, states[hd].astype(BF16), preferred_element_type=F32)
            o = r[DN_DIM:] + qku
            states[hd] = states[hd] * gl + r[:DN_DIM] + ku
            ccol = slice(hd * DN_DIM, (hd + 1) * DN_DIM)
            z = z_ref[crow, ccol]
            o_ref[crow, ccol] = (_rms_rows(o, nw) * (z * jax.nn.sigmoid(z))).astype(o_ref.dtype)
    for hd in range(heads):
        state_ref[hd] = states[hd]


def _deltanet(qkv, beta, g, proj, norm_w, bsz, seq, t=512, heads=4):
    n = qkv.shape[0]
    spb = seq // t
    hw = heads * DN_DIM
    hsteps = DN_HEADS // heads
    zoff = OFF_DZ // hw
    col = lambda off: (lambda b, h, i: (b * spb + i, off + h))
    return pl.pallas_call(
        _deltanet_kernel,
        grid=(bsz, hsteps, spb),
        in_specs=[
            pl.BlockSpec((t, hw), col(0)),
            pl.BlockSpec((t, hw), col(hsteps)),
            pl.BlockSpec((t, hw), col(2 * hsteps)),
            pl.BlockSpec((t, hw), col(0)),
            pl.BlockSpec((t, hw), col(0)),
            pl.BlockSpec((t, hw), col(zoff)),
            pl.BlockSpec((1, DN_DIM), lambda b, h, i: (0, 0)),
        ],
        out_specs=pl.BlockSpec((t, hw), col(0)),
        out_shape=jax.ShapeDtypeStruct((n, DN_HEADS * DN_DIM), BF16),
        scratch_shapes=[pltpu.VMEM((heads, DN_DIM, DN_DIM), F32)],
        compiler_params=_cparams(("arbitrary", "arbitrary", "arbitrary")),
        name="deltanet",
    )(qkv, qkv, qkv, beta, g, proj, norm_w)


def _regroup_w_in(w_in):
    d = w_in.shape[0]
    nsa_q, nsa_kv, dn = N_HEADS * HEAD_DIM, N_GROUPS * HEAD_DIM, DN_HEADS * DN_DIM
    sizes = (nsa_q,) + (nsa_kv,) * 6 + (3 * N_HEADS,) + (dn,) * 4 + (DN_HEADS,) * 2 + (d, d)
    offs = np.concatenate([[0], np.cumsum(sizes)])
    (nq, kc, vc, ksel, vsel, kwin, vwin, ngate, dq, dk, dv, dz, db, da, mga, mgb) = (
        w_in[:, offs[i]:offs[i + 1]] for i in range(len(sizes)))
    zeros = lambda w: jnp.zeros((d, w), w_in.dtype)
    ngate = jnp.transpose(ngate.reshape(d, N_GROUPS, HPG, 3), (0, 1, 3, 2)).reshape(d, N_GROUPS, 3 * HPG)
    ngate = jnp.pad(ngate, ((0, 0), (0, 0), (0, GATE_ROWS - 3 * HPG))).reshape(d, N_GROUPS * GATE_ROWS)
    cols = [mga, mgb, dq, dk, dv, nq, dz, kc, vc, ksel, vsel, kwin, vwin,
            ngate, zeros(LANES - N_GROUPS * GATE_ROWS), db, da, zeros(LANES - 2 * DN_HEADS), zeros(PROJ_DIM - PROJ_USED)]
    return jnp.concatenate(cols, axis=1).astype(BF16)


def _overlap_t(n_sb, n_pieces):
    n = np.arange(N_CMP_PAD) - CMP_PAD
    j = np.arange(n_sb)
    valid = (n >= 0) & (n < n_pieces - 1)
    c0 = n * CMP_STRIDE
    s0 = j * SEL_BLOCK
    ov = (c0[None, :] < s0[:, None] + SEL_BLOCK) & (c0[None, :] + CMP_LEN > s0[:, None]) & valid[None, :]
    return jnp.asarray(ov.astype(np.float32), dtype=BF16)


def _token_mixer(h, bsz, seq, norm_mix, w_in, q_norm, k_norm, cmp_pos, cmp_w1, cmp_b1, cmp_w2, dn_conv,
                 dn_a_log, dn_dt_bias, dn_norm, w_a, w_b, w_out, bias_tiles):
    n = bsz * seq
    nqb = seq // Q_BLOCK
    npc = seq // CMP_STRIDE
    n_sb = seq // SEL_BLOCK
    assert npc + CMP_PAD <= N_CMP_PAD and nqb * SUBLANES + NEAR_ROWS <= N_CMP_PAD
    assert nqb % (2 * FAR_SUB) == 0
    assert nqb % QB_PER_STEP == 0
    proj = _norm_matmul(h, norm_mix.reshape(1, -1), _regroup_w_in(w_in))

    k_norm8 = jnp.concatenate([k_norm, jnp.zeros((SUBLANES - 3, HEAD_DIM), F32)], axis=0)
    qT, kc_raw, vc_raw, ks, vsT, kw, vwT, gT = _nsa_prep(proj, q_norm.reshape(1, -1), k_norm8, bsz, seq)
    pieces = jnp.stack([kc_raw, vc_raw], axis=0).reshape(2, bsz, N_GROUPS, npc, CMP_STRIDE * HEAD_DIM)
    cmp = _compress(pieces, cmp_w1, cmp_pos.reshape(2, 1, CMP_LEN * HEAD_DIM), cmp_b1.reshape(2, 1, -1),
                    cmp_w2, k_norm8)
    cmp = jnp.pad(cmp, ((0, 0), (0, 0), (0, 0), (CMP_PAD, N_CMP_PAD - CMP_PAD - npc), (0, 0))).astype(BF16)
    kc = cmp[0]
    vcT = jnp.swapaxes(cmp[1], -1, -2)
    y_nsa = _nsa_attn(qT, gT, ks, vsT, kw, vwT, kc, vcT, _overlap_t(n_sb, npc), *bias_tiles)

    lane_pad = lambda v, off: jnp.zeros((1, LANES), F32).at[0, off:off + DN_HEADS].set(v)
    qkv, beta, g = _dn_prep(proj, dn_conv, lane_pad(dn_a_log, DN_HEADS), lane_pad(dn_dt_bias, DN_HEADS), seq)
    y_dn = _deltanet(qkv, beta, g, proj, dn_norm.reshape(1, -1), bsz, seq)

    return _merge_out(y_nsa, y_dn, w_a.astype(BF16), w_b.astype(BF16), proj, w_out.astype(BF16), h)


def kernel(x, rel_table, norm_ffn1, ffn1_gate, ffn1_up, ffn1_down, norm_mix, w_in, q_norm, k_norm, cmp_pos, cmp_w1, cmp_b1, cmp_w2, dn_conv, dn_a_log, dn_dt_bias, dn_norm, w_branch_nsa, w_branch_dn, w_out, norm_ffn2, ffn2_gate, ffn2_up, ffn2_down):
    bsz, seq, d = x.shape
    depth = w_in.shape[0]
    h = x.reshape(bsz * seq, d)
    bias_tiles = _bias_tiles(rel_table)
    for l in range(depth):
        h = _ffn(h, norm_ffn1[l].reshape(1, -1), ffn1_gate[l].astype(BF16), ffn1_up[l].astype(BF16),
                 ffn1_down[l].astype(BF16))
        h = _token_mixer(h, bsz, seq, norm_mix[l], w_in[l], q_norm[l], k_norm[l], cmp_pos[l], cmp_w1[l],
                         cmp_b1[l], cmp_w2[l], dn_conv[l], dn_a_log[l], dn_dt_bias[l], dn_norm[l],
                         w_branch_nsa[l], w_branch_dn[l], w_out[l], bias_tiles)
        h = _ffn(h, norm_ffn2[l].reshape(1, -1), ffn2_gate[l].astype(BF16), ffn2_up[l].astype(BF16),
                 ffn2_down[l].astype(BF16))
    return h.reshape(bsz, seq, d)
```

```python
import functools
import math

import numpy as np
import jax
import jax.numpy as jnp
from jax import lax
from jax.experimental import pallas as pl
from jax.experimental.pallas import tpu as pltpu

F32 = jnp.float32
BF16 = jnp.bfloat16
HI = lax.Precision.HIGHEST

N_HEADS = 8
N_GROUPS = 2
HPG = N_HEADS // N_GROUPS
HEAD_DIM = 128
CMP_LEN = 32
CMP_STRIDE = 16
CMP_HIDDEN = 128
SEL_BLOCK = 64
SEL_TOPK = 16
WINDOW = 512
Q_BLOCK = 128
DN_HEADS = 8
DN_DIM = 128
DN_CONV = 4
DN_CHUNK = 64
DN_GROUP = 256
REL_BUCKETS = 32
REL_MAX_DIST = 128
EPS = 1e-6
LOG2E = math.log2(math.e)
NEG_INF = -1e30

LANES = 128
SUBLANES = 8
VMEM_LIMIT = 56 * 1024 * 1024

OFF_MGA = 0
OFF_MGB = 2048
OFF_DQ, OFF_DK, OFF_DV = 4096, 5120, 6144
OFF_NQ = 7168
OFF_DZ = 8192
OFF_KV6 = 9216
OFF_NGATE = 10752
OFF_DBA = 10880
PROJ_USED = 11008
PROJ_DIM = 11264

CMP_PAD = 16
N_CMP_PAD = 1152
NEAR_ROWS = 24
NEAR_KEYS = 2 * Q_BLOCK
WIN_KEYS = WINDOW + Q_BLOCK
GATE_ROWS = 16
V_ROWS = HEAD_DIM + 16
FAR_SUB = 4
FAR_PAIRS = 4
FAR_TILES = 2 * FAR_PAIRS * FAR_SUB
QB_PER_STEP = 2


def _rel_bucket_thresholds():
    n = np.arange(0, 4 * REL_MAX_DIST, dtype=np.int64)
    max_exact = REL_BUCKETS // 2
    nf = np.maximum(n, max_exact).astype(np.float32)
    large = max_exact + (np.log(nf / np.float32(max_exact)) / np.float32(math.log(REL_MAX_DIST / max_exact))
                         * np.float32(REL_BUCKETS - max_exact)).astype(np.int32)
    bucket = np.where(n < max_exact, n, np.minimum(large, REL_BUCKETS - 1))
    assert np.all(np.diff(bucket) >= 0)
    return [int(np.argmax(bucket >= b)) for b in range(REL_BUCKETS)]


REL_THRESH = _rel_bucket_thresholds()
FAR_DIST = REL_THRESH[REL_BUCKETS - 1]
assert FAR_DIST <= Q_BLOCK


def _cparams(sem, vmem_limit=VMEM_LIMIT):
    return pltpu.CompilerParams(dimension_semantics=sem, vmem_limit_bytes=vmem_limit)


def _rms_rows(x, w_row):
    ms = jnp.mean(x * x, axis=-1, keepdims=True)
    return x * lax.rsqrt(ms + EPS) * w_row


def _ffn_kernel(x_ref, nw_ref, wg_ref, wu_ref, wd_ref, o_ref, xn_ref, acc_ref):
    j = pl.program_id(1)

    @pl.when(j == 0)
    def _():
        xn_ref[...] = _rms_rows(x_ref[...], nw_ref[...]).astype(BF16)
        acc_ref[...] = jnp.zeros_like(acc_ref)

    xn = xn_ref[...]
    g = jnp.dot(xn, wg_ref[...], preferred_element_type=F32)
    u = jnp.dot(xn, wu_ref[...], preferred_element_type=F32)
    a = (g * jax.nn.sigmoid(g) * u).astype(BF16)
    acc_ref[...] += jnp.dot(a, wd_ref[...], preferred_element_type=F32)

    @pl.when(j == pl.num_programs(1) - 1)
    def _():
        o_ref[...] = x_ref[...] + 0.5 * acc_ref[...]


def _ffn(h, nw, wg, wu, wd, tm=512, tf=512):
    n, d = h.shape
    f = wg.shape[1]
    return pl.pallas_call(
        _ffn_kernel,
        grid=(n // tm, f // tf),
        in_specs=[
            pl.BlockSpec((tm, d), lambda i, j: (i, 0)),
            pl.BlockSpec((1, d), lambda i, j: (0, 0)),
            pl.BlockSpec((d, tf), lambda i, j: (0, j)),
            pl.BlockSpec((d, tf), lambda i, j: (0, j)),
            pl.BlockSpec((tf, d), lambda i, j: (j, 0)),
        ],
        out_specs=pl.BlockSpec((tm, d), lambda i, j: (i, 0)),
        out_shape=jax.ShapeDtypeStruct((n, d), F32),
        scratch_shapes=[pltpu.VMEM((tm, d), BF16), pltpu.VMEM((tm, d), F32)],
        compiler_params=_cparams(("parallel", "arbitrary")),
        name="ffn",
    )(h, nw, wg, wu, wd)


def _norm_matmul_kernel(x_ref, nw_ref, w_ref, o_ref, xn_ref):
    @pl.when(pl.program_id(1) == 0)
    def _():
        xn_ref[...] = _rms_rows(x_ref[...], nw_ref[...]).astype(BF16)

    o_ref[...] = jnp.dot(xn_ref[...], w_ref[...], preferred_element_type=F32)


def _norm_matmul(h, nw, w, tm=1024, tn=1024):
    n, d = h.shape
    nout = w.shape[1]
    return pl.pallas_call(
        _norm_matmul_kernel,
        grid=(n // tm, nout // tn),
        in_specs=[
            pl.BlockSpec((tm, d), lambda i, j: (i, 0)),
            pl.BlockSpec((1, d), lambda i, j: (0, 0)),
            pl.BlockSpec((d, tn), lambda i, j: (0, j)),
        ],
        out_specs=pl.BlockSpec((tm, tn), lambda i, j: (i, j)),
        out_shape=jax.ShapeDtypeStruct((n, nout), F32),
        scratch_shapes=[pltpu.VMEM((tm, d), BF16)],
        compiler_params=_cparams(("parallel", "arbitrary")),
        name="in_proj",
    )(h, nw, w)


def _merge_out_kernel(ya_ref, yb_ref, wa_ref, wb_ref, ga_ref, gb_ref, wo_ref, h_ref, o_ref):
    a = jnp.dot(ya_ref[...], wa_ref[...], preferred_element_type=F32)
    b = jnp.dot(yb_ref[...], wb_ref[...], preferred_element_type=F32)
    merged = (jax.nn.sigmoid(ga_ref[...]) * a + jax.nn.sigmoid(gb_ref[...]) * b).astype(BF16)
    o_ref[...] = h_ref[...] + jnp.dot(merged, wo_ref[...], preferred_element_type=F32)


def _merge_out(ya, yb, wa, wb, proj, wo, h, tm=512):
    n, k = ya.shape
    d = wa.shape[1]
    resident = dict(pipeline_mode=pl.Buffered(1))
    return pl.pallas_call(
        _merge_out_kernel,
        grid=(n // tm,),
        in_specs=[
            pl.BlockSpec((tm, k), lambda i: (i, 0)),
            pl.BlockSpec((tm, k), lambda i: (i, 0)),
            pl.BlockSpec((k, d), lambda i: (0, 0), **resident),
            pl.BlockSpec((k, d), lambda i: (0, 0), **resident),
            pl.BlockSpec((tm, d), lambda i: (i, OFF_MGA // d)),
            pl.BlockSpec((tm, d), lambda i: (i, OFF_MGB // d)),
            pl.BlockSpec((d, d), lambda i: (0, 0), **resident),
            pl.BlockSpec((tm, d), lambda i: (i, 0)),
        ],
        out_specs=pl.BlockSpec((tm, d), lambda i: (i, 0)),
        out_shape=jax.ShapeDtypeStruct((n, d), F32),
        compiler_params=_cparams(("parallel",), vmem_limit=60000 * 1024),
        name="merge_out",
    )(ya, yb, wa, wb, proj, proj, wo, h)


def _nsa_prep_kernel(nq_ref, kv_ref, ng_ref, qn_ref, kn_ref,
                     qT_out, kc_out, vc_out, ks_out, vsT_out, kw_out, vwT_out, gT_out):
    tm = nq_ref.shape[0]
    scale = HEAD_DIM ** -0.5 * LOG2E
    for hh in range(N_HEADS):
        g, h = divmod(hh, HPG)
        qn = _rms_rows(nq_ref[:, hh * HEAD_DIM:(hh + 1) * HEAD_DIM], qn_ref[...]) * scale
        for qi in range(tm // Q_BLOCK):
            blk = qn[qi * Q_BLOCK:(qi + 1) * Q_BLOCK, :]
            qT_out[g, qi, :, h * Q_BLOCK:(h + 1) * Q_BLOCK] = jnp.transpose(blk).astype(BF16)
    ones_rows = jnp.where(lax.broadcasted_iota(jnp.int32, (V_ROWS - HEAD_DIM, tm), 0) == 0, 1.0, 0.0)
    for g in range(N_GROUPS):
        col = lambda kind: kv_ref[:, (kind * N_GROUPS + g) * HEAD_DIM:(kind * N_GROUPS + g + 1) * HEAD_DIM]
        kc_out[g] = col(0).astype(BF16)
        vc_out[g] = col(1).astype(BF16)
        ks_out[g, :, 0:HEAD_DIM] = _rms_rows(col(2), kn_ref[1:2, :]).astype(BF16)
        tok = pl.program_id(0) * tm + lax.broadcasted_iota(jnp.int32, (tm, HEAD_DIM), 0)
        blk_of_row = (tok % (FAR_SUB * Q_BLOCK)) // SEL_BLOCK
        ks_out[g, :, HEAD_DIM:] = jnp.where(
            lax.broadcasted_iota(jnp.int32, (tm, HEAD_DIM), 1) == blk_of_row, 1.0, 0.0).astype(BF16)
        kw_out[g] = _rms_rows(col(4), kn_ref[2:3, :]).astype(BF16)
        for v_out, kind in ((vsT_out, 3), (vwT_out, 5)):
            v_out[g, 0:HEAD_DIM, :] = jnp.transpose(col(kind)).astype(BF16)
            v_out[g, HEAD_DIM:, :] = ones_rows.astype(BF16)
    sig = jax.nn.sigmoid(ng_ref[...])
    for qi in range(tm // Q_BLOCK):
        t = jnp.transpose(sig[qi * Q_BLOCK:(qi + 1) * Q_BLOCK, :])
        for g in range(N_GROUPS):
            gT_out[g, qi] = t[g * GATE_ROWS:(g + 1) * GATE_ROWS, :]


def _nsa_prep(proj, q_norm, k_norm, bsz, seq, tm=512):
    kvw = 6 * N_GROUPS * HEAD_DIM
    spb = seq // tm
    qpb = tm // Q_BLOCK
    nqb = seq // Q_BLOCK
    wide = HPG * Q_BLOCK
    keys = pl.BlockSpec((None, N_GROUPS, tm, HEAD_DIM), lambda i: (i // spb, 0, i % spb, 0))
    keys_aug = pl.BlockSpec((None, N_GROUPS, tm, 2 * HEAD_DIM), lambda i: (i // spb, 0, i % spb, 0))
    vals = pl.BlockSpec((None, N_GROUPS, V_ROWS, tm), lambda i: (i // spb, 0, 0, i % spb))
    keys_shape = jax.ShapeDtypeStruct((bsz, N_GROUPS, seq, HEAD_DIM), BF16)
    vals_shape = jax.ShapeDtypeStruct((bsz, N_GROUPS, V_ROWS, seq), BF16)
    return pl.pallas_call(
        _nsa_prep_kernel,
        grid=(bsz * spb,),
        in_specs=[
            pl.BlockSpec((tm, N_HEADS * HEAD_DIM), lambda i: (i, OFF_NQ // (N_HEADS * HEAD_DIM))),
            pl.BlockSpec((tm, kvw), lambda i: (i, OFF_KV6 // kvw)),
            pl.BlockSpec((tm, LANES), lambda i: (i, OFF_NGATE // LANES)),
            pl.BlockSpec((1, HEAD_DIM), lambda i: (0, 0)),
            pl.BlockSpec((SUBLANES, HEAD_DIM), lambda i: (0, 0)),
        ],
        out_specs=[
            pl.BlockSpec((None, N_GROUPS, qpb, HEAD_DIM, wide), lambda i: (i // spb, 0, i % spb, 0, 0)),
            keys, keys, keys_aug, vals, keys, vals,
            pl.BlockSpec((None, N_GROUPS, qpb, GATE_ROWS, Q_BLOCK), lambda i: (i // spb, 0, i % spb, 0, 0)),
        ],
        out_shape=[
            jax.ShapeDtypeStruct((bsz, N_GROUPS, nqb, HEAD_DIM, wide), BF16),
            keys_shape, keys_shape, jax.ShapeDtypeStruct((bsz, N_GROUPS, seq, 2 * HEAD_DIM), BF16), vals_shape,
            keys_shape, vals_shape,
            jax.ShapeDtypeStruct((bsz, N_GROUPS, nqb, GATE_ROWS, Q_BLOCK), F32),
        ],
        compiler_params=_cparams(("parallel",)),
        name="nsa_prep",
    )(proj, proj, proj, q_norm, k_norm)


def _compress_kernel(p_ref, w1_ref, pos_ref, b1_ref, w2_ref, kn_ref, o_ref):
    half = CMP_STRIDE * HEAD_DIM
    p = p_ref[...]
    w1 = w1_ref[...]
    a = jnp.dot(p, w1[:half].astype(BF16), preferred_element_type=F32)
    b = jnp.dot(p, w1[half:].astype(BF16), preferred_element_type=F32)
    pos = jnp.broadcast_to(pos_ref[...], (SUBLANES, 2 * half))
    c0 = jnp.dot(pos, w1, preferred_element_type=F32, precision=HI)[0:1, :] + b1_ref[...]
    n = p.shape[0]
    hid = a + pltpu.roll(b, n - 1, 0) + c0
    hid = hid * jax.nn.sigmoid(hid)
    out = jnp.dot(hid.astype(BF16), w2_ref[...].astype(BF16), preferred_element_type=F32)
    normed = _rms_rows(out, kn_ref[0:1, :])
    out = jnp.where(pl.program_id(0) == 0, normed, out)
    row = lax.broadcasted_iota(jnp.int32, out.shape, 0)
    o_ref[...] = jnp.where(row < n - 1, out, 0.0)


def _compress(pieces, w1, pos, b1, w2, k_norm):
    _, bsz, ng, npc, width = pieces.shape
    return pl.pallas_call(
        _compress_kernel,
        grid=(2, bsz, ng),
        in_specs=[
            pl.BlockSpec((None, None, None, npc, width), lambda c, b, g: (c, b, g, 0, 0)),
            pl.BlockSpec((None, 2 * width, CMP_HIDDEN), lambda c, b, g: (c, 0, 0)),
            pl.BlockSpec((None, 1, 2 * width), lambda c, b, g: (c, 0, 0)),
            pl.BlockSpec((None, 1, CMP_HIDDEN), lambda c, b, g: (c, 0, 0)),
            pl.BlockSpec((None, CMP_HIDDEN, HEAD_DIM), lambda c, b, g: (c, 0, 0)),
            pl.BlockSpec((SUBLANES, HEAD_DIM), lambda c, b, g: (0, 0)),
        ],
        out_specs=pl.BlockSpec((None, None, None, npc, HEAD_DIM), lambda c, b, g: (c, b, g, 0, 0)),
        out_shape=jax.ShapeDtypeStruct((2, bsz, ng, npc, HEAD_DIM), F32),
        compiler_params=_cparams(("arbitrary", "arbitrary", "arbitrary")),
        name="compress",
    )(pieces, w1, pos, b1, w2, k_norm)


def _bias_kernel(tab_ref, bn_ref, bw_ref, bc_ref):
    g = pl.program_id(0)
    n_wt = WINDOW // Q_BLOCK

    def lookup(dist, head):
        v = jnp.full(dist.shape, tab_ref[0, head], F32)
        for b in range(1, REL_BUCKETS):
            v = jnp.where(dist >= REL_THRESH[b], tab_ref[b, head], v)
        return (v - tab_ref[REL_BUCKETS - 1, head]) * LOG2E

    ki = lax.broadcasted_iota(jnp.int32, (Q_BLOCK, Q_BLOCK), 0)
    qj = lax.broadcasted_iota(jnp.int32, (Q_BLOCK, Q_BLOCK), 1)
    r = lax.broadcasted_iota(jnp.int32, (NEAR_ROWS, Q_BLOCK), 0)
    qc = lax.broadcasted_iota(jnp.int32, (NEAR_ROWS, Q_BLOCK), 1)
    dist_c = qc - CMP_STRIDE * (r - CMP_PAD) - (CMP_LEN - 1)
    for h in range(HPG):
        head = g * HPG + h
        sl = slice(h * Q_BLOCK, (h + 1) * Q_BLOCK)
        diag = jnp.where(ki <= qj, lookup(qj - ki, head), NEG_INF)
        prev = lookup(qj - ki + Q_BLOCK, head)
        bn_ref[0:Q_BLOCK, sl] = prev
        bn_ref[Q_BLOCK:, sl] = diag
        bw_ref[0:Q_BLOCK, sl] = jnp.where(ki > qj, 0.0, NEG_INF)
        bw_ref[Q_BLOCK:(n_wt - 1) * Q_BLOCK, sl] = jnp.zeros(((n_wt - 2) * Q_BLOCK, Q_BLOCK), F32)
        bw_ref[(n_wt - 1) * Q_BLOCK:n_wt * Q_BLOCK, sl] = prev
        bw_ref[n_wt * Q_BLOCK:, sl] = diag
        bc_ref[:, sl] = jnp.where(dist_c >= 0, lookup(dist_c, head), NEG_INF)


def _bias_tiles(rel_table):
    wide = HPG * Q_BLOCK
    return pl.pallas_call(
        _bias_kernel,
        grid=(N_GROUPS,),
        in_specs=[pl.BlockSpec(memory_space=pltpu.SMEM)],
        out_specs=[
            pl.BlockSpec((None, NEAR_KEYS, wide), lambda g: (g, 0, 0)),
            pl.BlockSpec((None, WIN_KEYS, wide), lambda g: (g, 0, 0)),
            pl.BlockSpec((None, NEAR_ROWS, wide), lambda g: (g, 0, 0)),
        ],
        out_shape=[
            jax.ShapeDtypeStruct((N_GROUPS, NEAR_KEYS, wide), F32),
            jax.ShapeDtypeStruct((N_GROUPS, WIN_KEYS, wide), F32),
            jax.ShapeDtypeStruct((N_GROUPS, NEAR_ROWS, wide), F32),
        ],
        compiler_params=_cparams(("arbitrary",)),
        name="rel_bias",
    )(rel_table)


def _tile4(x):
    return jnp.concatenate([x] * HPG, axis=1)


def _nsa_attn_kernel(qT_ref, gT_ref, ks_ref, vsT_ref, kw_ref, vwT_ref, kc_ref, vcT_ref, ovT_ref,
                     bn_ref, bw_ref, bc_ref, o_ref, sc_ref, sel_ref, far_ref, sa_ref, sb_ref,
                     ra_ref, rb_ref):
    step = pl.program_id(2)
    wide = HPG * Q_BLOCK
    half = SEL_BLOCK
    n_wt = WINDOW // Q_BLOCK
    sub_w = FAR_SUB * Q_BLOCK
    blocks_per_sub = sub_w // SEL_BLOCK
    bf16_rows = 2 * SUBLANES
    mask_rows = -(-blocks_per_sub // bf16_rows) * bf16_rows
    assert mask_rows <= HEAD_DIM
    last_chunk = ks_ref.shape[0] // sub_w - 1

    def absent(cond):
        return jnp.where(cond, 0.0, NEG_INF)

    def key_rows(k_ref, kt, width=Q_BLOCK):
        return k_ref[pl.ds(pl.multiple_of(kt * width, width), width), :]

    def value_cols(vT_ref, kt, width=Q_BLOCK):
        return vT_ref[:, pl.ds(pl.multiple_of(kt * width, width), width)]

    def first(s, vT):
        m = jnp.max(s, axis=0, keepdims=True)
        p = jnp.exp2(s - m)
        return m, jnp.dot(vT, p.astype(BF16), preferred_element_type=F32)

    def update(state, s, vT_ref, kt, width=Q_BLOCK):
        m, acc = state
        m_new = jnp.maximum(m, jnp.max(s, axis=0, keepdims=True))
        alpha = jnp.exp2(m - m_new)
        p = jnp.exp2(s - m_new)
        acc = alpha * acc + jnp.dot(value_cols(vT_ref, kt, width), p.astype(BF16), preferred_element_type=F32)
        return m_new, acc

    def normalized(state):
        acc = state[1]
        return acc[:HEAD_DIM] * (1.0 / acc[HEAD_DIM:HEAD_DIM + 1])

    def straight(u):
        qb = step * QB_PER_STEP + u
        sc, sel, far = sc_ref.at[u], sel_ref.at[u], far_ref.at[u]
        qT = qT_ref[u]
        qpos = qb * Q_BLOCK + lax.broadcasted_iota(jnp.int32, (1, Q_BLOCK), 1)

        sc[...] = jnp.dot(kc_ref[...], qT, preferred_element_type=F32)
        sc[0:CMP_PAD, :] = jnp.full((CMP_PAD, wide), NEG_INF, F32)
        r0 = pl.multiple_of(qb * SUBLANES, SUBLANES)
        sc[pl.ds(r0, NEAR_ROWS), :] += bc_ref[...]
        rown = lax.broadcasted_iota(jnp.int32, (N_CMP_PAD, wide), 0)
        qpos4 = qb * Q_BLOCK + (lax.broadcasted_iota(jnp.int32, (1, wide), 1) & (Q_BLOCK - 1))
        yield
        s = jnp.where(rown < qb * SUBLANES + NEAR_ROWS, sc[...], NEG_INF)
        m = jnp.max(s, axis=0, keepdims=True)
        p = jnp.exp2(s - m)
        l = jnp.sum(p, axis=0, keepdims=True)
        inv = jnp.where(qpos4 >= CMP_LEN - 1, 1.0 / l, 0.0)
        pn = p * inv
        yield
        o_c = jnp.dot(vcT_ref[...], pn.astype(BF16), preferred_element_type=F32)
        psum = pn[:, 0:Q_BLOCK]
        for h in range(1, HPG):
            psum = psum + pn[:, h * Q_BLOCK:(h + 1) * Q_BLOCK]
        p_hi = psum.astype(BF16)
        p_lo = (psum - p_hi.astype(F32)).astype(BF16)
        imp2 = jnp.dot(ovT_ref[...], jnp.concatenate([p_hi, p_lo], axis=1), preferred_element_type=F32)
        imp = imp2[:, 0:Q_BLOCK] + imp2[:, Q_BLOCK:]

        n_sb = imp.shape[0]
        jblk = lax.broadcasted_iota(jnp.int32, (n_sb, Q_BLOCK), 0)
        cur = qpos // SEL_BLOCK
        eligible = jblk * SEL_BLOCK <= qpos
        forced = (jblk == 0) | (jblk == cur) | (jblk == cur - 1)
        picked = -3e38
        score = jnp.where(eligible, jnp.where(forced, picked, imp), NEG_INF)
        for _ in range(max(min(SEL_TOPK, n_sb) - 3, 0)):
            yield
            top = jnp.max(score, axis=0, keepdims=True)
            idx = jnp.min(jnp.where(score == top, jblk, n_sb), axis=0, keepdims=True)
            idx = jnp.where(top > -1.0, idx, -1)
            score = jnp.where(jblk == idx, picked, score)
        yield
        selmask = jnp.where(score == picked, 0.0, NEG_INF)
        sel[...] = selmask
        far[...] = jnp.where(jblk < 2 * (qb - 1), selmask, NEG_INF)

        def sel_mask_tile(kt):
            a = jnp.broadcast_to(sel[pl.ds(2 * kt, 1), :], (half, Q_BLOCK))
            b = jnp.broadcast_to(sel[pl.ds(2 * kt + 1, 1), :], (half, Q_BLOCK))
            return _tile4(jnp.concatenate([a, b], axis=0))

        tiles = [jnp.maximum(qb - back, 0) for back in range(n_wt, -1, -1)]
        kwin = jnp.concatenate([key_rows(kw_ref, kt) for kt in tiles], axis=0)
        sw = jnp.dot(kwin, qT, preferred_element_type=F32) + bw_ref[...]
        sw = jnp.concatenate(
            [sw[t * Q_BLOCK:(t + 1) * Q_BLOCK] + absent(qb >= n_wt - t) for t in range(n_wt)]
            + [sw[n_wt * Q_BLOCK:]], axis=0)
        o_w = normalized(first(sw, jnp.concatenate([value_cols(vwT_ref, kt) for kt in tiles], axis=1)))
        yield

        kp = jnp.maximum(qb - 1, 0)
        k2 = jnp.concatenate([key_rows(ks_ref, kp), key_rows(ks_ref, qb)], axis=0)[:, 0:HEAD_DIM]
        near_mask = jnp.concatenate([sel_mask_tile(kp) + absent(qb >= 1), sel_mask_tile(qb)], axis=0)
        s = jnp.dot(k2, qT, preferred_element_type=F32) + bn_ref[...] + near_mask
        st = first(s, jnp.concatenate([value_cols(vsT_ref, kp), value_cols(vsT_ref, qb)], axis=1))

        for r_ref in (ra_ref.at[u], rb_ref.at[u]):
            r_ref[0:HEAD_DIM, :] = qT
            r_ref[HEAD_DIM:, :] = jnp.zeros((HEAD_DIM, wide), BF16)
        return o_c, o_w, st

    def far_scores(u, cs, r_ref):
        rows = _tile4(far_ref[u, pl.ds(pl.multiple_of(blocks_per_sub * cs, blocks_per_sub), blocks_per_sub), :])
        if mask_rows > blocks_per_sub:
            rows = jnp.concatenate([rows, jnp.zeros((mask_rows - blocks_per_sub, wide), F32)], axis=0)
        r_ref[u, HEAD_DIM:HEAD_DIM + mask_rows, :] = rows.astype(BF16)
        k = ks_ref[pl.ds(pl.multiple_of(cs * sub_w, sub_w), sub_w), :]
        return jnp.dot(k, r_ref[u], preferred_element_type=F32)

    blocks = range(QB_PER_STEP)
    stages = [straight(u) for u in blocks]
    pre = [None] * QB_PER_STEP
    while any(r is None for r in pre):
        for u in blocks:
            if pre[u] is None:
                try:
                    next(stages[u])
                except StopIteration as done:
                    pre[u] = done.value

    def pair_step(c0, sts):
        sts = list(sts)
        for u in blocks:
            sb_ref[u] = far_scores(u, c0 + 1, rb_ref)
        for u in blocks:
            sts[u] = update(sts[u], sa_ref[u], vsT_ref, c0, sub_w)
        for u in blocks:
            sa_ref[u] = far_scores(u, jnp.minimum(c0 + 2, last_chunk), ra_ref)
        for u in blocks:
            sts[u] = update(sts[u], sb_ref[u], vsT_ref, c0 + 1, sub_w)
        return tuple(sts)

    def far_body(i, sts):
        for pair in range(FAR_PAIRS):
            sts = pair_step(2 * (FAR_PAIRS * i + pair), sts)
        return sts

    far_tiles = jnp.maximum(step * QB_PER_STEP + QB_PER_STEP - 2, 0)
    n_full = far_tiles // FAR_TILES
    n_pairs = (far_tiles + 2 * FAR_SUB - 1) // (2 * FAR_SUB)
    for u in blocks:
        sa_ref[u] = far_scores(u, 0, ra_ref)
    sts = tuple(pre[u][2] for u in blocks)
    sts = lax.fori_loop(0, n_full, far_body, sts)
    sts = lax.fori_loop(FAR_PAIRS * n_full, n_pairs, lambda j, sts: pair_step(2 * j, sts), sts)

    for u in blocks:
        o_c, o_w, _ = pre[u]
        o_s = normalized(sts[u])
        for h in range(HPG):
            sl = slice(h * Q_BLOCK, (h + 1) * Q_BLOCK)
            gate = lambda c: gT_ref[u, c * HPG + h:c * HPG + h + 1, :]
            mix = gate(0) * o_c[:, sl] + gate(1) * o_s[:, sl] + gate(2) * o_w[:, sl]
            o_ref[u * Q_BLOCK:(u + 1) * Q_BLOCK, h * HEAD_DIM:(h + 1) * HEAD_DIM] = (
                jnp.transpose(mix).astype(o_ref.dtype))


def _nsa_attn(qT, gT, ks, vsT, kw, vwT, kc, vcT, ovT, bn, bw, bc):
    bsz, ng, nqb, _, wide = qT.shape
    seq = ks.shape[2]
    n_sb = ovT.shape[0]
    qs = QB_PER_STEP
    steps = nqb // qs
    big = lambda b, g, q: (b, g, 0, 0)
    blk = lambda b, g, q: (b, g, q, 0, 0)
    grp = lambda b, g, q: (g, 0, 0)
    single = dict(pipeline_mode=pl.Buffered(1))
    return pl.pallas_call(
        _nsa_attn_kernel,
        grid=(bsz, ng, steps),
        in_specs=[
            pl.BlockSpec((None, None, qs, HEAD_DIM, wide), blk),
            pl.BlockSpec((None, None, qs, GATE_ROWS, Q_BLOCK), blk),
            pl.BlockSpec((None, None, seq, 2 * HEAD_DIM), big, **single),
            pl.BlockSpec((None, None, V_ROWS, seq), big, **single),
            pl.BlockSpec((None, None, seq, HEAD_DIM), big, **single),
            pl.BlockSpec((None, None, V_ROWS, seq), big, **single),
            pl.BlockSpec((None, None, N_CMP_PAD, HEAD_DIM), big),
            pl.BlockSpec((None, None, HEAD_DIM, N_CMP_PAD), big),
            pl.BlockSpec((n_sb, N_CMP_PAD), lambda b, g, q: (0, 0)),
            pl.BlockSpec((None, NEAR_KEYS, wide), grp),
            pl.BlockSpec((None, WIN_KEYS, wide), grp),
            pl.BlockSpec((None, NEAR_ROWS, wide), grp),
        ],
        out_specs=pl.BlockSpec((qs * Q_BLOCK, HPG * HEAD_DIM), lambda b, g, q: (b * steps + q, g)),
        out_shape=jax.ShapeDtypeStruct((bsz * seq, ng * HPG * HEAD_DIM), BF16),
        scratch_shapes=[pltpu.VMEM((qs, N_CMP_PAD, wide), F32), pltpu.VMEM((qs, n_sb, Q_BLOCK), F32),
                        pltpu.VMEM((qs, n_sb, Q_BLOCK), F32),
                        pltpu.VMEM((qs, FAR_SUB * Q_BLOCK, wide), F32),
                        pltpu.VMEM((qs, FAR_SUB * Q_BLOCK, wide), F32),
                        pltpu.VMEM((qs, 2 * HEAD_DIM, wide), BF16), pltpu.VMEM((qs, 2 * HEAD_DIM, wide), BF16)],
        compiler_params=_cparams(("arbitrary", "arbitrary", "arbitrary")),
        name="nsa_attn",
    )(qT, gT, ks, vsT, kw, vwT, kc, vcT, ovT, bn, bw, bc)


def _dot_exact01(a, b, left):
    x = b if left else a
    x1 = x.astype(BF16)
    r1 = x - x1.astype(F32)
    x2 = r1.astype(BF16)
    x3 = (r1 - x2.astype(F32)).astype(BF16)
    mm = (lambda p: jnp.dot(a, p, preferred_element_type=F32)) if left else (
        lambda p: jnp.dot(p, b, preferred_element_type=F32))
    return mm(x1) + mm(x2) + mm(x3)


def _dn_prep_kernel(x_ref, halo_ref, w_ref, dba_ref, alog_ref, dtb_ref, o_ref, beta_ref, g_ref, buf_ref,
                    *, blocks_per_seq):
    i = pl.program_id(0)
    j = pl.program_id(1)
    t = x_ref.shape[0]
    first_of_seq = (i % blocks_per_seq) == 0
    buf_ref[0:SUBLANES, :] = jnp.where(first_of_seq, 0.0, halo_ref[...])
    buf_ref[SUBLANES:, :] = x_ref[...]
    scale = jnp.where(j == 0, DN_DIM ** -0.5, 1.0)
    for h in range(DN_HEADS):
        sl = slice(h * DN_DIM, (h + 1) * DN_DIM)
        y = w_ref[DN_CONV - 1:DN_CONV, sl] * x_ref[:, sl]
        for back in range(1, DN_CONV):
            y = y + w_ref[DN_CONV - 1 - back:DN_CONV - back, sl] * buf_ref[pl.ds(SUBLANES - back, t), sl]
        y = y * jax.nn.sigmoid(y)
        nrm = y * lax.rsqrt(jnp.sum(y * y, axis=-1, keepdims=True) + EPS) * scale
        o_ref[:, sl] = jnp.where(j < 2, nrm, y)

    @pl.when(j == 0)
    def _():
        dba = dba_ref[...]
        beta = jax.nn.sigmoid(dba)
        z = dba + dtb_ref[...]
        softplus = jnp.maximum(z, 0.0) + jnp.log1p(jnp.exp(-jnp.abs(z)))
        g = -jnp.exp(alog_ref[...]) * softplus
        rt = lax.broadcasted_iota(jnp.int32, (t, t), 0)
        ct = lax.broadcasted_iota(jnp.int32, (t, t), 1)
        blocktri = jnp.where((rt >= ct) & (rt // DN_CHUNK == ct // DN_CHUNK), 1.0, 0.0).astype(BF16)
        g = _dot_exact01(blocktri, g, left=True)
        wide = DN_HEADS * DN_DIM
        src = lax.broadcasted_iota(jnp.int32, (LANES, wide), 0)
        head = lax.broadcasted_iota(jnp.int32, (LANES, wide), 1) // DN_DIM
        beta_ref[...] = _dot_exact01(beta, jnp.where(src == head, 1.0, 0.0).astype(BF16), left=False)
        g_ref[...] = _dot_exact01(g, jnp.where(src == head + DN_HEADS, 1.0, 0.0).astype(BF16), left=False)


def _dn_prep(proj, conv_w, alog_row, dtb_row, seq, t=512):
    n = proj.shape[0]
    width = DN_HEADS * DN_DIM
    hb = t // SUBLANES
    return pl.pallas_call(
        functools.partial(_dn_prep_kernel, blocks_per_seq=seq // t),
        grid=(n // t, 3),
        in_specs=[
            pl.BlockSpec((t, width), lambda i, j: (i, OFF_DQ // width + j)),
            pl.BlockSpec((SUBLANES, width), lambda i, j: (jnp.maximum(i * hb - 1, 0), OFF_DQ // width + j)),
            pl.BlockSpec((DN_CONV, width), lambda i, j: (0, j)),
            pl.BlockSpec((t, LANES), lambda i, j: (i, OFF_DBA // LANES)),
            pl.BlockSpec((1, LANES), lambda i, j: (0, 0)),
            pl.BlockSpec((1, LANES), lambda i, j: (0, 0)),
        ],
        out_specs=[
            pl.BlockSpec((t, width), lambda i, j: (i, j)),
            pl.BlockSpec((t, width), lambda i, j: (i, 0)),
            pl.BlockSpec((t, width), lambda i, j: (i, 0)),
        ],
        out_shape=[
            jax.ShapeDtypeStruct((n, 3 * width), F32),
            jax.ShapeDtypeStruct((n, width), F32),
            jax.ShapeDtypeStruct((n, width), F32),
        ],
        scratch_shapes=[pltpu.VMEM((t + SUBLANES, width), F32)],
        compiler_params=_cparams(("arbitrary", "arbitrary")),
        name="dn_prep",
    )(proj, proj, conv_w, proj, alog_row, dtb_row)


def _dot_nt(a, b):
    return lax.dot_general(a, b, (((1,), (1,)), ((), ())), preferred_element_type=F32)


def _mm_dn(a, b):
    return jnp.dot(a.astype(BF16), b.astype(BF16), preferred_element_type=F32)


def _deltanet_kernel(q_ref, k_ref, v_ref, beta_ref, g_ref, z_ref, nw_ref, o_ref, state_ref):
    c = DN_CHUNK
    t = q_ref.shape[0]
    n_chunks = t // c

    @pl.when(pl.program_id(2) == 0)
    def _():
        state_ref[...] = jnp.zeros_like(state_ref)

    gc_all = g_ref[...]
    beta_all = beta_ref[...]

    gsz = DN_GROUP
    cpg = gsz // c
    ri = lax.broadcasted_iota(jnp.int32, (gsz, gsz), 0)
    ci = lax.broadcasted_iota(jnp.int32, (gsz, gsz), 1)
    same = (ri // c) == (ci // c)
    causal = (ri >= ci) & same
    strict = (ri > ci) & same
    eye = jnp.where(ri == ci, 1.0, 0.0)
    chunk_of_col = lax.broadcasted_iota(jnp.int32, (DN_DIM, gsz), 1) // c

    heads = q_ref.shape[1] // DN_DIM
    groups = range(heads * (t // gsz))
    rows = [slice((p % (t // gsz)) * gsz, (p % (t // gsz) + 1) * gsz) for p in groups]
    cols = [slice((p // (t // gsz)) * DN_DIM, (p // (t // gsz) + 1) * DN_DIM) for p in groups]
    q = [q_ref[rows[p], cols[p]] for p in groups]
    k = [k_ref[rows[p], cols[p]] for p in groups]
    gc = [gc_all[rows[p], cols[p]] for p in groups]
    eg = [jnp.exp(x) for x in gc]
    beta = [beta_all[rows[p], cols[p]] for p in groups]
    kb = [k[gi] * beta[gi] for gi in groups]
    rhs = [jnp.concatenate([kb[gi] * eg[gi], v_ref[rows[gi], cols[gi]] * beta[gi]], axis=1) for gi in groups]
    decay = []
    for gi in groups:
        gc2 = jnp.concatenate([gc[gi]] * (gsz // DN_DIM), axis=1)
        gcr = jnp.sum(gc2 * eye, axis=0, keepdims=True)
        decay.append(jnp.exp(jnp.where(causal, gc2 - gcr, NEG_INF)))
    a2 = [_dot_nt(jnp.concatenate([kb[gi], q[gi]], axis=0).astype(BF16), k[gi].astype(BF16)) for gi in groups]
    low = [jnp.where(strict, a2[gi][:gsz] * decay[gi], 0.0) for gi in groups]
    qk = [a2[gi][gsz:] * decay[gi] for gi in groups]
    size = 2
    pair = ((ri // size) == (ci // size)) & (ri > ci)
    inv = [eye - jnp.where(pair, x, 0.0) for x in low]
    while size < c:
        size *= 2
        pair = ((ri // size) == (ci // size)) & ((ri // (size // 2)) != (ci // (size // 2))) & (ri > ci)
        sub = [jnp.where(pair, x, 0.0) for x in low]
        y = [_mm_dn(sub[gi], inv[gi]) for gi in groups]
        inv = [inv[gi] - _mm_dn(inv[gi], y[gi]) for gi in groups]
    wu = [_mm_dn(inv[gi], rhs[gi]) for gi in groups]
    lasts, res = [], []
    for gi in groups:
        last = [gc[gi][(n + 1) * c - 1:(n + 1) * c, :] for n in range(cpg)]
        lasts.append(last)
        g_last_rows = jnp.concatenate([jnp.broadcast_to(x, (c, DN_DIM)) for x in last], axis=0)
        kd_t = jnp.transpose(k[gi] * jnp.exp(g_last_rows - gc[gi]))
        xs = [jnp.where(chunk_of_col == n, kd_t, 0.0) for n in range(cpg)]
        xs += [qk[gi][n * c:(n + 1) * c, :] for n in range(cpg)]
        res.append(jnp.dot(jnp.concatenate(xs, axis=0).astype(BF16), wu[gi].astype(BF16),
                           preferred_element_type=F32))
    pre = []
    for gi in groups:
        qd = q[gi] * eg[gi]
        for n in range(cpg):
            kw = res[gi][n * DN_DIM:(n + 1) * DN_DIM, :DN_DIM]
            ku = res[gi][n * DN_DIM:(n + 1) * DN_DIM, DN_DIM:]
            base = cpg * DN_DIM + n * c
            coef = qd[n * c:(n + 1) * c, :] - res[gi][base:base + c, :DN_DIM]
            qku = res[gi][base:base + c, DN_DIM:]
            lhs = jnp.concatenate([-kw, coef], axis=0).astype(BF16)
            pre.append((lhs, ku, qku, jnp.exp(lasts[gi][n])))

    states = [state_ref[hd] for hd in range(heads)]
    nw = nw_ref[...]
    for n in range(n_chunks):
        crow = slice(n * c, (n + 1) * c)
        for hd in range(heads):
            lhs, ku, qku, gl = pre[hd * n_chunks + n]
            r = jnp.dot(lhs, states[hd].astype(BF16), preferred_element_type=F32)
            o = r[DN_DIM:] + qku
            states[hd] = states[hd] * gl + r[:DN_DIM] + ku
            ccol = slice(hd * DN_DIM, (hd + 1) * DN_DIM)
            z = z_ref[crow, ccol]
            o_ref[crow, ccol] = (_rms_rows(o, nw) * (z * jax.nn.sigmoid(z))).astype(o_ref.dtype)
    for hd in range(heads):
        state_ref[hd] = states[hd]


def _deltanet(qkv, beta, g, proj, norm_w, bsz, seq, t=512, heads=4):
    n = qkv.shape[0]
    spb = seq // t
    hw = heads * DN_DIM
    hsteps = DN_HEADS // heads
    zoff = OFF_DZ // hw
    col = lambda off: (lambda b, h, i: (b * spb + i, off + h))
    return pl.pallas_call(
        _deltanet_kernel,
        grid=(bsz, hsteps, spb),
        in_specs=[
            pl.BlockSpec((t, hw), col(0)),
            pl.BlockSpec((t, hw), col(hsteps)),
            pl.BlockSpec((t, hw), col(2 * hsteps)),
            pl.BlockSpec((t, hw), col(0)),
            pl.BlockSpec((t, hw), col(0)),
            pl.BlockSpec((t, hw), col(zoff)),
            pl.BlockSpec((1, DN_DIM), lambda b, h, i: (0, 0)),
        ],
        out_specs=pl.BlockSpec((t, hw), col(0)),
        out_shape=jax.ShapeDtypeStruct((n, DN_HEADS * DN_DIM), BF16),
        scratch_shapes=[pltpu.VMEM((heads, DN_DIM, DN_DIM), F32)],
        compiler_params=_cparams(("arbitrary", "arbitrary", "arbitrary")),
        name="deltanet",
    )(qkv, qkv, qkv, beta, g, proj, norm_w)


def _regroup_w_in(w_in):
    d = w_in.shape[0]
    nsa_q, nsa_kv, dn = N_HEADS * HEAD_DIM, N_GROUPS * HEAD_DIM, DN_HEADS * DN_DIM
    sizes = (nsa_q,) + (nsa_kv,) * 6 + (3 * N_HEADS,) + (dn,) * 4 + (DN_HEADS,) * 2 + (d, d)
    offs = np.concatenate([[0], np.cumsum(sizes)])
    (nq, kc, vc, ksel, vsel, kwin, vwin, ngate, dq, dk, dv, dz, db, da, mga, mgb) = (
        w_in[:, offs[i]:offs[i + 1]] for i in range(len(sizes)))
    zeros = lambda w: jnp.zeros((d, w), w_in.dtype)
    ngate = jnp.transpose(ngate.reshape(d, N_GROUPS, HPG, 3), (0, 1, 3, 2)).reshape(d, N_GROUPS, 3 * HPG)
    ngate = jnp.pad(ngate, ((0, 0), (0, 0), (0, GATE_ROWS - 3 * HPG))).reshape(d, N_GROUPS * GATE_ROWS)
    cols = [mga, mgb, dq, dk, dv, nq, dz, kc, vc, ksel, vsel, kwin, vwin,
            ngate, zeros(LANES - N_GROUPS * GATE_ROWS), db, da, zeros(LANES - 2 * DN_HEADS), zeros(PROJ_DIM - PROJ_USED)]
    return jnp.concatenate(cols, axis=1).astype(BF16)


def _overlap_t(n_sb, n_pieces):
    n = np.arange(N_CMP_PAD) - CMP_PAD
    j = np.arange(n_sb)
    valid = (n >= 0) & (n < n_pieces - 1)
    c0 = n * CMP_STRIDE
    s0 = j * SEL_BLOCK
    ov = (c0[None, :] < s0[:, None] + SEL_BLOCK) & (c0[None, :] + CMP_LEN > s0[:, None]) & valid[None, :]
    return jnp.asarray(ov.astype(np.float32), dtype=BF16)


def _token_mixer(h, bsz, seq, norm_mix, w_in, q_norm, k_norm, cmp_pos, cmp_w1, cmp_b1, cmp_w2, dn_conv,
                 dn_a_log, dn_dt_bias, dn_norm, w_a, w_b, w_out, bias_tiles):
    n = bsz * seq
    nqb = seq // Q_BLOCK
    npc = seq // CMP_STRIDE
    n_sb = seq // SEL_BLOCK
    assert npc + CMP_PAD <= N_CMP_PAD and nqb * SUBLANES + NEAR_ROWS <= N_CMP_PAD
    assert nqb % (2 * FAR_SUB) == 0
    assert nqb % QB_PER_STEP == 0
    proj = _norm_matmul(h, norm_mix.reshape(1, -1), _regroup_w_in(w_in))

    k_norm8 = jnp.concatenate([k_norm, jnp.zeros((SUBLANES - 3, HEAD_DIM), F32)], axis=0)
    qT, kc_raw, vc_raw, ks, vsT, kw, vwT, gT = _nsa_prep(proj, q_norm.reshape(1, -1), k_norm8, bsz, seq)
    pieces = jnp.stack([kc_raw, vc_raw], axis=0).reshape(2, bsz, N_GROUPS, npc, CMP_STRIDE * HEAD_DIM)
    cmp = _compress(pieces, cmp_w1, cmp_pos.reshape(2, 1, CMP_LEN * HEAD_DIM), cmp_b1.reshape(2, 1, -1),
                    cmp_w2, k_norm8)
    cmp = jnp.pad(cmp, ((0, 0), (0, 0), (0, 0), (CMP_PAD, N_CMP_PAD - CMP_PAD - npc), (0, 0))).astype(BF16)
    kc = cmp[0]
    vcT = jnp.swapaxes(cmp[1], -1, -2)
    y_nsa = _nsa_attn(qT, gT, ks, vsT, kw, vwT, kc, vcT, _overlap_t(n_sb, npc), *bias_tiles)

    lane_pad = lambda v, off: jnp.zeros((1, LANES), F32).at[0, off:off + DN_HEADS].set(v)
    qkv, beta, g = _dn_prep(proj, dn_conv, lane_pad(dn_a_log, DN_HEADS), lane_pad(dn_dt_bias, DN_HEADS), seq)
    y_dn = _deltanet(qkv, beta, g, proj, dn_norm.reshape(1, -1), bsz, seq)

    return _merge_out(y_nsa, y_dn, w_a.astype(BF16), w_b.astype(BF16), proj, w_out.astype(BF16), h)


def kernel(x, rel_table, norm_ffn1, ffn1_gate, ffn1_up, ffn1_down, norm_mix, w_in, q_norm, k_norm, cmp_pos, cmp_w1, cmp_b1, cmp_w2, dn_conv, dn_a_log, dn_dt_bias, dn_norm, w_branch_nsa, w_branch_dn, w_out, norm_ffn2, ffn2_gate, ffn2_up, ffn2_down):
    bsz, seq, d = x.shape
    depth = w_in.shape[0]
    h = x.reshape(bsz * seq, d)
    bias_tiles = _bias_tiles(rel_table)
    for l in range(depth):
        h = _ffn(h, norm_ffn1[l].reshape(1, -1), ffn1_gate[l].astype(BF16), ffn1_up[l].astype(BF16),
                 ffn1_down[l].astype(BF16))
        h = _token_mixer(h, bsz, seq, norm_mix[l], w_in[l], q_norm[l], k_norm[l], cmp_pos[l], cmp_w1[l],
                         cmp_b1[l], cmp_w2[l], dn_conv[l], dn_a_log[l], dn_dt_bias[l], dn_norm[l],
                         w_branch_nsa[l], w_branch_dn[l], w_out[l], bias_tiles)
        h = _ffn(h, norm_ffn2[l].reshape(1, -1), ffn2_gate[l].astype(BF16), ffn2_up[l].astype(BF16),
                 ffn2_down[l].astype(BF16))
    return h.reshape(bsz, seq, d)
```
